```python
import jax, jax.numpy as jnp
from jax import lax
import numpy as np

D_MODEL = 1024
BATCH = 2
SEQ = 16384
DEPTH = 4

GRID_W = 64
CTX_LEN = 256
D_MIX = D_MODEL
HEAD_DIM = 64
CONV_CH = D_MIX // 4
CONV_K = 3
ATT_HEADS = D_MIX // 2 // HEAD_DIM
ATT_KV_HEADS = 2
ATT_GROUP = ATT_HEADS // ATT_KV_HEADS
WINDOW = 128
BLOCK = 128
GLA_HEADS = D_MIX // 4 // HEAD_DIM
GLA_DK = HEAD_DIM
GLA_DV = HEAD_DIM
GLA_RANK = 16
GLA_NORMALIZER = 16.0
GLA_CHUNK = 32
D_FF = ((8 * D_MODEL + 3 * 256 - 1) // (3 * 256)) * 256
N_MOD = 6
ROPE_BASE = 10000.0
EPS = 1e-6
COL_SIZES = (CONV_CH, CONV_CH, CONV_CH,
             ATT_HEADS * HEAD_DIM, ATT_KV_HEADS * HEAD_DIM, ATT_KV_HEADS * HEAD_DIM,
             GLA_HEADS * GLA_DK, GLA_HEADS * GLA_DK, GLA_HEADS * GLA_DV, GLA_HEADS * GLA_DV,
             2 * GLA_RANK)
N_IN = sum(COL_SIZES)
COL_SPLITS = tuple(int(v) for v in np.cumsum(COL_SIZES)[:-1])

kernel_name = "hybrid_parallel_groups_dit"


def rmsnorm(x, g):
    x32 = x.astype(jnp.float32)
    y = x32 * lax.rsqrt(jnp.mean(x32 * x32, axis=-1, keepdims=True) + EPS)
    return (y * g.astype(jnp.float32)).astype(x.dtype)


def heads(a, n):
    return a.reshape(a.shape[:-1] + (n, a.shape[-1] // n))


def axial_rope_tables(rows):
    t = jnp.arange(rows * GRID_W)
    row = (t // GRID_W).astype(jnp.float32)
    col = (t % GRID_W).astype(jnp.float32)
    n_freq = HEAD_DIM // 4
    inv_freq = ROPE_BASE ** (-jnp.arange(n_freq, dtype=jnp.float32) / n_freq)
    ang = jnp.stack([row[:, None] * inv_freq, col[:, None] * inv_freq], axis=1)
    return jnp.cos(ang), jnp.sin(ang)


def apply_rope(x, cos, sin):
    B, S, H, D = x.shape
    xs = x.astype(jnp.float32).reshape(B, S, H, 2, D // 2)
    half = D // 4
    x1, x2 = xs[..., :half], xs[..., half:]
    c = cos[None, :, None]
    s = sin[None, :, None]
    out = jnp.concatenate([x1 * c - x2 * s, x2 * c + x1 * s], axis=-1)
    return out.reshape(B, S, H, D).astype(x.dtype)


def short_conv(x_in, b_gate, c_gate, w):
    T = x_in.shape[1]
    u = c_gate * x_in
    up = jnp.pad(u, ((0, 0), (CONV_K // 2, CONV_K // 2), (0, 0)))
    y = sum(w[j] * up[:, j:j + T] for j in range(CONV_K))
    return b_gate * y


def softmax_with_sink(scores, sink):
    s = jnp.broadcast_to(sink.astype(jnp.float32).reshape(ATT_KV_HEADS, ATT_GROUP, 1, 1),
                         scores.shape[:-1] + (1,))
    p = jax.nn.softmax(jnp.concatenate([s, scores], axis=-1), axis=-1)
    return p[..., 1:]


def context_attention(q, k, v, sink):
    B, L = q.shape[:2]
    qg = q.reshape(B, L, ATT_KV_HEADS, ATT_GROUP, HEAD_DIM)
    s = jnp.einsum('bqhgd,bkhd->bhgqk', qg, k).astype(jnp.float32) * HEAD_DIM ** -0.5
    p = softmax_with_sink(s, sink).astype(v.dtype)
    o = jnp.einsum('bhgqk,bkhd->bqhgd', p, v)
    return o.reshape(B, L, ATT_HEADS * HEAD_DIM)


def window_attention(q, k, v, k_ctx, v_ctx, sink):
    B, S = q.shape[:2]
    L = k_ctx.shape[1]
    nb = S // BLOCK
    qb = q.reshape(B, nb, BLOCK, ATT_KV_HEADS, ATT_GROUP, HEAD_DIM).transpose(1, 0, 2, 3, 4, 5)
    pad = ((0, 0), (BLOCK, BLOCK), (0, 0), (0, 0))
    kp = jnp.pad(k, pad)
    vp = jnp.pad(v, pad)
    r = jnp.arange(BLOCK)
    j = jnp.arange(3 * BLOCK)
    scale = HEAD_DIM ** -0.5

    def one_block(args):
        qi, i = args
        start = i * BLOCK
        ks = lax.dynamic_slice_in_dim(kp, start, 3 * BLOCK, axis=1)
        vs = lax.dynamic_slice_in_dim(vp, start, 3 * BLOCK, axis=1)
        s_ctx = jnp.einsum('bqhgd,bkhd->bhgqk', qi, k_ctx).astype(jnp.float32)
        s_loc = jnp.einsum('bqhgd,bkhd->bhgqk', qi, ks).astype(jnp.float32)
        tq = start + r
        tk = start - BLOCK + j
        valid = (jnp.abs(tq[:, None] - tk[None, :]) <= WINDOW) & (tk[None, :] >= 0) & (tk[None, :] < S)
        s_loc = jnp.where(valid, s_loc, -jnp.inf)
        p = softmax_with_sink(jnp.concatenate([s_ctx, s_loc], axis=-1) * scale, sink).astype(v.dtype)
        o = (jnp.einsum('bhgqk,bkhd->bqhgd', p[..., :L], v_ctx)
             + jnp.einsum('bhgqk,bkhd->bqhgd', p[..., L:], vs))
        return o.reshape(B, BLOCK, ATT_HEADS * HEAD_DIM)

    out = lax.map(one_block, (qb, jnp.arange(nb)))
    return out.transpose(1, 0, 2, 3).reshape(B, S, ATT_HEADS * HEAD_DIM)


def gla_chunk_scan(q, k, v, log_g, state0, with_output):
    B, T, H, _ = q.shape
    n = T // GLA_CHUNK

    def to_chunks(a):
        return a.reshape(B, n, GLA_CHUNK, H, a.shape[-1]).transpose(1, 0, 3, 2, 4)

    lower_tri = jnp.tril(jnp.ones((GLA_CHUNK, GLA_CHUNK), dtype=bool))[:, :, None]

    def step(state, inp):
        qc, kc, vc, gc = inp
        b = jnp.cumsum(gc.astype(jnp.float32), axis=2)
        b_last = b[:, :, -1:, :]
        new_state = (jnp.exp(b_last[:, :, 0, :])[..., None] * state
                     + jnp.einsum('bhsk,bhsv->bhkv', kc * jnp.exp(b_last - b), vc))
        if not with_output:
            return new_state, None
        diff = jnp.where(lower_tri, b[:, :, :, None, :] - b[:, :, None, :, :], -jnp.inf)
        a = jnp.einsum('bhtk,bhsk,bhtsk->bhts', qc, kc, jnp.exp(diff))
        o = (jnp.einsum('bhts,bhsv->bhtv', a, vc)
             + jnp.einsum('bhtk,bhkv->bhtv', qc * jnp.exp(b), state))
        return new_state, o

    state, o = lax.scan(step, state0, (to_chunks(q), to_chunks(k), to_chunks(v), to_chunks(log_g)))
    if not with_output:
        return None, state
    return o.transpose(1, 0, 3, 2, 4).reshape(B, T, H, v.shape[-1]), state


def gla_output(o, g, norm_g):
    o32 = o.astype(jnp.float32)
    o32 = o32 * lax.rsqrt(jnp.mean(o32 * o32, axis=-1, keepdims=True) + EPS) * norm_g.astype(jnp.float32)
    B, T = o.shape[:2]
    return (o32.reshape(B, T, GLA_HEADS * GLA_DV) * jax.nn.silu(g.astype(jnp.float32))).astype(g.dtype)


def gla_bidir(q, k, v, lr, g, q_c, k_c, v_c, lr_c, g_c, w_gate, b_gate, norm_g, ctx_out):
    B = q.shape[0]
    q = heads(q, GLA_HEADS) * GLA_DK ** -0.5
    k = heads(k, GLA_HEADS)
    v = heads(v, GLA_HEADS)
    q_c = heads(q_c, GLA_HEADS) * GLA_DK ** -0.5
    k_c = heads(k_c, GLA_HEADS)
    v_c = heads(v_c, GLA_HEADS)
    outs_x, outs_c = [], []
    for d in range(2):
        sl = slice(d * GLA_RANK, (d + 1) * GLA_RANK)
        gate_x = heads(jax.nn.log_sigmoid(lr[..., sl] @ w_gate[d] + b_gate[d]) / GLA_NORMALIZER, GLA_HEADS)
        gate_c = heads(jax.nn.log_sigmoid(lr_c[..., sl] @ w_gate[d] + b_gate[d]) / GLA_NORMALIZER, GLA_HEADS)
        flip = (lambda a: jnp.flip(a, axis=1)) if d == 1 else (lambda a: a)
        state0 = jnp.zeros((B, GLA_HEADS, GLA_DK, GLA_DV), jnp.float32)
        o_c, s_c = gla_chunk_scan(flip(q_c), flip(k_c), flip(v_c), flip(gate_c), state0, ctx_out)
        o_x, _ = gla_chunk_scan(flip(q), flip(k), flip(v), flip(gate_x), s_c, True)
        outs_x.append(flip(o_x))
        if ctx_out:
            outs_c.append(flip(o_c))
    y_x = gla_output(outs_x[0] + outs_x[1], g, norm_g)
    y_c = gla_output(outs_c[0] + outs_c[1], g_c, norm_g) if ctx_out else None
    return y_x, y_c


def swiglu(h, w_up, w_down):
    a, b = jnp.split(h @ w_up, 2, axis=-1)
    return (jax.nn.silu(a) * b) @ w_down


def layer(x, ctx, mod_x, mod_c, cos, sin, g1, g2, w_in, conv_w, sink, gate_w, gate_b, norm_g,
          w_out, w_up, w_down, ctx_out):
    sh1, sc1, gt1, sh2, sc2, gt2 = jnp.split(mod_x[:, None, :], N_MOD, axis=-1)
    csh1, csc1, cgt1, csh2, csc2, cgt2 = jnp.split(mod_c, N_MOD, axis=-1)
    hx = rmsnorm(x, g1) * (1 + sc1) + sh1
    hc = rmsnorm(ctx, g1) * (1 + csc1) + csh1
    (xc_in, xc_b, xc_c, xq, xk, xv, xgq, xgk, xgv, xgg, xlr) = jnp.split(hx @ w_in, COL_SPLITS, axis=-1)
    (cc_in, cc_b, cc_c, cq, ck, cv, cgq, cgk, cgv, cgg, clr) = jnp.split(hc @ w_in, COL_SPLITS, axis=-1)
    k_ctx = heads(ck, ATT_KV_HEADS)
    v_ctx = heads(cv, ATT_KV_HEADS)
    conv_x = short_conv(xc_in, xc_b, xc_c, conv_w)
    attn_x = window_attention(apply_rope(heads(xq, ATT_HEADS), cos, sin),
                              apply_rope(heads(xk, ATT_KV_HEADS), cos, sin),
                              heads(xv, ATT_KV_HEADS), k_ctx, v_ctx, sink)
    gla_x, gla_c = gla_bidir(xgq, xgk, xgv, xlr, xgg, cgq, cgk, cgv, clr, cgg,
                             gate_w, gate_b, norm_g, ctx_out)
    x = x + gt1 * (jnp.concatenate([conv_x, attn_x, gla_x], axis=-1) @ w_out)
    x = x + gt2 * swiglu(rmsnorm(x, g2) * (1 + sc2) + sh2, w_up, w_down)
    if ctx_out:
        conv_c = short_conv(cc_in, cc_b, cc_c, conv_w)
        attn_c = context_attention(heads(cq, ATT_HEADS), k_ctx, v_ctx, sink)
        ctx = ctx + cgt1 * (jnp.concatenate([conv_c, attn_c, gla_c], axis=-1) @ w_out)
        ctx = ctx + cgt2 * swiglu(rmsnorm(ctx, g2) * (1 + csc2) + csh2, w_up, w_down)
    return x, ctx


def setup_inputs(seed: int = 0) -> dict:
    key = jax.random.key(seed)
    ks = jax.random.split(key, 18)

    def nrm(k, shape, s):
        return jax.random.normal(k, shape, jnp.float32) * s

    return {
        "x": nrm(ks[0], (BATCH, SEQ, D_MODEL), 1.0),
        "c": nrm(ks[1], (BATCH, D_MODEL), 1.0),
        "ctx": nrm(ks[2], (BATCH, CTX_LEN, D_MODEL), 1.0),
        "c_ctx": nrm(ks[3], (D_MODEL,), 1.0),
        "w_mod": nrm(ks[4], (DEPTH, D_MODEL, N_MOD * D_MODEL), 0.02),
        "b_mod": nrm(ks[5], (DEPTH, N_MOD * D_MODEL), 0.01),
        "norm1_g": 1.0 + nrm(ks[6], (DEPTH, D_MODEL), 0.02),
        "norm2_g": 1.0 + nrm(ks[7], (DEPTH, D_MODEL), 0.02),
        "w_in": nrm(ks[8], (DEPTH, D_MODEL, N_IN), D_MODEL ** -0.5),
        "conv_w": nrm(ks[9], (DEPTH, CONV_K, CONV_CH), CONV_K ** -0.5),
        "attn_sink": nrm(ks[10], (DEPTH, ATT_HEADS), 1.0),
        "gla_gate_w": nrm(ks[11], (DEPTH, 2, GLA_RANK, GLA_HEADS * GLA_DK), GLA_RANK ** -0.5),
        "gla_gate_b": nrm(ks[12], (DEPTH, 2, GLA_HEADS * GLA_DK), 0.1),
        "gla_norm_g": 1.0 + nrm(ks[13], (DEPTH, GLA_DV), 0.02),
        "w_out": nrm(ks[14], (DEPTH, D_MIX, D_MODEL), D_MIX ** -0.5),
        "w_up": nrm(ks[15], (DEPTH, D_MODEL, 2 * D_FF), D_MODEL ** -0.5),
        "w_down": nrm(ks[16], (DEPTH, D_FF, D_MODEL), D_FF ** -0.5),
        "final_norm_g": 1.0 + nrm(ks[17], (D_MODEL,), 0.02),
    }


def reference(x, c, ctx, c_ctx, w_mod, b_mod, norm1_g, norm2_g, w_in, conv_w, attn_sink,
              gla_gate_w, gla_gate_b, gla_norm_g, w_out, w_up, w_down, final_norm_g):
    ROWS = x.shape[1] // GRID_W
    cos, sin = axial_rope_tables(ROWS)
    for l in range(DEPTH):
        mod_x = jax.nn.silu(c) @ w_mod[l] + b_mod[l]
        mod_c = jax.nn.silu(c_ctx) @ w_mod[l] + b_mod[l]
        x, ctx = layer(x, ctx, mod_x, mod_c, cos, sin, norm1_g[l], norm2_g[l], w_in[l], conv_w[l],
                       attn_sink[l], gla_gate_w[l], gla_gate_b[l], gla_norm_g[l], w_out[l],
                       w_up[l], w_down[l], l < DEPTH - 1)
    return rmsnorm(x, final_norm_g)
```

```python
import functools

import numpy as np
import jax
import jax.numpy as jnp
from jax import lax
from jax.experimental import pallas as pl
from jax.experimental.pallas import tpu as pltpu

F32 = jnp.float32
BF16 = jnp.bfloat16

D_MODEL = 1024
HEAD_DIM = 64
CONV_CH = 256
CONV_K = 3
ATT_HEADS = 8
ATT_KV_HEADS = 2
ATT_W = ATT_HEADS * HEAD_DIM
KV_W = ATT_KV_HEADS * HEAD_DIM
WINDOW = 128
GLA_HEADS = 4
GLA_W = GLA_HEADS * HEAD_DIM
GLA_RANK = 16
GLA_NORMALIZER = 16.0
D_FF = 2816
N_MOD = 6
GRID_W = 64
ROPE_BASE = 10000.0
EPS = 1e-6

LANES = 128
C_CONV = 0
C_QK = 768
C_V = 1408
C_GLA = 1536
C_LR = 2560
N_IN_PAD = C_LR + LANES
Q_HEAD_ORDER = (0, 4, 1, 5, 2, 6, 3, 7)

GLA_CHUNK = 64
GLA_SUB = 16
FF_CHUNK = 256
N_FF_CHUNKS = D_FF // FF_CHUNK

VMEM_LIMIT = 56 * 1024 * 1024


def _const_spec(shape):
    nd = len(shape)
    return pl.BlockSpec(shape, lambda *_: (0,) * nd, pipeline_mode=pl.Buffered(1))


def _params(n_grid):
    return pltpu.CompilerParams(
        dimension_semantics=("arbitrary",) * n_grid, vmem_limit_bytes=VMEM_LIMIT)


def _silu(a):
    return a * (1.0 / (1.0 + jnp.exp(-a)))


def _dot(a, b):
    return jnp.dot(a, b, preferred_element_type=F32)


def _dot_nt(a, b):
    return lax.dot_general(a, b, (((1,), (1,)), ((), ())), preferred_element_type=F32)


def _dot_tn(a, b):
    return lax.dot_general(a, b, (((0,), (0,)), ((), ())), preferred_element_type=F32)


def _mod_kernel(c_ref, w_ref, b_ref, o_ref):
    a = _silu(c_ref[...]).astype(BF16)
    o_ref[...] = _dot(a, w_ref[...].astype(BF16)) + b_ref[...]


def _modulation(cvec, w_mod, b_mod):
    L, D, W = w_mod.shape
    tn = 2048
    return pl.pallas_call(
        _mod_kernel,
        grid=(L, W // tn),
        in_specs=[
            pl.BlockSpec((8, D), lambda l, j: (0, 0)),
            pl.BlockSpec((None, D, tn), lambda l, j: (l, 0, j)),
            pl.BlockSpec((None, 1, tn), lambda l, j: (l, 0, j)),
        ],
        out_specs=pl.BlockSpec((None, 8, tn), lambda l, j: (l, 0, j)),
        out_shape=jax.ShapeDtypeStruct((L, 8, W), F32),
        compiler_params=_params(2),
        name="modulation",
    )(cvec, w_mod, b_mod.reshape(L, 1, W))


def _proj_kernel(x_ref, mod_ref, g1_ref, w_ref, wg_ref, bg_ref, *rest, rope):
    if rope:
        cos_ref, sa_ref, sb_ref = rest[:3]
        rest = rest[3:]
    cu_ref, cb_ref, q_ref, k_ref, v_ref, gl_ref, gate_ref = rest

    x = x_ref[...]
    ms = jnp.mean(x * x, axis=-1, keepdims=True)
    y = x * lax.rsqrt(ms + EPS) * g1_ref[...]
    h = (y * (1.0 + mod_ref[1:2, :]) + mod_ref[0:1, :]).astype(BF16)

    def proj(lo, hi):
        return _dot(h, w_ref[:, lo:hi])

    pc = proj(C_CONV, C_QK)
    cu_ref[...] = pc[:, 2 * CONV_CH:3 * CONV_CH] * pc[:, 0:CONV_CH]
    cb_ref[...] = pc[:, CONV_CH:2 * CONV_CH]

    pqk = proj(C_QK, C_V)
    scale = HEAD_DIM ** -0.5
    for m in range((ATT_W + KV_W) // LANES):
        col = pqk[:, m * LANES:(m + 1) * LANES]
        if rope:
            col = (col * cos_ref[...]
                   + pltpu.roll(col, LANES - HEAD_DIM // 4, axis=1) * sa_ref[...]
                   + pltpu.roll(col, HEAD_DIM // 4, axis=1) * sb_ref[...])
        if m < ATT_W // LANES:
            q_ref[:, m * LANES:(m + 1) * LANES] = (col * scale).astype(BF16)
        else:
            k_ref[...] = col.astype(BF16)

    v_ref[...] = proj(C_V, C_GLA).astype(BF16)

    pg = proj(C_GLA, C_LR)
    gl_ref[:, 0:GLA_W] = pg[:, 0:GLA_W] * (HEAD_DIM ** -0.5)
    gl_ref[:, GLA_W:4 * GLA_W] = pg[:, GLA_W:4 * GLA_W]

    lr = proj(C_LR, N_IN_PAD).astype(BF16)
    z = _dot(lr, wg_ref[...]) + bg_ref[...]
    log_sig = jnp.minimum(z, 0.0) - jnp.log1p(jnp.exp(-jnp.abs(z)))
    gate_ref[...] = log_sig * (1.0 / GLA_NORMALIZER)


def _proj(xs, mods_l, mod_row, g1, w_in, wg, bg, rope_tabs, tm):
    B, S, D = xs.shape
    rope = rope_tabs is not None
    nt = S // tm
    row = lambda w: pl.BlockSpec((None, tm, w), lambda b, i: (b, i, 0))
    if mod_row is None:
        mod_spec = pl.BlockSpec((None, 8, D), lambda b, i: (b, 0, 0))
    else:
        mod_spec = pl.BlockSpec((None, 8, D), lambda b, i: (mod_row, 0, 0))
    in_specs = [row(D), mod_spec, _const_spec((1, D)), _const_spec((D, N_IN_PAD)),
                _const_spec((LANES, 2 * GLA_W)), _const_spec((1, 2 * GLA_W))]
    args = [xs, mods_l, g1, w_in, wg, bg]
    if rope:
        in_specs += [pl.BlockSpec((tm, LANES), lambda b, i: (i, 0))] * 3
        args += list(rope_tabs)
    widths = (CONV_CH, CONV_CH, ATT_W, KV_W, KV_W, 4 * GLA_W, 2 * GLA_W)
    dtypes = (F32, F32, BF16, BF16, BF16, F32, F32)
    return pl.pallas_call(
        functools.partial(_proj_kernel, rope=rope),
        grid=(B, nt),
        in_specs=in_specs,
        out_specs=[row(w) for w in widths],
        out_shape=[jax.ShapeDtypeStruct((B, S, w), dt) for w, dt in zip(widths, dtypes)],
        compiler_params=_params(2),
        name="proj_rope" if rope else "proj_ctx",
    )(*args)


def _attn_kernel(sink_ref, q_ref, kc_ref, vc_ref, *rest, local, tq):
    if local:
        kp_ref, km_ref, kn_ref, vp_ref, vm_ref, vn_ref, o_ref = rest
        i = pl.program_id(1)
        n = pl.num_programs(1)
        kk = jnp.concatenate([kc_ref[...], kp_ref[...], km_ref[...], kn_ref[...]], axis=0)
        vv = jnp.concatenate([vc_ref[...], vp_ref[...], vm_ref[...], vn_ref[...]], axis=0)
        n_ctx = kc_ref.shape[0]
        r = lax.broadcasted_iota(jnp.int32, (tq, 3 * tq), 0)
        j = lax.broadcasted_iota(jnp.int32, (tq, 3 * tq), 1)
        valid = (j >= r) & (j <= r + 2 * WINDOW)
        valid = valid & ((j >= tq) | (i > 0)) & ((j < 2 * tq) | (i < n - 1))
        bias = jnp.concatenate(
            [jnp.zeros((tq, n_ctx), F32), jnp.where(valid, 0.0, -jnp.inf)], axis=1)
    else:
        (o_ref,) = rest
        kk = kc_ref[...]
        vv = vc_ref[...]
    lo = lax.broadcasted_iota(jnp.int32, (1, LANES), 1) < HEAD_DIM

    for m in range(ATT_W // LANES):
        qc = q_ref[:, m * LANES:(m + 1) * LANES]
        halves = []
        for g in range(ATT_KV_HEADS):
            sink = sink_ref[m + (ATT_HEADS // ATT_KV_HEADS) * g]
            qz = jnp.where(lo if g == 0 else jnp.logical_not(lo), qc, jnp.zeros_like(qc))
            s = _dot_nt(qz, kk)
            if local:
                s = s + bias
            mx = jnp.maximum(jnp.max(s, axis=-1, keepdims=True), sink)
            p = jnp.exp(s - mx)
            den = jnp.sum(p, axis=-1, keepdims=True) + jnp.exp(sink - mx)
            halves.append(_dot(p.astype(BF16), vv) * (1.0 / den))
        o_ref[:, m * LANES:(m + 1) * LANES] = jnp.where(lo, halves[0], halves[1]).astype(BF16)


def _attention(q, k, v, k_ctx, v_ctx, sink, local):
    B, S, _ = q.shape
    Lc = k_ctx.shape[1]
    tq = WINDOW if local else S
    nq = S // tq
    in_specs = [
        pl.BlockSpec(memory_space=pltpu.SMEM),
        pl.BlockSpec((None, tq, ATT_W), lambda b, i: (b, i, 0)),
        pl.BlockSpec((None, Lc, KV_W), lambda b, i: (b, 0, 0)),
        pl.BlockSpec((None, Lc, KV_W), lambda b, i: (b, 0, 0)),
    ]
    args = [sink, q, k_ctx, v_ctx]
    if local:
        nb = [pl.BlockSpec((None, tq, KV_W), lambda b, i: (b, jnp.maximum(i - 1, 0), 0)),
              pl.BlockSpec((None, tq, KV_W), lambda b, i: (b, i, 0)),
              pl.BlockSpec((None, tq, KV_W), lambda b, i: (b, jnp.minimum(i + 1, nq - 1), 0))]
        in_specs += nb + nb
        args += [k, k, k, v, v, v]
    return pl.pallas_call(
        functools.partial(_attn_kernel, local=local, tq=tq),
        grid=(B, nq),
        in_specs=in_specs,
        out_specs=pl.BlockSpec((None, tq, ATT_W), lambda b, i: (b, i, 0)),
        out_shape=jax.ShapeDtypeStruct((B, S, ATT_W), BF16),
        compiler_params=_params(2),
        name="attn_window" if local else "attn_ctx",
    )(*args)


def _gla_consts(rev):
    t = np.arange(GLA_CHUNK)[:, None]
    s = np.arange(GLA_CHUNK)[None, :]
    same = (t // GLA_SUB) == (s // GLA_SUB)
    before = (s >= t) if rev else (s <= t)
    mats = [same & before, same & ~before, before, ~before]
    return np.concatenate(mats, axis=0).astype(np.float32)


def _head_blocks():
    h = np.arange(GLA_W) // HEAD_DIM
    return (h[:, None] == h[None, :]).astype(np.float32)


def _gla_kernel(cum_ref, ones_ref, bd_ref, gl_ref, g_ref, s0_ref, o_ref, sfin_ref, st_ref,
                *, rev, n_chunks):
    i = pl.program_id(1)
    n_sub = GLA_CHUNK // GLA_SUB
    order = list(range(n_sub))[::-1] if rev else list(range(n_sub))
    lane_head = lax.broadcasted_iota(jnp.int32, (1, GLA_W), 1) // HEAD_DIM
    head_mask = [lane_head == hh for hh in range(GLA_HEADS)]
    sub_row = lax.broadcasted_iota(jnp.int32, (GLA_SUB, 1), 0)

    @pl.when(i == 0)
    def _():
        st_ref[...] = s0_ref[...]

    def rows(a, blk):
        return a[blk * GLA_SUB:(blk + 1) * GLA_SUB, :]

    def head_expand(a):
        return jnp.concatenate([jnp.where(hm, a, 0.0) for hm in head_mask], axis=0).astype(BF16)

    def chunk(ci, carry):
        c = (n_chunks - 1 - ci) if rev else ci
        r0 = pl.multiple_of(c * GLA_CHUNK, GLA_CHUNK)
        q = gl_ref[pl.ds(r0, GLA_CHUNK), 0:GLA_W]
        k = gl_ref[pl.ds(r0, GLA_CHUNK), GLA_W:2 * GLA_W]
        v = gl_ref[pl.ds(r0, GLA_CHUNK), 2 * GLA_W:3 * GLA_W]
        g = g_ref[pl.ds(r0, GLA_CHUNK), :]

        g_hi = g.astype(BF16)
        rem = g - g_hi.astype(F32)
        g_mid = rem.astype(BF16)
        g_lo = (rem - g_mid.astype(F32)).astype(BF16)
        cum = cum_ref[...]
        cs = _dot(cum, g_hi) + _dot(cum, g_mid) + _dot(cum, g_lo)
        beta = cs[0:GLA_CHUNK]
        gam = cs[GLA_CHUNK:2 * GLA_CHUNK]
        bch = cs[2 * GLA_CHUNK:3 * GLA_CHUNK]
        dch = cs[3 * GLA_CHUNK:4 * GLA_CHUNK]

        last = (lambda blk: blk * GLA_SUB) if rev else (lambda blk: blk * GLA_SUB + GLA_SUB - 1)
        g_tot = [beta[last(blk):last(blk) + 1, :] for blk in range(n_sub)]
        end_row = last(order[-1])
        b_end = bch[end_row:end_row + 1, :]

        q_loc = q * jnp.exp(beta)
        k_loc = k * jnp.exp(gam)

        st = st_ref[...]
        o_inter = _dot_nt((q * jnp.exp(bch)).astype(BF16), st.astype(BF16))
        upd = _dot_tn(v.astype(BF16), (k * jnp.exp(dch)).astype(BF16))
        st_ref[...] = st * jnp.exp(b_end) + upd * bd_ref[...]

        out_blocks = [None] * n_sub
        for p, tb in enumerate(order):
            bt, qb, kb, vb = rows(beta, tb), rows(q, tb), rows(k, tb), rows(v, tb)
            slabs = []
            for s in range(GLA_SUB):
                e = jnp.exp(jnp.minimum(bt - bt[s:s + 1, :], 0.0))
                keep = (sub_row <= s) if rev else (sub_row >= s)
                slabs.append(jnp.where(keep, qb * e * kb[s:s + 1, :], 0.0))
            pr = _dot(jnp.concatenate(slabs, axis=0).astype(BF16), ones_ref[...])
            acc = rows(o_inter, tb)
            for s in range(GLA_SUB):
                acc = acc + pr[s * GLA_SUB:(s + 1) * GLA_SUB, :] * vb[s:s + 1, :]
            if p > 0:
                ks, vs = [], []
                for pp in range(p):
                    sb = order[pp]
                    kk = rows(k_loc, sb)
                    mids = [order[x] for x in range(pp + 1, p)]
                    if mids:
                        tot = g_tot[mids[0]]
                        for mb in mids[1:]:
                            tot = tot + g_tot[mb]
                        kk = kk * jnp.exp(tot)
                    ks.append(kk)
                    vs.append(rows(v, sb))
                a = _dot_nt(rows(q_loc, tb).astype(BF16), head_expand(jnp.concatenate(ks, axis=0)))
                acc = acc + _dot(a.astype(BF16), head_expand(jnp.concatenate(vs, axis=0)))
            out_blocks[tb] = acc
        o_ref[pl.ds(r0, GLA_CHUNK), :] = jnp.concatenate(out_blocks, axis=0)
        return carry

    lax.fori_loop(0, n_chunks, chunk, 0)

    @pl.when(i == pl.num_programs(1) - 1)
    def _():
        sfin_ref[...] = st_ref[...]


def _gla(gl, gates, s0, rev, tile):
    B, S, _ = gl.shape
    nt = S // tile
    n_chunks = tile // GLA_CHUNK
    tmap = (lambda i: nt - 1 - i) if rev else (lambda i: i)
    d = 1 if rev else 0
    cum = jnp.asarray(_gla_consts(rev), BF16)
    ones_bd = jnp.asarray(_head_blocks(), BF16)
    bd = jnp.asarray(_head_blocks(), F32)
    return pl.pallas_call(
        functools.partial(_gla_kernel, rev=rev, n_chunks=n_chunks),
        grid=(B, nt),
        in_specs=[
            _const_spec((4 * GLA_CHUNK, GLA_CHUNK)),
            _const_spec((GLA_W, GLA_W)),
            _const_spec((GLA_W, GLA_W)),
            pl.BlockSpec((None, tile, 3 * GLA_W), lambda b, i: (b, tmap(i), 0)),
            pl.BlockSpec((None, tile, GLA_W), lambda b, i: (b, tmap(i), d)),
            pl.BlockSpec((None, GLA_W, GLA_W), lambda b, i: (b, 0, 0)),
        ],
        out_specs=[
            pl.BlockSpec((None, tile, GLA_W), lambda b, i: (b, tmap(i), 0)),
            pl.BlockSpec((None, GLA_W, GLA_W), lambda b, i: (b, 0, 0)),
        ],
        out_shape=[jax.ShapeDtypeStruct((B, S, GLA_W), F32),
                   jax.ShapeDtypeStruct((B, GLA_W, GLA_W), F32)],
        scratch_shapes=[pltpu.VMEM((GLA_W, GLA_W), F32)],
        compiler_params=_params(2),
        name="gla_bwd" if rev else "gla_fwd",
    )(cum, ones_bd, bd, gl, gates, s0)


def _post_kernel(x_ref, mod_ref, g2_ref, cu_ref, cup_ref, cun_ref, cb_ref, cw_ref, at_ref,
                 of_ref, ob_ref, gg_ref, ng_ref, ones_ref, wo_ref, wa_ref, wb_ref, wd_ref,
                 *rest, final):
    if final:
        fg_ref, o_ref, acc_ref = rest
    else:
        o_ref, acc_ref = rest
    i = pl.program_id(1)
    n = pl.num_programs(1)
    tm = x_ref.shape[0]

    u = cu_ref[...]
    row = lax.broadcasted_iota(jnp.int32, (tm, 1), 0)
    prev_row = jnp.where(i > 0, cup_ref[7:8, :], 0.0)
    next_row = jnp.where(i < n - 1, cun_ref[0:1, :], 0.0)
    u_prev = jnp.where(row == 0, prev_row, pltpu.roll(u, 1, axis=0))
    u_next = jnp.where(row == tm - 1, next_row, pltpu.roll(u, tm - 1, axis=0))
    conv = cb_ref[...] * (cw_ref[0:1, :] * u_prev + cw_ref[1:2, :] * u + cw_ref[2:3, :] * u_next)

    o = of_ref[...] + ob_ref[...]
    sq = o * o
    sq_hi = sq.astype(BF16)
    sq_lo = (sq - sq_hi.astype(F32)).astype(BF16)
    ms = (_dot(sq_hi, ones_ref[...]) + _dot(sq_lo, ones_ref[...])) * (1.0 / HEAD_DIM)
    gla = o * lax.rsqrt(ms + EPS) * ng_ref[...] * _silu(gg_ref[...])

    mix = (_dot(conv.astype(BF16), wo_ref[0:CONV_CH, :])
           + _dot(at_ref[...], wo_ref[CONV_CH:CONV_CH + ATT_W, :])
           + _dot(gla.astype(BF16), wo_ref[CONV_CH + ATT_W:, :]))
    x1 = x_ref[...] + mod_ref[2:3, :] * mix

    ms2 = jnp.mean(x1 * x1, axis=-1, keepdims=True)
    y2 = x1 * lax.rsqrt(ms2 + EPS) * g2_ref[...]
    h2 = (y2 * (1.0 + mod_ref[4:5, :]) + mod_ref[3:4, :]).astype(BF16)

    acc_ref[...] = jnp.zeros_like(acc_ref)

    def ff(jc, carry):
        a = _dot(h2, wa_ref[jc])
        b = _dot(h2, wb_ref[jc])
        acc_ref[...] += _dot((_silu(a) * b).astype(BF16), wd_ref[jc])
        return carry

    lax.fori_loop(0, N_FF_CHUNKS, ff, 0)
    x2 = x1 + mod_ref[5:6, :] * acc_ref[...]
    if final:
        msf = jnp.mean(x2 * x2, axis=-1, keepdims=True)
        x2 = x2 * lax.rsqrt(msf + EPS) * fg_ref[...]
    o_ref[...] = x2


def _post(xs, mods_l, mod_row, g2, cu, cb, conv_w, attn, o_f, o_b, gl, norm_g, w_out, wa, wb, wd,
          final_g, tm):
    B, S, D = xs.shape
    nt = S // tm
    hb = tm // 8
    row = lambda w: pl.BlockSpec((None, tm, w), lambda b, i: (b, i, 0))
    if mod_row is None:
        mod_spec = pl.BlockSpec((None, 8, D), lambda b, i: (b, 0, 0))
    else:
        mod_spec = pl.BlockSpec((None, 8, D), lambda b, i: (mod_row, 0, 0))
    final = final_g is not None
    in_specs = [
        row(D), mod_spec, _const_spec((1, D)),
        row(CONV_CH),
        pl.BlockSpec((None, 8, CONV_CH), lambda b, i: (b, jnp.maximum(i * hb - 1, 0), 0)),
        pl.BlockSpec((None, 8, CONV_CH), lambda b, i: (b, jnp.minimum((i + 1) * hb, S // 8 - 1), 0)),
        row(CONV_CH), _const_spec((8, CONV_CH)), row(ATT_W), row(GLA_W), row(GLA_W),
        pl.BlockSpec((None, tm, GLA_W), lambda b, i: (b, i, 3)),
        _const_spec((1, GLA_W)), _const_spec((GLA_W, GLA_W)), _const_spec((D, D)),
        _const_spec((N_FF_CHUNKS, D, FF_CHUNK)), _const_spec((N_FF_CHUNKS, D, FF_CHUNK)),
        _const_spec((N_FF_CHUNKS, FF_CHUNK, D)),
    ]
    args = [xs, mods_l, g2, cu, cu, cu, cb, conv_w, attn, o_f, o_b, gl, norm_g,
            jnp.asarray(_head_blocks(), BF16), w_out, wa, wb, wd]
    if final:
        in_specs.append(_const_spec((1, D)))
        args.append(final_g)
    return pl.pallas_call(
        functools.partial(_post_kernel, final=final),
        grid=(B, nt),
        in_specs=in_specs,
        out_specs=row(D),
        out_shape=jax.ShapeDtypeStruct((B, S, D), F32),
        scratch_shapes=[pltpu.VMEM((tm, D), F32)],
        compiler_params=_params(2),
        name="post_final" if final else "post",
    )(*args)


def _rope_tables(seq):
    t = jnp.arange(seq)
    pos = jnp.stack([(t // GRID_W).astype(F32), (t % GRID_W).astype(F32)], axis=1)
    n_freq = HEAD_DIM // 4
    inv_freq = ROPE_BASE ** (-jnp.arange(n_freq, dtype=F32) / n_freq)
    ang = pos[:, :, None] * inv_freq
    cos, sin = jnp.cos(ang), jnp.sin(ang)
    zero = jnp.zeros_like(sin)
    cos_t = jnp.stack([cos, cos], axis=2).reshape(seq, HEAD_DIM)
    sa_t = jnp.stack([-sin, zero], axis=2).reshape(seq, HEAD_DIM)
    sb_t = jnp.stack([zero, sin], axis=2).reshape(seq, HEAD_DIM)
    rep = LANES // HEAD_DIM
    return tuple(jnp.tile(a, (1, rep)) for a in (cos_t, sa_t, sb_t))


def _tile_rows(S, want):
    t = min(want, S)
    while S % t:
        t //= 2
    return t


def kernel(x, c, ctx, c_ctx, w_mod, b_mod, norm1_g, norm2_g, w_in, conv_w, attn_sink,
           gla_gate_w, gla_gate_b, gla_norm_g, w_out, w_up, w_down, final_norm_g):
    B, S, D = x.shape
    L = w_mod.shape[0]
    Lc = ctx.shape[1]
    assert D == D_MODEL and S % WINDOW == 0 and Lc % GLA_CHUNK == 0 and B + 1 <= 8

    cvec = jnp.zeros((8, D), F32).at[:B].set(c).at[B].set(c_ctx)
    mods = _modulation(cvec, w_mod, b_mod).reshape(L, 8, N_MOD, D)
    mods = jnp.pad(mods, ((0, 0), (0, 0), (0, 8 - N_MOD), (0, 0)))

    order = np.array(Q_HEAD_ORDER)
    wq = w_in[:, :, 768:1280].reshape(L, D, ATT_HEADS, HEAD_DIM)[:, :, order].reshape(L, D, ATT_W)
    w_in_p = jnp.concatenate(
        [w_in[:, :, :768], wq, w_in[:, :, 1280:2592],
         jnp.zeros((L, D, LANES - 2 * GLA_RANK), F32)], axis=2).astype(BF16)
    wo_att = w_out[:, 256:768].reshape(L, ATT_HEADS, HEAD_DIM, D)[:, order].reshape(L, ATT_W, D)
    w_out_p = jnp.concatenate([w_out[:, :256], wo_att, w_out[:, 768:]], axis=1).astype(BF16)
    wg = jnp.zeros((L, LANES, 2 * GLA_W), F32)
    wg = wg.at[:, :GLA_RANK, :GLA_W].set(gla_gate_w[:, 0])
    wg = wg.at[:, GLA_RANK:2 * GLA_RANK, GLA_W:].set(gla_gate_w[:, 1]).astype(BF16)
    bg = gla_gate_b.reshape(L, 1, 2 * GLA_W)
    cw = jnp.pad(conv_w, ((0, 0), (0, 8 - CONV_K), (0, 0)))
    ng = jnp.tile(gla_norm_g, (1, GLA_HEADS)).reshape(L, 1, GLA_W)
    wa = w_up[:, :, :D_FF].reshape(L, D, N_FF_CHUNKS, FF_CHUNK).transpose(0, 2, 1, 3).astype(BF16)
    wb = w_up[:, :, D_FF:].reshape(L, D, N_FF_CHUNKS, FF_CHUNK).transpose(0, 2, 1, 3).astype(BF16)
    wd = w_down.reshape(L, N_FF_CHUNKS, FF_CHUNK, D).astype(BF16)
    g1 = norm1_g.reshape(L, 1, D)
    g2 = norm2_g.reshape(L, 1, D)

    rope_tabs = _rope_tables(S)
    tm = _tile_rows(S, 512)
    gla_tile = _tile_rows(S, 512)
    zero_state = jnp.zeros((B, GLA_W, GLA_W), F32)

    for l in range(L):
        last = l == L - 1
        cu, cb, q, k, v, gl, gates = _proj(x, mods[l], None, g1[l], w_in_p[l], wg[l], bg[l],
                                           rope_tabs, tm)
        ccu, ccb, cq, ck, cv, cgl, cgates = _proj(ctx, mods[l], B, g1[l], w_in_p[l], wg[l], bg[l],
                                                  None, Lc)
        attn = _attention(q, k, v, ck, cv, attn_sink[l], local=True)
        oc_f, sc_f = _gla(cgl, cgates, zero_state, False, Lc)
        oc_b, sc_b = _gla(cgl, cgates, zero_state, True, Lc)
        o_f, _ = _gla(gl, gates, sc_f, False, gla_tile)
        o_b, _ = _gla(gl, gates, sc_b, True, gla_tile)
        fin = final_norm_g.reshape(1, D) if last else None
        x = _post(x, mods[l], None, g2[l], cu, cb, cw[l], attn, o_f, o_b, gl, ng[l],
                  w_out_p[l], wa[l], wb[l], wd[l], fin, tm)
        if not last:
            attn_c = _attention(cq, None, None, ck, cv, attn_sink[l], local=False)
            ctx = _post(ctx, mods[l], B, g2[l], ccu, ccb, cw[l], attn_c, oc_f, oc_b, cgl, ng[l],
                        w_out_p[l], wa[l], wb[l], wd[l], None, Lc)
    return x
```

```python
import functools

import numpy as np
import jax
import jax.numpy as jnp
from jax import lax
from jax.experimental import pallas as pl
from jax.experimental.pallas import tpu as pltpu

F32 = jnp.float32
BF16 = jnp.bfloat16

D_MODEL = 1024
HEAD_DIM = 64
CONV_CH = 256
CONV_K = 3
ATT_HEADS = 8
ATT_KV_HEADS = 2
ATT_W = ATT_HEADS * HEAD_DIM
KV_W = ATT_KV_HEADS * HEAD_DIM
WINDOW = 128
GLA_HEADS = 4
GLA_W = GLA_HEADS * HEAD_DIM
GLA_RANK = 16
GLA_NORMALIZER = 16.0
D_FF = 2816
N_MOD = 6
GRID_W = 64
ROPE_BASE = 10000.0
EPS = 1e-6
LOG2_E = 1.4426950408889634

LANES = 128
C_CONV = 0
C_QK = 768
C_V = 1408
C_GLA = 1536
C_LR = 2560
N_IN_PAD = C_LR + LANES
Q_HEAD_ORDER = (0, 4, 1, 5, 2, 6, 3, 7)

ATT_ROW_GROUP = 16
GLA_CHUNK = 64
GLA_SUB = 16
FF_CHUNK = 256
N_FF_CHUNKS = D_FF // FF_CHUNK

VMEM_LIMIT = 56 * 1024 * 1024


def _const_spec(shape):
    nd = len(shape)
    return pl.BlockSpec(shape, lambda *_: (0,) * nd, pipeline_mode=pl.Buffered(1))


def _params(n_grid):
    return pltpu.CompilerParams(
        dimension_semantics=("arbitrary",) * n_grid, vmem_limit_bytes=VMEM_LIMIT)


def _silu(a):
    return a * (1.0 / (1.0 + jnp.exp(-a)))


def _dot(a, b):
    return jnp.dot(a, b, preferred_element_type=F32)


def _dot_nt(a, b):
    return lax.dot_general(a, b, (((1,), (1,)), ((), ())), preferred_element_type=F32)


def _dot_tn(a, b):
    return lax.dot_general(a, b, (((0,), (0,)), ((), ())), preferred_element_type=F32)


def _mod_kernel(c_ref, w_ref, b_ref, o_ref):
    a = _silu(c_ref[...]).astype(BF16)
    o_ref[...] = _dot(a, w_ref[...].astype(BF16)) + b_ref[...]


def _modulation(cvec, w_mod, b_mod):
    L, D, W = w_mod.shape
    tn = 2048
    return pl.pallas_call(
        _mod_kernel,
        grid=(L, W // tn),
        in_specs=[
            pl.BlockSpec((8, D), lambda l, j: (0, 0)),
            pl.BlockSpec((None, D, tn), lambda l, j: (l, 0, j)),
            pl.BlockSpec((None, 1, tn), lambda l, j: (l, 0, j)),
        ],
        out_specs=pl.BlockSpec((None, 8, tn), lambda l, j: (l, 0, j)),
        out_shape=jax.ShapeDtypeStruct((L, 8, W), F32),
        compiler_params=_params(2),
        name="modulation",
    )(cvec, w_mod, b_mod.reshape(L, 1, W))


def _proj_kernel(x_ref, mod_ref, g1_ref, w_ref, wg_ref, bg_ref, *rest, rope):
    if rope:
        cos_ref, sa_ref, sb_ref = rest[:3]
        rest = rest[3:]
    cu_ref, cb_ref, q_ref, k_ref, v_ref, gl_ref, gate_ref = rest

    x = x_ref[...]
    ms = jnp.mean(x * x, axis=-1, keepdims=True)
    y = x * lax.rsqrt(ms + EPS) * g1_ref[...]
    h = (y * (1.0 + mod_ref[1:2, :]) + mod_ref[0:1, :]).astype(BF16)

    def proj(lo, hi):
        return _dot(h, w_ref[:, lo:hi])

    pc = proj(C_CONV, C_QK)
    cu_ref[...] = pc[:, 2 * CONV_CH:3 * CONV_CH] * pc[:, 0:CONV_CH]
    cb_ref[...] = pc[:, CONV_CH:2 * CONV_CH]

    pqk = proj(C_QK, C_V)
    scale = HEAD_DIM ** -0.5
    for m in range((ATT_W + KV_W) // LANES):
        col = pqk[:, m * LANES:(m + 1) * LANES]
        if rope:
            col = (col * cos_ref[...]
                   + pltpu.roll(col, LANES - HEAD_DIM // 4, axis=1) * sa_ref[...]
                   + pltpu.roll(col, HEAD_DIM // 4, axis=1) * sb_ref[...])
        if m < ATT_W // LANES:
            q_ref[:, m * LANES:(m + 1) * LANES] = (col * scale).astype(BF16)
        else:
            k_ref[...] = col.astype(BF16)

    v_ref[...] = proj(C_V, C_GLA).astype(BF16)

    pg = proj(C_GLA, C_LR)
    gl_ref[:, 0:GLA_W] = pg[:, 0:GLA_W] * (HEAD_DIM ** -0.5)
    gl_ref[:, GLA_W:4 * GLA_W] = pg[:, GLA_W:4 * GLA_W]

    lr = proj(C_LR, N_IN_PAD).astype(BF16)
    z = _dot(lr, wg_ref[...]) + bg_ref[...]
    log_sig = jnp.minimum(z, 0.0) - jnp.log1p(jnp.exp(-jnp.abs(z)))
    gate_ref[...] = log_sig * (1.0 / GLA_NORMALIZER)


def _proj(xs, mods_l, mod_row, g1, w_in, wg, bg, rope_tabs, tm):
    B, S, D = xs.shape
    rope = rope_tabs is not None
    nt = S // tm
    row = lambda w: pl.BlockSpec((None, tm, w), lambda b, i: (b, i, 0))
    if mod_row is None:
        mod_spec = pl.BlockSpec((None, 8, D), lambda b, i: (b, 0, 0))
    else:
        mod_spec = pl.BlockSpec((None, 8, D), lambda b, i: (mod_row, 0, 0))
    in_specs = [row(D), mod_spec, _const_spec((1, D)), _const_spec((D, N_IN_PAD)),
                _const_spec((LANES, 2 * GLA_W)), _const_spec((1, 2 * GLA_W))]
    args = [xs, mods_l, g1, w_in, wg, bg]
    if rope:
        in_specs += [pl.BlockSpec((tm, LANES), lambda b, i: (i, 0))] * 3
        args += list(rope_tabs)
    widths = (CONV_CH, CONV_CH, ATT_W, KV_W, KV_W, 4 * GLA_W, 2 * GLA_W)
    dtypes = (F32, F32, BF16, BF16, BF16, F32, F32)
    return pl.pallas_call(
        functools.partial(_proj_kernel, rope=rope),
        grid=(B, nt),
        in_specs=in_specs,
        out_specs=[row(w) for w in widths],
        out_shape=[jax.ShapeDtypeStruct((B, S, w), dt) for w, dt in zip(widths, dtypes)],
        compiler_params=_params(2),
        name="proj_rope" if rope else "proj_ctx",
    )(*args)


def _attn_kernel(sink_ref, q_ref, kc_ref, vc_ref, *rest, local, n_sub):
    tb = q_ref.shape[0] // n_sub
    n_pairs = ATT_W // LANES
    n_ctx = kc_ref.shape[0]
    rg = ATT_ROW_GROUP
    if local:
        kp_ref, km_ref, kn_ref, vp_ref, vm_ref, vn_ref, o_ref, s_scr, p_scr, bias_scr = rest
        i = pl.program_id(1)
        n = pl.num_programs(1)
        k_loc = jnp.concatenate([kp_ref[...], km_ref[...], kn_ref[...]], axis=0)
        v_loc = jnp.concatenate([vp_ref[...], vm_ref[...], vn_ref[...]], axis=0)
        r = lax.broadcasted_iota(jnp.int32, (tb, tb), 0)
        j = lax.broadcasted_iota(jnp.int32, (tb, tb), 1)
        bias_prev = jnp.where(j >= r, 0.0, -jnp.inf)
        bias_next = jnp.where(j <= r, 0.0, -jnp.inf)
        bias_scr[0] = bias_prev + jnp.where(i > 0, 0.0, -jnp.inf)
        bias_scr[1] = bias_prev
        bias_scr[2] = bias_next
        bias_scr[3] = bias_next + jnp.where(i < n - 1, 0.0, -jnp.inf)
        nk = n_ctx + 3 * tb
    else:
        o_ref, s_scr, p_scr = rest
        nk = n_ctx
    lane = lax.broadcasted_iota(jnp.int32, (1, LANES), 1)
    lo = lane < HEAD_DIM
    hi = jnp.logical_not(lo)
    first_row = lax.broadcasted_iota(jnp.int32, (LANES, LANES), 0) == 0
    v_tail = jnp.concatenate([jnp.zeros((LANES, LANES), BF16),
                              jnp.where(first_row, 1.0, 0.0).astype(BF16)], axis=1)
    ones_blk = jnp.ones((nk, LANES), BF16)

    for jb in range(n_sub):
        q = q_ref[jb * tb:(jb + 1) * tb, :]
        zero = jnp.zeros((tb, LANES), BF16)
        lhs = jnp.concatenate(
            [jnp.where(lo if g == 0 else hi, q[:, m * LANES:(m + 1) * LANES], zero)
             for m in range(n_pairs) for g in range(ATT_KV_HEADS)], axis=0)
        if local:
            kk = jnp.concatenate([kc_ref[...], k_loc[jb * tb:(jb + 3) * tb]], axis=0)
            vv = jnp.concatenate([vc_ref[...], v_loc[jb * tb:(jb + 3) * tb]], axis=0)
        else:
            kk, vv = kc_ref[...], vc_ref[...]
        v_aug = jnp.concatenate([jnp.concatenate([vv, ones_blk], axis=1), v_tail], axis=0)
        s_scr[jb] = _dot_nt(lhs, kk)

        for hs in range(ATT_HEADS):
            m, g = hs // ATT_KV_HEADS, hs % ATT_KV_HEADS
            sink = sink_ref[m + (ATT_HEADS // ATT_KV_HEADS) * g]
            sink_tile = jnp.where(lane == 0, sink, -jnp.inf) + jnp.zeros((rg, LANES), F32)
            for gi in range(tb // rg):
                rq = gi * rg
                r0 = hs * tb + rq
                s = s_scr[jb, r0:r0 + rg, :]
                if local:
                    bp = bias_scr[0 if jb == 0 else 1, rq:rq + rg, :]
                    bn = bias_scr[3 if jb == n_sub - 1 else 2, rq:rq + rg, :]
                    cols = [s[:, :n_ctx], s[:, n_ctx:n_ctx + tb] + bp,
                            s[:, n_ctx + tb:n_ctx + 2 * tb], s[:, n_ctx + 2 * tb:] + bn, sink_tile]
                else:
                    cols = [s, sink_tile]
                s_all = jnp.concatenate(cols, axis=1)
                mx = jnp.max(s_all, axis=-1, keepdims=True)
                p_scr[jb, r0:r0 + rg, :] = jnp.exp(s_all - mx).astype(BF16)

        o_aug = _dot(p_scr[jb], v_aug)
        o = o_aug[:, :LANES] * (1.0 / o_aug[:, LANES:])
        for m in range(n_pairs):
            o_ref[jb * tb:(jb + 1) * tb, m * LANES:(m + 1) * LANES] = jnp.where(
                lo, o[(2 * m) * tb:(2 * m + 1) * tb], o[(2 * m + 1) * tb:(2 * m + 2) * tb]).astype(BF16)


def _attention(q, k, v, k_ctx, v_ctx, sink, local, n_sub=2):
    B, S, _ = q.shape
    Lc = k_ctx.shape[1]
    if not local:
        n_sub = 1
    tq = WINDOW * n_sub if local else S
    nq = S // tq
    in_specs = [
        pl.BlockSpec(memory_space=pltpu.SMEM),
        pl.BlockSpec((None, tq, ATT_W), lambda b, i: (b, i, 0)),
        pl.BlockSpec((None, Lc, KV_W), lambda b, i: (b, 0, 0)),
        pl.BlockSpec((None, Lc, KV_W), lambda b, i: (b, 0, 0)),
    ]
    args = [sink, q, k_ctx, v_ctx]
    if local:
        nw = S // WINDOW
        nb = [pl.BlockSpec((None, WINDOW, KV_W), lambda b, i: (b, jnp.maximum(i * n_sub - 1, 0), 0)),
              pl.BlockSpec((None, tq, KV_W), lambda b, i: (b, i, 0)),
              pl.BlockSpec((None, WINDOW, KV_W),
                           lambda b, i: (b, jnp.minimum((i + 1) * n_sub, nw - 1), 0))]
        in_specs += nb + nb
        args += [k, k, k, v, v, v]
    tb = tq // n_sub
    nk = Lc + (3 * tb if local else 0)
    scratch = [pltpu.VMEM((n_sub, ATT_HEADS * tb, nk), F32),
               pltpu.VMEM((n_sub, ATT_HEADS * tb, nk + LANES), BF16)]
    if local:
        scratch.append(pltpu.VMEM((4, tb, tb), F32))
    return pl.pallas_call(
        functools.partial(_attn_kernel, local=local, n_sub=n_sub),
        grid=(B, nq),
        in_specs=in_specs,
        out_specs=pl.BlockSpec((None, tq, ATT_W), lambda b, i: (b, i, 0)),
        out_shape=jax.ShapeDtypeStruct((B, S, ATT_W), BF16),
        scratch_shapes=scratch,
        compiler_params=_params(2),
        name="attn_window" if local else "attn_ctx",
    )(*args)


def _gla_consts(rev):
    t = np.arange(GLA_CHUNK)[:, None]
    s = np.arange(GLA_CHUNK)[None, :]
    same = (t // GLA_SUB) == (s // GLA_SUB)
    before = (s >= t) if rev else (s <= t)
    mats = [same & before, same & ~before, before, ~before]
    return np.concatenate(mats, axis=0).astype(np.float32)


def _head_blocks():
    h = np.arange(GLA_W) // HEAD_DIM
    return (h[:, None] == h[None, :]).astype(np.float32)


def _gla_chunk(q, k, v, g, st, cum, ones, bd, head_mask, sub_row, rev):
    n_sub = GLA_CHUNK // GLA_SUB
    order = list(range(n_sub))[::-1] if rev else list(range(n_sub))

    def rows(a, blk):
        return a[blk * GLA_SUB:(blk + 1) * GLA_SUB, :]

    g2 = g * LOG2_E
    g_hi = g2.astype(BF16)
    rem = g2 - g_hi.astype(F32)
    g_mid = rem.astype(BF16)
    g_lo = (rem - g_mid.astype(F32)).astype(BF16)
    cs = _dot(cum, g_hi) + _dot(cum, g_mid) + _dot(cum, g_lo)
    beta = cs[0:GLA_CHUNK]
    gam = cs[GLA_CHUNK:2 * GLA_CHUNK]
    bch = cs[2 * GLA_CHUNK:3 * GLA_CHUNK]
    dch = cs[3 * GLA_CHUNK:4 * GLA_CHUNK]

    last = (lambda blk: blk * GLA_SUB) if rev else (lambda blk: blk * GLA_SUB + GLA_SUB - 1)
    g_tot = [beta[last(blk):last(blk) + 1, :] for blk in range(n_sub)]
    end_row = last(order[-1])
    b_end = bch[end_row:end_row + 1, :]

    q_loc = q * jnp.exp2(beta)
    k_loc = k * jnp.exp2(gam)

    o_inter = _dot_nt((q * jnp.exp2(bch)).astype(BF16), st.astype(BF16))
    upd = _dot_tn(v.astype(BF16), (k * jnp.exp2(dch)).astype(BF16))
    st_new = st * jnp.exp2(b_end) + upd * bd

    out_blocks = [None] * n_sub
    for p, tb in enumerate(order):
        bt, qb, kb, vb = rows(beta, tb), rows(q, tb), rows(k, tb), rows(v, tb)
        slabs = []
        for s in range(GLA_SUB):
            keep = (sub_row <= s) if rev else (sub_row >= s)
            e = jnp.exp2(jnp.where(keep, bt - bt[s:s + 1, :], -jnp.inf))
            slabs.append(qb * e * kb[s:s + 1, :])
        pr = _dot(jnp.concatenate(slabs, axis=0).astype(BF16), ones)
        acc = rows(o_inter, tb)
        for s in range(GLA_SUB):
            acc = acc + pr[s * GLA_SUB:(s + 1) * GLA_SUB, :] * vb[s:s + 1, :]
        if p > 0:
            ks, vs = [], []
            for pp in range(p):
                sb = order[pp]
                kk = rows(k_loc, sb)
                mids = [order[x] for x in range(pp + 1, p)]
                if mids:
                    tot = g_tot[mids[0]]
                    for mb in mids[1:]:
                        tot = tot + g_tot[mb]
                    kk = kk * jnp.exp2(tot)
                ks.append(kk)
                vs.append(rows(v, sb))
            qt = rows(q_loc, tb)
            q_heads = jnp.concatenate([jnp.where(hm, qt, 0.0) for hm in head_mask], axis=0)
            a = _dot_nt(q_heads.astype(BF16), jnp.concatenate(ks, axis=0).astype(BF16))
            r = _dot(a.astype(BF16), jnp.concatenate(vs, axis=0).astype(BF16))
            for hh, hm in enumerate(head_mask):
                acc = acc + jnp.where(hm, r[hh * GLA_SUB:(hh + 1) * GLA_SUB, :], 0.0)
        out_blocks[tb] = acc
    return jnp.concatenate(out_blocks, axis=0), st_new


def _gla_kernel(cumf_ref, cumb_ref, ones_ref, bd_ref, glf_ref, glb_ref, gf_ref, gb_ref,
                s0f_ref, s0b_ref, of_ref, ob_ref, sff_ref, sfb_ref, stf_ref, stb_ref, *, n_chunks):
    i = pl.program_id(1)
    lane_head = lax.broadcasted_iota(jnp.int32, (1, GLA_W), 1) // HEAD_DIM
    head_mask = [lane_head == hh for hh in range(GLA_HEADS)]
    sub_row = lax.broadcasted_iota(jnp.int32, (GLA_SUB, 1), 0)

    @pl.when(i == 0)
    def _():
        stf_ref[...] = s0f_ref[...]
        stb_ref[...] = s0b_ref[...]

    def one(gl_ref, g_ref, o_ref, st_ref, cum_ref, c, rev):
        r0 = pl.multiple_of(c * GLA_CHUNK, GLA_CHUNK)
        o, st_new = _gla_chunk(
            gl_ref[pl.ds(r0, GLA_CHUNK), 0:GLA_W], gl_ref[pl.ds(r0, GLA_CHUNK), GLA_W:2 * GLA_W],
            gl_ref[pl.ds(r0, GLA_CHUNK), 2 * GLA_W:3 * GLA_W], g_ref[pl.ds(r0, GLA_CHUNK), :],
            st_ref[...], cum_ref[...], ones_ref[...], bd_ref[...], head_mask, sub_row, rev)
        o_ref[pl.ds(r0, GLA_CHUNK), :] = o
        st_ref[...] = st_new

    def chunk(ci, carry):
        one(glf_ref, gf_ref, of_ref, stf_ref, cumf_ref, ci, False)
        one(glb_ref, gb_ref, ob_ref, stb_ref, cumb_ref, n_chunks - 1 - ci, True)
        return carry

    lax.fori_loop(0, n_chunks, chunk, 0)

    @pl.when(i == pl.num_programs(1) - 1)
    def _():
        sff_ref[...] = stf_ref[...]
        sfb_ref[...] = stb_ref[...]


def _gla(gl, gates, s0f, s0b, tile):
    B, S, _ = gl.shape
    nt = S // tile
    n_chunks = tile // GLA_CHUNK
    ones_bd = jnp.asarray(_head_blocks(), BF16)
    bd = jnp.asarray(_head_blocks(), F32)
    state_spec = pl.BlockSpec((None, GLA_W, GLA_W), lambda b, i: (b, 0, 0))
    return pl.pallas_call(
        functools.partial(_gla_kernel, n_chunks=n_chunks),
        grid=(B, nt),
        in_specs=[
            _const_spec((4 * GLA_CHUNK, GLA_CHUNK)),
            _const_spec((4 * GLA_CHUNK, GLA_CHUNK)),
            _const_spec((GLA_W, GLA_W)),
            _const_spec((GLA_W, GLA_W)),
            pl.BlockSpec((None, tile, 3 * GLA_W), lambda b, i: (b, i, 0)),
            pl.BlockSpec((None, tile, 3 * GLA_W), lambda b, i: (b, nt - 1 - i, 0)),
            pl.BlockSpec((None, tile, GLA_W), lambda b, i: (b, i, 0)),
            pl.BlockSpec((None, tile, GLA_W), lambda b, i: (b, nt - 1 - i, 1)),
            state_spec, state_spec,
        ],
        out_specs=[
            pl.BlockSpec((None, tile, GLA_W), lambda b, i: (b, i, 0)),
            pl.BlockSpec((None, tile, GLA_W), lambda b, i: (b, nt - 1 - i, 0)),
            state_spec, state_spec,
        ],
        out_shape=[jax.ShapeDtypeStruct((B, S, GLA_W), F32),
                   jax.ShapeDtypeStruct((B, S, GLA_W), F32),
                   jax.ShapeDtypeStruct((B, GLA_W, GLA_W), F32),
                   jax.ShapeDtypeStruct((B, GLA_W, GLA_W), F32)],
        scratch_shapes=[pltpu.VMEM((GLA_W, GLA_W), F32), pltpu.VMEM((GLA_W, GLA_W), F32)],
        compiler_params=_params(2),
        name="gla",
    )(jnp.asarray(_gla_consts(False), BF16), jnp.asarray(_gla_consts(True), BF16),
      ones_bd, bd, gl, gl, gates, gates, s0f, s0b)


def _post_kernel(x_ref, mod_ref, g2_ref, cu_ref, cup_ref, cun_ref, cb_ref, cw_ref, at_ref,
                 of_ref, ob_ref, gg_ref, ng_ref, ones_ref, wo_ref, wa_ref, wb_ref, wd_ref,
                 *rest, final):
    if final:
        fg_ref, o_ref, acc_ref = rest
    else:
        o_ref, acc_ref = rest
    i = pl.program_id(1)
    n = pl.num_programs(1)
    tm = x_ref.shape[0]

    u = cu_ref[...]
    row = lax.broadcasted_iota(jnp.int32, (tm, 1), 0)
    prev_row = jnp.where(i > 0, cup_ref[7:8, :], 0.0)
    next_row = jnp.where(i < n - 1, cun_ref[0:1, :], 0.0)
    u_prev = jnp.where(row == 0, prev_row, pltpu.roll(u, 1, axis=0))
    u_next = jnp.where(row == tm - 1, next_row, pltpu.roll(u, tm - 1, axis=0))
    conv = cb_ref[...] * (cw_ref[0:1, :] * u_prev + cw_ref[1:2, :] * u + cw_ref[2:3, :] * u_next)

    o = of_ref[...] + ob_ref[...]
    sq = o * o
    sq_hi = sq.astype(BF16)
    sq_lo = (sq - sq_hi.astype(F32)).astype(BF16)
    ms = (_dot(sq_hi, ones_ref[...]) + _dot(sq_lo, ones_ref[...])) * (1.0 / HEAD_DIM)
    gla = o * lax.rsqrt(ms + EPS) * ng_ref[...] * _silu(gg_ref[...])

    mix = (_dot(conv.astype(BF16), wo_ref[0:CONV_CH, :])
           + _dot(at_ref[...], wo_ref[CONV_CH:CONV_CH + ATT_W, :])
           + _dot(gla.astype(BF16), wo_ref[CONV_CH + ATT_W:, :]))
    x1 = x_ref[...] + mod_ref[2:3, :] * mix

    ms2 = jnp.mean(x1 * x1, axis=-1, keepdims=True)
    y2 = x1 * lax.rsqrt(ms2 + EPS) * g2_ref[...]
    h2 = (y2 * (1.0 + mod_ref[4:5, :]) + mod_ref[3:4, :]).astype(BF16)

    acc_ref[...] = jnp.zeros_like(acc_ref)

    def ff(jc, carry):
        a = _dot(h2, wa_ref[jc])
        b = _dot(h2, wb_ref[jc])
        acc_ref[...] += _dot((_silu(a) * b).astype(BF16), wd_ref[jc])
        return carry

    lax.fori_loop(0, N_FF_CHUNKS, ff, 0)
    x2 = x1 + mod_ref[5:6, :] * acc_ref[...]
    if final:
        msf = jnp.mean(x2 * x2, axis=-1, keepdims=True)
        x2 = x2 * lax.rsqrt(msf + EPS) * fg_ref[...]
    o_ref[...] = x2


def _post(xs, mods_l, mod_row, g2, cu, cb, conv_w, attn, o_f, o_b, gl, norm_g, w_out, wa, wb, wd,
          final_g, tm):
    B, S, D = xs.shape
    nt = S // tm
    hb = tm // 8
    row = lambda w: pl.BlockSpec((None, tm, w), lambda b, i: (b, i, 0))
    if mod_row is None:
        mod_spec = pl.BlockSpec((None, 8, D), lambda b, i: (b, 0, 0))
    else:
        mod_spec = pl.BlockSpec((None, 8, D), lambda b, i: (mod_row, 0, 0))
    final = final_g is not None
    in_specs = [
        row(D), mod_spec, _const_spec((1, D)),
        row(CONV_CH),
        pl.BlockSpec((None, 8, CONV_CH), lambda b, i: (b, jnp.maximum(i * hb - 1, 0), 0)),
        pl.BlockSpec((None, 8, CONV_CH), lambda b, i: (b, jnp.minimum((i + 1) * hb, S // 8 - 1), 0)),
        row(CONV_CH), _const_spec((8, CONV_CH)), row(ATT_W), row(GLA_W), row(GLA_W),
        pl.BlockSpec((None, tm, GLA_W), lambda b, i: (b, i, 3)),
        _const_spec((1, GLA_W)), _const_spec((GLA_W, GLA_W)), _const_spec((D, D)),
        _const_spec((N_FF_CHUNKS, D, FF_CHUNK)), _const_spec((N_FF_CHUNKS, D, FF_CHUNK)),
        _const_spec((N_FF_CHUNKS, FF_CHUNK, D)),
    ]
    args = [xs, mods_l, g2, cu, cu, cu, cb, conv_w, attn, o_f, o_b, gl, norm_g,
            jnp.asarray(_head_blocks(), BF16), w_out, wa, wb, wd]
    if final:
        in_specs.append(_const_spec((1, D)))
        args.append(final_g)
    return pl.pallas_call(
        functools.partial(_post_kernel, final=final),
        grid=(B, nt),
        in_specs=in_specs,
        out_specs=row(D),
        out_shape=jax.ShapeDtypeStruct((B, S, D), F32),
        scratch_shapes=[pltpu.VMEM((tm, D), F32)],
        compiler_params=_params(2),
        name="post_final" if final else "post",
    )(*args)


def _rope_tables(seq):
    t = jnp.arange(seq)
    pos = jnp.stack([(t // GRID_W).astype(F32), (t % GRID_W).astype(F32)], axis=1)
    n_freq = HEAD_DIM // 4
    inv_freq = ROPE_BASE ** (-jnp.arange(n_freq, dtype=F32) / n_freq)
    ang = pos[:, :, None] * inv_freq
    cos, sin = jnp.cos(ang), jnp.sin(ang)
    zero = jnp.zeros_like(sin)
    cos_t = jnp.stack([cos, cos], axis=2).reshape(seq, HEAD_DIM)
    sa_t = jnp.stack([-sin, zero], axis=2).reshape(seq, HEAD_DIM)
    sb_t = jnp.stack([zero, sin], axis=2).reshape(seq, HEAD_DIM)
    rep = LANES // HEAD_DIM
    return tuple(jnp.tile(a, (1, rep)) for a in (cos_t, sa_t, sb_t))


def _tile_rows(S, want):
    t = min(want, S)
    while S % t:
        t //= 2
    return t


def kernel(x, c, ctx, c_ctx, w_mod, b_mod, norm1_g, norm2_g, w_in, conv_w, attn_sink,
           gla_gate_w, gla_gate_b, gla_norm_g, w_out, w_up, w_down, final_norm_g):
    B, S, D = x.shape
    L = w_mod.shape[0]
    Lc = ctx.shape[1]
    assert D == D_MODEL and S % WINDOW == 0 and Lc % GLA_CHUNK == 0 and B + 1 <= 8

    cvec = jnp.zeros((8, D), F32).at[:B].set(c).at[B].set(c_ctx)
    mods = _modulation(cvec, w_mod, b_mod).reshape(L, 8, N_MOD, D)
    mods = jnp.pad(mods, ((0, 0), (0, 0), (0, 8 - N_MOD), (0, 0)))

    order = np.array(Q_HEAD_ORDER)
    wq = w_in[:, :, 768:1280].reshape(L, D, ATT_HEADS, HEAD_DIM)[:, :, order].reshape(L, D, ATT_W)
    w_in_p = jnp.concatenate(
        [w_in[:, :, :768], wq, w_in[:, :, 1280:2592],
         jnp.zeros((L, D, LANES - 2 * GLA_RANK), F32)], axis=2).astype(BF16)
    wo_att = w_out[:, 256:768].reshape(L, ATT_HEADS, HEAD_DIM, D)[:, order].reshape(L, ATT_W, D)
    w_out_p = jnp.concatenate([w_out[:, :256], wo_att, w_out[:, 768:]], axis=1).astype(BF16)
    wg = jnp.zeros((L, LANES, 2 * GLA_W), F32)
    wg = wg.at[:, :GLA_RANK, :GLA_W].set(gla_gate_w[:, 0])
    wg = wg.at[:, GLA_RANK:2 * GLA_RANK, GLA_W:].set(gla_gate_w[:, 1]).astype(BF16)
    bg = gla_gate_b.reshape(L, 1, 2 * GLA_W)
    cw = jnp.pad(conv_w, ((0, 0), (0, 8 - CONV_K), (0, 0)))
    ng = jnp.tile(gla_norm_g, (1, GLA_HEADS)).reshape(L, 1, GLA_W)
    wa = w_up[:, :, :D_FF].reshape(L, D, N_FF_CHUNKS, FF_CHUNK).transpose(0, 2, 1, 3).astype(BF16)
    wb = w_up[:, :, D_FF:].reshape(L, D, N_FF_CHUNKS, FF_CHUNK).transpose(0, 2, 1, 3).astype(BF16)
    wd = w_down.reshape(L, N_FF_CHUNKS, FF_CHUNK, D).astype(BF16)
    g1 = norm1_g.reshape(L, 1, D)
    g2 = norm2_g.reshape(L, 1, D)

    rope_tabs = _rope_tables(S)
    tm = _tile_rows(S, 512)
    gla_tile = _tile_rows(S, 512)
    zero_state = jnp.zeros((B, GLA_W, GLA_W), F32)

    for l in range(L):
        last = l == L - 1
        cu, cb, q, k, v, gl, gates = _proj(x, mods[l], None, g1[l], w_in_p[l], wg[l], bg[l],
                                           rope_tabs, tm)
        ccu, ccb, cq, ck, cv, cgl, cgates = _proj(ctx, mods[l], B, g1[l], w_in_p[l], wg[l], bg[l],
                                                  None, Lc)
        attn = _attention(q, k, v, ck, cv, attn_sink[l], local=True)
        oc_f, oc_b, sc_f, sc_b = _gla(cgl, cgates, zero_state, zero_state, Lc)
        o_f, o_b, _, _ = _gla(gl, gates, sc_f, sc_b, gla_tile)
        fin = final_norm_g.reshape(1, D) if last else None
        x = _post(x, mods[l], None, g2[l], cu, cb, cw[l], attn, o_f, o_b, gl, ng[l],
                  w_out_p[l], wa[l], wb[l], wd[l], fin, tm)
        if not last:
            attn_c = _attention(cq, None, None, ck, cv, attn_sink[l], local=False)
            ctx = _post(ctx, mods[l], B, g2[l], ccu, ccb, cw[l], attn_c, oc_f, oc_b, cgl, ng[l],
                        w_out_p[l], wa[l], wb[l], wd[l], None, Lc)
    return x
```

```python
import functools

import numpy as np
import jax
import jax.numpy as jnp
from jax import lax
from jax.experimental import pallas as pl
from jax.experimental.pallas import tpu as pltpu

F32 = jnp.float32
BF16 = jnp.bfloat16

D_MODEL = 1024
HEAD_DIM = 64
CONV_CH = 256
CONV_K = 3
ATT_HEADS = 8
ATT_KV_HEADS = 2
ATT_W = ATT_HEADS * HEAD_DIM
KV_W = ATT_KV_HEADS * HEAD_DIM
WINDOW = 128
GLA_HEADS = 4
GLA_W = GLA_HEADS * HEAD_DIM
GLA_RANK = 16
GLA_NORMALIZER = 16.0
D_FF = 2816
N_MOD = 6
GRID_W = 64
ROPE_BASE = 10000.0
EPS = 1e-6
LOG2_E = 1.4426950408889634

LANES = 128
C_CONV = 0
C_QK = 768
C_V = 1408
C_GLA = 1536
C_LR = 2560
N_IN_PAD = C_LR + LANES
Q_HEAD_ORDER = (0, 4, 1, 5, 2, 6, 3, 7)

ATT_ROW_GROUP = 16
GLA_CHUNK = 64
GLA_SUB = 16
FF_CHUNK = 256
N_FF_CHUNKS = D_FF // FF_CHUNK

VMEM_LIMIT = 56 * 1024 * 1024


def _const_spec(shape):
    nd = len(shape)
    return pl.BlockSpec(shape, lambda *_: (0,) * nd, pipeline_mode=pl.Buffered(1))


def _params(n_grid):
    return pltpu.CompilerParams(
        dimension_semantics=("arbitrary",) * n_grid, vmem_limit_bytes=VMEM_LIMIT)


def _silu(a):
    return a * (1.0 / (1.0 + jnp.exp(-a)))


def _dot(a, b):
    return jnp.dot(a, b, preferred_element_type=F32)


def _dot_nt(a, b):
    return lax.dot_general(a, b, (((1,), (1,)), ((), ())), preferred_element_type=F32)


def _dot_tn(a, b):
    return lax.dot_general(a, b, (((0,), (0,)), ((), ())), preferred_element_type=F32)


def _mod_kernel(c_ref, w_ref, b_ref, o_ref):
    a = _silu(c_ref[...]).astype(BF16)
    o_ref[...] = _dot(a, w_ref[...].astype(BF16)) + b_ref[...]


def _modulation(cvec, w_mod, b_mod):
    L, D, W = w_mod.shape
    tn = 2048
    return pl.pallas_call(
        _mod_kernel,
        grid=(L, W // tn),
        in_specs=[
            pl.BlockSpec((8, D), lambda l, j: (0, 0)),
            pl.BlockSpec((None, D, tn), lambda l, j: (l, 0, j)),
            pl.BlockSpec((None, 1, tn), lambda l, j: (l, 0, j)),
        ],
        out_specs=pl.BlockSpec((None, 8, tn), lambda l, j: (l, 0, j)),
        out_shape=jax.ShapeDtypeStruct((L, 8, W), F32),
        compiler_params=_params(2),
        name="modulation",
    )(cvec, w_mod, b_mod.reshape(L, 1, W))


def _proj_kernel(x_ref, mod_ref, g1_ref, w_ref, wg_ref, bg_ref, *rest, rope):
    if rope:
        cos_ref, sa_ref, sb_ref = rest[:3]
        rest = rest[3:]
    cu_ref, cb_ref, q_ref, k_ref, v_ref, gl_ref, gate_ref = rest

    x = x_ref[...]
    ms = jnp.mean(x * x, axis=-1, keepdims=True)
    y = x * lax.rsqrt(ms + EPS) * g1_ref[...]
    h = (y * (1.0 + mod_ref[1:2, :]) + mod_ref[0:1, :]).astype(BF16)

    def proj(lo, hi):
        return _dot(h, w_ref[:, lo:hi])

    pc = proj(C_CONV, C_QK)
    cu_ref[...] = pc[:, 2 * CONV_CH:3 * CONV_CH] * pc[:, 0:CONV_CH]
    cb_ref[...] = pc[:, CONV_CH:2 * CONV_CH]

    pqk = proj(C_QK, C_V)
    scale = HEAD_DIM ** -0.5
    for m in range((ATT_W + KV_W) // LANES):
        col = pqk[:, m * LANES:(m + 1) * LANES]
        if rope:
            col = (col * cos_ref[...]
                   + pltpu.roll(col, LANES - HEAD_DIM // 4, axis=1) * sa_ref[...]
                   + pltpu.roll(col, HEAD_DIM // 4, axis=1) * sb_ref[...])
        if m < ATT_W // LANES:
            q_ref[:, m * LANES:(m + 1) * LANES] = (col * scale).astype(BF16)
        else:
            k_ref[...] = col.astype(BF16)

    v_ref[...] = proj(C_V, C_GLA).astype(BF16)

    pg = proj(C_GLA, C_LR)
    gl_ref[:, 0:GLA_W] = pg[:, 0:GLA_W] * (HEAD_DIM ** -0.5)
    gl_ref[:, GLA_W:4 * GLA_W] = pg[:, GLA_W:4 * GLA_W]

    lr = proj(C_LR, N_IN_PAD).astype(BF16)
    z = _dot(lr, wg_ref[...]) + bg_ref[...]
    log_sig = jnp.minimum(z, 0.0) - jnp.log1p(jnp.exp(-jnp.abs(z)))
    gate_ref[...] = log_sig * (1.0 / GLA_NORMALIZER)


def _proj(xs, mods_l, mod_row, g1, w_in, wg, bg, rope_tabs, tm):
    B, S, D = xs.shape
    rope = rope_tabs is not None
    nt = S // tm
    row = lambda w: pl.BlockSpec((None, tm, w), lambda b, i: (b, i, 0))
    if mod_row is None:
        mod_spec = pl.BlockSpec((None, 8, D), lambda b, i: (b, 0, 0))
    else:
        mod_spec = pl.BlockSpec((None, 8, D), lambda b, i: (mod_row, 0, 0))
    in_specs = [row(D), mod_spec, _const_spec((1, D)), _const_spec((D, N_IN_PAD)),
                _const_spec((LANES, 2 * GLA_W)), _const_spec((1, 2 * GLA_W))]
    args = [xs, mods_l, g1, w_in, wg, bg]
    if rope:
        in_specs += [pl.BlockSpec((tm, LANES), lambda b, i: (i, 0))] * 3
        args += list(rope_tabs)
    widths = (CONV_CH, CONV_CH, ATT_W, KV_W, KV_W, 4 * GLA_W, 2 * GLA_W)
    dtypes = (F32, F32, BF16, BF16, BF16, F32, F32)
    return pl.pallas_call(
        functools.partial(_proj_kernel, rope=rope),
        grid=(B, nt),
        in_specs=in_specs,
        out_specs=[row(w) for w in widths],
        out_shape=[jax.ShapeDtypeStruct((B, S, w), dt) for w, dt in zip(widths, dtypes)],
        compiler_params=_params(2),
        name="proj_rope" if rope else "proj_ctx",
    )(*args)


def _attn_kernel(sink_ref, q_ref, kc_ref, vc_ref, *rest, local, n_sub):
    tb = q_ref.shape[0] // n_sub
    n_pairs = ATT_W // LANES
    n_ctx = kc_ref.shape[0]
    rg = ATT_ROW_GROUP
    if local:
        kp_ref, km_ref, kn_ref, vp_ref, vm_ref, vn_ref, o_ref, s_scr, p_scr, bias_scr = rest
        i = pl.program_id(1)
        n = pl.num_programs(1)
        k_loc = jnp.concatenate([kp_ref[...], km_ref[...], kn_ref[...]], axis=0)
        v_loc = jnp.concatenate([vp_ref[...], vm_ref[...], vn_ref[...]], axis=0)
        r = lax.broadcasted_iota(jnp.int32, (tb, tb), 0)
        j = lax.broadcasted_iota(jnp.int32, (tb, tb), 1)
        bias_prev = jnp.where(j >= r, 0.0, -jnp.inf)
        bias_next = jnp.where(j <= r, 0.0, -jnp.inf)
        bias_scr[0] = bias_prev + jnp.where(i > 0, 0.0, -jnp.inf)
        bias_scr[1] = bias_prev
        bias_scr[2] = bias_next
        bias_scr[3] = bias_next + jnp.where(i < n - 1, 0.0, -jnp.inf)
        nk = n_ctx + 3 * tb
    else:
        o_ref, s_scr, p_scr = rest
        nk = n_ctx
    lane = lax.broadcasted_iota(jnp.int32, (1, LANES), 1)
    lo = lane < HEAD_DIM
    hi = jnp.logical_not(lo)
    first_row = lax.broadcasted_iota(jnp.int32, (LANES, LANES), 0) == 0
    v_tail = jnp.concatenate([jnp.zeros((LANES, LANES), BF16),
                              jnp.where(first_row, 1.0, 0.0).astype(BF16)], axis=1)
    ones_blk = jnp.ones((nk, LANES), BF16)

    for jb in range(n_sub):
        q = q_ref[jb * tb:(jb + 1) * tb, :]
        zero = jnp.zeros((tb, LANES), BF16)
        lhs = jnp.concatenate(
            [jnp.where(lo if g == 0 else hi, q[:, m * LANES:(m + 1) * LANES], zero)
             for m in range(n_pairs) for g in range(ATT_KV_HEADS)], axis=0)
        if local:
            kk = jnp.concatenate([kc_ref[...], k_loc[jb * tb:(jb + 3) * tb]], axis=0)
            vv = jnp.concatenate([vc_ref[...], v_loc[jb * tb:(jb + 3) * tb]], axis=0)
        else:
            kk, vv = kc_ref[...], vc_ref[...]
        v_aug = jnp.concatenate([jnp.concatenate([vv, ones_blk], axis=1), v_tail], axis=0)
        s_scr[jb] = _dot_nt(lhs, kk)

        for hs in range(ATT_HEADS):
            m, g = hs // ATT_KV_HEADS, hs % ATT_KV_HEADS
            sink = sink_ref[m + (ATT_HEADS // ATT_KV_HEADS) * g]
            sink_tile = jnp.where(lane == 0, sink, -jnp.inf) + jnp.zeros((rg, LANES), F32)
            for gi in range(tb // rg):
                rq = gi * rg
                r0 = hs * tb + rq
                s = s_scr[jb, r0:r0 + rg, :]
                if local:
                    bp = bias_scr[0 if jb == 0 else 1, rq:rq + rg, :]
                    bn = bias_scr[3 if jb == n_sub - 1 else 2, rq:rq + rg, :]
                    cols = [s[:, :n_ctx], s[:, n_ctx:n_ctx + tb] + bp,
                            s[:, n_ctx + tb:n_ctx + 2 * tb], s[:, n_ctx + 2 * tb:] + bn, sink_tile]
                else:
                    cols = [s, sink_tile]
                s_all = jnp.concatenate(cols, axis=1)
                mx = jnp.max(s_all, axis=-1, keepdims=True)
                p_scr[jb, r0:r0 + rg, :] = jnp.exp(s_all - mx).astype(BF16)

        o_aug = _dot(p_scr[jb], v_aug)
        o = o_aug[:, :LANES] * (1.0 / o_aug[:, LANES:])
        for m in range(n_pairs):
            o_ref[jb * tb:(jb + 1) * tb, m * LANES:(m + 1) * LANES] = jnp.where(
                lo, o[(2 * m) * tb:(2 * m + 1) * tb], o[(2 * m + 1) * tb:(2 * m + 2) * tb]).astype(BF16)


def _attention(q, k, v, k_ctx, v_ctx, sink, local, n_sub=2):
    B, S, _ = q.shape
    Lc = k_ctx.shape[1]
    if not local:
        n_sub = 1
    tq = WINDOW * n_sub if local else S
    nq = S // tq
    in_specs = [
        pl.BlockSpec(memory_space=pltpu.SMEM),
        pl.BlockSpec((None, tq, ATT_W), lambda b, i: (b, i, 0)),
        pl.BlockSpec((None, Lc, KV_W), lambda b, i: (b, 0, 0)),
        pl.BlockSpec((None, Lc, KV_W), lambda b, i: (b, 0, 0)),
    ]
    args = [sink, q, k_ctx, v_ctx]
    if local:
        nw = S // WINDOW
        nb = [pl.BlockSpec((None, WINDOW, KV_W), lambda b, i: (b, jnp.maximum(i * n_sub - 1, 0), 0)),
              pl.BlockSpec((None, tq, KV_W), lambda b, i: (b, i, 0)),
              pl.BlockSpec((None, WINDOW, KV_W),
                           lambda b, i: (b, jnp.minimum((i + 1) * n_sub, nw - 1), 0))]
        in_specs += nb + nb
        args += [k, k, k, v, v, v]
    tb = tq // n_sub
    nk = Lc + (3 * tb if local else 0)
    scratch = [pltpu.VMEM((n_sub, ATT_HEADS * tb, nk), F32),
               pltpu.VMEM((n_sub, ATT_HEADS * tb, nk + LANES), BF16)]
    if local:
        scratch.append(pltpu.VMEM((4, tb, tb), F32))
    return pl.pallas_call(
        functools.partial(_attn_kernel, local=local, n_sub=n_sub),
        grid=(B, nq),
        in_specs=in_specs,
        out_specs=pl.BlockSpec((None, tq, ATT_W), lambda b, i: (b, i, 0)),
        out_shape=jax.ShapeDtypeStruct((B, S, ATT_W), BF16),
        scratch_shapes=scratch,
        compiler_params=_params(2),
        name="attn_window" if local else "attn_ctx",
    )(*args)


def _gla_consts(rev):
    t = np.arange(GLA_CHUNK)[:, None]
    s = np.arange(GLA_CHUNK)[None, :]
    same = (t // GLA_SUB) == (s // GLA_SUB)
    before = (s >= t) if rev else (s <= t)
    return np.concatenate([same & before, before], axis=0).astype(np.float32)


def _head_blocks():
    h = np.arange(GLA_W) // HEAD_DIM
    return (h[:, None] == h[None, :]).astype(np.float32)


def _gla_last_row(blk, rev):
    return blk * GLA_SUB if rev else blk * GLA_SUB + GLA_SUB - 1


def _gla_prepare(q, k, g, cum, rev):
    n_sub = GLA_CHUNK // GLA_SUB
    g2 = g * LOG2_E
    g_hi = g2.astype(BF16)
    rem = g2 - g_hi.astype(F32)
    g_mid = rem.astype(BF16)
    g_lo = (rem - g_mid.astype(F32)).astype(BF16)
    cs = _dot(cum, g_hi) + _dot(cum, g_mid) + _dot(cum, g_lo)
    beta, bch = cs[0:GLA_CHUNK], cs[GLA_CHUNK:2 * GLA_CHUNK]
    gam = jnp.concatenate(
        [beta[_gla_last_row(b, rev):_gla_last_row(b, rev) + 1, :] - beta[b * GLA_SUB:(b + 1) * GLA_SUB, :]
         for b in range(n_sub)], axis=0)
    end_row = _gla_last_row(0 if rev else n_sub - 1, rev)
    dch = bch[end_row:end_row + 1, :] - bch
    f32_rows = jnp.concatenate([beta, bch, q * jnp.exp2(beta), k * jnp.exp2(gam)], axis=0)
    b16_rows = jnp.concatenate([(q * jnp.exp2(bch)).astype(BF16), (k * jnp.exp2(dch)).astype(BF16)],
                               axis=0)
    return f32_rows, b16_rows


def _gla_chunk(gl_ref, pf_ref, pb_ref, st_ref, o_ref, r0, ones, bd, head_mask, rev):
    n_sub = GLA_CHUNK // GLA_SUB
    order = list(range(n_sub))[::-1] if rev else list(range(n_sub))

    def rows(a, blk):
        return a[blk * GLA_SUB:(blk + 1) * GLA_SUB, :]

    half = GLA_SUB // 2
    q = gl_ref[pl.ds(r0, GLA_CHUNK), 0:GLA_W]
    k = gl_ref[pl.ds(r0, GLA_CHUNK), GLA_W:2 * GLA_W]
    v = gl_ref[pl.ds(r0, GLA_CHUNK), 2 * GLA_W:3 * GLA_W]
    beta = pf_ref[0:GLA_CHUNK]
    q_loc = pf_ref[2 * GLA_CHUNK:3 * GLA_CHUNK]
    k_loc = pf_ref[3 * GLA_CHUNK:4 * GLA_CHUNK]
    g_tot = [beta[_gla_last_row(b, rev):_gla_last_row(b, rev) + 1, :] for b in range(n_sub)]
    end_row = GLA_CHUNK + _gla_last_row(order[-1], rev)
    b_end = pf_ref[end_row:end_row + 1, :]
    half_row = lax.broadcasted_iota(jnp.int32, (half, 1), 0)

    st = st_ref[...]
    o_inter = _dot_nt(pb_ref[0:GLA_CHUNK], st.astype(BF16))
    upd = _dot_tn(v.astype(BF16), pb_ref[GLA_CHUNK:2 * GLA_CHUNK])
    scores, values = {}, {}
    for p in range(1, n_sub):
        ks, vs = [], []
        for pp in range(p):
            sb = order[pp]
            kk = rows(k_loc, sb)
            mids = [order[x] for x in range(pp + 1, p)]
            if mids:
                tot = g_tot[mids[0]]
                for mb in mids[1:]:
                    tot = tot + g_tot[mb]
                kk = kk * jnp.exp2(tot)
            ks.append(kk)
            vs.append(rows(v, sb))
        qt = rows(q_loc, order[p])
        q_heads = jnp.concatenate([jnp.where(hm, qt, 0.0) for hm in head_mask], axis=0)
        scores[p] = _dot_nt(q_heads.astype(BF16), jnp.concatenate(ks, axis=0).astype(BF16))
        values[p] = jnp.concatenate(vs, axis=0).astype(BF16)
    yield

    pairs = []
    for s in range(GLA_SUB):
        hs = s // half
        for hb in (range(0, hs + 1) if rev else range(hs, GLA_SUB // half)):
            pairs.append((s, hb))
    sums = {}
    for p, tb in enumerate(order):
        bt, qb, kb = rows(beta, tb), rows(q, tb), rows(k, tb)
        slabs = []
        for s, hb in pairs:
            d = bt[hb * half:(hb + 1) * half, :] - bt[s:s + 1, :]
            if hb == s // half:
                keep = (half_row <= s % half) if rev else (half_row >= s % half)
                d = jnp.where(keep, d, -jnp.inf)
            slabs.append(qb[hb * half:(hb + 1) * half, :] * jnp.exp2(d) * kb[s:s + 1, :])
        sums[p] = _dot(jnp.concatenate(slabs, axis=0).astype(BF16), ones)
        yield

    mixed = {p: _dot(scores[p].astype(BF16), values[p]) for p in range(1, n_sub)}
    yield

    for p, tb in enumerate(order):
        vb = rows(v, tb)
        acc = rows(o_inter, tb)
        if p > 0:
            for hh, hm in enumerate(head_mask):
                acc = acc + jnp.where(hm, mixed[p][hh * GLA_SUB:(hh + 1) * GLA_SUB, :], 0.0)
        halves = [acc[hb * half:(hb + 1) * half, :] for hb in range(GLA_SUB // half)]
        for idx, (s, hb) in enumerate(pairs):
            halves[hb] = halves[hb] + sums[p][idx * half:(idx + 1) * half, :] * vb[s:s + 1, :]
        o_ref[pl.ds(r0 + tb * GLA_SUB, GLA_SUB), :] = jnp.concatenate(halves, axis=0)
        yield

    st_ref[...] = st * jnp.exp2(b_end) + upd * bd


def _interleave(gens):
    active = list(gens)
    while active:
        for gen in list(active):
            try:
                next(gen)
            except StopIteration:
                active.remove(gen)


def _gla_kernel(cumf_ref, cumb_ref, ones_ref, bd_ref, glf_ref, glb_ref, gf_ref, gb_ref,
                s0f_ref, s0b_ref, of_ref, ob_ref, sff_ref, sfb_ref, stf_ref, stb_ref, pf_scr, pb_scr,
                *, n_chunks):
    i = pl.program_id(1)
    lane_head = lax.broadcasted_iota(jnp.int32, (1, GLA_W), 1) // HEAD_DIM
    head_mask = [lane_head == hh for hh in range(GLA_HEADS)]

    @pl.when(i == 0)
    def _():
        stf_ref[...] = s0f_ref[...]
        stb_ref[...] = s0b_ref[...]

    for c in range(n_chunks):
        r0 = c * GLA_CHUNK
        for d, (gl_ref, g_ref, cum_ref) in enumerate(
                ((glf_ref, gf_ref, cumf_ref), (glb_ref, gb_ref, cumb_ref))):
            pf_scr[d, c], pb_scr[d, c] = _gla_prepare(
                gl_ref[r0:r0 + GLA_CHUNK, 0:GLA_W], gl_ref[r0:r0 + GLA_CHUNK, GLA_W:2 * GLA_W],
                g_ref[r0:r0 + GLA_CHUNK, :], cum_ref[...], d == 1)

    def chunk(ci, carry):
        cb = n_chunks - 1 - ci
        _interleave([
            _gla_chunk(glf_ref, pf_scr.at[0, ci], pb_scr.at[0, ci], stf_ref, of_ref,
                       pl.multiple_of(ci * GLA_CHUNK, GLA_CHUNK), ones_ref[...], bd_ref[...],
                       head_mask, False),
            _gla_chunk(glb_ref, pf_scr.at[1, cb], pb_scr.at[1, cb], stb_ref, ob_ref,
                       pl.multiple_of(cb * GLA_CHUNK, GLA_CHUNK), ones_ref[...], bd_ref[...],
                       head_mask, True),
        ])
        return carry

    lax.fori_loop(0, n_chunks, chunk, 0)

    @pl.when(i == pl.num_programs(1) - 1)
    def _():
        sff_ref[...] = stf_ref[...]
        sfb_ref[...] = stb_ref[...]


def _gla(gl, gates, s0f, s0b, tile):
    B, S, _ = gl.shape
    nt = S // tile
    n_chunks = tile // GLA_CHUNK
    ones_bd = jnp.asarray(_head_blocks(), BF16)
    bd = jnp.asarray(_head_blocks(), F32)
    state_spec = pl.BlockSpec((None, GLA_W, GLA_W), lambda b, i: (b, 0, 0))
    return pl.pallas_call(
        functools.partial(_gla_kernel, n_chunks=n_chunks),
        grid=(B, nt),
        in_specs=[
            _const_spec((2 * GLA_CHUNK, GLA_CHUNK)),
            _const_spec((2 * GLA_CHUNK, GLA_CHUNK)),
            _const_spec((GLA_W, GLA_W)),
            _const_spec((GLA_W, GLA_W)),
            pl.BlockSpec((None, tile, 3 * GLA_W), lambda b, i: (b, i, 0)),
            pl.BlockSpec((None, tile, 3 * GLA_W), lambda b, i: (b, nt - 1 - i, 0)),
            pl.BlockSpec((None, tile, GLA_W), lambda b, i: (b, i, 0)),
            pl.BlockSpec((None, tile, GLA_W), lambda b, i: (b, nt - 1 - i, 1)),
            state_spec, state_spec,
        ],
        out_specs=[
            pl.BlockSpec((None, tile, GLA_W), lambda b, i: (b, i, 0)),
            pl.BlockSpec((None, tile, GLA_W), lambda b, i: (b, nt - 1 - i, 0)),
            state_spec, state_spec,
        ],
        out_shape=[jax.ShapeDtypeStruct((B, S, GLA_W), F32),
                   jax.ShapeDtypeStruct((B, S, GLA_W), F32),
                   jax.ShapeDtypeStruct((B, GLA_W, GLA_W), F32),
                   jax.ShapeDtypeStruct((B, GLA_W, GLA_W), F32)],
        scratch_shapes=[pltpu.VMEM((GLA_W, GLA_W), F32), pltpu.VMEM((GLA_W, GLA_W), F32),
                        pltpu.VMEM((2, n_chunks, 4 * GLA_CHUNK, GLA_W), F32),
                        pltpu.VMEM((2, n_chunks, 2 * GLA_CHUNK, GLA_W), BF16)],
        compiler_params=_params(2),
        name="gla",
    )(jnp.asarray(_gla_consts(False), BF16), jnp.asarray(_gla_consts(True), BF16),
      ones_bd, bd, gl, gl, gates, gates, s0f, s0b)


def _post_kernel(x_ref, mod_ref, g2_ref, cu_ref, cup_ref, cun_ref, cb_ref, cw_ref, at_ref,
                 of_ref, ob_ref, gg_ref, ng_ref, ones_ref, wo_ref, wa_ref, wb_ref, wd_ref,
                 *rest, final):
    if final:
        fg_ref, o_ref, acc_ref = rest
    else:
        o_ref, acc_ref = rest
    i = pl.program_id(1)
    n = pl.num_programs(1)
    tm = x_ref.shape[0]

    u = cu_ref[...]
    row = lax.broadcasted_iota(jnp.int32, (tm, 1), 0)
    prev_row = jnp.where(i > 0, cup_ref[7:8, :], 0.0)
    next_row = jnp.where(i < n - 1, cun_ref[0:1, :], 0.0)
    u_prev = jnp.where(row == 0, prev_row, pltpu.roll(u, 1, axis=0))
    u_next = jnp.where(row == tm - 1, next_row, pltpu.roll(u, tm - 1, axis=0))
    conv = cb_ref[...] * (cw_ref[0:1, :] * u_prev + cw_ref[1:2, :] * u + cw_ref[2:3, :] * u_next)

    o = of_ref[...] + ob_ref[...]
    sq = o * o
    sq_hi = sq.astype(BF16)
    sq_lo = (sq - sq_hi.astype(F32)).astype(BF16)
    ms = (_dot(sq_hi, ones_ref[...]) + _dot(sq_lo, ones_ref[...])) * (1.0 / HEAD_DIM)
    gla = o * lax.rsqrt(ms + EPS) * ng_ref[...] * _silu(gg_ref[...])

    mix = (_dot(conv.astype(BF16), wo_ref[0:CONV_CH, :])
           + _dot(at_ref[...], wo_ref[CONV_CH:CONV_CH + ATT_W, :])
           + _dot(gla.astype(BF16), wo_ref[CONV_CH + ATT_W:, :]))
    x1 = x_ref[...] + mod_ref[2:3, :] * mix

    ms2 = jnp.mean(x1 * x1, axis=-1, keepdims=True)
    y2 = x1 * lax.rsqrt(ms2 + EPS) * g2_ref[...]
    h2 = (y2 * (1.0 + mod_ref[4:5, :]) + mod_ref[3:4, :]).astype(BF16)

    acc_ref[...] = jnp.zeros_like(acc_ref)

    def ff(jc, carry):
        a = _dot(h2, wa_ref[jc])
        b = _dot(h2, wb_ref[jc])
        acc_ref[...] += _dot((_silu(a) * b).astype(BF16), wd_ref[jc])
        return carry

    lax.fori_loop(0, N_FF_CHUNKS, ff, 0)
    x2 = x1 + mod_ref[5:6, :] * acc_ref[...]
    if final:
        msf = jnp.mean(x2 * x2, axis=-1, keepdims=True)
        x2 = x2 * lax.rsqrt(msf + EPS) * fg_ref[...]
    o_ref[...] = x2


def _post(xs, mods_l, mod_row, g2, cu, cb, conv_w, attn, o_f, o_b, gl, norm_g, w_out, wa, wb, wd,
          final_g, tm):
    B, S, D = xs.shape
    nt = S // tm
    hb = tm // 8
    row = lambda w: pl.BlockSpec((None, tm, w), lambda b, i: (b, i, 0))
    if mod_row is None:
        mod_spec = pl.BlockSpec((None, 8, D), lambda b, i: (b, 0, 0))
    else:
        mod_spec = pl.BlockSpec((None, 8, D), lambda b, i: (mod_row, 0, 0))
    final = final_g is not None
    in_specs = [
        row(D), mod_spec, _const_spec((1, D)),
        row(CONV_CH),
        pl.BlockSpec((None, 8, CONV_CH), lambda b, i: (b, jnp.maximum(i * hb - 1, 0), 0)),
        pl.BlockSpec((None, 8, CONV_CH), lambda b, i: (b, jnp.minimum((i + 1) * hb, S // 8 - 1), 0)),
        row(CONV_CH), _const_spec((8, CONV_CH)), row(ATT_W), row(GLA_W), row(GLA_W),
        pl.BlockSpec((None, tm, GLA_W), lambda b, i: (b, i, 3)),
        _const_spec((1, GLA_W)), _const_spec((GLA_W, GLA_W)), _const_spec((D, D)),
        _const_spec((N_FF_CHUNKS, D, FF_CHUNK)), _const_spec((N_FF_CHUNKS, D, FF_CHUNK)),
        _const_spec((N_FF_CHUNKS, FF_CHUNK, D)),
    ]
    args = [xs, mods_l, g2, cu, cu, cu, cb, conv_w, attn, o_f, o_b, gl, norm_g,
            jnp.asarray(_head_blocks(), BF16), w_out, wa, wb, wd]
    if final:
        in_specs.append(_const_spec((1, D)))
        args.append(final_g)
    return pl.pallas_call(
        functools.partial(_post_kernel, final=final),
        grid=(B, nt),
        in_specs=in_specs,
        out_specs=row(D),
        out_shape=jax.ShapeDtypeStruct((B, S, D), F32),
        scratch_shapes=[pltpu.VMEM((tm, D), F32)],
        compiler_params=_params(2),
        name="post_final" if final else "post",
    )(*args)


def _rope_tables(seq):
    t = jnp.arange(seq)
    pos = jnp.stack([(t // GRID_W).astype(F32), (t % GRID_W).astype(F32)], axis=1)
    n_freq = HEAD_DIM // 4
    inv_freq = ROPE_BASE ** (-jnp.arange(n_freq, dtype=F32) / n_freq)
    ang = pos[:, :, None] * inv_freq
    cos, sin = jnp.cos(ang), jnp.sin(ang)
    zero = jnp.zeros_like(sin)
    cos_t = jnp.stack([cos, cos], axis=2).reshape(seq, HEAD_DIM)
    sa_t = jnp.stack([-sin, zero], axis=2).reshape(seq, HEAD_DIM)
    sb_t = jnp.stack([zero, sin], axis=2).reshape(seq, HEAD_DIM)
    rep = LANES // HEAD_DIM
    return tuple(jnp.tile(a, (1, rep)) for a in (cos_t, sa_t, sb_t))


def _tile_rows(S, want):
    t = min(want, S)
    while S % t:
        t //= 2
    return t


def kernel(x, c, ctx, c_ctx, w_mod, b_mod, norm1_g, norm2_g, w_in, conv_w, attn_sink,
           gla_gate_w, gla_gate_b, gla_norm_g, w_out, w_up, w_down, final_norm_g):
    B, S, D = x.shape
    L = w_mod.shape[0]
    Lc = ctx.shape[1]
    assert D == D_MODEL and S % WINDOW == 0 and Lc % GLA_CHUNK == 0 and B + 1 <= 8

    cvec = jnp.zeros((8, D), F32).at[:B].set(c).at[B].set(c_ctx)
    mods = _modulation(cvec, w_mod, b_mod).reshape(L, 8, N_MOD, D)
    mods = jnp.pad(mods, ((0, 0), (0, 0), (0, 8 - N_MOD), (0, 0)))

    order = np.array(Q_HEAD_ORDER)
    wq = w_in[:, :, 768:1280].reshape(L, D, ATT_HEADS, HEAD_DIM)[:, :, order].reshape(L, D, ATT_W)
    w_in_p = jnp.concatenate(
        [w_in[:, :, :768], wq, w_in[:, :, 1280:2592],
         jnp.zeros((L, D, LANES - 2 * GLA_RANK), F32)], axis=2).astype(BF16)
    wo_att = w_out[:, 256:768].reshape(L, ATT_HEADS, HEAD_DIM, D)[:, order].reshape(L, ATT_W, D)
    w_out_p = jnp.concatenate([w_out[:, :256], wo_att, w_out[:, 768:]], axis=1).astype(BF16)
    wg = jnp.zeros((L, LANES, 2 * GLA_W), F32)
    wg = wg.at[:, :GLA_RANK, :GLA_W].set(gla_gate_w[:, 0])
    wg = wg.at[:, GLA_RANK:2 * GLA_RANK, GLA_W:].set(gla_gate_w[:, 1]).astype(BF16)
    bg = gla_gate_b.reshape(L, 1, 2 * GLA_W)
    cw = jnp.pad(conv_w, ((0, 0), (0, 8 - CONV_K), (0, 0)))
    ng = jnp.tile(gla_norm_g, (1, GLA_HEADS)).reshape(L, 1, GLA_W)
    wa = w_up[:, :, :D_FF].reshape(L, D, N_FF_CHUNKS, FF_CHUNK).transpose(0, 2, 1, 3).astype(BF16)
    wb = w_up[:, :, D_FF:].reshape(L, D, N_FF_CHUNKS, FF_CHUNK).transpose(0, 2, 1, 3).astype(BF16)
    wd = w_down.reshape(L, N_FF_CHUNKS, FF_CHUNK, D).astype(BF16)
    g1 = norm1_g.reshape(L, 1, D)
    g2 = norm2_g.reshape(L, 1, D)

    rope_tabs = _rope_tables(S)
    tm = _tile_rows(S, 512)
    gla_tile = _tile_rows(S, 512)
    zero_state = jnp.zeros((B, GLA_W, GLA_W), F32)

    for l in range(L):
        last = l == L - 1
        cu, cb, q, k, v, gl, gates = _proj(x, mods[l], None, g1[l], w_in_p[l], wg[l], bg[l],
                                           rope_tabs, tm)
        ccu, ccb, cq, ck, cv, cgl, cgates = _proj(ctx, mods[l], B, g1[l], w_in_p[l], wg[l], bg[l],
                                                  None, Lc)
        attn = _attention(q, k, v, ck, cv, attn_sink[l], local=True)
        oc_f, oc_b, sc_f, sc_b = _gla(cgl, cgates, zero_state, zero_state, Lc)
        o_f, o_b, _, _ = _gla(gl, gates, sc_f, sc_b, gla_tile)
        fin = final_norm_g.reshape(1, D) if last else None
        x = _post(x, mods[l], None, g2[l], cu, cb, cw[l], attn, o_f, o_b, gl, ng[l],
                  w_out_p[l], wa[l], wb[l], wd[l], fin, tm)
        if not last:
            attn_c = _attention(cq, None, None, ck, cv, attn_sink[l], local=False)
            ctx = _post(ctx, mods[l], B, g2[l], ccu, ccb, cw[l], attn_c, oc_f, oc_b, cgl, ng[l],
                        w_out_p[l], wa[l], wb[l], wd[l], None, Lc)
    return x
```

```python
import functools

import numpy as np
import jax
import jax.numpy as jnp
from jax import lax
from jax.experimental import pallas as pl
from jax.experimental.pallas import tpu as pltpu

F32 = jnp.float32
BF16 = jnp.bfloat16

D_MODEL = 1024
HEAD_DIM = 64
CONV_CH = 256
CONV_K = 3
ATT_HEADS = 8
ATT_KV_HEADS = 2
ATT_W = ATT_HEADS * HEAD_DIM
KV_W = ATT_KV_HEADS * HEAD_DIM
WINDOW = 128
GLA_HEADS = 4
GLA_W = GLA_HEADS * HEAD_DIM
GLA_RANK = 16
GLA_NORMALIZER = 16.0
D_FF = 2816
N_MOD = 6
GRID_W = 64
ROPE_BASE = 10000.0
EPS = 1e-6
LOG2_E = 1.4426950408889634

LANES = 128
C_CONV = 0
C_QK = 768
C_V = 1408
C_GLA = 1536
C_LR = 2560
N_IN_PAD = C_LR + LANES
Q_HEAD_ORDER = (0, 4, 1, 5, 2, 6, 3, 7)

ATT_ROW_GROUP = 16
GLA_CHUNK = 64
GLA_SUB = 16
GLA_FAST_RANGE = 60.0
FF_CHUNK = 256
N_FF_CHUNKS = D_FF // FF_CHUNK

VMEM_LIMIT = 56 * 1024 * 1024


def _const_spec(shape):
    nd = len(shape)
    return pl.BlockSpec(shape, lambda *_: (0,) * nd, pipeline_mode=pl.Buffered(1))


def _params(n_grid):
    return pltpu.CompilerParams(
        dimension_semantics=("arbitrary",) * n_grid, vmem_limit_bytes=VMEM_LIMIT)


def _silu(a):
    return a * (1.0 / (1.0 + jnp.exp(-a)))


def _dot(a, b):
    return jnp.dot(a, b, preferred_element_type=F32)


def _dot_nt(a, b):
    return lax.dot_general(a, b, (((1,), (1,)), ((), ())), preferred_element_type=F32)


def _dot_tn(a, b):
    return lax.dot_general(a, b, (((0,), (0,)), ((), ())), preferred_element_type=F32)


def _mod_kernel(c_ref, w_ref, b_ref, o_ref):
    a = _silu(c_ref[...]).astype(BF16)
    o_ref[...] = _dot(a, w_ref[...].astype(BF16)) + b_ref[...]


def _modulation(cvec, w_mod, b_mod):
    L, D, W = w_mod.shape
    tn = 2048
    return pl.pallas_call(
        _mod_kernel,
        grid=(L, W // tn),
        in_specs=[
            pl.BlockSpec((8, D), lambda l, j: (0, 0)),
            pl.BlockSpec((None, D, tn), lambda l, j: (l, 0, j)),
            pl.BlockSpec((None, 1, tn), lambda l, j: (l, 0, j)),
        ],
        out_specs=pl.BlockSpec((None, 8, tn), lambda l, j: (l, 0, j)),
        out_shape=jax.ShapeDtypeStruct((L, 8, W), F32),
        compiler_params=_params(2),
        name="modulation",
    )(cvec, w_mod, b_mod.reshape(L, 1, W))


def _proj_kernel(x_ref, mod_ref, g1_ref, w_ref, wg_ref, bg_ref, *rest, rope):
    if rope:
        cos_ref, sa_ref, sb_ref = rest[:3]
        rest = rest[3:]
    cu_ref, cb_ref, q_ref, k_ref, v_ref, gl_ref, gate_ref = rest

    x = x_ref[...]
    ms = jnp.mean(x * x, axis=-1, keepdims=True)
    y = x * lax.rsqrt(ms + EPS) * g1_ref[...]
    h = (y * (1.0 + mod_ref[1:2, :]) + mod_ref[0:1, :]).astype(BF16)

    def proj(lo, hi):
        return _dot(h, w_ref[:, lo:hi])

    pc = proj(C_CONV, C_QK)
    cu_ref[...] = pc[:, 2 * CONV_CH:3 * CONV_CH] * pc[:, 0:CONV_CH]
    cb_ref[...] = pc[:, CONV_CH:2 * CONV_CH]

    pqk = proj(C_QK, C_V)
    scale = HEAD_DIM ** -0.5
    for m in range((ATT_W + KV_W) // LANES):
        col = pqk[:, m * LANES:(m + 1) * LANES]
        if rope:
            col = (col * cos_ref[...]
                   + pltpu.roll(col, LANES - HEAD_DIM // 4, axis=1) * sa_ref[...]
                   + pltpu.roll(col, HEAD_DIM // 4, axis=1) * sb_ref[...])
        if m < ATT_W // LANES:
            q_ref[:, m * LANES:(m + 1) * LANES] = (col * scale).astype(BF16)
        else:
            k_ref[...] = col.astype(BF16)

    v_ref[...] = proj(C_V, C_GLA).astype(BF16)

    pg = proj(C_GLA, C_LR)
    gl_ref[:, 0:GLA_W] = pg[:, 0:GLA_W] * (HEAD_DIM ** -0.5)
    gl_ref[:, GLA_W:4 * GLA_W] = pg[:, GLA_W:4 * GLA_W]

    lr = proj(C_LR, N_IN_PAD).astype(BF16)
    z = _dot(lr, wg_ref[...]) + bg_ref[...]
    log_sig = jnp.minimum(z, 0.0) - jnp.log1p(jnp.exp(-jnp.abs(z)))
    gate_ref[...] = log_sig * (1.0 / GLA_NORMALIZER)


def _proj(xs, mods_l, mod_row, g1, w_in, wg, bg, rope_tabs, tm):
    B, S, D = xs.shape
    rope = rope_tabs is not None
    nt = S // tm
    row = lambda w: pl.BlockSpec((None, tm, w), lambda b, i: (b, i, 0))
    if mod_row is None:
        mod_spec = pl.BlockSpec((None, 8, D), lambda b, i: (b, 0, 0))
    else:
        mod_spec = pl.BlockSpec((None, 8, D), lambda b, i: (mod_row, 0, 0))
    in_specs = [row(D), mod_spec, _const_spec((1, D)), _const_spec((D, N_IN_PAD)),
                _const_spec((LANES, 2 * GLA_W)), _const_spec((1, 2 * GLA_W))]
    args = [xs, mods_l, g1, w_in, wg, bg]
    if rope:
        in_specs += [pl.BlockSpec((tm, LANES), lambda b, i: (i, 0))] * 3
        args += list(rope_tabs)
    widths = (CONV_CH, CONV_CH, ATT_W, KV_W, KV_W, 4 * GLA_W, 2 * GLA_W)
    dtypes = (F32, F32, BF16, BF16, BF16, F32, F32)
    return pl.pallas_call(
        functools.partial(_proj_kernel, rope=rope),
        grid=(B, nt),
        in_specs=in_specs,
        out_specs=[row(w) for w in widths],
        out_shape=[jax.ShapeDtypeStruct((B, S, w), dt) for w, dt in zip(widths, dtypes)],
        compiler_params=_params(2),
        name="proj_rope" if rope else "proj_ctx",
    )(*args)


def _attn_kernel(sink_ref, q_ref, kc_ref, vc_ref, *rest, local, n_sub):
    tb = q_ref.shape[0] // n_sub
    n_pairs = ATT_W // LANES
    n_ctx = kc_ref.shape[0]
    rg = ATT_ROW_GROUP
    if local:
        kp_ref, km_ref, kn_ref, vp_ref, vm_ref, vn_ref, o_ref, s_scr, p_scr, bias_scr = rest
        i = pl.program_id(1)
        n = pl.num_programs(1)
        k_loc = jnp.concatenate([kp_ref[...], km_ref[...], kn_ref[...]], axis=0)
        v_loc = jnp.concatenate([vp_ref[...], vm_ref[...], vn_ref[...]], axis=0)
        r = lax.broadcasted_iota(jnp.int32, (tb, tb), 0)
        j = lax.broadcasted_iota(jnp.int32, (tb, tb), 1)
        bias_prev = jnp.where(j >= r, 0.0, -jnp.inf)
        bias_next = jnp.where(j <= r, 0.0, -jnp.inf)
        bias_scr[0] = bias_prev + jnp.where(i > 0, 0.0, -jnp.inf)
        bias_scr[1] = bias_prev
        bias_scr[2] = bias_next
        bias_scr[3] = bias_next + jnp.where(i < n - 1, 0.0, -jnp.inf)
        nk = n_ctx + 3 * tb
    else:
        o_ref, s_scr, p_scr = rest
        nk = n_ctx
    lane = lax.broadcasted_iota(jnp.int32, (1, LANES), 1)
    lo = lane < HEAD_DIM
    hi = jnp.logical_not(lo)
    first_row = lax.broadcasted_iota(jnp.int32, (LANES, LANES), 0) == 0
    v_tail = jnp.concatenate([jnp.zeros((LANES, LANES), BF16),
                              jnp.where(first_row, 1.0, 0.0).astype(BF16)], axis=1)
    ones_blk = jnp.ones((nk, LANES), BF16)

    for jb in range(n_sub):
        q = q_ref[jb * tb:(jb + 1) * tb, :]
        zero = jnp.zeros((tb, LANES), BF16)
        lhs = jnp.concatenate(
            [jnp.where(lo if g == 0 else hi, q[:, m * LANES:(m + 1) * LANES], zero)
             for m in range(n_pairs) for g in range(ATT_KV_HEADS)], axis=0)
        if local:
            kk = jnp.concatenate([kc_ref[...], k_loc[jb * tb:(jb + 3) * tb]], axis=0)
            vv = jnp.concatenate([vc_ref[...], v_loc[jb * tb:(jb + 3) * tb]], axis=0)
        else:
            kk, vv = kc_ref[...], vc_ref[...]
        v_aug = jnp.concatenate([jnp.concatenate([vv, ones_blk], axis=1), v_tail], axis=0)
        s_scr[jb] = _dot_nt(lhs, kk)

        for hs in range(ATT_HEADS):
            m, g = hs // ATT_KV_HEADS, hs % ATT_KV_HEADS
            sink = sink_ref[m + (ATT_HEADS // ATT_KV_HEADS) * g]
            sink_tile = jnp.where(lane == 0, sink, -jnp.inf) + jnp.zeros((rg, LANES), F32)
            for gi in range(tb // rg):
                rq = gi * rg
                r0 = hs * tb + rq
                s = s_scr[jb, r0:r0 + rg, :]
                if local:
                    bp = bias_scr[0 if jb == 0 else 1, rq:rq + rg, :]
                    bn = bias_scr[3 if jb == n_sub - 1 else 2, rq:rq + rg, :]
                    cols = [s[:, :n_ctx], s[:, n_ctx:n_ctx + tb] + bp,
                            s[:, n_ctx + tb:n_ctx + 2 * tb], s[:, n_ctx + 2 * tb:] + bn, sink_tile]
                else:
                    cols = [s, sink_tile]
                s_all = jnp.concatenate(cols, axis=1)
                mx = jnp.max(s_all, axis=-1, keepdims=True)
                p_scr[jb, r0:r0 + rg, :] = jnp.exp(s_all - mx).astype(BF16)

        o_aug = _dot(p_scr[jb], v_aug)
        o = o_aug[:, :LANES] * (1.0 / o_aug[:, LANES:])
        for m in range(n_pairs):
            o_ref[jb * tb:(jb + 1) * tb, m * LANES:(m + 1) * LANES] = jnp.where(
                lo, o[(2 * m) * tb:(2 * m + 1) * tb], o[(2 * m + 1) * tb:(2 * m + 2) * tb]).astype(BF16)


def _attention(q, k, v, k_ctx, v_ctx, sink, local, n_sub=4):
    B, S, _ = q.shape
    Lc = k_ctx.shape[1]
    if not local:
        n_sub = 1
    tq = WINDOW * n_sub if local else S
    nq = S // tq
    in_specs = [
        pl.BlockSpec(memory_space=pltpu.SMEM),
        pl.BlockSpec((None, tq, ATT_W), lambda b, i: (b, i, 0)),
        pl.BlockSpec((None, Lc, KV_W), lambda b, i: (b, 0, 0)),
        pl.BlockSpec((None, Lc, KV_W), lambda b, i: (b, 0, 0)),
    ]
    args = [sink, q, k_ctx, v_ctx]
    if local:
        nw = S // WINDOW
        nb = [pl.BlockSpec((None, WINDOW, KV_W), lambda b, i: (b, jnp.maximum(i * n_sub - 1, 0), 0)),
              pl.BlockSpec((None, tq, KV_W), lambda b, i: (b, i, 0)),
              pl.BlockSpec((None, WINDOW, KV_W),
                           lambda b, i: (b, jnp.minimum((i + 1) * n_sub, nw - 1), 0))]
        in_specs += nb + nb
        args += [k, k, k, v, v, v]
    tb = tq // n_sub
    nk = Lc + (3 * tb if local else 0)
    scratch = [pltpu.VMEM((n_sub, ATT_HEADS * tb, nk), F32),
               pltpu.VMEM((n_sub, ATT_HEADS * tb, nk + LANES), BF16)]
    if local:
        scratch.append(pltpu.VMEM((4, tb, tb), F32))
    return pl.pallas_call(
        functools.partial(_attn_kernel, local=local, n_sub=n_sub),
        grid=(B, nq),
        in_specs=in_specs,
        out_specs=pl.BlockSpec((None, tq, ATT_W), lambda b, i: (b, i, 0)),
        out_shape=jax.ShapeDtypeStruct((B, S, ATT_W), BF16),
        scratch_shapes=scratch,
        compiler_params=_params(2),
        name="attn_window" if local else "attn_ctx",
    )(*args)


def _gla_consts(rev):
    t = np.arange(GLA_CHUNK)[:, None]
    s = np.arange(GLA_CHUNK)[None, :]
    same = (t // GLA_SUB) == (s // GLA_SUB)
    before = (s >= t) if rev else (s <= t)
    return np.concatenate([same & before, before], axis=0).astype(np.float32)


def _head_blocks():
    h = np.arange(GLA_W) // HEAD_DIM
    return (h[:, None] == h[None, :]).astype(np.float32)


def _gla_last_row(blk, rev):
    return blk * GLA_SUB if rev else blk * GLA_SUB + GLA_SUB - 1


def _gla_prepare(q, k, g, cum, rev):
    n_sub = GLA_CHUNK // GLA_SUB
    g2 = g * LOG2_E
    g_hi = g2.astype(BF16)
    rem = g2 - g_hi.astype(F32)
    g_mid = rem.astype(BF16)
    g_lo = (rem - g_mid.astype(F32)).astype(BF16)
    cs = _dot(cum, g_hi) + _dot(cum, g_mid) + _dot(cum, g_lo)
    beta, bch = cs[0:GLA_CHUNK], cs[GLA_CHUNK:2 * GLA_CHUNK]
    gam = jnp.concatenate(
        [beta[_gla_last_row(b, rev):_gla_last_row(b, rev) + 1, :] - beta[b * GLA_SUB:(b + 1) * GLA_SUB, :]
         for b in range(n_sub)], axis=0)
    end_row = _gla_last_row(0 if rev else n_sub - 1, rev)
    dch = bch[end_row:end_row + 1, :] - bch
    f32_rows = jnp.concatenate([beta, bch, q * jnp.exp2(beta), k * jnp.exp2(gam)], axis=0)
    b16_rows = jnp.concatenate([(q * jnp.exp2(bch)).astype(BF16), (k * jnp.exp2(dch)).astype(BF16)],
                               axis=0)
    return f32_rows, b16_rows


def _gla_causal(rev):
    t = (np.arange(GLA_HEADS * GLA_CHUNK) % GLA_CHUNK)[:, None]
    s = np.arange(GLA_CHUNK)[None, :]
    return ((s >= t) if rev else (s <= t)).astype(np.float32)


def _gla_prepare_fast(q, k, v, g, cum_chunk, head_mask, rev):
    g2 = g * LOG2_E
    g_hi = g2.astype(BF16)
    rem = g2 - g_hi.astype(F32)
    g_mid = rem.astype(BF16)
    g_lo = (rem - g_mid.astype(F32)).astype(BF16)
    bch = _dot(cum_chunk, g_hi) + _dot(cum_chunk, g_mid) + _dot(cum_chunk, g_lo)
    end_row = 0 if rev else GLA_CHUNK - 1
    mid_row = GLA_CHUNK // 2 if rev else GLA_CHUNK // 2 - 1
    b_end = bch[end_row:end_row + 1, :]
    b_mid = bch[mid_row:mid_row + 1, :]
    ok = (jnp.max(-b_end) <= GLA_FAST_RANGE).astype(jnp.int32)
    q_mid = (q * jnp.exp2(jnp.minimum(bch - b_mid, GLA_FAST_RANGE))).astype(BF16)
    k_mid = (k * jnp.exp2(jnp.minimum(b_mid - bch, GLA_FAST_RANGE))).astype(BF16)
    zero = jnp.zeros_like(q_mid)
    rows = jnp.concatenate(
        [jnp.where(hm, q_mid, zero) for hm in head_mask]
        + [k_mid, (q * jnp.exp2(bch)).astype(BF16), (k * jnp.exp2(b_end - bch)).astype(BF16),
           v.astype(BF16)], axis=0)
    return rows, jnp.exp2(b_end), ok


def _gla_chunk_fast(pb_ref, dec, st_ref, o_ref, r0, bd, causal, head_mask):
    n_q = GLA_HEADS * GLA_CHUNK
    k_mid = pb_ref[n_q:n_q + GLA_CHUNK]
    q_int = pb_ref[n_q + GLA_CHUNK:n_q + 2 * GLA_CHUNK]
    k_dec = pb_ref[n_q + 2 * GLA_CHUNK:n_q + 3 * GLA_CHUNK]
    vb = pb_ref[n_q + 3 * GLA_CHUNK:n_q + 4 * GLA_CHUNK]
    st = st_ref[...]
    o_inter = _dot_nt(q_int, st.astype(BF16))
    upd = _dot_tn(vb, k_dec)
    a = _dot_nt(pb_ref[0:n_q], k_mid)
    yield
    r = _dot((a * causal).astype(BF16), vb)
    yield
    o = o_inter
    for hh, hm in enumerate(head_mask):
        o = o + jnp.where(hm, r[hh * GLA_CHUNK:(hh + 1) * GLA_CHUNK, :], 0.0)
    o_ref[pl.ds(r0, GLA_CHUNK), :] = o
    st_ref[...] = st * dec + upd * bd


def _gla_chunk(q, k, v, pf_ref, pb_ref, st_ref, o_ref, r0, ones, bd, head_mask, rev):
    n_sub = GLA_CHUNK // GLA_SUB
    order = list(range(n_sub))[::-1] if rev else list(range(n_sub))

    def rows(a, blk):
        return a[blk * GLA_SUB:(blk + 1) * GLA_SUB, :]

    half = GLA_SUB // 2
    beta = pf_ref[0:GLA_CHUNK]
    q_loc = pf_ref[2 * GLA_CHUNK:3 * GLA_CHUNK]
    k_loc = pf_ref[3 * GLA_CHUNK:4 * GLA_CHUNK]
    g_tot = [beta[_gla_last_row(b, rev):_gla_last_row(b, rev) + 1, :] for b in range(n_sub)]
    end_row = GLA_CHUNK + _gla_last_row(order[-1], rev)
    b_end = pf_ref[end_row:end_row + 1, :]
    half_row = lax.broadcasted_iota(jnp.int32, (half, 1), 0)

    st = st_ref[...]
    o_inter = _dot_nt(pb_ref[0:GLA_CHUNK], st.astype(BF16))
    upd = _dot_tn(v.astype(BF16), pb_ref[GLA_CHUNK:2 * GLA_CHUNK])
    scores, values = {}, {}
    for p in range(1, n_sub):
        ks, vs = [], []
        for pp in range(p):
            sb = order[pp]
            kk = rows(k_loc, sb)
            mids = [order[x] for x in range(pp + 1, p)]
            if mids:
                tot = g_tot[mids[0]]
                for mb in mids[1:]:
                    tot = tot + g_tot[mb]
                kk = kk * jnp.exp2(tot)
            ks.append(kk)
            vs.append(rows(v, sb))
        qt = rows(q_loc, order[p])
        q_heads = jnp.concatenate([jnp.where(hm, qt, 0.0) for hm in head_mask], axis=0)
        scores[p] = _dot_nt(q_heads.astype(BF16), jnp.concatenate(ks, axis=0).astype(BF16))
        values[p] = jnp.concatenate(vs, axis=0).astype(BF16)
    yield

    pairs = []
    for s in range(GLA_SUB):
        hs = s // half
        for hb in (range(0, hs + 1) if rev else range(hs, GLA_SUB // half)):
            pairs.append((s, hb))
    sums = {}
    for p, tb in enumerate(order):
        bt, qb, kb = rows(beta, tb), rows(q, tb), rows(k, tb)
        slabs = []
        for s, hb in pairs:
            d = bt[hb * half:(hb + 1) * half, :] - bt[s:s + 1, :]
            if hb == s // half:
                keep = (half_row <= s % half) if rev else (half_row >= s % half)
                d = jnp.where(keep, d, -jnp.inf)
            slabs.append(qb[hb * half:(hb + 1) * half, :] * jnp.exp2(d) * kb[s:s + 1, :])
        sums[p] = _dot(jnp.concatenate(slabs, axis=0).astype(BF16), ones)
        yield

    mixed = {p: _dot(scores[p].astype(BF16), values[p]) for p in range(1, n_sub)}
    yield

    for p, tb in enumerate(order):
        vb = rows(v, tb)
        acc = rows(o_inter, tb)
        if p > 0:
            for hh, hm in enumerate(head_mask):
                acc = acc + jnp.where(hm, mixed[p][hh * GLA_SUB:(hh + 1) * GLA_SUB, :], 0.0)
        halves = [acc[hb * half:(hb + 1) * half, :] for hb in range(GLA_SUB // half)]
        for idx, (s, hb) in enumerate(pairs):
            halves[hb] = halves[hb] + sums[p][idx * half:(idx + 1) * half, :] * vb[s:s + 1, :]
        o_ref[pl.ds(r0 + tb * GLA_SUB, GLA_SUB), :] = jnp.concatenate(halves, axis=0)
        yield

    st_ref[...] = st * jnp.exp2(b_end) + upd * bd


def _interleave(gens):
    active = list(gens)
    while active:
        for gen in list(active):
            try:
                next(gen)
            except StopIteration:
                active.remove(gen)


def _gla_kernel(cumf_ref, cumb_ref, ones_ref, bd_ref, causf_ref, causb_ref, glf_ref, glb_ref,
                gf_ref, gb_ref, s0f_ref, s0b_ref, of_ref, ob_ref, sff_ref, sfb_ref,
                stf_ref, stb_ref, pb_scr, dec_scr, ok_ref, *, n_chunks):
    i = pl.program_id(1)
    lane_head = lax.broadcasted_iota(jnp.int32, (1, GLA_W), 1) // HEAD_DIM
    head_mask = [lane_head == hh for hh in range(GLA_HEADS)]
    dirs = ((glf_ref, gf_ref, cumf_ref, stf_ref, of_ref, causf_ref, False),
            (glb_ref, gb_ref, cumb_ref, stb_ref, ob_ref, causb_ref, True))

    @pl.when(i == 0)
    def _():
        stf_ref[...] = s0f_ref[...]
        stb_ref[...] = s0b_ref[...]

    def load(gl_ref, g_ref, r0):
        rows = pl.ds(r0, GLA_CHUNK)
        return (gl_ref[rows, 0:GLA_W], gl_ref[rows, GLA_W:2 * GLA_W],
                gl_ref[rows, 2 * GLA_W:3 * GLA_W], g_ref[rows, :])

    for c in range(n_chunks):
        for d, (gl_ref, g_ref, cum_ref, _, _, _, rev) in enumerate(dirs):
            q, k, v, g = load(gl_ref, g_ref, c * GLA_CHUNK)
            rows, dec, ok = _gla_prepare_fast(q, k, v, g, cum_ref[GLA_CHUNK:2 * GLA_CHUNK, :],
                                              head_mask, rev)
            pb_scr[d, c] = rows
            dec_scr[d, c] = jnp.broadcast_to(dec, (8, GLA_W))
            ok_ref[d, c] = ok

    def chunk(ci, carry):
        cidx = (ci, n_chunks - 1 - ci)
        r0s = [pl.multiple_of(c * GLA_CHUNK, GLA_CHUNK) for c in cidx]
        fast = (ok_ref[0, cidx[0]] + ok_ref[1, cidx[1]]) == 2

        @pl.when(fast)
        def _():
            _interleave([
                _gla_chunk_fast(pb_scr.at[d, cidx[d]], dec_scr[d, cidx[d], 0:1, :], st_ref, o_ref,
                                r0s[d], bd_ref[...], caus_ref[...], head_mask)
                for d, (_, _, _, st_ref, o_ref, caus_ref, _) in enumerate(dirs)])

        @pl.when(jnp.logical_not(fast))
        def _():
            gens = []
            for d, (gl_ref, g_ref, cum_ref, st_ref, o_ref, _, rev) in enumerate(dirs):
                q, k, v, g = load(gl_ref, g_ref, r0s[d])
                pf, pb = _gla_prepare(q, k, g, cum_ref[...], rev)
                gens.append(_gla_chunk(q, k, v, pf, pb, st_ref, o_ref, r0s[d], ones_ref[...],
                                       bd_ref[...], head_mask, rev))
            _interleave(gens)

        return carry

    lax.fori_loop(0, n_chunks, chunk, 0)

    @pl.when(i == pl.num_programs(1) - 1)
    def _():
        sff_ref[...] = stf_ref[...]
        sfb_ref[...] = stb_ref[...]


def _gla(gl, gates, s0f, s0b, tile):
    B, S, _ = gl.shape
    nt = S // tile
    n_chunks = tile // GLA_CHUNK
    ones_bd = jnp.asarray(_head_blocks(), BF16)
    bd = jnp.asarray(_head_blocks(), F32)
    state_spec = pl.BlockSpec((None, GLA_W, GLA_W), lambda b, i: (b, 0, 0))
    return pl.pallas_call(
        functools.partial(_gla_kernel, n_chunks=n_chunks),
        grid=(B, nt),
        in_specs=[
            _const_spec((2 * GLA_CHUNK, GLA_CHUNK)),
            _const_spec((2 * GLA_CHUNK, GLA_CHUNK)),
            _const_spec((GLA_W, GLA_W)),
            _const_spec((GLA_W, GLA_W)),
            _const_spec((GLA_HEADS * GLA_CHUNK, GLA_CHUNK)),
            _const_spec((GLA_HEADS * GLA_CHUNK, GLA_CHUNK)),
            pl.BlockSpec((None, tile, 3 * GLA_W), lambda b, i: (b, i, 0)),
            pl.BlockSpec((None, tile, 3 * GLA_W), lambda b, i: (b, nt - 1 - i, 0)),
            pl.BlockSpec((None, tile, GLA_W), lambda b, i: (b, i, 0)),
            pl.BlockSpec((None, tile, GLA_W), lambda b, i: (b, nt - 1 - i, 1)),
            state_spec, state_spec,
        ],
        out_specs=[
            pl.BlockSpec((None, tile, GLA_W), lambda b, i: (b, i, 0)),
            pl.BlockSpec((None, tile, GLA_W), lambda b, i: (b, nt - 1 - i, 0)),
            state_spec, state_spec,
        ],
        out_shape=[jax.ShapeDtypeStruct((B, S, GLA_W), F32),
                   jax.ShapeDtypeStruct((B, S, GLA_W), F32),
                   jax.ShapeDtypeStruct((B, GLA_W, GLA_W), F32),
                   jax.ShapeDtypeStruct((B, GLA_W, GLA_W), F32)],
        scratch_shapes=[pltpu.VMEM((GLA_W, GLA_W), F32), pltpu.VMEM((GLA_W, GLA_W), F32),
                        pltpu.VMEM((2, n_chunks, (GLA_HEADS + 4) * GLA_CHUNK, GLA_W), BF16),
                        pltpu.VMEM((2, n_chunks, 8, GLA_W), F32),
                        pltpu.SMEM((2, n_chunks), jnp.int32)],
        compiler_params=_params(2),
        name="gla",
    )(jnp.asarray(_gla_consts(False), BF16), jnp.asarray(_gla_consts(True), BF16),
      ones_bd, bd, jnp.asarray(_gla_causal(False), F32), jnp.asarray(_gla_causal(True), F32),
      gl, gl, gates, gates, s0f, s0b)


def _post_kernel(x_ref, mod_ref, g2_ref, cu_ref, cup_ref, cun_ref, cb_ref, cw_ref, at_ref,
                 of_ref, ob_ref, gg_ref, ng_ref, ones_ref, wo_ref, wa_ref, wb_ref, wd_ref,
                 *rest, final):
    if final:
        fg_ref, o_ref, acc_ref = rest
    else:
        o_ref, acc_ref = rest
    i = pl.program_id(1)
    n = pl.num_programs(1)
    tm = x_ref.shape[0]

    u = cu_ref[...]
    row = lax.broadcasted_iota(jnp.int32, (tm, 1), 0)
    prev_row = jnp.where(i > 0, cup_ref[7:8, :], 0.0)
    next_row = jnp.where(i < n - 1, cun_ref[0:1, :], 0.0)
    u_prev = jnp.where(row == 0, prev_row, pltpu.roll(u, 1, axis=0))
    u_next = jnp.where(row == tm - 1, next_row, pltpu.roll(u, tm - 1, axis=0))
    conv = cb_ref[...] * (cw_ref[0:1, :] * u_prev + cw_ref[1:2, :] * u + cw_ref[2:3, :] * u_next)

    o = of_ref[...] + ob_ref[...]
    sq = o * o
    sq_hi = sq.astype(BF16)
    sq_lo = (sq - sq_hi.astype(F32)).astype(BF16)
    ms = (_dot(sq_hi, ones_ref[...]) + _dot(sq_lo, ones_ref[...])) * (1.0 / HEAD_DIM)
    gla = o * lax.rsqrt(ms + EPS) * ng_ref[...] * _silu(gg_ref[...])

    mix = (_dot(conv.astype(BF16), wo_ref[0:CONV_CH, :])
           + _dot(at_ref[...], wo_ref[CONV_CH:CONV_CH + ATT_W, :])
           + _dot(gla.astype(BF16), wo_ref[CONV_CH + ATT_W:, :]))
    x1 = x_ref[...] + mod_ref[2:3, :] * mix

    ms2 = jnp.mean(x1 * x1, axis=-1, keepdims=True)
    y2 = x1 * lax.rsqrt(ms2 + EPS) * g2_ref[...]
    h2 = (y2 * (1.0 + mod_ref[4:5, :]) + mod_ref[3:4, :]).astype(BF16)

    acc_ref[...] = jnp.zeros_like(acc_ref)

    def ff(jc, carry):
        a = _dot(h2, wa_ref[jc])
        b = _dot(h2, wb_ref[jc])
        acc_ref[...] += _dot((_silu(a) * b).astype(BF16), wd_ref[jc])
        return carry

    lax.fori_loop(0, N_FF_CHUNKS, ff, 0)
    x2 = x1 + mod_ref[5:6, :] * acc_ref[...]
    if final:
        msf = jnp.mean(x2 * x2, axis=-1, keepdims=True)
        x2 = x2 * lax.rsqrt(msf + EPS) * fg_ref[...]
    o_ref[...] = x2


def _post(xs, mods_l, mod_row, g2, cu, cb, conv_w, attn, o_f, o_b, gl, norm_g, w_out, wa, wb, wd,
          final_g, tm):
    B, S, D = xs.shape
    nt = S // tm
    hb = tm // 8
    row = lambda w: pl.BlockSpec((None, tm, w), lambda b, i: (b, i, 0))
    if mod_row is None:
        mod_spec = pl.BlockSpec((None, 8, D), lambda b, i: (b, 0, 0))
    else:
        mod_spec = pl.BlockSpec((None, 8, D), lambda b, i: (mod_row, 0, 0))
    final = final_g is not None
    in_specs = [
        row(D), mod_spec, _const_spec((1, D)),
        row(CONV_CH),
        pl.BlockSpec((None, 8, CONV_CH), lambda b, i: (b, jnp.maximum(i * hb - 1, 0), 0)),
        pl.BlockSpec((None, 8, CONV_CH), lambda b, i: (b, jnp.minimum((i + 1) * hb, S // 8 - 1), 0)),
        row(CONV_CH), _const_spec((8, CONV_CH)), row(ATT_W), row(GLA_W), row(GLA_W),
        pl.BlockSpec((None, tm, GLA_W), lambda b, i: (b, i, 3)),
        _const_spec((1, GLA_W)), _const_spec((GLA_W, GLA_W)), _const_spec((D, D)),
        _const_spec((N_FF_CHUNKS, D, FF_CHUNK)), _const_spec((N_FF_CHUNKS, D, FF_CHUNK)),
        _const_spec((N_FF_CHUNKS, FF_CHUNK, D)),
    ]
    args = [xs, mods_l, g2, cu, cu, cu, cb, conv_w, attn, o_f, o_b, gl, norm_g,
            jnp.asarray(_head_blocks(), BF16), w_out, wa, wb, wd]
    if final:
        in_specs.append(_const_spec((1, D)))
        args.append(final_g)
    return pl.pallas_call(
        functools.partial(_post_kernel, final=final),
        grid=(B, nt),
        in_specs=in_specs,
        out_specs=row(D),
        out_shape=jax.ShapeDtypeStruct((B, S, D), F32),
        scratch_shapes=[pltpu.VMEM((tm, D), F32)],
        compiler_params=_params(2),
        name="post_final" if final else "post",
    )(*args)


def _rope_tables(seq):
    t = jnp.arange(seq)
    pos = jnp.stack([(t // GRID_W).astype(F32), (t % GRID_W).astype(F32)], axis=1)
    n_freq = HEAD_DIM // 4
    inv_freq = ROPE_BASE ** (-jnp.arange(n_freq, dtype=F32) / n_freq)
    ang = pos[:, :, None] * inv_freq
    cos, sin = jnp.cos(ang), jnp.sin(ang)
    zero = jnp.zeros_like(sin)
    cos_t = jnp.stack([cos, cos], axis=2).reshape(seq, HEAD_DIM)
    sa_t = jnp.stack([-sin, zero], axis=2).reshape(seq, HEAD_DIM)
    sb_t = jnp.stack([zero, sin], axis=2).reshape(seq, HEAD_DIM)
    rep = LANES // HEAD_DIM
    return tuple(jnp.tile(a, (1, rep)) for a in (cos_t, sa_t, sb_t))


def _tile_rows(S, want):
    t = min(want, S)
    while S % t:
        t //= 2
    return t


def kernel(x, c, ctx, c_ctx, w_mod, b_mod, norm1_g, norm2_g, w_in, conv_w, attn_sink,
           gla_gate_w, gla_gate_b, gla_norm_g, w_out, w_up, w_down, final_norm_g):
    B, S, D = x.shape
    L = w_mod.shape[0]
    Lc = ctx.shape[1]
    assert D == D_MODEL and S % WINDOW == 0 and Lc % GLA_CHUNK == 0 and B + 1 <= 8

    cvec = jnp.zeros((8, D), F32).at[:B].set(c).at[B].set(c_ctx)
    mods = _modulation(cvec, w_mod, b_mod).reshape(L, 8, N_MOD, D)
    mods = jnp.pad(mods, ((0, 0), (0, 0), (0, 8 - N_MOD), (0, 0)))

    order = np.array(Q_HEAD_ORDER)
    wq = w_in[:, :, 768:1280].reshape(L, D, ATT_HEADS, HEAD_DIM)[:, :, order].reshape(L, D, ATT_W)
    w_in_p = jnp.concatenate(
        [w_in[:, :, :768], wq, w_in[:, :, 1280:2592],
         jnp.zeros((L, D, LANES - 2 * GLA_RANK), F32)], axis=2).astype(BF16)
    wo_att = w_out[:, 256:768].reshape(L, ATT_HEADS, HEAD_DIM, D)[:, order].reshape(L, ATT_W, D)
    w_out_p = jnp.concatenate([w_out[:, :256], wo_att, w_out[:, 768:]], axis=1).astype(BF16)
    wg = jnp.zeros((L, LANES, 2 * GLA_W), F32)
    wg = wg.at[:, :GLA_RANK, :GLA_W].set(gla_gate_w[:, 0])
    wg = wg.at[:, GLA_RANK:2 * GLA_RANK, GLA_W:].set(gla_gate_w[:, 1]).astype(BF16)
    bg = gla_gate_b.reshape(L, 1, 2 * GLA_W)
    cw = jnp.pad(conv_w, ((0, 0), (0, 8 - CONV_K), (0, 0)))
    ng = jnp.tile(gla_norm_g, (1, GLA_HEADS)).reshape(L, 1, GLA_W)
    wa = w_up[:, :, :D_FF].reshape(L, D, N_FF_CHUNKS, FF_CHUNK).transpose(0, 2, 1, 3).astype(BF16)
    wb = w_up[:, :, D_FF:].reshape(L, D, N_FF_CHUNKS, FF_CHUNK).transpose(0, 2, 1, 3).astype(BF16)
    wd = w_down.reshape(L, N_FF_CHUNKS, FF_CHUNK, D).astype(BF16)
    g1 = norm1_g.reshape(L, 1, D)
    g2 = norm2_g.reshape(L, 1, D)

    rope_tabs = _rope_tables(S)
    tm = _tile_rows(S, 512)
    gla_tile = _tile_rows(S, 512)
    zero_state = jnp.zeros((B, GLA_W, GLA_W), F32)

    for l in range(L):
        last = l == L - 1
        cu, cb, q, k, v, gl, gates = _proj(x, mods[l], None, g1[l], w_in_p[l], wg[l], bg[l],
                                           rope_tabs, tm)
        ccu, ccb, cq, ck, cv, cgl, cgates = _proj(ctx, mods[l], B, g1[l], w_in_p[l], wg[l], bg[l],
                                                  None, Lc)
        attn = _attention(q, k, v, ck, cv, attn_sink[l], local=True)
        oc_f, oc_b, sc_f, sc_b = _gla(cgl, cgates, zero_state, zero_state, Lc)
        o_f, o_b, _, _ = _gla(gl, gates, sc_f, sc_b, gla_tile)
        fin = final_norm_g.reshape(1, D) if last else None
        x = _post(x, mods[l], None, g2[l], cu, cb, cw[l], attn, o_f, o_b, gl, ng[l],
                  w_out_p[l], wa[l], wb[l], wd[l], fin, tm)
        if not last:
            attn_c = _attention(cq, None, None, ck, cv, attn_sink[l], local=False)
            ctx = _post(ctx, mods[l], B, g2[l], ccu, ccb, cw[l], attn_c, oc_f, oc_b, cgl, ng[l],
                        w_out_p[l], wa[l], wb[l], wd[l], None, Lc)
    return x
```

```python
import functools

import numpy as np
import jax
import jax.numpy as jnp
from jax import lax
from jax.experimental import pallas as pl
from jax.experimental.pallas import tpu as pltpu

F32 = jnp.float32
BF16 = jnp.bfloat16

D_MODEL = 1024
HEAD_DIM = 64
CONV_CH = 256
CONV_K = 3
ATT_HEADS = 8
ATT_KV_HEADS = 2
ATT_W = ATT_HEADS * HEAD_DIM
KV_W = ATT_KV_HEADS * HEAD_DIM
WINDOW = 128
GLA_HEADS = 4
GLA_W = GLA_HEADS * HEAD_DIM
GLA_RANK = 16
GLA_NORMALIZER = 16.0
D_FF = 2816
N_MOD = 6
GRID_W = 64
ROPE_BASE = 10000.0
EPS = 1e-6
LOG2_E = 1.4426950408889634

LANES = 128
C_CONV = 0
C_QK = 768
C_V = 1408
C_GLA = 1536
C_LR = 2560
N_IN_PAD = C_LR + LANES
Q_HEAD_ORDER = (0, 4, 1, 5, 2, 6, 3, 7)

ATT_ROW_GROUP = 16
GLA_CHUNK = 64
GLA_SUB = 16
GLA_FAST_RANGE = 60.0
FF_CHUNK = 256
N_FF_CHUNKS = D_FF // FF_CHUNK
FF_UNROLL = 3
assert (N_FF_CHUNKS - 2) % FF_UNROLL == 0

VMEM_LIMIT = 56 * 1024 * 1024


def _const_spec(shape):
    nd = len(shape)
    return pl.BlockSpec(shape, lambda *_: (0,) * nd, pipeline_mode=pl.Buffered(1))


def _params(n_grid):
    return pltpu.CompilerParams(
        dimension_semantics=("arbitrary",) * n_grid, vmem_limit_bytes=VMEM_LIMIT)


def _silu(a):
    return a * (1.0 / (1.0 + jnp.exp(-a)))


def _dot(a, b):
    return jnp.dot(a, b, preferred_element_type=F32)


def _dot_nt(a, b):
    return lax.dot_general(a, b, (((1,), (1,)), ((), ())), preferred_element_type=F32)


def _dot_tn(a, b):
    return lax.dot_general(a, b, (((0,), (0,)), ((), ())), preferred_element_type=F32)


def _mod_kernel(c_ref, w_ref, b_ref, o_ref):
    a = _silu(c_ref[...]).astype(BF16)
    o_ref[...] = _dot(a, w_ref[...].astype(BF16)) + b_ref[...]


def _modulation(cvec, w_mod, b_mod):
    L, D, W = w_mod.shape
    tn = 2048
    return pl.pallas_call(
        _mod_kernel,
        grid=(L, W // tn),
        in_specs=[
            pl.BlockSpec((8, D), lambda l, j: (0, 0)),
            pl.BlockSpec((None, D, tn), lambda l, j: (l, 0, j)),
            pl.BlockSpec((None, 1, tn), lambda l, j: (l, 0, j)),
        ],
        out_specs=pl.BlockSpec((None, 8, tn), lambda l, j: (l, 0, j)),
        out_shape=jax.ShapeDtypeStruct((L, 8, W), F32),
        compiler_params=_params(2),
        name="modulation",
    )(cvec, w_mod, b_mod.reshape(L, 1, W))


def _proj_kernel(x_ref, mod_ref, g1_ref, w_ref, wg_ref, bg_ref, *rest, rope):
    if rope:
        cos_ref, sa_ref, sb_ref = rest[:3]
        rest = rest[3:]
    cu_ref, cb_ref, q_ref, k_ref, v_ref, gl_ref, gate_ref = rest

    x = x_ref[...]
    ms = jnp.mean(x * x, axis=-1, keepdims=True)
    y = x * lax.rsqrt(ms + EPS) * g1_ref[...]
    h = (y * (1.0 + mod_ref[1:2, :]) + mod_ref[0:1, :]).astype(BF16)

    def proj(lo, hi):
        return _dot(h, w_ref[:, lo:hi])

    pc = proj(C_CONV, C_QK)
    cu_ref[...] = pc[:, 2 * CONV_CH:3 * CONV_CH] * pc[:, 0:CONV_CH]
    cb_ref[...] = pc[:, CONV_CH:2 * CONV_CH]

    pqk = proj(C_QK, C_V)
    scale = HEAD_DIM ** -0.5
    for m in range((ATT_W + KV_W) // LANES):
        col = pqk[:, m * LANES:(m + 1) * LANES]
        if rope:
            col = (col * cos_ref[...]
                   + pltpu.roll(col, LANES - HEAD_DIM // 4, axis=1) * sa_ref[...]
                   + pltpu.roll(col, HEAD_DIM // 4, axis=1) * sb_ref[...])
        if m < ATT_W // LANES:
            q_ref[:, m * LANES:(m + 1) * LANES] = (col * scale).astype(BF16)
        else:
            k_ref[...] = col.astype(BF16)

    v_ref[...] = proj(C_V, C_GLA).astype(BF16)

    pg = proj(C_GLA, C_LR)
    gl_ref[:, 0:GLA_W] = pg[:, 0:GLA_W] * (HEAD_DIM ** -0.5)
    gl_ref[:, GLA_W:4 * GLA_W] = pg[:, GLA_W:4 * GLA_W]

    lr = proj(C_LR, N_IN_PAD).astype(BF16)
    z = _dot(lr, wg_ref[...]) + bg_ref[...]
    log_sig = jnp.minimum(z, 0.0) - jnp.log1p(jnp.exp(-jnp.abs(z)))
    gate_ref[...] = log_sig * (1.0 / GLA_NORMALIZER)


def _proj(xs, mods_l, mod_row, g1, w_in, wg, bg, rope_tabs, tm):
    B, S, D = xs.shape
    rope = rope_tabs is not None
    nt = S // tm
    row = lambda w: pl.BlockSpec((None, tm, w), lambda b, i: (b, i, 0))
    if mod_row is None:
        mod_spec = pl.BlockSpec((None, 8, D), lambda b, i: (b, 0, 0))
    else:
        mod_spec = pl.BlockSpec((None, 8, D), lambda b, i: (mod_row, 0, 0))
    in_specs = [row(D), mod_spec, _const_spec((1, D)), _const_spec((D, N_IN_PAD)),
                _const_spec((LANES, 2 * GLA_W)), _const_spec((1, 2 * GLA_W))]
    args = [xs, mods_l, g1, w_in, wg, bg]
    if rope:
        in_specs += [pl.BlockSpec((tm, LANES), lambda b, i: (i, 0))] * 3
        args += list(rope_tabs)
    widths = (CONV_CH, CONV_CH, ATT_W, KV_W, KV_W, 4 * GLA_W, 2 * GLA_W)
    dtypes = (F32, F32, BF16, BF16, BF16, F32, F32)
    return pl.pallas_call(
        functools.partial(_proj_kernel, rope=rope),
        grid=(B, nt),
        in_specs=in_specs,
        out_specs=[row(w) for w in widths],
        out_shape=[jax.ShapeDtypeStruct((B, S, w), dt) for w, dt in zip(widths, dtypes)],
        compiler_params=_params(2),
        name="proj_rope" if rope else "proj_ctx",
    )(*args)


def _attn_kernel(sink_ref, q_ref, kc_ref, vc_ref, *rest, local, n_sub):
    tb = q_ref.shape[0] // n_sub
    n_pairs = ATT_W // LANES
    n_ctx = kc_ref.shape[0]
    rg = ATT_ROW_GROUP
    if local:
        kp_ref, km_ref, kn_ref, vp_ref, vm_ref, vn_ref, o_ref, s_scr, p_scr, bias_scr = rest
        i = pl.program_id(1)
        n = pl.num_programs(1)
        k_loc = jnp.concatenate([kp_ref[...], km_ref[...], kn_ref[...]], axis=0)
        v_loc = jnp.concatenate([vp_ref[...], vm_ref[...], vn_ref[...]], axis=0)
        r = lax.broadcasted_iota(jnp.int32, (tb, tb), 0)
        j = lax.broadcasted_iota(jnp.int32, (tb, tb), 1)
        bias_prev = jnp.where(j >= r, 0.0, -jnp.inf)
        bias_next = jnp.where(j <= r, 0.0, -jnp.inf)
        bias_scr[0] = bias_prev + jnp.where(i > 0, 0.0, -jnp.inf)
        bias_scr[1] = bias_prev
        bias_scr[2] = bias_next
        bias_scr[3] = bias_next + jnp.where(i < n - 1, 0.0, -jnp.inf)
        nk = n_ctx + 3 * tb
    else:
        o_ref, s_scr, p_scr = rest
        nk = n_ctx
    lane = lax.broadcasted_iota(jnp.int32, (1, LANES), 1)
    lo = lane < HEAD_DIM
    hi = jnp.logical_not(lo)
    first_row = lax.broadcasted_iota(jnp.int32, (LANES, LANES), 0) == 0
    v_tail = jnp.concatenate([jnp.zeros((LANES, LANES), BF16),
                              jnp.where(first_row, 1.0, 0.0).astype(BF16)], axis=1)
    ones_blk = jnp.ones((nk, LANES), BF16)

    for jb in range(n_sub):
        q = q_ref[jb * tb:(jb + 1) * tb, :]
        zero = jnp.zeros((tb, LANES), BF16)
        lhs = jnp.concatenate(
            [jnp.where(lo if g == 0 else hi, q[:, m * LANES:(m + 1) * LANES], zero)
             for m in range(n_pairs) for g in range(ATT_KV_HEADS)], axis=0)
        if local:
            kk = jnp.concatenate([kc_ref[...], k_loc[jb * tb:(jb + 3) * tb]], axis=0)
            vv = jnp.concatenate([vc_ref[...], v_loc[jb * tb:(jb + 3) * tb]], axis=0)
        else:
            kk, vv = kc_ref[...], vc_ref[...]
        v_aug = jnp.concatenate([jnp.concatenate([vv, ones_blk], axis=1), v_tail], axis=0)
        s_scr[jb] = _dot_nt(lhs, kk)

        for hs in range(ATT_HEADS):
            m, g = hs // ATT_KV_HEADS, hs % ATT_KV_HEADS
            sink = sink_ref[m + (ATT_HEADS // ATT_KV_HEADS) * g]
            sink_tile = jnp.where(lane == 0, sink, -jnp.inf) + jnp.zeros((rg, LANES), F32)
            for gi in range(tb // rg):
                rq = gi * rg
                r0 = hs * tb + rq
                s = s_scr[jb, r0:r0 + rg, :]
                if local:
                    bp = bias_scr[0 if jb == 0 else 1, rq:rq + rg, :]
                    bn = bias_scr[3 if jb == n_sub - 1 else 2, rq:rq + rg, :]
                    cols = [s[:, :n_ctx], s[:, n_ctx:n_ctx + tb] + bp,
                            s[:, n_ctx + tb:n_ctx + 2 * tb], s[:, n_ctx + 2 * tb:] + bn, sink_tile]
                else:
                    cols = [s, sink_tile]
                s_all = jnp.concatenate(cols, axis=1)
                mx = jnp.max(s_all, axis=-1, keepdims=True)
                p_scr[jb, r0:r0 + rg, :] = jnp.exp(s_all - mx).astype(BF16)

        o_aug = _dot(p_scr[jb], v_aug)
        o = o_aug[:, :LANES] * (1.0 / o_aug[:, LANES:])
        for m in range(n_pairs):
            o_ref[jb * tb:(jb + 1) * tb, m * LANES:(m + 1) * LANES] = jnp.where(
                lo, o[(2 * m) * tb:(2 * m + 1) * tb], o[(2 * m + 1) * tb:(2 * m + 2) * tb]).astype(BF16)


def _attention(q, k, v, k_ctx, v_ctx, sink, local, n_sub=4):
    B, S, _ = q.shape
    Lc = k_ctx.shape[1]
    if not local:
        n_sub = 1
    tq = WINDOW * n_sub if local else S
    nq = S // tq
    in_specs = [
        pl.BlockSpec(memory_space=pltpu.SMEM),
        pl.BlockSpec((None, tq, ATT_W), lambda b, i: (b, i, 0)),
        pl.BlockSpec((None, Lc, KV_W), lambda b, i: (b, 0, 0)),
        pl.BlockSpec((None, Lc, KV_W), lambda b, i: (b, 0, 0)),
    ]
    args = [sink, q, k_ctx, v_ctx]
    if local:
        nw = S // WINDOW
        nb = [pl.BlockSpec((None, WINDOW, KV_W), lambda b, i: (b, jnp.maximum(i * n_sub - 1, 0), 0)),
              pl.BlockSpec((None, tq, KV_W), lambda b, i: (b, i, 0)),
              pl.BlockSpec((None, WINDOW, KV_W),
                           lambda b, i: (b, jnp.minimum((i + 1) * n_sub, nw - 1), 0))]
        in_specs += nb + nb
        args += [k, k, k, v, v, v]
    tb = tq // n_sub
    nk = Lc + (3 * tb if local else 0)
    scratch = [pltpu.VMEM((n_sub, ATT_HEADS * tb, nk), F32),
               pltpu.VMEM((n_sub, ATT_HEADS * tb, nk + LANES), BF16)]
    if local:
        scratch.append(pltpu.VMEM((4, tb, tb), F32))
    return pl.pallas_call(
        functools.partial(_attn_kernel, local=local, n_sub=n_sub),
        grid=(B, nq),
        in_specs=in_specs,
        out_specs=pl.BlockSpec((None, tq, ATT_W), lambda b, i: (b, i, 0)),
        out_shape=jax.ShapeDtypeStruct((B, S, ATT_W), BF16),
        scratch_shapes=scratch,
        compiler_params=_params(2),
        name="attn_window" if local else "attn_ctx",
    )(*args)


def _gla_consts(rev):
    t = np.arange(GLA_CHUNK)[:, None]
    s = np.arange(GLA_CHUNK)[None, :]
    same = (t // GLA_SUB) == (s // GLA_SUB)
    before = (s >= t) if rev else (s <= t)
    return np.concatenate([same & before, before], axis=0).astype(np.float32)


def _head_blocks():
    h = np.arange(GLA_W) // HEAD_DIM
    return (h[:, None] == h[None, :]).astype(np.float32)


def _gla_last_row(blk, rev):
    return blk * GLA_SUB if rev else blk * GLA_SUB + GLA_SUB - 1


def _gla_prepare(q, k, g, cum, rev):
    n_sub = GLA_CHUNK // GLA_SUB
    g2 = g * LOG2_E
    g_hi = g2.astype(BF16)
    rem = g2 - g_hi.astype(F32)
    g_mid = rem.astype(BF16)
    g_lo = (rem - g_mid.astype(F32)).astype(BF16)
    cs = _dot(cum, g_hi) + _dot(cum, g_mid) + _dot(cum, g_lo)
    beta, bch = cs[0:GLA_CHUNK], cs[GLA_CHUNK:2 * GLA_CHUNK]
    gam = jnp.concatenate(
        [beta[_gla_last_row(b, rev):_gla_last_row(b, rev) + 1, :] - beta[b * GLA_SUB:(b + 1) * GLA_SUB, :]
         for b in range(n_sub)], axis=0)
    end_row = _gla_last_row(0 if rev else n_sub - 1, rev)
    dch = bch[end_row:end_row + 1, :] - bch
    f32_rows = jnp.concatenate([beta, bch, q * jnp.exp2(beta), k * jnp.exp2(gam)], axis=0)
    b16_rows = jnp.concatenate([(q * jnp.exp2(bch)).astype(BF16), (k * jnp.exp2(dch)).astype(BF16)],
                               axis=0)
    return f32_rows, b16_rows


def _gla_causal(rev):
    t = (np.arange(GLA_HEADS * GLA_CHUNK) % GLA_CHUNK)[:, None]
    s = np.arange(GLA_CHUNK)[None, :]
    return ((s >= t) if rev else (s <= t)).astype(np.float32)


def _gla_prepare_fast(q, k, v, g, cum_chunk, head_mask, rev):
    g2 = g * LOG2_E
    g_hi = g2.astype(BF16)
    rem = g2 - g_hi.astype(F32)
    g_mid = rem.astype(BF16)
    g_lo = (rem - g_mid.astype(F32)).astype(BF16)
    bch = _dot(cum_chunk, g_hi) + _dot(cum_chunk, g_mid) + _dot(cum_chunk, g_lo)
    end_row = 0 if rev else GLA_CHUNK - 1
    mid_row = GLA_CHUNK // 2 if rev else GLA_CHUNK // 2 - 1
    b_end = bch[end_row:end_row + 1, :]
    b_mid = bch[mid_row:mid_row + 1, :]
    ok = (jnp.max(-b_end) <= GLA_FAST_RANGE).astype(jnp.int32)
    q_mid = (q * jnp.exp2(jnp.minimum(bch - b_mid, GLA_FAST_RANGE))).astype(BF16)
    k_mid = (k * jnp.exp2(jnp.minimum(b_mid - bch, GLA_FAST_RANGE))).astype(BF16)
    zero = jnp.zeros_like(q_mid)
    rows = jnp.concatenate(
        [jnp.where(hm, q_mid, zero) for hm in head_mask]
        + [k_mid, (q * jnp.exp2(bch)).astype(BF16), (k * jnp.exp2(b_end - bch)).astype(BF16),
           v.astype(BF16)], axis=0)
    return rows, jnp.exp2(b_end), ok


def _gla_chunk_fast(pb_ref, dec, st_ref, o_ref, r0, bd, causal, head_mask):
    n_q = GLA_HEADS * GLA_CHUNK
    k_mid = pb_ref[n_q:n_q + GLA_CHUNK]
    q_int = pb_ref[n_q + GLA_CHUNK:n_q + 2 * GLA_CHUNK]
    k_dec = pb_ref[n_q + 2 * GLA_CHUNK:n_q + 3 * GLA_CHUNK]
    vb = pb_ref[n_q + 3 * GLA_CHUNK:n_q + 4 * GLA_CHUNK]
    st = st_ref[...]
    o_inter = _dot_nt(q_int, st.astype(BF16))
    upd = _dot_tn(vb, k_dec)
    a = _dot_nt(pb_ref[0:n_q], k_mid)
    yield
    r = _dot((a * causal).astype(BF16), vb)
    yield
    o = o_inter
    for hh, hm in enumerate(head_mask):
        o = o + jnp.where(hm, r[hh * GLA_CHUNK:(hh + 1) * GLA_CHUNK, :], 0.0)
    o_ref[pl.ds(r0, GLA_CHUNK), :] = o
    st_ref[...] = st * dec + upd * bd


def _gla_chunk(q, k, v, pf_ref, pb_ref, st_ref, o_ref, r0, ones, bd, head_mask, rev):
    n_sub = GLA_CHUNK // GLA_SUB
    order = list(range(n_sub))[::-1] if rev else list(range(n_sub))

    def rows(a, blk):
        return a[blk * GLA_SUB:(blk + 1) * GLA_SUB, :]

    half = GLA_SUB // 2
    beta = pf_ref[0:GLA_CHUNK]
    q_loc = pf_ref[2 * GLA_CHUNK:3 * GLA_CHUNK]
    k_loc = pf_ref[3 * GLA_CHUNK:4 * GLA_CHUNK]
    g_tot = [beta[_gla_last_row(b, rev):_gla_last_row(b, rev) + 1, :] for b in range(n_sub)]
    end_row = GLA_CHUNK + _gla_last_row(order[-1], rev)
    b_end = pf_ref[end_row:end_row + 1, :]
    half_row = lax.broadcasted_iota(jnp.int32, (half, 1), 0)

    st = st_ref[...]
    o_inter = _dot_nt(pb_ref[0:GLA_CHUNK], st.astype(BF16))
    upd = _dot_tn(v.astype(BF16), pb_ref[GLA_CHUNK:2 * GLA_CHUNK])
    scores, values = {}, {}
    for p in range(1, n_sub):
        ks, vs = [], []
        for pp in range(p):
            sb = order[pp]
            kk = rows(k_loc, sb)
            mids = [order[x] for x in range(pp + 1, p)]
            if mids:
                tot = g_tot[mids[0]]
                for mb in mids[1:]:
                    tot = tot + g_tot[mb]
                kk = kk * jnp.exp2(tot)
            ks.append(kk)
            vs.append(rows(v, sb))
        qt = rows(q_loc, order[p])
        q_heads = jnp.concatenate([jnp.where(hm, qt, 0.0) for hm in head_mask], axis=0)
        scores[p] = _dot_nt(q_heads.astype(BF16), jnp.concatenate(ks, axis=0).astype(BF16))
        values[p] = jnp.concatenate(vs, axis=0).astype(BF16)
    yield

    pairs = []
    for s in range(GLA_SUB):
        hs = s // half
        for hb in (range(0, hs + 1) if rev else range(hs, GLA_SUB // half)):
            pairs.append((s, hb))
    sums = {}
    for p, tb in enumerate(order):
        bt, qb, kb = rows(beta, tb), rows(q, tb), rows(k, tb)
        slabs = []
        for s, hb in pairs:
            d = bt[hb * half:(hb + 1) * half, :] - bt[s:s + 1, :]
            if hb == s // half:
                keep = (half_row <= s % half) if rev else (half_row >= s % half)
                d = jnp.where(keep, d, -jnp.inf)
            slabs.append(qb[hb * half:(hb + 1) * half, :] * jnp.exp2(d) * kb[s:s + 1, :])
        sums[p] = _dot(jnp.concatenate(slabs, axis=0).astype(BF16), ones)
        yield

    mixed = {p: _dot(scores[p].astype(BF16), values[p]) for p in range(1, n_sub)}
    yield

    for p, tb in enumerate(order):
        vb = rows(v, tb)
        acc = rows(o_inter, tb)
        if p > 0:
            for hh, hm in enumerate(head_mask):
                acc = acc + jnp.where(hm, mixed[p][hh * GLA_SUB:(hh + 1) * GLA_SUB, :], 0.0)
        halves = [acc[hb * half:(hb + 1) * half, :] for hb in range(GLA_SUB // half)]
        for idx, (s, hb) in enumerate(pairs):
            halves[hb] = halves[hb] + sums[p][idx * half:(idx + 1) * half, :] * vb[s:s + 1, :]
        o_ref[pl.ds(r0 + tb * GLA_SUB, GLA_SUB), :] = jnp.concatenate(halves, axis=0)
        yield

    st_ref[...] = st * jnp.exp2(b_end) + upd * bd


def _interleave(gens):
    active = list(gens)
    while active:
        for gen in list(active):
            try:
                next(gen)
            except StopIteration:
                active.remove(gen)


def _gla_kernel(cumf_ref, cumb_ref, ones_ref, bd_ref, causf_ref, causb_ref, glf_ref, glb_ref,
                gf_ref, gb_ref, s0f_ref, s0b_ref, of_ref, ob_ref, sff_ref, sfb_ref,
                stf_ref, stb_ref, pb_scr, dec_scr, ok_ref, *, n_chunks):
    i = pl.program_id(1)
    lane_head = lax.broadcasted_iota(jnp.int32, (1, GLA_W), 1) // HEAD_DIM
    head_mask = [lane_head == hh for hh in range(GLA_HEADS)]
    dirs = ((glf_ref, gf_ref, cumf_ref, stf_ref, of_ref, causf_ref, False),
            (glb_ref, gb_ref, cumb_ref, stb_ref, ob_ref, causb_ref, True))

    @pl.when(i == 0)
    def _():
        stf_ref[...] = s0f_ref[...]
        stb_ref[...] = s0b_ref[...]

    def load(gl_ref, g_ref, r0):
        rows = pl.ds(r0, GLA_CHUNK)
        return (gl_ref[rows, 0:GLA_W], gl_ref[rows, GLA_W:2 * GLA_W],
                gl_ref[rows, 2 * GLA_W:3 * GLA_W], g_ref[rows, :])

    for c in range(n_chunks):
        for d, (gl_ref, g_ref, cum_ref, _, _, _, rev) in enumerate(dirs):
            q, k, v, g = load(gl_ref, g_ref, c * GLA_CHUNK)
            rows, dec, ok = _gla_prepare_fast(q, k, v, g, cum_ref[GLA_CHUNK:2 * GLA_CHUNK, :],
                                              head_mask, rev)
            pb_scr[d, c] = rows
            dec_scr[d, c] = jnp.broadcast_to(dec, (8, GLA_W))
            ok_ref[d, c] = ok

    def chunk(ci, carry):
        cidx = (ci, n_chunks - 1 - ci)
        r0s = [pl.multiple_of(c * GLA_CHUNK, GLA_CHUNK) for c in cidx]
        fast = (ok_ref[0, cidx[0]] + ok_ref[1, cidx[1]]) == 2

        @pl.when(fast)
        def _():
            _interleave([
                _gla_chunk_fast(pb_scr.at[d, cidx[d]], dec_scr[d, cidx[d], 0:1, :], st_ref, o_ref,
                                r0s[d], bd_ref[...], caus_ref[...], head_mask)
                for d, (_, _, _, st_ref, o_ref, caus_ref, _) in enumerate(dirs)])

        @pl.when(jnp.logical_not(fast))
        def _():
            gens = []
            for d, (gl_ref, g_ref, cum_ref, st_ref, o_ref, _, rev) in enumerate(dirs):
                q, k, v, g = load(gl_ref, g_ref, r0s[d])
                pf, pb = _gla_prepare(q, k, g, cum_ref[...], rev)
                gens.append(_gla_chunk(q, k, v, pf, pb, st_ref, o_ref, r0s[d], ones_ref[...],
                                       bd_ref[...], head_mask, rev))
            _interleave(gens)

        return carry

    lax.fori_loop(0, n_chunks, chunk, 0)

    @pl.when(i == pl.num_programs(1) - 1)
    def _():
        sff_ref[...] = stf_ref[...]
        sfb_ref[...] = stb_ref[...]


def _gla(gl, gates, s0f, s0b, tile):
    B, S, _ = gl.shape
    nt = S // tile
    n_chunks = tile // GLA_CHUNK
    ones_bd = jnp.asarray(_head_blocks(), BF16)
    bd = jnp.asarray(_head_blocks(), F32)
    state_spec = pl.BlockSpec((None, GLA_W, GLA_W), lambda b, i: (b, 0, 0))
    return pl.pallas_call(
        functools.partial(_gla_kernel, n_chunks=n_chunks),
        grid=(B, nt),
        in_specs=[
            _const_spec((2 * GLA_CHUNK, GLA_CHUNK)),
            _const_spec((2 * GLA_CHUNK, GLA_CHUNK)),
            _const_spec((GLA_W, GLA_W)),
            _const_spec((GLA_W, GLA_W)),
            _const_spec((GLA_HEADS * GLA_CHUNK, GLA_CHUNK)),
            _const_spec((GLA_HEADS * GLA_CHUNK, GLA_CHUNK)),
            pl.BlockSpec((None, tile, 3 * GLA_W), lambda b, i: (b, i, 0)),
            pl.BlockSpec((None, tile, 3 * GLA_W), lambda b, i: (b, nt - 1 - i, 0)),
            pl.BlockSpec((None, tile, GLA_W), lambda b, i: (b, i, 0)),
            pl.BlockSpec((None, tile, GLA_W), lambda b, i: (b, nt - 1 - i, 1)),
            state_spec, state_spec,
        ],
        out_specs=[
            pl.BlockSpec((None, tile, GLA_W), lambda b, i: (b, i, 0)),
            pl.BlockSpec((None, tile, GLA_W), lambda b, i: (b, nt - 1 - i, 0)),
            state_spec, state_spec,
        ],
        out_shape=[jax.ShapeDtypeStruct((B, S, GLA_W), F32),
                   jax.ShapeDtypeStruct((B, S, GLA_W), F32),
                   jax.ShapeDtypeStruct((B, GLA_W, GLA_W), F32),
                   jax.ShapeDtypeStruct((B, GLA_W, GLA_W), F32)],
        scratch_shapes=[pltpu.VMEM((GLA_W, GLA_W), F32), pltpu.VMEM((GLA_W, GLA_W), F32),
                        pltpu.VMEM((2, n_chunks, (GLA_HEADS + 4) * GLA_CHUNK, GLA_W), BF16),
                        pltpu.VMEM((2, n_chunks, 8, GLA_W), F32),
                        pltpu.SMEM((2, n_chunks), jnp.int32)],
        compiler_params=_params(2),
        name="gla",
    )(jnp.asarray(_gla_consts(False), BF16), jnp.asarray(_gla_consts(True), BF16),
      ones_bd, bd, jnp.asarray(_gla_causal(False), F32), jnp.asarray(_gla_causal(True), F32),
      gl, gl, gates, gates, s0f, s0b)


def _post_kernel(x_ref, mod_ref, g2_ref, cu_ref, cup_ref, cun_ref, cb_ref, cw_ref, at_ref,
                 of_ref, ob_ref, gg_ref, ng_ref, ones_ref, wo_ref, wa_ref, wb_ref, wd_ref,
                 *rest, final):
    if final:
        fg_ref, o_ref, acc_ref, act_ref = rest
    else:
        o_ref, acc_ref, act_ref = rest
    i = pl.program_id(1)
    n = pl.num_programs(1)
    tm = x_ref.shape[0]

    u = cu_ref[...]
    row = lax.broadcasted_iota(jnp.int32, (tm, 1), 0)
    prev_row = jnp.where(i > 0, cup_ref[7:8, :], 0.0)
    next_row = jnp.where(i < n - 1, cun_ref[0:1, :], 0.0)
    u_prev = jnp.where(row == 0, prev_row, pltpu.roll(u, 1, axis=0))
    u_next = jnp.where(row == tm - 1, next_row, pltpu.roll(u, tm - 1, axis=0))
    conv = cb_ref[...] * (cw_ref[0:1, :] * u_prev + cw_ref[1:2, :] * u + cw_ref[2:3, :] * u_next)

    o = of_ref[...] + ob_ref[...]
    sq = o * o
    sq_hi = sq.astype(BF16)
    sq_lo = (sq - sq_hi.astype(F32)).astype(BF16)
    ms = (_dot(sq_hi, ones_ref[...]) + _dot(sq_lo, ones_ref[...])) * (1.0 / HEAD_DIM)
    gla = o * lax.rsqrt(ms + EPS) * ng_ref[...] * _silu(gg_ref[...])

    mix = (_dot(conv.astype(BF16), wo_ref[0:CONV_CH, :])
           + _dot(at_ref[...], wo_ref[CONV_CH:CONV_CH + ATT_W, :])
           + _dot(gla.astype(BF16), wo_ref[CONV_CH + ATT_W:, :]))
    x1 = x_ref[...] + mod_ref[2:3, :] * mix

    ms2 = jnp.mean(x1 * x1, axis=-1, keepdims=True)
    y2 = x1 * lax.rsqrt(ms2 + EPS) * g2_ref[...]
    h2 = (y2 * (1.0 + mod_ref[4:5, :]) + mod_ref[3:4, :]).astype(BF16)

    def up(jc):
        a = _dot(h2, wa_ref[jc])
        b = _dot(h2, wb_ref[jc])
        return a, b

    def act(ab):
        return (_silu(ab[0]) * ab[1]).astype(BF16)

    prev = act(up(0))
    ab = up(1)
    acc_ref[...] = _dot(prev, wd_ref[0])
    prev = act(ab)

    def ff(t, prev):
        for u in range(FF_UNROLL):
            jc = 2 + t * FF_UNROLL + u
            ab = up(jc)
            acc_ref[...] += _dot(prev, wd_ref[jc - 1])
            prev = act(ab)
        return prev

    prev = lax.fori_loop(0, (N_FF_CHUNKS - 2) // FF_UNROLL, ff, prev)
    x2 = x1 + mod_ref[5:6, :] * (acc_ref[...] + _dot(prev, wd_ref[N_FF_CHUNKS - 1]))
    if final:
        msf = jnp.mean(x2 * x2, axis=-1, keepdims=True)
        x2 = x2 * lax.rsqrt(msf + EPS) * fg_ref[...]
    o_ref[...] = x2


def _post(xs, mods_l, mod_row, g2, cu, cb, conv_w, attn, o_f, o_b, gl, norm_g, w_out, wa, wb, wd,
          final_g, tm):
    B, S, D = xs.shape
    nt = S // tm
    hb = tm // 8
    row = lambda w: pl.BlockSpec((None, tm, w), lambda b, i: (b, i, 0))
    if mod_row is None:
        mod_spec = pl.BlockSpec((None, 8, D), lambda b, i: (b, 0, 0))
    else:
        mod_spec = pl.BlockSpec((None, 8, D), lambda b, i: (mod_row, 0, 0))
    final = final_g is not None
    in_specs = [
        row(D), mod_spec, _const_spec((1, D)),
        row(CONV_CH),
        pl.BlockSpec((None, 8, CONV_CH), lambda b, i: (b, jnp.maximum(i * hb - 1, 0), 0)),
        pl.BlockSpec((None, 8, CONV_CH), lambda b, i: (b, jnp.minimum((i + 1) * hb, S // 8 - 1), 0)),
        row(CONV_CH), _const_spec((8, CONV_CH)), row(ATT_W), row(GLA_W), row(GLA_W),
        pl.BlockSpec((None, tm, GLA_W), lambda b, i: (b, i, 3)),
        _const_spec((1, GLA_W)), _const_spec((GLA_W, GLA_W)), _const_spec((D, D)),
        _const_spec((N_FF_CHUNKS, D, FF_CHUNK)), _const_spec((N_FF_CHUNKS, D, FF_CHUNK)),
        _const_spec((N_FF_CHUNKS, FF_CHUNK, D)),
    ]
    args = [xs, mods_l, g2, cu, cu, cu, cb, conv_w, attn, o_f, o_b, gl, norm_g,
            jnp.asarray(_head_blocks(), BF16), w_out, wa, wb, wd]
    if final:
        in_specs.append(_const_spec((1, D)))
        args.append(final_g)
    return pl.pallas_call(
        functools.partial(_post_kernel, final=final),
        grid=(B, nt),
        in_specs=in_specs,
        out_specs=row(D),
        out_shape=jax.ShapeDtypeStruct((B, S, D), F32),
        scratch_shapes=[pltpu.VMEM((tm, D), F32), pltpu.VMEM((tm, FF_CHUNK), BF16)],
        compiler_params=_params(2),
        name="post_final" if final else "post",
    )(*args)


def _rope_tables(seq):
    t = jnp.arange(seq)
    pos = jnp.stack([(t // GRID_W).astype(F32), (t % GRID_W).astype(F32)], axis=1)
    n_freq = HEAD_DIM // 4
    inv_freq = ROPE_BASE ** (-jnp.arange(n_freq, dtype=F32) / n_freq)
    ang = pos[:, :, None] * inv_freq
    cos, sin = jnp.cos(ang), jnp.sin(ang)
    zero = jnp.zeros_like(sin)
    cos_t = jnp.stack([cos, cos], axis=2).reshape(seq, HEAD_DIM)
    sa_t = jnp.stack([-sin, zero], axis=2).reshape(seq, HEAD_DIM)
    sb_t = jnp.stack([zero, sin], axis=2).reshape(seq, HEAD_DIM)
    rep = LANES // HEAD_DIM
    return tuple(jnp.tile(a, (1, rep)) for a in (cos_t, sa_t, sb_t))


def _tile_rows(S, want):
    t = min(want, S)
    while S % t:
        t //= 2
    return t


def kernel(x, c, ctx, c_ctx, w_mod, b_mod, norm1_g, norm2_g, w_in, conv_w, attn_sink,
           gla_gate_w, gla_gate_b, gla_norm_g, w_out, w_up, w_down, final_norm_g):
    B, S, D = x.shape
    L = w_mod.shape[0]
    Lc = ctx.shape[1]
    assert D == D_MODEL and S % WINDOW == 0 and Lc % GLA_CHUNK == 0 and B + 1 <= 8

    cvec = jnp.zeros((8, D), F32).at[:B].set(c).at[B].set(c_ctx)
    mods = _modulation(cvec, w_mod, b_mod).reshape(L, 8, N_MOD, D)
    mods = jnp.pad(mods, ((0, 0), (0, 0), (0, 8 - N_MOD), (0, 0)))

    order = np.array(Q_HEAD_ORDER)
    wq = w_in[:, :, 768:1280].reshape(L, D, ATT_HEADS, HEAD_DIM)[:, :, order].reshape(L, D, ATT_W)
    w_in_p = jnp.concatenate(
        [w_in[:, :, :768], wq, w_in[:, :, 1280:2592],
         jnp.zeros((L, D, LANES - 2 * GLA_RANK), F32)], axis=2).astype(BF16)
    wo_att = w_out[:, 256:768].reshape(L, ATT_HEADS, HEAD_DIM, D)[:, order].reshape(L, ATT_W, D)
    w_out_p = jnp.concatenate([w_out[:, :256], wo_att, w_out[:, 768:]], axis=1).astype(BF16)
    wg = jnp.zeros((L, LANES, 2 * GLA_W), F32)
    wg = wg.at[:, :GLA_RANK, :GLA_W].set(gla_gate_w[:, 0])
    wg = wg.at[:, GLA_RANK:2 * GLA_RANK, GLA_W:].set(gla_gate_w[:, 1]).astype(BF16)
    bg = gla_gate_b.reshape(L, 1, 2 * GLA_W)
    cw = jnp.pad(conv_w, ((0, 0), (0, 8 - CONV_K), (0, 0)))
    ng = jnp.tile(gla_norm_g, (1, GLA_HEADS)).reshape(L, 1, GLA_W)
    wa = w_up[:, :, :D_FF].reshape(L, D, N_FF_CHUNKS, FF_CHUNK).transpose(0, 2, 1, 3).astype(BF16)
    wb = w_up[:, :, D_FF:].reshape(L, D, N_FF_CHUNKS, FF_CHUNK).transpose(0, 2, 1, 3).astype(BF16)
    wd = w_down.reshape(L, N_FF_CHUNKS, FF_CHUNK, D).astype(BF16)
    g1 = norm1_g.reshape(L, 1, D)
    g2 = norm2_g.reshape(L, 1, D)

    rope_tabs = _rope_tables(S)
    tm = _tile_rows(S, 512)
    gla_tile = _tile_rows(S, 512)
    zero_state = jnp.zeros((B, GLA_W, GLA_W), F32)

    for l in range(L):
        last = l == L - 1
        cu, cb, q, k, v, gl, gates = _proj(x, mods[l], None, g1[l], w_in_p[l], wg[l], bg[l],
                                           rope_tabs, tm)
        ccu, ccb, cq, ck, cv, cgl, cgates = _proj(ctx, mods[l], B, g1[l], w_in_p[l], wg[l], bg[l],
                                                  None, Lc)
        attn = _attention(q, k, v, ck, cv, attn_sink[l], local=True)
        oc_f, oc_b, sc_f, sc_b = _gla(cgl, cgates, zero_state, zero_state, Lc)
        o_f, o_b, _, _ = _gla(gl, gates, sc_f, sc_b, gla_tile)
        fin = final_norm_g.reshape(1, D) if last else None
        x = _post(x, mods[l], None, g2[l], cu, cb, cw[l], attn, o_f, o_b, gl, ng[l],
                  w_out_p[l], wa[l], wb[l], wd[l], fin, tm)
        if not last:
            attn_c = _attention(cq, None, None, ck, cv, attn_sink[l], local=False)
            ctx = _post(ctx, mods[l], B, g2[l], ccu, ccb, cw[l], attn_c, oc_f, oc_b, cgl, ng[l],
                        w_out_p[l], wa[l], wb[l], wd[l], None, Lc)
    return x
```

```python
import functools

import numpy as np
import jax
import jax.numpy as jnp
from jax import lax
from jax.experimental import pallas as pl
from jax.experimental.pallas import tpu as pltpu

F32 = jnp.float32
BF16 = jnp.bfloat16

D_MODEL = 1024
HEAD_DIM = 64
CONV_CH = 256
CONV_K = 3
ATT_HEADS = 8
ATT_KV_HEADS = 2
ATT_W = ATT_HEADS * HEAD_DIM
KV_W = ATT_KV_HEADS * HEAD_DIM
WINDOW = 128
GLA_HEADS = 4
GLA_W = GLA_HEADS * HEAD_DIM
GLA_RANK = 16
GLA_NORMALIZER = 16.0
D_FF = 2816
N_MOD = 6
GRID_W = 64
ROPE_BASE = 10000.0
EPS = 1e-6
LOG2_E = 1.4426950408889634

LANES = 128
C_CONV = 0
C_QK = 768
C_V = 1408
C_LR = 1536
C_GLA = 1664
N_IN_PAD = C_GLA + 4 * GLA_W
Q_HEAD_ORDER = (0, 4, 1, 5, 2, 6, 3, 7)

ATT_ROW_GROUP = 16
GLA_CHUNK = 64
GLA_SUB = 16
GLA_FAST_RANGE = 60.0
FF_CHUNK = 256
N_FF_CHUNKS = D_FF // FF_CHUNK
FF_UNROLL = 3
assert (N_FF_CHUNKS - 2) % FF_UNROLL == 0

VMEM_LIMIT = 56 * 1024 * 1024


def _const_spec(shape):
    nd = len(shape)
    return pl.BlockSpec(shape, lambda *_: (0,) * nd, pipeline_mode=pl.Buffered(1))


def _params(n_grid):
    return pltpu.CompilerParams(
        dimension_semantics=("arbitrary",) * n_grid, vmem_limit_bytes=VMEM_LIMIT)


def _silu(a):
    return a * (1.0 / (1.0 + jnp.exp(-a)))


def _dot(a, b):
    return jnp.dot(a, b, preferred_element_type=F32)


def _dot_nt(a, b):
    return lax.dot_general(a, b, (((1,), (1,)), ((), ())), preferred_element_type=F32)


def _dot_tn(a, b):
    return lax.dot_general(a, b, (((0,), (0,)), ((), ())), preferred_element_type=F32)


def _mod_kernel(c_ref, w_ref, b_ref, o_ref):
    a = _silu(c_ref[...]).astype(BF16)
    o_ref[...] = _dot(a, w_ref[...].astype(BF16)) + b_ref[...]


def _modulation(cvec, w_mod, b_mod):
    L, D, W = w_mod.shape
    tn = 2048
    return pl.pallas_call(
        _mod_kernel,
        grid=(L, W // tn),
        in_specs=[
            pl.BlockSpec((8, D), lambda l, j: (0, 0)),
            pl.BlockSpec((None, D, tn), lambda l, j: (l, 0, j)),
            pl.BlockSpec((None, 1, tn), lambda l, j: (l, 0, j)),
        ],
        out_specs=pl.BlockSpec((None, 8, tn), lambda l, j: (l, 0, j)),
        out_shape=jax.ShapeDtypeStruct((L, 8, W), F32),
        compiler_params=_params(2),
        name="modulation",
    )(cvec, w_mod, b_mod.reshape(L, 1, W))


def _proj_kernel(x_ref, mod_ref, g1_ref, w_ref, wg_ref, bg_ref, *rest, rope):
    if rope:
        cos_ref, sa_ref, sb_ref = rest[:3]
        rest = rest[3:]
    cu_ref, cb_ref, q_ref, k_ref, v_ref, gl_ref, gate_ref = rest

    x = x_ref[...]
    ms = jnp.mean(x * x, axis=-1, keepdims=True)
    y = x * lax.rsqrt(ms + EPS) * g1_ref[...]
    h = (y * (1.0 + mod_ref[1:2, :]) + mod_ref[0:1, :]).astype(BF16)

    def proj(lo, hi):
        return _dot(h, w_ref[:, lo:hi])

    pc = proj(C_CONV, C_QK)
    cu_ref[...] = pc[:, 2 * CONV_CH:3 * CONV_CH] * pc[:, 0:CONV_CH]
    cb_ref[...] = pc[:, CONV_CH:2 * CONV_CH]

    pqk = proj(C_QK, C_GLA)
    scale = HEAD_DIM ** -0.5
    for m in range((ATT_W + KV_W) // LANES):
        col = pqk[:, m * LANES:(m + 1) * LANES]
        if rope:
            col = (col * cos_ref[...]
                   + pltpu.roll(col, LANES - HEAD_DIM // 4, axis=1) * sa_ref[...]
                   + pltpu.roll(col, HEAD_DIM // 4, axis=1) * sb_ref[...])
        if m < ATT_W // LANES:
            q_ref[:, m * LANES:(m + 1) * LANES] = (col * scale).astype(BF16)
        else:
            k_ref[...] = col.astype(BF16)

    v_ref[...] = pqk[:, C_V - C_QK:C_LR - C_QK].astype(BF16)

    pg = proj(C_GLA, N_IN_PAD)
    gl_ref[:, 0:GLA_W] = pg[:, 0:GLA_W] * (HEAD_DIM ** -0.5)
    gl_ref[:, GLA_W:4 * GLA_W] = pg[:, GLA_W:4 * GLA_W]

    lr = pqk[:, C_LR - C_QK:C_GLA - C_QK].astype(BF16)
    z = _dot(lr, wg_ref[...]) + bg_ref[...]
    log_sig = jnp.minimum(z, 0.0) - jnp.log1p(jnp.exp(-jnp.abs(z)))
    gate_ref[...] = log_sig * (1.0 / GLA_NORMALIZER)


def _proj(xs, mods_l, mod_row, g1, w_in, wg, bg, rope_tabs, tm):
    B, S, D = xs.shape
    rope = rope_tabs is not None
    nt = S // tm
    row = lambda w: pl.BlockSpec((None, tm, w), lambda b, i: (b, i, 0))
    if mod_row is None:
        mod_spec = pl.BlockSpec((None, 8, D), lambda b, i: (b, 0, 0))
    else:
        mod_spec = pl.BlockSpec((None, 8, D), lambda b, i: (mod_row, 0, 0))
    in_specs = [row(D), mod_spec, _const_spec((1, D)), _const_spec((D, N_IN_PAD)),
                _const_spec((LANES, 2 * GLA_W)), _const_spec((1, 2 * GLA_W))]
    args = [xs, mods_l, g1, w_in, wg, bg]
    if rope:
        in_specs += [pl.BlockSpec((tm, LANES), lambda b, i: (i, 0))] * 3
        args += list(rope_tabs)
    widths = (CONV_CH, CONV_CH, ATT_W, KV_W, KV_W, 4 * GLA_W, 2 * GLA_W)
    dtypes = (F32, F32, BF16, BF16, BF16, F32, F32)
    return pl.pallas_call(
        functools.partial(_proj_kernel, rope=rope),
        grid=(B, nt),
        in_specs=in_specs,
        out_specs=[row(w) for w in widths],
        out_shape=[jax.ShapeDtypeStruct((B, S, w), dt) for w, dt in zip(widths, dtypes)],
        compiler_params=_params(2),
        name="proj_rope" if rope else "proj_ctx",
    )(*args)


def _attn_kernel(sink_ref, q_ref, kc_ref, vc_ref, *rest, local, n_sub):
    tb = q_ref.shape[0] // n_sub
    n_pairs = ATT_W // LANES
    n_ctx = kc_ref.shape[0]
    rg = ATT_ROW_GROUP
    if local:
        kp_ref, km_ref, kn_ref, vp_ref, vm_ref, vn_ref, o_ref, s_scr, p_scr, bias_scr = rest
        i = pl.program_id(1)
        n = pl.num_programs(1)
        k_loc = jnp.concatenate([kp_ref[...], km_ref[...], kn_ref[...]], axis=0)
        v_loc = jnp.concatenate([vp_ref[...], vm_ref[...], vn_ref[...]], axis=0)
        r = lax.broadcasted_iota(jnp.int32, (tb, tb), 0)
        j = lax.broadcasted_iota(jnp.int32, (tb, tb), 1)
        bias_prev = jnp.where(j >= r, 0.0, -jnp.inf)
        bias_next = jnp.where(j <= r, 0.0, -jnp.inf)
        bias_scr[0] = bias_prev + jnp.where(i > 0, 0.0, -jnp.inf)
        bias_scr[1] = bias_prev
        bias_scr[2] = bias_next
        bias_scr[3] = bias_next + jnp.where(i < n - 1, 0.0, -jnp.inf)
        nk = n_ctx + 3 * tb
    else:
        o_ref, s_scr, p_scr = rest
        nk = n_ctx
    lane = lax.broadcasted_iota(jnp.int32, (1, LANES), 1)
    lo = lane < HEAD_DIM
    hi = jnp.logical_not(lo)
    first_row = lax.broadcasted_iota(jnp.int32, (LANES, LANES), 0) == 0
    v_tail = jnp.concatenate([jnp.zeros((LANES, LANES), BF16),
                              jnp.where(first_row, 1.0, 0.0).astype(BF16)], axis=1)
    ones_blk = jnp.ones((nk, LANES), BF16)

    for jb in range(n_sub):
        q = q_ref[jb * tb:(jb + 1) * tb, :]
        zero = jnp.zeros((tb, LANES), BF16)
        lhs = jnp.concatenate(
            [jnp.where(lo if g == 0 else hi, q[:, m * LANES:(m + 1) * LANES], zero)
             for m in range(n_pairs) for g in range(ATT_KV_HEADS)], axis=0)
        if local:
            kk = jnp.concatenate([kc_ref[...], k_loc[jb * tb:(jb + 3) * tb]], axis=0)
            vv = jnp.concatenate([vc_ref[...], v_loc[jb * tb:(jb + 3) * tb]], axis=0)
        else:
            kk, vv = kc_ref[...], vc_ref[...]
        v_aug = jnp.concatenate([jnp.concatenate([vv, ones_blk], axis=1), v_tail], axis=0)
        s_scr[jb] = _dot_nt(lhs, kk)

        for hs in range(ATT_HEADS):
            m, g = hs // ATT_KV_HEADS, hs % ATT_KV_HEADS
            sink = sink_ref[m + (ATT_HEADS // ATT_KV_HEADS) * g]
            sink_tile = jnp.where(lane == 0, sink, -jnp.inf) + jnp.zeros((rg, LANES), F32)
            for gi in range(tb // rg):
                rq = gi * rg
                r0 = hs * tb + rq
                s = s_scr[jb, r0:r0 + rg, :]
                if local:
                    bp = bias_scr[0 if jb == 0 else 1, rq:rq + rg, :]
                    bn = bias_scr[3 if jb == n_sub - 1 else 2, rq:rq + rg, :]
                    cols = [s[:, :n_ctx], s[:, n_ctx:n_ctx + tb] + bp,
                            s[:, n_ctx + tb:n_ctx + 2 * tb], s[:, n_ctx + 2 * tb:] + bn, sink_tile]
                else:
                    cols = [s, sink_tile]
                s_all = jnp.concatenate(cols, axis=1)
                mx = jnp.max(s_all, axis=-1, keepdims=True)
                p_scr[jb, r0:r0 + rg, :] = jnp.exp(s_all - mx).astype(BF16)

        o_aug = _dot(p_scr[jb], v_aug)
        o = o_aug[:, :LANES] * (1.0 / o_aug[:, LANES:])
        for m in range(n_pairs):
            o_ref[jb * tb:(jb + 1) * tb, m * LANES:(m + 1) * LANES] = jnp.where(
                lo, o[(2 * m) * tb:(2 * m + 1) * tb], o[(2 * m + 1) * tb:(2 * m + 2) * tb]).astype(BF16)


def _attention(q, k, v, k_ctx, v_ctx, sink, local, n_sub=4):
    B, S, _ = q.shape
    Lc = k_ctx.shape[1]
    if not local:
        n_sub = 1
    tq = WINDOW * n_sub if local else S
    nq = S // tq
    in_specs = [
        pl.BlockSpec(memory_space=pltpu.SMEM),
        pl.BlockSpec((None, tq, ATT_W), lambda b, i: (b, i, 0)),
        pl.BlockSpec((None, Lc, KV_W), lambda b, i: (b, 0, 0)),
        pl.BlockSpec((None, Lc, KV_W), lambda b, i: (b, 0, 0)),
    ]
    args = [sink, q, k_ctx, v_ctx]
    if local:
        nw = S // WINDOW
        nb = [pl.BlockSpec((None, WINDOW, KV_W), lambda b, i: (b, jnp.maximum(i * n_sub - 1, 0), 0)),
              pl.BlockSpec((None, tq, KV_W), lambda b, i: (b, i, 0)),
              pl.BlockSpec((None, WINDOW, KV_W),
                           lambda b, i: (b, jnp.minimum((i + 1) * n_sub, nw - 1), 0))]
        in_specs += nb + nb
        args += [k, k, k, v, v, v]
    tb = tq // n_sub
    nk = Lc + (3 * tb if local else 0)
    scratch = [pltpu.VMEM((n_sub, ATT_HEADS * tb, nk), F32),
               pltpu.VMEM((n_sub, ATT_HEADS * tb, nk + LANES), BF16)]
    if local:
        scratch.append(pltpu.VMEM((4, tb, tb), F32))
    return pl.pallas_call(
        functools.partial(_attn_kernel, local=local, n_sub=n_sub),
        grid=(B, nq),
        in_specs=in_specs,
        out_specs=pl.BlockSpec((None, tq, ATT_W), lambda b, i: (b, i, 0)),
        out_shape=jax.ShapeDtypeStruct((B, S, ATT_W), BF16),
        scratch_shapes=scratch,
        compiler_params=_params(2),
        name="attn_window" if local else "attn_ctx",
    )(*args)


def _gla_consts(rev):
    t = np.arange(GLA_CHUNK)[:, None]
    s = np.arange(GLA_CHUNK)[None, :]
    same = (t // GLA_SUB) == (s // GLA_SUB)
    before = (s >= t) if rev else (s <= t)
    return np.concatenate([same & before, before], axis=0).astype(np.float32)


def _head_blocks():
    h = np.arange(GLA_W) // HEAD_DIM
    return (h[:, None] == h[None, :]).astype(np.float32)


def _gla_last_row(blk, rev):
    return blk * GLA_SUB if rev else blk * GLA_SUB + GLA_SUB - 1


def _gla_prepare(q, k, g, cum, rev):
    n_sub = GLA_CHUNK // GLA_SUB
    g2 = g * LOG2_E
    g_hi = g2.astype(BF16)
    rem = g2 - g_hi.astype(F32)
    g_mid = rem.astype(BF16)
    g_lo = (rem - g_mid.astype(F32)).astype(BF16)
    cs = _dot(cum, g_hi) + _dot(cum, g_mid) + _dot(cum, g_lo)
    beta, bch = cs[0:GLA_CHUNK], cs[GLA_CHUNK:2 * GLA_CHUNK]
    gam = jnp.concatenate(
        [beta[_gla_last_row(b, rev):_gla_last_row(b, rev) + 1, :] - beta[b * GLA_SUB:(b + 1) * GLA_SUB, :]
         for b in range(n_sub)], axis=0)
    end_row = _gla_last_row(0 if rev else n_sub - 1, rev)
    dch = bch[end_row:end_row + 1, :] - bch
    f32_rows = jnp.concatenate([beta, bch, q * jnp.exp2(beta), k * jnp.exp2(gam)], axis=0)
    b16_rows = jnp.concatenate([(q * jnp.exp2(bch)).astype(BF16), (k * jnp.exp2(dch)).astype(BF16)],
                               axis=0)
    return f32_rows, b16_rows


def _gla_causal(rev):
    t = (np.arange(GLA_HEADS * GLA_CHUNK) % GLA_CHUNK)[:, None]
    s = np.arange(GLA_CHUNK)[None, :]
    return ((s >= t) if rev else (s <= t)).astype(np.float32)


def _gla_prepare_fast(q, k, v, g, cum_chunk, head_mask, rev):
    g2 = g * LOG2_E
    g_hi = g2.astype(BF16)
    rem = g2 - g_hi.astype(F32)
    g_mid = rem.astype(BF16)
    g_lo = (rem - g_mid.astype(F32)).astype(BF16)
    bch = _dot(cum_chunk, g_hi) + _dot(cum_chunk, g_mid) + _dot(cum_chunk, g_lo)
    end_row = 0 if rev else GLA_CHUNK - 1
    mid_row = GLA_CHUNK // 2 if rev else GLA_CHUNK // 2 - 1
    b_end = bch[end_row:end_row + 1, :]
    b_mid = bch[mid_row:mid_row + 1, :]
    ok = (jnp.max(-b_end) <= GLA_FAST_RANGE).astype(jnp.int32)
    q_mid = (q * jnp.exp2(jnp.minimum(bch - b_mid, GLA_FAST_RANGE))).astype(BF16)
    k_mid = (k * jnp.exp2(jnp.minimum(b_mid - bch, GLA_FAST_RANGE))).astype(BF16)
    zero = jnp.zeros_like(q_mid)
    rows = jnp.concatenate(
        [jnp.where(hm, q_mid, zero) for hm in head_mask]
        + [k_mid, (q * jnp.exp2(bch)).astype(BF16), (k * jnp.exp2(b_end - bch)).astype(BF16),
           v.astype(BF16)], axis=0)
    return rows, jnp.exp2(b_end), ok


def _gla_chunk_fast(pb_ref, dec, st_ref, o_ref, r0, bd, causal, head_mask):
    n_q = GLA_HEADS * GLA_CHUNK
    k_mid = pb_ref[n_q:n_q + GLA_CHUNK]
    q_int = pb_ref[n_q + GLA_CHUNK:n_q + 2 * GLA_CHUNK]
    k_dec = pb_ref[n_q + 2 * GLA_CHUNK:n_q + 3 * GLA_CHUNK]
    vb = pb_ref[n_q + 3 * GLA_CHUNK:n_q + 4 * GLA_CHUNK]
    st = st_ref[...]
    o_inter = _dot_nt(q_int, st.astype(BF16))
    upd = _dot_tn(vb, k_dec)
    a = _dot_nt(pb_ref[0:n_q], k_mid)
    yield
    r = _dot((a * causal).astype(BF16), vb)
    yield
    o = o_inter
    for hh, hm in enumerate(head_mask):
        o = o + jnp.where(hm, r[hh * GLA_CHUNK:(hh + 1) * GLA_CHUNK, :], 0.0)
    o_ref[pl.ds(r0, GLA_CHUNK), :] = o
    st_ref[...] = st * dec + upd * bd


def _gla_chunk(q, k, v, pf_ref, pb_ref, st_ref, o_ref, r0, ones, bd, head_mask, rev):
    n_sub = GLA_CHUNK // GLA_SUB
    order = list(range(n_sub))[::-1] if rev else list(range(n_sub))

    def rows(a, blk):
        return a[blk * GLA_SUB:(blk + 1) * GLA_SUB, :]

    half = GLA_SUB // 2
    beta = pf_ref[0:GLA_CHUNK]
    q_loc = pf_ref[2 * GLA_CHUNK:3 * GLA_CHUNK]
    k_loc = pf_ref[3 * GLA_CHUNK:4 * GLA_CHUNK]
    g_tot = [beta[_gla_last_row(b, rev):_gla_last_row(b, rev) + 1, :] for b in range(n_sub)]
    end_row = GLA_CHUNK + _gla_last_row(order[-1], rev)
    b_end = pf_ref[end_row:end_row + 1, :]
    half_row = lax.broadcasted_iota(jnp.int32, (half, 1), 0)

    st = st_ref[...]
    o_inter = _dot_nt(pb_ref[0:GLA_CHUNK], st.astype(BF16))
    upd = _dot_tn(v.astype(BF16), pb_ref[GLA_CHUNK:2 * GLA_CHUNK])
    scores, values = {}, {}
    for p in range(1, n_sub):
        ks, vs = [], []
        for pp in range(p):
            sb = order[pp]
            kk = rows(k_loc, sb)
            mids = [order[x] for x in range(pp + 1, p)]
            if mids:
                tot = g_tot[mids[0]]
                for mb in mids[1:]:
                    tot = tot + g_tot[mb]
                kk = kk * jnp.exp2(tot)
            ks.append(kk)
            vs.append(rows(v, sb))
        qt = rows(q_loc, order[p])
        q_heads = jnp.concatenate([jnp.where(hm, qt, 0.0) for hm in head_mask], axis=0)
        scores[p] = _dot_nt(q_heads.astype(BF16), jnp.concatenate(ks, axis=0).astype(BF16))
        values[p] = jnp.concatenate(vs, axis=0).astype(BF16)
    yield

    pairs = []
    for s in range(GLA_SUB):
        hs = s // half
        for hb in (range(0, hs + 1) if rev else range(hs, GLA_SUB // half)):
            pairs.append((s, hb))
    sums = {}
    for p, tb in enumerate(order):
        bt, qb, kb = rows(beta, tb), rows(q, tb), rows(k, tb)
        slabs = []
        for s, hb in pairs:
            d = bt[hb * half:(hb + 1) * half, :] - bt[s:s + 1, :]
            if hb == s // half:
                keep = (half_row <= s % half) if rev else (half_row >= s % half)
                d = jnp.where(keep, d, -jnp.inf)
            slabs.append(qb[hb * half:(hb + 1) * half, :] * jnp.exp2(d) * kb[s:s + 1, :])
        sums[p] = _dot(jnp.concatenate(slabs, axis=0).astype(BF16), ones)
        yield

    mixed = {p: _dot(scores[p].astype(BF16), values[p]) for p in range(1, n_sub)}
    yield

    for p, tb in enumerate(order):
        vb = rows(v, tb)
        acc = rows(o_inter, tb)
        if p > 0:
            for hh, hm in enumerate(head_mask):
                acc = acc + jnp.where(hm, mixed[p][hh * GLA_SUB:(hh + 1) * GLA_SUB, :], 0.0)
        halves = [acc[hb * half:(hb + 1) * half, :] for hb in range(GLA_SUB // half)]
        for idx, (s, hb) in enumerate(pairs):
            halves[hb] = halves[hb] + sums[p][idx * half:(idx + 1) * half, :] * vb[s:s + 1, :]
        o_ref[pl.ds(r0 + tb * GLA_SUB, GLA_SUB), :] = jnp.concatenate(halves, axis=0)
        yield

    st_ref[...] = st * jnp.exp2(b_end) + upd * bd


def _interleave(gens):
    active = list(gens)
    while active:
        for gen in list(active):
            try:
                next(gen)
            except StopIteration:
                active.remove(gen)


def _gla_kernel(cumf_ref, cumb_ref, ones_ref, bd_ref, causf_ref, causb_ref, glf_ref, glb_ref,
                gf_ref, gb_ref, s0f_ref, s0b_ref, of_ref, ob_ref, sff_ref, sfb_ref,
                stf_ref, stb_ref, pb_scr, dec_scr, ok_ref, *, n_chunks):
    i = pl.program_id(1)
    lane_head = lax.broadcasted_iota(jnp.int32, (1, GLA_W), 1) // HEAD_DIM
    head_mask = [lane_head == hh for hh in range(GLA_HEADS)]
    dirs = ((glf_ref, gf_ref, cumf_ref, stf_ref, of_ref, causf_ref, False),
            (glb_ref, gb_ref, cumb_ref, stb_ref, ob_ref, causb_ref, True))

    @pl.when(i == 0)
    def _():
        stf_ref[...] = s0f_ref[...]
        stb_ref[...] = s0b_ref[...]

    def load(gl_ref, g_ref, r0):
        rows = pl.ds(r0, GLA_CHUNK)
        return (gl_ref[rows, 0:GLA_W], gl_ref[rows, GLA_W:2 * GLA_W],
                gl_ref[rows, 2 * GLA_W:3 * GLA_W], g_ref[rows, :])

    for c in range(n_chunks):
        for d, (gl_ref, g_ref, cum_ref, _, _, _, rev) in enumerate(dirs):
            q, k, v, g = load(gl_ref, g_ref, c * GLA_CHUNK)
            rows, dec, ok = _gla_prepare_fast(q, k, v, g, cum_ref[GLA_CHUNK:2 * GLA_CHUNK, :],
                                              head_mask, rev)
            pb_scr[d, c] = rows
            dec_scr[d, c] = jnp.broadcast_to(dec, (8, GLA_W))
            ok_ref[d, c] = ok

    def chunk(ci, carry):
        cidx = (ci, n_chunks - 1 - ci)
        r0s = [pl.multiple_of(c * GLA_CHUNK, GLA_CHUNK) for c in cidx]
        fast = (ok_ref[0, cidx[0]] + ok_ref[1, cidx[1]]) == 2

        @pl.when(fast)
        def _():
            _interleave([
                _gla_chunk_fast(pb_scr.at[d, cidx[d]], dec_scr[d, cidx[d], 0:1, :], st_ref, o_ref,
                                r0s[d], bd_ref[...], caus_ref[...], head_mask)
                for d, (_, _, _, st_ref, o_ref, caus_ref, _) in enumerate(dirs)])

        @pl.when(jnp.logical_not(fast))
        def _():
            gens = []
            for d, (gl_ref, g_ref, cum_ref, st_ref, o_ref, _, rev) in enumerate(dirs):
                q, k, v, g = load(gl_ref, g_ref, r0s[d])
                pf, pb = _gla_prepare(q, k, g, cum_ref[...], rev)
                gens.append(_gla_chunk(q, k, v, pf, pb, st_ref, o_ref, r0s[d], ones_ref[...],
                                       bd_ref[...], head_mask, rev))
            _interleave(gens)

        return carry

    lax.fori_loop(0, n_chunks, chunk, 0)

    @pl.when(i == pl.num_programs(1) - 1)
    def _():
        sff_ref[...] = stf_ref[...]
        sfb_ref[...] = stb_ref[...]


def _gla(gl, gates, s0f, s0b, tile):
    B, S, _ = gl.shape
    nt = S // tile
    n_chunks = tile // GLA_CHUNK
    ones_bd = jnp.asarray(_head_blocks(), BF16)
    bd = jnp.asarray(_head_blocks(), F32)
    state_spec = pl.BlockSpec((None, GLA_W, GLA_W), lambda b, i: (b, 0, 0))
    return pl.pallas_call(
        functools.partial(_gla_kernel, n_chunks=n_chunks),
        grid=(B, nt),
        in_specs=[
            _const_spec((2 * GLA_CHUNK, GLA_CHUNK)),
            _const_spec((2 * GLA_CHUNK, GLA_CHUNK)),
            _const_spec((GLA_W, GLA_W)),
            _const_spec((GLA_W, GLA_W)),
            _const_spec((GLA_HEADS * GLA_CHUNK, GLA_CHUNK)),
            _const_spec((GLA_HEADS * GLA_CHUNK, GLA_CHUNK)),
            pl.BlockSpec((None, tile, 3 * GLA_W), lambda b, i: (b, i, 0)),
            pl.BlockSpec((None, tile, 3 * GLA_W), lambda b, i: (b, nt - 1 - i, 0)),
            pl.BlockSpec((None, tile, GLA_W), lambda b, i: (b, i, 0)),
            pl.BlockSpec((None, tile, GLA_W), lambda b, i: (b, nt - 1 - i, 1)),
            state_spec, state_spec,
        ],
        out_specs=[
            pl.BlockSpec((None, tile, GLA_W), lambda b, i: (b, i, 0)),
            pl.BlockSpec((None, tile, GLA_W), lambda b, i: (b, nt - 1 - i, 0)),
            state_spec, state_spec,
        ],
        out_shape=[jax.ShapeDtypeStruct((B, S, GLA_W), F32),
                   jax.ShapeDtypeStruct((B, S, GLA_W), F32),
                   jax.ShapeDtypeStruct((B, GLA_W, GLA_W), F32),
                   jax.ShapeDtypeStruct((B, GLA_W, GLA_W), F32)],
        scratch_shapes=[pltpu.VMEM((GLA_W, GLA_W), F32), pltpu.VMEM((GLA_W, GLA_W), F32),
                        pltpu.VMEM((2, n_chunks, (GLA_HEADS + 4) * GLA_CHUNK, GLA_W), BF16),
                        pltpu.VMEM((2, n_chunks, 8, GLA_W), F32),
                        pltpu.SMEM((2, n_chunks), jnp.int32)],
        compiler_params=_params(2),
        name="gla",
    )(jnp.asarray(_gla_consts(False), BF16), jnp.asarray(_gla_consts(True), BF16),
      ones_bd, bd, jnp.asarray(_gla_causal(False), F32), jnp.asarray(_gla_causal(True), F32),
      gl, gl, gates, gates, s0f, s0b)


def _post_kernel(x_ref, mod_ref, g2_ref, cu_ref, cup_ref, cun_ref, cb_ref, cw_ref, at_ref,
                 of_ref, ob_ref, gg_ref, ng_ref, ones_ref, wo_ref, wu_ref, wd_ref,
                 *rest, final):
    if final:
        fg_ref, o_ref, acc_ref = rest
    else:
        o_ref, acc_ref = rest
    i = pl.program_id(1)
    n = pl.num_programs(1)
    tm = x_ref.shape[0]

    u = cu_ref[...]
    row = lax.broadcasted_iota(jnp.int32, (tm, 1), 0)
    prev_row = jnp.where(i > 0, cup_ref[7:8, :], 0.0)
    next_row = jnp.where(i < n - 1, cun_ref[0:1, :], 0.0)
    u_prev = jnp.where(row == 0, prev_row, pltpu.roll(u, 1, axis=0))
    u_next = jnp.where(row == tm - 1, next_row, pltpu.roll(u, tm - 1, axis=0))
    conv = cb_ref[...] * (cw_ref[0:1, :] * u_prev + cw_ref[1:2, :] * u + cw_ref[2:3, :] * u_next)

    o = of_ref[...] + ob_ref[...]
    sq = o * o
    sq_hi = sq.astype(BF16)
    sq_lo = (sq - sq_hi.astype(F32)).astype(BF16)
    ms = (_dot(sq_hi, ones_ref[...]) + _dot(sq_lo, ones_ref[...])) * (1.0 / HEAD_DIM)
    gla = o * lax.rsqrt(ms + EPS) * ng_ref[...] * _silu(gg_ref[...])

    mix = (_dot(conv.astype(BF16), wo_ref[0:CONV_CH, :])
           + _dot(at_ref[...], wo_ref[CONV_CH:CONV_CH + ATT_W, :])
           + _dot(gla.astype(BF16), wo_ref[CONV_CH + ATT_W:, :]))
    x1 = x_ref[...] + mod_ref[2:3, :] * mix

    ms2 = jnp.mean(x1 * x1, axis=-1, keepdims=True)
    y2 = x1 * lax.rsqrt(ms2 + EPS) * g2_ref[...]
    h2 = (y2 * (1.0 + mod_ref[4:5, :]) + mod_ref[3:4, :]).astype(BF16)

    def up(jc):
        if isinstance(jc, int):
            ca, cb = jc * FF_CHUNK, D_FF + jc * FF_CHUNK
        else:
            ca = pl.multiple_of(jc * FF_CHUNK, FF_CHUNK)
            cb = pl.multiple_of(D_FF + jc * FF_CHUNK, FF_CHUNK)
        a = _dot(h2, wu_ref[:, pl.ds(ca, FF_CHUNK)])
        b = _dot(h2, wu_ref[:, pl.ds(cb, FF_CHUNK)])
        return a, b

    def act(ab):
        return (_silu(ab[0]) * ab[1]).astype(BF16)

    prev = act(up(0))
    ab = up(1)
    acc_ref[...] = _dot(prev, wd_ref[0])
    prev = act(ab)

    def ff(t, prev):
        for u in range(FF_UNROLL):
            jc = 2 + t * FF_UNROLL + u
            ab = up(jc)
            acc_ref[...] += _dot(prev, wd_ref[jc - 1])
            prev = act(ab)
        return prev

    prev = lax.fori_loop(0, (N_FF_CHUNKS - 2) // FF_UNROLL, ff, prev)
    x2 = x1 + mod_ref[5:6, :] * (acc_ref[...] + _dot(prev, wd_ref[N_FF_CHUNKS - 1]))
    if final:
        msf = jnp.mean(x2 * x2, axis=-1, keepdims=True)
        x2 = x2 * lax.rsqrt(msf + EPS) * fg_ref[...]
    o_ref[...] = x2


def _post(xs, mods_l, mod_row, g2, cu, cb, conv_w, attn, o_f, o_b, gl, norm_g, w_out, wu, wd,
          final_g, tm):
    B, S, D = xs.shape
    nt = S // tm
    hb = tm // 8
    row = lambda w: pl.BlockSpec((None, tm, w), lambda b, i: (b, i, 0))
    if mod_row is None:
        mod_spec = pl.BlockSpec((None, 8, D), lambda b, i: (b, 0, 0))
    else:
        mod_spec = pl.BlockSpec((None, 8, D), lambda b, i: (mod_row, 0, 0))
    final = final_g is not None
    in_specs = [
        row(D), mod_spec, _const_spec((1, D)),
        row(CONV_CH),
        pl.BlockSpec((None, 8, CONV_CH), lambda b, i: (b, jnp.maximum(i * hb - 1, 0), 0)),
        pl.BlockSpec((None, 8, CONV_CH), lambda b, i: (b, jnp.minimum((i + 1) * hb, S // 8 - 1), 0)),
        row(CONV_CH), _const_spec((8, CONV_CH)), row(ATT_W), row(GLA_W), row(GLA_W),
        pl.BlockSpec((None, tm, GLA_W), lambda b, i: (b, i, 3)),
        _const_spec((1, GLA_W)), _const_spec((GLA_W, GLA_W)), _const_spec((D, D)),
        _const_spec((D, 2 * D_FF)), _const_spec((N_FF_CHUNKS, FF_CHUNK, D)),
    ]
    args = [xs, mods_l, g2, cu, cu, cu, cb, conv_w, attn, o_f, o_b, gl, norm_g,
            jnp.asarray(_head_blocks(), BF16), w_out, wu, wd]
    if final:
        in_specs.append(_const_spec((1, D)))
        args.append(final_g)
    return pl.pallas_call(
        functools.partial(_post_kernel, final=final),
        grid=(B, nt),
        in_specs=in_specs,
        out_specs=row(D),
        out_shape=jax.ShapeDtypeStruct((B, S, D), F32),
        scratch_shapes=[pltpu.VMEM((tm, D), F32)],
        compiler_params=_params(2),
        name="post_final" if final else "post",
    )(*args)


def _rope_tables(seq):
    t = jnp.arange(seq)
    pos = jnp.stack([(t // GRID_W).astype(F32), (t % GRID_W).astype(F32)], axis=1)
    n_freq = HEAD_DIM // 4
    inv_freq = ROPE_BASE ** (-jnp.arange(n_freq, dtype=F32) / n_freq)
    ang = pos[:, :, None] * inv_freq
    cos, sin = jnp.cos(ang), jnp.sin(ang)
    zero = jnp.zeros_like(sin)
    cos_t = jnp.stack([cos, cos], axis=2).reshape(seq, HEAD_DIM)
    sa_t = jnp.stack([-sin, zero], axis=2).reshape(seq, HEAD_DIM)
    sb_t = jnp.stack([zero, sin], axis=2).reshape(seq, HEAD_DIM)
    rep = LANES // HEAD_DIM
    return tuple(jnp.tile(a, (1, rep)) for a in (cos_t, sa_t, sb_t))


def _tile_rows(S, want):
    t = min(want, S)
    while S % t:
        t //= 2
    return t


def kernel(x, c, ctx, c_ctx, w_mod, b_mod, norm1_g, norm2_g, w_in, conv_w, attn_sink,
           gla_gate_w, gla_gate_b, gla_norm_g, w_out, w_up, w_down, final_norm_g):
    B, S, D = x.shape
    L = w_mod.shape[0]
    Lc = ctx.shape[1]
    assert D == D_MODEL and S % WINDOW == 0 and Lc % GLA_CHUNK == 0 and B + 1 <= 8

    cvec = jnp.zeros((8, D), F32).at[:B].set(c).at[B].set(c_ctx)
    mods = _modulation(cvec, w_mod, b_mod).reshape(L, 8, N_MOD, D)
    mods = jnp.pad(mods, ((0, 0), (0, 0), (0, 8 - N_MOD), (0, 0)))

    order = np.array(Q_HEAD_ORDER)
    wq = w_in[:, :, 768:1280].reshape(L, D, ATT_HEADS, HEAD_DIM)[:, :, order].reshape(L, D, ATT_W)
    w_in_p = jnp.concatenate(
        [w_in[:, :, :768], wq, w_in[:, :, 1280:1536], w_in[:, :, 2560:2592],
         jnp.zeros((L, D, LANES - 2 * GLA_RANK), F32), w_in[:, :, 1536:2560]], axis=2).astype(BF16)
    wo_att = w_out[:, 256:768].reshape(L, ATT_HEADS, HEAD_DIM, D)[:, order].reshape(L, ATT_W, D)
    w_out_p = jnp.concatenate([w_out[:, :256], wo_att, w_out[:, 768:]], axis=1).astype(BF16)
    wg = jnp.zeros((L, LANES, 2 * GLA_W), F32)
    wg = wg.at[:, :GLA_RANK, :GLA_W].set(gla_gate_w[:, 0])
    wg = wg.at[:, GLA_RANK:2 * GLA_RANK, GLA_W:].set(gla_gate_w[:, 1]).astype(BF16)
    bg = gla_gate_b.reshape(L, 1, 2 * GLA_W)
    cw = jnp.pad(conv_w, ((0, 0), (0, 8 - CONV_K), (0, 0)))
    ng = jnp.tile(gla_norm_g, (1, GLA_HEADS)).reshape(L, 1, GLA_W)
    wu = w_up.astype(BF16)
    wd = w_down.reshape(L, N_FF_CHUNKS, FF_CHUNK, D).astype(BF16)
    g1 = norm1_g.reshape(L, 1, D)
    g2 = norm2_g.reshape(L, 1, D)

    rope_tabs = _rope_tables(S)
    tm = _tile_rows(S, 512)
    gla_tile = _tile_rows(S, 512)
    zero_state = jnp.zeros((B, GLA_W, GLA_W), F32)

    for l in range(L):
        last = l == L - 1
        cu, cb, q, k, v, gl, gates = _proj(x, mods[l], None, g1[l], w_in_p[l], wg[l], bg[l],
                                           rope_tabs, tm)
        ccu, ccb, cq, ck, cv, cgl, cgates = _proj(ctx, mods[l], B, g1[l], w_in_p[l], wg[l], bg[l],
                                                  None, Lc)
        attn = _attention(q, k, v, ck, cv, attn_sink[l], local=True)
        oc_f, oc_b, sc_f, sc_b = _gla(cgl, cgates, zero_state, zero_state, Lc)
        o_f, o_b, _, _ = _gla(gl, gates, sc_f, sc_b, gla_tile)
        fin = final_norm_g.reshape(1, D) if last else None
        x = _post(x, mods[l], None, g2[l], cu, cb, cw[l], attn, o_f, o_b, gl, ng[l],
                  w_out_p[l], wu[l], wd[l], fin, tm)
        if not last:
            attn_c = _attention(cq, None, None, ck, cv, attn_sink[l], local=False)
            ctx = _post(ctx, mods[l], B, g2[l], ccu, ccb, cw[l], attn_c, oc_f, oc_b, cgl, ng[l],
                        w_out_p[l], wu[l], wd[l], None, Lc)
    return x
```

```python
import functools

import numpy as np
import jax
import jax.numpy as jnp
from jax import lax
from jax.experimental import pallas as pl
from jax.experimental.pallas import tpu as pltpu

F32 = jnp.float32
BF16 = jnp.bfloat16

D_MODEL = 1024
HEAD_DIM = 64
CONV_CH = 256
CONV_K = 3
ATT_HEADS = 8
ATT_KV_HEADS = 2
ATT_W = ATT_HEADS * HEAD_DIM
KV_W = ATT_KV_HEADS * HEAD_DIM
WINDOW = 128
GLA_HEADS = 4
GLA_W = GLA_HEADS * HEAD_DIM
GLA_RANK = 16
GLA_NORMALIZER = 16.0
D_FF = 2816
N_MOD = 6
GRID_W = 64
ROPE_BASE = 10000.0
EPS = 1e-6
LOG2_E = 1.4426950408889634

LANES = 128
C_CONV = 0
C_QK = 768
C_V = 1408
C_LR = 1536
C_GLA = 1664
N_IN_PAD = C_GLA + 4 * GLA_W
Q_HEAD_ORDER = (0, 4, 1, 5, 2, 6, 3, 7)

ATT_ROW_GROUP = 16
GLA_CHUNK = 64
GLA_SUB = 16
GLA_FAST_RANGE = 60.0
FF_CHUNK = 256
N_FF_CHUNKS = D_FF // FF_CHUNK
FF_UNROLL = 3
assert (N_FF_CHUNKS - 2) % FF_UNROLL == 0

VMEM_LIMIT = 56 * 1024 * 1024


def _const_spec(shape):
    nd = len(shape)
    return pl.BlockSpec(shape, lambda *_: (0,) * nd, pipeline_mode=pl.Buffered(1))


def _params(n_grid):
    return pltpu.CompilerParams(
        dimension_semantics=("arbitrary",) * n_grid, vmem_limit_bytes=VMEM_LIMIT)


def _silu(a):
    return a * (1.0 / (1.0 + jnp.exp(-a)))


def _dot(a, b):
    return jnp.dot(a, b, preferred_element_type=F32)


def _dot_nt(a, b):
    return lax.dot_general(a, b, (((1,), (1,)), ((), ())), preferred_element_type=F32)


def _dot_tn(a, b):
    return lax.dot_general(a, b, (((0,), (0,)), ((), ())), preferred_element_type=F32)


def _mod_kernel(c_ref, w_ref, b_ref, o_ref):
    a = _silu(c_ref[...]).astype(BF16)
    o_ref[...] = _dot(a, w_ref[...].astype(BF16)) + b_ref[...]


def _modulation(cvec, w_mod, b_mod):
    L, D, W = w_mod.shape
    tn = 2048
    return pl.pallas_call(
        _mod_kernel,
        grid=(L, W // tn),
        in_specs=[
            pl.BlockSpec((8, D), lambda l, j: (0, 0)),
            pl.BlockSpec((None, D, tn), lambda l, j: (l, 0, j)),
            pl.BlockSpec((None, 1, tn), lambda l, j: (l, 0, j)),
        ],
        out_specs=pl.BlockSpec((None, 8, tn), lambda l, j: (l, 0, j)),
        out_shape=jax.ShapeDtypeStruct((L, 8, W), F32),
        compiler_params=_params(2),
        name="modulation",
    )(cvec, w_mod, b_mod.reshape(L, 1, W))


def _proj_kernel(x_ref, mod_ref, g1_ref, w_ref, wg_ref, bg_ref, *rest, rope):
    if rope:
        cos_ref, sa_ref, sb_ref = rest[:3]
        rest = rest[3:]
    cu_ref, cb_ref, q_ref, k_ref, v_ref, gl_ref, gate_ref = rest

    x = x_ref[...]
    ms = jnp.mean(x * x, axis=-1, keepdims=True)
    y = x * lax.rsqrt(ms + EPS) * g1_ref[...]
    h = (y * (1.0 + mod_ref[1:2, :]) + mod_ref[0:1, :]).astype(BF16)

    def proj(lo, hi):
        return _dot(h, w_ref[:, lo:hi])

    pc = proj(C_CONV, C_QK)
    cu_ref[...] = pc[:, 2 * CONV_CH:3 * CONV_CH] * pc[:, 0:CONV_CH]
    cb_ref[...] = pc[:, CONV_CH:2 * CONV_CH]

    pqk = proj(C_QK, C_GLA)
    scale = HEAD_DIM ** -0.5
    for m in range((ATT_W + KV_W) // LANES):
        col = pqk[:, m * LANES:(m + 1) * LANES]
        if rope:
            col = (col * cos_ref[...]
                   + pltpu.roll(col, LANES - HEAD_DIM // 4, axis=1) * sa_ref[...]
                   + pltpu.roll(col, HEAD_DIM // 4, axis=1) * sb_ref[...])
        if m < ATT_W // LANES:
            q_ref[:, m * LANES:(m + 1) * LANES] = (col * scale).astype(BF16)
        else:
            k_ref[...] = col.astype(BF16)

    v_ref[...] = pqk[:, C_V - C_QK:C_LR - C_QK].astype(BF16)

    pg = proj(C_GLA, N_IN_PAD)
    gl_ref[:, 0:GLA_W] = pg[:, 0:GLA_W] * (HEAD_DIM ** -0.5)
    gl_ref[:, GLA_W:4 * GLA_W] = pg[:, GLA_W:4 * GLA_W]

    lr = pqk[:, C_LR - C_QK:C_GLA - C_QK].astype(BF16)
    z = _dot(lr, wg_ref[...]) + bg_ref[...]
    log_sig = jnp.minimum(z, 0.0) - jnp.log1p(jnp.exp(-jnp.abs(z)))
    gate_ref[...] = log_sig * (1.0 / GLA_NORMALIZER)


def _proj(xs, mods_l, mod_row, g1, w_in, wg, bg, rope_tabs, tm):
    B, S, D = xs.shape
    rope = rope_tabs is not None
    nt = S // tm
    row = lambda w: pl.BlockSpec((None, tm, w), lambda b, i: (b, i, 0))
    if mod_row is None:
        mod_spec = pl.BlockSpec((None, 8, D), lambda b, i: (b, 0, 0))
    else:
        mod_spec = pl.BlockSpec((None, 8, D), lambda b, i: (mod_row, 0, 0))
    in_specs = [row(D), mod_spec, _const_spec((1, D)), _const_spec((D, N_IN_PAD)),
                _const_spec((LANES, 2 * GLA_W)), _const_spec((1, 2 * GLA_W))]
    args = [xs, mods_l, g1, w_in, wg, bg]
    if rope:
        in_specs += [pl.BlockSpec((tm, LANES), lambda b, i: (i, 0))] * 3
        args += list(rope_tabs)
    widths = (CONV_CH, CONV_CH, ATT_W, KV_W, KV_W, 4 * GLA_W, 2 * GLA_W)
    dtypes = (F32, F32, BF16, BF16, BF16, F32, F32)
    return pl.pallas_call(
        functools.partial(_proj_kernel, rope=rope),
        grid=(B, nt),
        in_specs=in_specs,
        out_specs=[row(w) for w in widths],
        out_shape=[jax.ShapeDtypeStruct((B, S, w), dt) for w, dt in zip(widths, dtypes)],
        compiler_params=_params(2),
        name="proj_rope" if rope else "proj_ctx",
    )(*args)


def _attn_kernel(sink_ref, q_ref, kc_ref, vc_ref, *rest, local, n_sub):
    tb = q_ref.shape[0] // n_sub
    n_pairs = ATT_W // LANES
    n_ctx = kc_ref.shape[0]
    rg = ATT_ROW_GROUP
    if local:
        kp_ref, km_ref, kn_ref, vp_ref, vm_ref, vn_ref, o_ref, s_scr, p_scr, bias_scr = rest
        i = pl.program_id(1)
        n = pl.num_programs(1)
        k_loc = jnp.concatenate([kp_ref[...], km_ref[...], kn_ref[...]], axis=0)
        v_loc = jnp.concatenate([vp_ref[...], vm_ref[...], vn_ref[...]], axis=0)
        r = lax.broadcasted_iota(jnp.int32, (tb, tb), 0)
        j = lax.broadcasted_iota(jnp.int32, (tb, tb), 1)
        bias_prev = jnp.where(j >= r, 0.0, -jnp.inf)
        bias_next = jnp.where(j <= r, 0.0, -jnp.inf)
        bias_scr[0] = bias_prev + jnp.where(i > 0, 0.0, -jnp.inf)
        bias_scr[1] = bias_prev
        bias_scr[2] = bias_next
        bias_scr[3] = bias_next + jnp.where(i < n - 1, 0.0, -jnp.inf)
        nk = n_ctx + 3 * tb
    else:
        o_ref, s_scr, p_scr = rest
        nk = n_ctx
    lane = lax.broadcasted_iota(jnp.int32, (1, LANES), 1)
    lo = lane < HEAD_DIM
    hi = jnp.logical_not(lo)
    first_row = lax.broadcasted_iota(jnp.int32, (LANES, LANES), 0) == 0
    v_tail = jnp.concatenate([jnp.zeros((LANES, LANES), BF16),
                              jnp.where(first_row, 1.0, 0.0).astype(BF16)], axis=1)
    ones_blk = jnp.ones((nk, LANES), BF16)

    for jb in range(n_sub):
        q = q_ref[jb * tb:(jb + 1) * tb, :]
        zero = jnp.zeros((tb, LANES), BF16)
        lhs = jnp.concatenate(
            [jnp.where(lo if g == 0 else hi, q[:, m * LANES:(m + 1) * LANES], zero)
             for m in range(n_pairs) for g in range(ATT_KV_HEADS)], axis=0)
        if local:
            kk = jnp.concatenate([kc_ref[...], k_loc[jb * tb:(jb + 3) * tb]], axis=0)
            vv = jnp.concatenate([vc_ref[...], v_loc[jb * tb:(jb + 3) * tb]], axis=0)
        else:
            kk, vv = kc_ref[...], vc_ref[...]
        v_aug = jnp.concatenate([jnp.concatenate([vv, ones_blk], axis=1), v_tail], axis=0)
        s_scr[jb] = _dot_nt(lhs, kk)

        for hs in range(ATT_HEADS):
            m, g = hs // ATT_KV_HEADS, hs % ATT_KV_HEADS
            sink = sink_ref[m + (ATT_HEADS // ATT_KV_HEADS) * g]
            sink_tile = jnp.where(lane == 0, sink, -jnp.inf) + jnp.zeros((rg, LANES), F32)
            for gi in range(tb // rg):
                rq = gi * rg
                r0 = hs * tb + rq
                s = s_scr[jb, r0:r0 + rg, :]
                if local:
                    bp = bias_scr[0 if jb == 0 else 1, rq:rq + rg, :]
                    bn = bias_scr[3 if jb == n_sub - 1 else 2, rq:rq + rg, :]
                    cols = [s[:, :n_ctx], s[:, n_ctx:n_ctx + tb] + bp,
                            s[:, n_ctx + tb:n_ctx + 2 * tb], s[:, n_ctx + 2 * tb:] + bn, sink_tile]
                else:
                    cols = [s, sink_tile]
                s_all = jnp.concatenate(cols, axis=1)
                mx = jnp.max(s_all, axis=-1, keepdims=True)
                p_scr[jb, r0:r0 + rg, :] = jnp.exp(s_all - mx).astype(BF16)

        o_aug = _dot(p_scr[jb], v_aug)
        o = o_aug[:, :LANES] * (1.0 / o_aug[:, LANES:])
        for m in range(n_pairs):
            o_ref[jb * tb:(jb + 1) * tb, m * LANES:(m + 1) * LANES] = jnp.where(
                lo, o[(2 * m) * tb:(2 * m + 1) * tb], o[(2 * m + 1) * tb:(2 * m + 2) * tb]).astype(BF16)


def _attention(q, k, v, k_ctx, v_ctx, sink, local, n_sub=4):
    B, S, _ = q.shape
    Lc = k_ctx.shape[1]
    if not local:
        n_sub = 1
    tq = WINDOW * n_sub if local else S
    nq = S // tq
    in_specs = [
        pl.BlockSpec(memory_space=pltpu.SMEM),
        pl.BlockSpec((None, tq, ATT_W), lambda b, i: (b, i, 0)),
        pl.BlockSpec((None, Lc, KV_W), lambda b, i: (b, 0, 0)),
        pl.BlockSpec((None, Lc, KV_W), lambda b, i: (b, 0, 0)),
    ]
    args = [sink, q, k_ctx, v_ctx]
    if local:
        nw = S // WINDOW
        nb = [pl.BlockSpec((None, WINDOW, KV_W), lambda b, i: (b, jnp.maximum(i * n_sub - 1, 0), 0)),
              pl.BlockSpec((None, tq, KV_W), lambda b, i: (b, i, 0)),
              pl.BlockSpec((None, WINDOW, KV_W),
                           lambda b, i: (b, jnp.minimum((i + 1) * n_sub, nw - 1), 0))]
        in_specs += nb + nb
        args += [k, k, k, v, v, v]
    tb = tq // n_sub
    nk = Lc + (3 * tb if local else 0)
    scratch = [pltpu.VMEM((n_sub, ATT_HEADS * tb, nk), F32),
               pltpu.VMEM((n_sub, ATT_HEADS * tb, nk + LANES), BF16)]
    if local:
        scratch.append(pltpu.VMEM((4, tb, tb), F32))
    return pl.pallas_call(
        functools.partial(_attn_kernel, local=local, n_sub=n_sub),
        grid=(B, nq),
        in_specs=in_specs,
        out_specs=pl.BlockSpec((None, tq, ATT_W), lambda b, i: (b, i, 0)),
        out_shape=jax.ShapeDtypeStruct((B, S, ATT_W), BF16),
        scratch_shapes=scratch,
        compiler_params=_params(2),
        name="attn_window" if local else "attn_ctx",
    )(*args)


def _gla_consts(rev):
    t = np.arange(GLA_CHUNK)[:, None]
    s = np.arange(GLA_CHUNK)[None, :]
    same = (t // GLA_SUB) == (s // GLA_SUB)
    before = (s >= t) if rev else (s <= t)
    return np.concatenate([same & before, before], axis=0).astype(np.float32)


def _head_blocks():
    h = np.arange(GLA_W) // HEAD_DIM
    return (h[:, None] == h[None, :]).astype(np.float32)


def _gla_last_row(blk, rev):
    return blk * GLA_SUB if rev else blk * GLA_SUB + GLA_SUB - 1


def _gla_prepare(q, k, g, cum, rev):
    n_sub = GLA_CHUNK // GLA_SUB
    g2 = g * LOG2_E
    g_hi = g2.astype(BF16)
    rem = g2 - g_hi.astype(F32)
    g_mid = rem.astype(BF16)
    g_lo = (rem - g_mid.astype(F32)).astype(BF16)
    cs = _dot(cum, g_hi) + _dot(cum, g_mid) + _dot(cum, g_lo)
    beta, bch = cs[0:GLA_CHUNK], cs[GLA_CHUNK:2 * GLA_CHUNK]
    gam = jnp.concatenate(
        [beta[_gla_last_row(b, rev):_gla_last_row(b, rev) + 1, :] - beta[b * GLA_SUB:(b + 1) * GLA_SUB, :]
         for b in range(n_sub)], axis=0)
    end_row = _gla_last_row(0 if rev else n_sub - 1, rev)
    dch = bch[end_row:end_row + 1, :] - bch
    f32_rows = jnp.concatenate([beta, bch, q * jnp.exp2(beta), k * jnp.exp2(gam)], axis=0)
    b16_rows = jnp.concatenate([(q * jnp.exp2(bch)).astype(BF16), (k * jnp.exp2(dch)).astype(BF16)],
                               axis=0)
    return f32_rows, b16_rows


def _gla_causal(rev):
    t = (np.arange(GLA_HEADS * GLA_CHUNK) % GLA_CHUNK)[:, None]
    s = np.arange(GLA_CHUNK)[None, :]
    return ((s >= t) if rev else (s <= t)).astype(np.float32)


def _gla_prepare_fast(q, k, v, g, cum_chunk, head_mask, rev):
    g2 = g * LOG2_E
    g_hi = g2.astype(BF16)
    rem = g2 - g_hi.astype(F32)
    g_mid = rem.astype(BF16)
    g_lo = (rem - g_mid.astype(F32)).astype(BF16)
    bch = _dot(cum_chunk, g_hi) + _dot(cum_chunk, g_mid) + _dot(cum_chunk, g_lo)
    end_row = 0 if rev else GLA_CHUNK - 1
    mid_row = GLA_CHUNK // 2 if rev else GLA_CHUNK // 2 - 1
    b_end = bch[end_row:end_row + 1, :]
    b_mid = bch[mid_row:mid_row + 1, :]
    ok = (jnp.max(-b_end) <= GLA_FAST_RANGE).astype(jnp.int32)
    q_mid = (q * jnp.exp2(jnp.minimum(bch - b_mid, GLA_FAST_RANGE))).astype(BF16)
    k_mid = (k * jnp.exp2(jnp.minimum(b_mid - bch, GLA_FAST_RANGE))).astype(BF16)
    zero = jnp.zeros_like(q_mid)
    rows = jnp.concatenate(
        [jnp.where(hm, q_mid, zero) for hm in head_mask]
        + [k_mid, (q * jnp.exp2(bch)).astype(BF16), (k * jnp.exp2(b_end - bch)).astype(BF16),
           v.astype(BF16)], axis=0)
    return rows, jnp.exp2(b_end), ok


def _gla_tile_fast(pb_scr, dec_scr, dirs, bd, head_mask, n_chunks):
    n_q = GLA_HEADS * GLA_CHUNK
    work = []
    for d, (_, _, _, st_ref, o_ref, caus_ref, rev) in enumerate(dirs):
        for c in (range(n_chunks - 1, -1, -1) if rev else range(n_chunks)):
            pb = pb_scr.at[d, c]
            vb = pb[n_q + 3 * GLA_CHUNK:n_q + 4 * GLA_CHUNK]
            upd = _dot_tn(vb, pb[n_q + 2 * GLA_CHUNK:n_q + 3 * GLA_CHUNK])
            a = _dot_nt(pb[0:n_q], pb[n_q:n_q + GLA_CHUNK])
            work.append(dict(d=d, c=c, pb=pb, vb=vb, upd=upd, a=a, o_ref=o_ref, caus=caus_ref))
    for d, (_, _, _, st_ref, _, _, _) in enumerate(dirs):
        st = st_ref[...]
        for w in (w for w in work if w["d"] == d):
            w["o_inter"] = _dot_nt(w["pb"][n_q + GLA_CHUNK:n_q + 2 * GLA_CHUNK], st.astype(BF16))
            st = st * dec_scr[d, w["c"], 0:1, :] + w["upd"] * bd
        st_ref[...] = st
    for w in work:
        w["r"] = _dot((w["a"] * w["caus"][...]).astype(BF16), w["vb"])
    for w in work:
        o = w["o_inter"]
        for hh, hm in enumerate(head_mask):
            o = o + jnp.where(hm, w["r"][hh * GLA_CHUNK:(hh + 1) * GLA_CHUNK, :], 0.0)
        w["o_ref"][w["c"] * GLA_CHUNK:(w["c"] + 1) * GLA_CHUNK, :] = o


def _gla_chunk(q, k, v, pf_ref, pb_ref, st_ref, o_ref, r0, ones, bd, head_mask, rev):
    n_sub = GLA_CHUNK // GLA_SUB
    order = list(range(n_sub))[::-1] if rev else list(range(n_sub))

    def rows(a, blk):
        return a[blk * GLA_SUB:(blk + 1) * GLA_SUB, :]

    half = GLA_SUB // 2
    beta = pf_ref[0:GLA_CHUNK]
    q_loc = pf_ref[2 * GLA_CHUNK:3 * GLA_CHUNK]
    k_loc = pf_ref[3 * GLA_CHUNK:4 * GLA_CHUNK]
    g_tot = [beta[_gla_last_row(b, rev):_gla_last_row(b, rev) + 1, :] for b in range(n_sub)]
    end_row = GLA_CHUNK + _gla_last_row(order[-1], rev)
    b_end = pf_ref[end_row:end_row + 1, :]
    half_row = lax.broadcasted_iota(jnp.int32, (half, 1), 0)

    st = st_ref[...]
    o_inter = _dot_nt(pb_ref[0:GLA_CHUNK], st.astype(BF16))
    upd = _dot_tn(v.astype(BF16), pb_ref[GLA_CHUNK:2 * GLA_CHUNK])
    scores, values = {}, {}
    for p in range(1, n_sub):
        ks, vs = [], []
        for pp in range(p):
            sb = order[pp]
            kk = rows(k_loc, sb)
            mids = [order[x] for x in range(pp + 1, p)]
            if mids:
                tot = g_tot[mids[0]]
                for mb in mids[1:]:
                    tot = tot + g_tot[mb]
                kk = kk * jnp.exp2(tot)
            ks.append(kk)
            vs.append(rows(v, sb))
        qt = rows(q_loc, order[p])
        q_heads = jnp.concatenate([jnp.where(hm, qt, 0.0) for hm in head_mask], axis=0)
        scores[p] = _dot_nt(q_heads.astype(BF16), jnp.concatenate(ks, axis=0).astype(BF16))
        values[p] = jnp.concatenate(vs, axis=0).astype(BF16)
    yield

    pairs = []
    for s in range(GLA_SUB):
        hs = s // half
        for hb in (range(0, hs + 1) if rev else range(hs, GLA_SUB // half)):
            pairs.append((s, hb))
    sums = {}
    for p, tb in enumerate(order):
        bt, qb, kb = rows(beta, tb), rows(q, tb), rows(k, tb)
        slabs = []
        for s, hb in pairs:
            d = bt[hb * half:(hb + 1) * half, :] - bt[s:s + 1, :]
            if hb == s // half:
                keep = (half_row <= s % half) if rev else (half_row >= s % half)
                d = jnp.where(keep, d, -jnp.inf)
            slabs.append(qb[hb * half:(hb + 1) * half, :] * jnp.exp2(d) * kb[s:s + 1, :])
        sums[p] = _dot(jnp.concatenate(slabs, axis=0).astype(BF16), ones)
        yield

    mixed = {p: _dot(scores[p].astype(BF16), values[p]) for p in range(1, n_sub)}
    yield

    for p, tb in enumerate(order):
        vb = rows(v, tb)
        acc = rows(o_inter, tb)
        if p > 0:
            for hh, hm in enumerate(head_mask):
                acc = acc + jnp.where(hm, mixed[p][hh * GLA_SUB:(hh + 1) * GLA_SUB, :], 0.0)
        halves = [acc[hb * half:(hb + 1) * half, :] for hb in range(GLA_SUB // half)]
        for idx, (s, hb) in enumerate(pairs):
            halves[hb] = halves[hb] + sums[p][idx * half:(idx + 1) * half, :] * vb[s:s + 1, :]
        o_ref[pl.ds(r0 + tb * GLA_SUB, GLA_SUB), :] = jnp.concatenate(halves, axis=0)
        yield

    st_ref[...] = st * jnp.exp2(b_end) + upd * bd


def _interleave(gens):
    active = list(gens)
    while active:
        for gen in list(active):
            try:
                next(gen)
            except StopIteration:
                active.remove(gen)


def _gla_kernel(cumf_ref, cumb_ref, ones_ref, bd_ref, causf_ref, causb_ref, glf_ref, glb_ref,
                gf_ref, gb_ref, s0f_ref, s0b_ref, of_ref, ob_ref, sff_ref, sfb_ref,
                stf_ref, stb_ref, pb_scr, dec_scr, ok_ref, *, n_chunks):
    i = pl.program_id(1)
    lane_head = lax.broadcasted_iota(jnp.int32, (1, GLA_W), 1) // HEAD_DIM
    head_mask = [lane_head == hh for hh in range(GLA_HEADS)]
    dirs = ((glf_ref, gf_ref, cumf_ref, stf_ref, of_ref, causf_ref, False),
            (glb_ref, gb_ref, cumb_ref, stb_ref, ob_ref, causb_ref, True))

    @pl.when(i == 0)
    def _():
        stf_ref[...] = s0f_ref[...]
        stb_ref[...] = s0b_ref[...]

    def load(gl_ref, g_ref, r0):
        rows = pl.ds(r0, GLA_CHUNK)
        return (gl_ref[rows, 0:GLA_W], gl_ref[rows, GLA_W:2 * GLA_W],
                gl_ref[rows, 2 * GLA_W:3 * GLA_W], g_ref[rows, :])

    for c in range(n_chunks):
        for d, (gl_ref, g_ref, cum_ref, _, _, _, rev) in enumerate(dirs):
            q, k, v, g = load(gl_ref, g_ref, c * GLA_CHUNK)
            rows, dec, ok = _gla_prepare_fast(q, k, v, g, cum_ref[GLA_CHUNK:2 * GLA_CHUNK, :],
                                              head_mask, rev)
            pb_scr[d, c] = rows
            dec_scr[d, c] = jnp.broadcast_to(dec, (8, GLA_W))
            ok_ref[d, c] = ok

    n_ok = ok_ref[0, 0]
    for d in range(2):
        for c in range(n_chunks):
            if (d, c) != (0, 0):
                n_ok = n_ok + ok_ref[d, c]
    fast = n_ok == 2 * n_chunks

    @pl.when(fast)
    def _():
        _gla_tile_fast(pb_scr, dec_scr, dirs, bd_ref[...], head_mask, n_chunks)

    @pl.when(jnp.logical_not(fast))
    def _():
        def chunk(ci, carry):
            cidx = (ci, n_chunks - 1 - ci)
            gens = []
            for d, (gl_ref, g_ref, cum_ref, st_ref, o_ref, _, rev) in enumerate(dirs):
                r0 = pl.multiple_of(cidx[d] * GLA_CHUNK, GLA_CHUNK)
                q, k, v, g = load(gl_ref, g_ref, r0)
                pf, pb = _gla_prepare(q, k, g, cum_ref[...], rev)
                gens.append(_gla_chunk(q, k, v, pf, pb, st_ref, o_ref, r0, ones_ref[...],
                                       bd_ref[...], head_mask, rev))
            _interleave(gens)
            return carry

        lax.fori_loop(0, n_chunks, chunk, 0)

    @pl.when(i == pl.num_programs(1) - 1)
    def _():
        sff_ref[...] = stf_ref[...]
        sfb_ref[...] = stb_ref[...]


def _gla(gl, gates, s0f, s0b, tile):
    B, S, _ = gl.shape
    nt = S // tile
    n_chunks = tile // GLA_CHUNK
    ones_bd = jnp.asarray(_head_blocks(), BF16)
    bd = jnp.asarray(_head_blocks(), F32)
    state_spec = pl.BlockSpec((None, GLA_W, GLA_W), lambda b, i: (b, 0, 0))
    return pl.pallas_call(
        functools.partial(_gla_kernel, n_chunks=n_chunks),
        grid=(B, nt),
        in_specs=[
            _const_spec((2 * GLA_CHUNK, GLA_CHUNK)),
            _const_spec((2 * GLA_CHUNK, GLA_CHUNK)),
            _const_spec((GLA_W, GLA_W)),
            _const_spec((GLA_W, GLA_W)),
            _const_spec((GLA_HEADS * GLA_CHUNK, GLA_CHUNK)),
            _const_spec((GLA_HEADS * GLA_CHUNK, GLA_CHUNK)),
            pl.BlockSpec((None, tile, 3 * GLA_W), lambda b, i: (b, i, 0)),
            pl.BlockSpec((None, tile, 3 * GLA_W), lambda b, i: (b, nt - 1 - i, 0)),
            pl.BlockSpec((None, tile, GLA_W), lambda b, i: (b, i, 0)),
            pl.BlockSpec((None, tile, GLA_W), lambda b, i: (b, nt - 1 - i, 1)),
            state_spec, state_spec,
        ],
        out_specs=[
            pl.BlockSpec((None, tile, GLA_W), lambda b, i: (b, i, 0)),
            pl.BlockSpec((None, tile, GLA_W), lambda b, i: (b, nt - 1 - i, 0)),
            state_spec, state_spec,
        ],
        out_shape=[jax.ShapeDtypeStruct((B, S, GLA_W), F32),
                   jax.ShapeDtypeStruct((B, S, GLA_W), F32),
                   jax.ShapeDtypeStruct((B, GLA_W, GLA_W), F32),
                   jax.ShapeDtypeStruct((B, GLA_W, GLA_W), F32)],
        scratch_shapes=[pltpu.VMEM((GLA_W, GLA_W), F32), pltpu.VMEM((GLA_W, GLA_W), F32),
                        pltpu.VMEM((2, n_chunks, (GLA_HEADS + 4) * GLA_CHUNK, GLA_W), BF16),
                        pltpu.VMEM((2, n_chunks, 8, GLA_W), F32),
                        pltpu.SMEM((2, n_chunks), jnp.int32)],
        compiler_params=_params(2),
        name="gla",
    )(jnp.asarray(_gla_consts(False), BF16), jnp.asarray(_gla_consts(True), BF16),
      ones_bd, bd, jnp.asarray(_gla_causal(False), F32), jnp.asarray(_gla_causal(True), F32),
      gl, gl, gates, gates, s0f, s0b)


def _post_kernel(x_ref, mod_ref, g2_ref, cu_ref, cup_ref, cun_ref, cb_ref, cw_ref, at_ref,
                 of_ref, ob_ref, gg_ref, ng_ref, ones_ref, wo_ref, wu_ref, wd_ref,
                 *rest, final):
    if final:
        fg_ref, o_ref, acc_ref = rest
    else:
        o_ref, acc_ref = rest
    i = pl.program_id(1)
    n = pl.num_programs(1)
    tm = x_ref.shape[0]

    u = cu_ref[...]
    row = lax.broadcasted_iota(jnp.int32, (tm, 1), 0)
    prev_row = jnp.where(i > 0, cup_ref[7:8, :], 0.0)
    next_row = jnp.where(i < n - 1, cun_ref[0:1, :], 0.0)
    u_prev = jnp.where(row == 0, prev_row, pltpu.roll(u, 1, axis=0))
    u_next = jnp.where(row == tm - 1, next_row, pltpu.roll(u, tm - 1, axis=0))
    conv = cb_ref[...] * (cw_ref[0:1, :] * u_prev + cw_ref[1:2, :] * u + cw_ref[2:3, :] * u_next)

    o = of_ref[...] + ob_ref[...]
    sq = o * o
    sq_hi = sq.astype(BF16)
    sq_lo = (sq - sq_hi.astype(F32)).astype(BF16)
    ms = (_dot(sq_hi, ones_ref[...]) + _dot(sq_lo, ones_ref[...])) * (1.0 / HEAD_DIM)
    gla = o * lax.rsqrt(ms + EPS) * ng_ref[...] * _silu(gg_ref[...])

    mix = (_dot(conv.astype(BF16), wo_ref[0:CONV_CH, :])
           + _dot(at_ref[...], wo_ref[CONV_CH:CONV_CH + ATT_W, :])
           + _dot(gla.astype(BF16), wo_ref[CONV_CH + ATT_W:, :]))
    x1 = x_ref[...] + mod_ref[2:3, :] * mix

    ms2 = jnp.mean(x1 * x1, axis=-1, keepdims=True)
    y2 = x1 * lax.rsqrt(ms2 + EPS) * g2_ref[...]
    h2 = (y2 * (1.0 + mod_ref[4:5, :]) + mod_ref[3:4, :]).astype(BF16)

    def up(jc):
        if isinstance(jc, int):
            ca, cb = jc * FF_CHUNK, D_FF + jc * FF_CHUNK
        else:
            ca = pl.multiple_of(jc * FF_CHUNK, FF_CHUNK)
            cb = pl.multiple_of(D_FF + jc * FF_CHUNK, FF_CHUNK)
        a = _dot(h2, wu_ref[:, pl.ds(ca, FF_CHUNK)])
        b = _dot(h2, wu_ref[:, pl.ds(cb, FF_CHUNK)])
        return a, b

    def act(ab):
        return (_silu(ab[0]) * ab[1]).astype(BF16)

    prev = act(up(0))
    ab = up(1)
    acc_ref[...] = _dot(prev, wd_ref[0])
    prev = act(ab)

    def ff(t, prev):
        for u in range(FF_UNROLL):
            jc = 2 + t * FF_UNROLL + u
            ab = up(jc)
            acc_ref[...] += _dot(prev, wd_ref[jc - 1])
            prev = act(ab)
        return prev

    prev = lax.fori_loop(0, (N_FF_CHUNKS - 2) // FF_UNROLL, ff, prev)
    x2 = x1 + mod_ref[5:6, :] * (acc_ref[...] + _dot(prev, wd_ref[N_FF_CHUNKS - 1]))
    if final:
        msf = jnp.mean(x2 * x2, axis=-1, keepdims=True)
        x2 = x2 * lax.rsqrt(msf + EPS) * fg_ref[...]
    o_ref[...] = x2


def _post(xs, mods_l, mod_row, g2, cu, cb, conv_w, attn, o_f, o_b, gl, norm_g, w_out, wu, wd,
          final_g, tm):
    B, S, D = xs.shape
    nt = S // tm
    hb = tm // 8
    row = lambda w: pl.BlockSpec((None, tm, w), lambda b, i: (b, i, 0))
    if mod_row is None:
        mod_spec = pl.BlockSpec((None, 8, D), lambda b, i: (b, 0, 0))
    else:
        mod_spec = pl.BlockSpec((None, 8, D), lambda b, i: (mod_row, 0, 0))
    final = final_g is not None
    in_specs = [
        row(D), mod_spec, _const_spec((1, D)),
        row(CONV_CH),
        pl.BlockSpec((None, 8, CONV_CH), lambda b, i: (b, jnp.maximum(i * hb - 1, 0), 0)),
        pl.BlockSpec((None, 8, CONV_CH), lambda b, i: (b, jnp.minimum((i + 1) * hb, S // 8 - 1), 0)),
        row(CONV_CH), _const_spec((8, CONV_CH)), row(ATT_W), row(GLA_W), row(GLA_W),
        pl.BlockSpec((None, tm, GLA_W), lambda b, i: (b, i, 3)),
        _const_spec((1, GLA_W)), _const_spec((GLA_W, GLA_W)), _const_spec((D, D)),
        _const_spec((D, 2 * D_FF)), _const_spec((N_FF_CHUNKS, FF_CHUNK, D)),
    ]
    args = [xs, mods_l, g2, cu, cu, cu, cb, conv_w, attn, o_f, o_b, gl, norm_g,
            jnp.asarray(_head_blocks(), BF16), w_out, wu, wd]
    if final:
        in_specs.append(_const_spec((1, D)))
        args.append(final_g)
    return pl.pallas_call(
        functools.partial(_post_kernel, final=final),
        grid=(B, nt),
        in_specs=in_specs,
        out_specs=row(D),
        out_shape=jax.ShapeDtypeStruct((B, S, D), F32),
        scratch_shapes=[pltpu.VMEM((tm, D), F32)],
        compiler_params=_params(2),
        name="post_final" if final else "post",
    )(*args)


def _rope_tables(seq):
    t = jnp.arange(seq)
    pos = jnp.stack([(t // GRID_W).astype(F32), (t % GRID_W).astype(F32)], axis=1)
    n_freq = HEAD_DIM // 4
    inv_freq = ROPE_BASE ** (-jnp.arange(n_freq, dtype=F32) / n_freq)
    ang = pos[:, :, None] * inv_freq
    cos, sin = jnp.cos(ang), jnp.sin(ang)
    zero = jnp.zeros_like(sin)
    cos_t = jnp.stack([cos, cos], axis=2).reshape(seq, HEAD_DIM)
    sa_t = jnp.stack([-sin, zero], axis=2).reshape(seq, HEAD_DIM)
    sb_t = jnp.stack([zero, sin], axis=2).reshape(seq, HEAD_DIM)
    rep = LANES // HEAD_DIM
    return tuple(jnp.tile(a, (1, rep)) for a in (cos_t, sa_t, sb_t))


def _tile_rows(S, want):
    t = min(want, S)
    while S % t:
        t //= 2
    return t


def kernel(x, c, ctx, c_ctx, w_mod, b_mod, norm1_g, norm2_g, w_in, conv_w, attn_sink,
           gla_gate_w, gla_gate_b, gla_norm_g, w_out, w_up, w_down, final_norm_g):
    B, S, D = x.shape
    L = w_mod.shape[0]
    Lc = ctx.shape[1]
    assert D == D_MODEL and S % WINDOW == 0 and Lc % GLA_CHUNK == 0 and B + 1 <= 8

    cvec = jnp.zeros((8, D), F32).at[:B].set(c).at[B].set(c_ctx)
    mods = _modulation(cvec, w_mod, b_mod).reshape(L, 8, N_MOD, D)
    mods = jnp.pad(mods, ((0, 0), (0, 0), (0, 8 - N_MOD), (0, 0)))

    order = np.array(Q_HEAD_ORDER)
    wq = w_in[:, :, 768:1280].reshape(L, D, ATT_HEADS, HEAD_DIM)[:, :, order].reshape(L, D, ATT_W)
    w_in_p = jnp.concatenate(
        [w_in[:, :, :768], wq, w_in[:, :, 1280:1536], w_in[:, :, 2560:2592],
         jnp.zeros((L, D, LANES - 2 * GLA_RANK), F32), w_in[:, :, 1536:2560]], axis=2).astype(BF16)
    wo_att = w_out[:, 256:768].reshape(L, ATT_HEADS, HEAD_DIM, D)[:, order].reshape(L, ATT_W, D)
    w_out_p = jnp.concatenate([w_out[:, :256], wo_att, w_out[:, 768:]], axis=1).astype(BF16)
    wg = jnp.zeros((L, LANES, 2 * GLA_W), F32)
    wg = wg.at[:, :GLA_RANK, :GLA_W].set(gla_gate_w[:, 0])
    wg = wg.at[:, GLA_RANK:2 * GLA_RANK, GLA_W:].set(gla_gate_w[:, 1]).astype(BF16)
    bg = gla_gate_b.reshape(L, 1, 2 * GLA_W)
    cw = jnp.pad(conv_w, ((0, 0), (0, 8 - CONV_K), (0, 0)))
    ng = jnp.tile(gla_norm_g, (1, GLA_HEADS)).reshape(L, 1, GLA_W)
    wu = w_up.astype(BF16)
    wd = w_down.reshape(L, N_FF_CHUNKS, FF_CHUNK, D).astype(BF16)
    g1 = norm1_g.reshape(L, 1, D)
    g2 = norm2_g.reshape(L, 1, D)

    rope_tabs = _rope_tables(S)
    tm = _tile_rows(S, 512)
    gla_tile = _tile_rows(S, 512)
    zero_state = jnp.zeros((B, GLA_W, GLA_W), F32)

    for l in range(L):
        last = l == L - 1
        cu, cb, q, k, v, gl, gates = _proj(x, mods[l], None, g1[l], w_in_p[l], wg[l], bg[l],
                                           rope_tabs, tm)
        ccu, ccb, cq, ck, cv, cgl, cgates = _proj(ctx, mods[l], B, g1[l], w_in_p[l], wg[l], bg[l],
                                                  None, Lc)
        attn = _attention(q, k, v, ck, cv, attn_sink[l], local=True)
        oc_f, oc_b, sc_f, sc_b = _gla(cgl, cgates, zero_state, zero_state, Lc)
        o_f, o_b, _, _ = _gla(gl, gates, sc_f, sc_b, gla_tile)
        fin = final_norm_g.reshape(1, D) if last else None
        x = _post(x, mods[l], None, g2[l], cu, cb, cw[l], attn, o_f, o_b, gl, ng[l],
                  w_out_p[l], wu[l], wd[l], fin, tm)
        if not last:
            attn_c = _attention(cq, None, None, ck, cv, attn_sink[l], local=False)
            ctx = _post(ctx, mods[l], B, g2[l], ccu, ccb, cw[l], attn_c, oc_f, oc_b, cgl, ng[l],
                        w_out_p[l], wu[l], wd[l], None, Lc)
    return x
```

```python
import functools

import numpy as np
import jax
import jax.numpy as jnp
from jax import lax
from jax.experimental import pallas as pl
from jax.experimental.pallas import tpu as pltpu

F32 = jnp.float32
BF16 = jnp.bfloat16

D_MODEL = 1024
HEAD_DIM = 64
CONV_CH = 256
CONV_K = 3
ATT_HEADS = 8
ATT_KV_HEADS = 2
ATT_W = ATT_HEADS * HEAD_DIM
KV_W = ATT_KV_HEADS * HEAD_DIM
WINDOW = 128
GLA_HEADS = 4
GLA_W = GLA_HEADS * HEAD_DIM
GLA_RANK = 16
GLA_NORMALIZER = 16.0
D_FF = 2816
N_MOD = 6
GRID_W = 64
ROPE_BASE = 10000.0
EPS = 1e-6
LOG2_E = 1.4426950408889634

LANES = 128
C_CONV = 0
C_QK = 768
C_V = 1408
C_LR = 1536
C_GLA = 1664
N_IN_PAD = C_GLA + 4 * GLA_W
Q_HEAD_ORDER = (0, 4, 1, 5, 2, 6, 3, 7)

ATT_ROW_GROUP = 16
GLA_CHUNK = 64
GLA_SUB = 16
GLA_FAST_RANGE = 60.0
FF_CHUNK = 256
N_FF_CHUNKS = D_FF // FF_CHUNK
FF_UNROLL = 3
assert (N_FF_CHUNKS - 2) % FF_UNROLL == 0

VMEM_LIMIT = 56 * 1024 * 1024


def _const_spec(shape):
    nd = len(shape)
    return pl.BlockSpec(shape, lambda *_: (0,) * nd, pipeline_mode=pl.Buffered(1))


def _params(n_grid):
    return pltpu.CompilerParams(
        dimension_semantics=("arbitrary",) * n_grid, vmem_limit_bytes=VMEM_LIMIT)


def _silu(a):
    return a * (1.0 / (1.0 + jnp.exp(-a)))


def _dot(a, b):
    return jnp.dot(a, b, preferred_element_type=F32)


def _dot_nt(a, b):
    return lax.dot_general(a, b, (((1,), (1,)), ((), ())), preferred_element_type=F32)


def _dot_tn(a, b):
    return lax.dot_general(a, b, (((0,), (0,)), ((), ())), preferred_element_type=F32)


def _mod_kernel(c_ref, w_ref, b_ref, o_ref):
    a = _silu(c_ref[...]).astype(BF16)
    o_ref[...] = _dot(a, w_ref[...].astype(BF16)) + b_ref[...]


def _modulation(cvec, w_mod, b_mod):
    L, D, W = w_mod.shape
    tn = 2048
    return pl.pallas_call(
        _mod_kernel,
        grid=(L, W // tn),
        in_specs=[
            pl.BlockSpec((8, D), lambda l, j: (0, 0)),
            pl.BlockSpec((None, D, tn), lambda l, j: (l, 0, j)),
            pl.BlockSpec((None, 1, tn), lambda l, j: (l, 0, j)),
        ],
        out_specs=pl.BlockSpec((None, 8, tn), lambda l, j: (l, 0, j)),
        out_shape=jax.ShapeDtypeStruct((L, 8, W), F32),
        compiler_params=_params(2),
        name="modulation",
    )(cvec, w_mod, b_mod.reshape(L, 1, W))


def _proj_kernel(x_ref, mod_ref, g1_ref, w_ref, wg_ref, bg_ref, *rest, rope):
    if rope:
        cos_ref, sa_ref, sb_ref = rest[:3]
        rest = rest[3:]
    cu_ref, cb_ref, q_ref, k_ref, v_ref, gl_ref, gate_ref = rest

    x = x_ref[...]
    ms = jnp.mean(x * x, axis=-1, keepdims=True)
    y = x * lax.rsqrt(ms + EPS) * g1_ref[...]
    h = (y * (1.0 + mod_ref[1:2, :]) + mod_ref[0:1, :]).astype(BF16)

    def proj(lo, hi):
        return _dot(h, w_ref[:, lo:hi])

    pc = proj(C_CONV, C_QK)
    cu_ref[...] = pc[:, 2 * CONV_CH:3 * CONV_CH] * pc[:, 0:CONV_CH]
    cb_ref[...] = pc[:, CONV_CH:2 * CONV_CH]

    pqk = proj(C_QK, C_GLA)
    scale = HEAD_DIM ** -0.5
    for m in range((ATT_W + KV_W) // LANES):
        col = pqk[:, m * LANES:(m + 1) * LANES]
        if rope:
            col = (col * cos_ref[...]
                   + pltpu.roll(col, LANES - HEAD_DIM // 4, axis=1) * sa_ref[...]
                   + pltpu.roll(col, HEAD_DIM // 4, axis=1) * sb_ref[...])
        if m < ATT_W // LANES:
            q_ref[:, m * LANES:(m + 1) * LANES] = (col * scale).astype(BF16)
        else:
            k_ref[...] = col.astype(BF16)

    v_ref[...] = pqk[:, C_V - C_QK:C_LR - C_QK].astype(BF16)

    pg = proj(C_GLA, N_IN_PAD)
    gl_ref[:, 0:GLA_W] = pg[:, 0:GLA_W] * (HEAD_DIM ** -0.5)
    gl_ref[:, GLA_W:4 * GLA_W] = pg[:, GLA_W:4 * GLA_W]

    lr = pqk[:, C_LR - C_QK:C_GLA - C_QK].astype(BF16)
    z = _dot(lr, wg_ref[...]) + bg_ref[...]
    log_sig = jnp.minimum(z, 0.0) - jnp.log1p(jnp.exp(-jnp.abs(z)))
    gate_ref[...] = log_sig * (1.0 / GLA_NORMALIZER)


def _proj(xs, mods_l, mod_row, g1, w_in, wg, bg, rope_tabs, tm):
    B, S, D = xs.shape
    rope = rope_tabs is not None
    nt = S // tm
    row = lambda w: pl.BlockSpec((None, tm, w), lambda b, i: (b, i, 0))
    if mod_row is None:
        mod_spec = pl.BlockSpec((None, 8, D), lambda b, i: (b, 0, 0))
    else:
        mod_spec = pl.BlockSpec((None, 8, D), lambda b, i: (mod_row, 0, 0))
    in_specs = [row(D), mod_spec, _const_spec((1, D)), _const_spec((D, N_IN_PAD)),
                _const_spec((LANES, 2 * GLA_W)), _const_spec((1, 2 * GLA_W))]
    args = [xs, mods_l, g1, w_in, wg, bg]
    if rope:
        in_specs += [pl.BlockSpec((tm, LANES), lambda b, i: (i, 0))] * 3
        args += list(rope_tabs)
    widths = (CONV_CH, CONV_CH, ATT_W, KV_W, KV_W, 4 * GLA_W, 2 * GLA_W)
    dtypes = (F32, F32, BF16, BF16, BF16, F32, F32)
    return pl.pallas_call(
        functools.partial(_proj_kernel, rope=rope),
        grid=(B, nt),
        in_specs=in_specs,
        out_specs=[row(w) for w in widths],
        out_shape=[jax.ShapeDtypeStruct((B, S, w), dt) for w, dt in zip(widths, dtypes)],
        compiler_params=_params(2),
        name="proj_rope" if rope else "proj_ctx",
    )(*args)


def _attn_kernel(sink_ref, q_ref, kc_ref, vc_ref, *rest, local, n_sub):
    tb = q_ref.shape[0] // n_sub
    n_pairs = ATT_W // LANES
    n_ctx = kc_ref.shape[0]
    rg = ATT_ROW_GROUP
    if local:
        kp_ref, km_ref, kn_ref, vp_ref, vm_ref, vn_ref, o_ref, s_scr, p_scr, e_scr, bias_scr = rest
        i = pl.program_id(1)
        n = pl.num_programs(1)
        k_loc = jnp.concatenate([kp_ref[...], km_ref[...], kn_ref[...]], axis=0)
        v_loc = jnp.concatenate([vp_ref[...], vm_ref[...], vn_ref[...]], axis=0)
        r = lax.broadcasted_iota(jnp.int32, (tb, tb), 0)
        j = lax.broadcasted_iota(jnp.int32, (tb, tb), 1)
        bias_prev = jnp.where(j >= r, 0.0, -jnp.inf)
        bias_next = jnp.where(j <= r, 0.0, -jnp.inf)
        bias_scr[0] = bias_prev + jnp.where(i > 0, 0.0, -jnp.inf)
        bias_scr[1] = bias_prev
        bias_scr[2] = bias_next
        bias_scr[3] = bias_next + jnp.where(i < n - 1, 0.0, -jnp.inf)
        nk = n_ctx + 3 * tb
    else:
        o_ref, s_scr, p_scr, e_scr = rest
        nk = n_ctx
    lane = lax.broadcasted_iota(jnp.int32, (1, LANES), 1)
    lo = lane < HEAD_DIM
    hi = jnp.logical_not(lo)
    lo_k = lax.broadcasted_iota(jnp.int32, (nk, LANES), 1) < HEAD_DIM
    group_ones = [jnp.where(lo_k, 1.0, 0.0).astype(BF16), jnp.where(lo_k, 0.0, 1.0).astype(BF16)]

    for jb in range(n_sub):
        q = q_ref[jb * tb:(jb + 1) * tb, :]
        lhs = jnp.concatenate([q[:, m * LANES:(m + 1) * LANES] for m in range(n_pairs)], axis=0)
        if local:
            kk = jnp.concatenate([kc_ref[...], k_loc[jb * tb:(jb + 3) * tb]], axis=0)
            vv = jnp.concatenate([vc_ref[...], v_loc[jb * tb:(jb + 3) * tb]], axis=0)
        else:
            kk, vv = kc_ref[...], vc_ref[...]
        k2 = jnp.concatenate([kk * group_ones[0], kk * group_ones[1]], axis=0)
        v2 = jnp.concatenate(
            [jnp.concatenate([vv * group_ones[0], group_ones[0]], axis=1),
             jnp.concatenate([vv * group_ones[1], group_ones[1]], axis=1)], axis=0)
        s_scr[jb] = _dot_nt(lhs, k2)

        for m in range(n_pairs):
            sinks = [sink_ref[m + (ATT_HEADS // ATT_KV_HEADS) * g] for g in range(ATT_KV_HEADS)]
            for gi in range(tb // rg):
                rq = gi * rg
                r0 = m * tb + rq
                s = s_scr[jb, r0:r0 + rg, :]
                probs, sink_terms = [], []
                for g in range(ATT_KV_HEADS):
                    sg = s[:, g * nk:(g + 1) * nk]
                    if local:
                        bp = bias_scr[0 if jb == 0 else 1, rq:rq + rg, :]
                        bn = bias_scr[3 if jb == n_sub - 1 else 2, rq:rq + rg, :]
                        sg = jnp.concatenate(
                            [sg[:, :n_ctx], sg[:, n_ctx:n_ctx + tb] + bp,
                             sg[:, n_ctx + tb:n_ctx + 2 * tb], sg[:, n_ctx + 2 * tb:] + bn], axis=1)
                    mx = jnp.maximum(jnp.max(sg, axis=-1, keepdims=True), sinks[g])
                    probs.append(jnp.exp(sg - mx).astype(BF16))
                    sink_terms.append(jnp.exp(sinks[g] - mx))
                p_scr[jb, r0:r0 + rg, :] = jnp.concatenate(probs, axis=1)
                e_scr[jb, r0:r0 + rg, :] = jnp.where(lo, sink_terms[0], sink_terms[1])

        o_aug = _dot(p_scr[jb], v2)
        o = o_aug[:, :LANES] * (1.0 / (o_aug[:, LANES:] + e_scr[jb]))
        for m in range(n_pairs):
            o_ref[jb * tb:(jb + 1) * tb, m * LANES:(m + 1) * LANES] = (
                o[m * tb:(m + 1) * tb].astype(BF16))


def _attention(q, k, v, k_ctx, v_ctx, sink, local, n_sub=4):
    B, S, _ = q.shape
    Lc = k_ctx.shape[1]
    if not local:
        n_sub = 1
    tq = WINDOW * n_sub if local else S
    nq = S // tq
    in_specs = [
        pl.BlockSpec(memory_space=pltpu.SMEM),
        pl.BlockSpec((None, tq, ATT_W), lambda b, i: (b, i, 0)),
        pl.BlockSpec((None, Lc, KV_W), lambda b, i: (b, 0, 0)),
        pl.BlockSpec((None, Lc, KV_W), lambda b, i: (b, 0, 0)),
    ]
    args = [sink, q, k_ctx, v_ctx]
    if local:
        nw = S // WINDOW
        nb = [pl.BlockSpec((None, WINDOW, KV_W), lambda b, i: (b, jnp.maximum(i * n_sub - 1, 0), 0)),
              pl.BlockSpec((None, tq, KV_W), lambda b, i: (b, i, 0)),
              pl.BlockSpec((None, WINDOW, KV_W),
                           lambda b, i: (b, jnp.minimum((i + 1) * n_sub, nw - 1), 0))]
        in_specs += nb + nb
        args += [k, k, k, v, v, v]
    tb = tq // n_sub
    nk = Lc + (3 * tb if local else 0)
    n_rows = (ATT_W // LANES) * tb
    scratch = [pltpu.VMEM((n_sub, n_rows, ATT_KV_HEADS * nk), F32),
               pltpu.VMEM((n_sub, n_rows, ATT_KV_HEADS * nk), BF16),
               pltpu.VMEM((n_sub, n_rows, LANES), F32)]
    if local:
        scratch.append(pltpu.VMEM((4, tb, tb), F32))
    return pl.pallas_call(
        functools.partial(_attn_kernel, local=local, n_sub=n_sub),
        grid=(B, nq),
        in_specs=in_specs,
        out_specs=pl.BlockSpec((None, tq, ATT_W), lambda b, i: (b, i, 0)),
        out_shape=jax.ShapeDtypeStruct((B, S, ATT_W), BF16),
        scratch_shapes=scratch,
        compiler_params=_params(2),
        name="attn_window" if local else "attn_ctx",
    )(*args)


def _gla_consts(rev):
    t = np.arange(GLA_CHUNK)[:, None]
    s = np.arange(GLA_CHUNK)[None, :]
    same = (t // GLA_SUB) == (s // GLA_SUB)
    before = (s >= t) if rev else (s <= t)
    return np.concatenate([same & before, before], axis=0).astype(np.float32)


def _head_blocks():
    h = np.arange(GLA_W) // HEAD_DIM
    return (h[:, None] == h[None, :]).astype(np.float32)


def _gla_last_row(blk, rev):
    return blk * GLA_SUB if rev else blk * GLA_SUB + GLA_SUB - 1


def _gla_prepare(q, k, g, cum, rev):
    n_sub = GLA_CHUNK // GLA_SUB
    g2 = g * LOG2_E
    g_hi = g2.astype(BF16)
    rem = g2 - g_hi.astype(F32)
    g_mid = rem.astype(BF16)
    g_lo = (rem - g_mid.astype(F32)).astype(BF16)
    cs = _dot(cum, g_hi) + _dot(cum, g_mid) + _dot(cum, g_lo)
    beta, bch = cs[0:GLA_CHUNK], cs[GLA_CHUNK:2 * GLA_CHUNK]
    gam = jnp.concatenate(
        [beta[_gla_last_row(b, rev):_gla_last_row(b, rev) + 1, :] - beta[b * GLA_SUB:(b + 1) * GLA_SUB, :]
         for b in range(n_sub)], axis=0)
    end_row = _gla_last_row(0 if rev else n_sub - 1, rev)
    dch = bch[end_row:end_row + 1, :] - bch
    f32_rows = jnp.concatenate([beta, bch, q * jnp.exp2(beta), k * jnp.exp2(gam)], axis=0)
    b16_rows = jnp.concatenate([(q * jnp.exp2(bch)).astype(BF16), (k * jnp.exp2(dch)).astype(BF16)],
                               axis=0)
    return f32_rows, b16_rows


def _gla_causal(rev):
    t = (np.arange(GLA_HEADS * GLA_CHUNK) % GLA_CHUNK)[:, None]
    s = np.arange(GLA_CHUNK)[None, :]
    return ((s >= t) if rev else (s <= t)).astype(np.float32)


def _gla_prepare_fast(q, k, v, g, cum_chunk, head_mask, rev):
    g2 = g * LOG2_E
    g_hi = g2.astype(BF16)
    rem = g2 - g_hi.astype(F32)
    g_mid = rem.astype(BF16)
    g_lo = (rem - g_mid.astype(F32)).astype(BF16)
    bch = _dot(cum_chunk, g_hi) + _dot(cum_chunk, g_mid) + _dot(cum_chunk, g_lo)
    end_row = 0 if rev else GLA_CHUNK - 1
    mid_row = GLA_CHUNK // 2 if rev else GLA_CHUNK // 2 - 1
    b_end = bch[end_row:end_row + 1, :]
    b_mid = bch[mid_row:mid_row + 1, :]
    ok = (jnp.max(-b_end) <= GLA_FAST_RANGE).astype(jnp.int32)
    q_mid = (q * jnp.exp2(jnp.minimum(bch - b_mid, GLA_FAST_RANGE))).astype(BF16)
    k_mid = (k * jnp.exp2(jnp.minimum(b_mid - bch, GLA_FAST_RANGE))).astype(BF16)
    zero = jnp.zeros_like(q_mid)
    rows = jnp.concatenate(
        [jnp.where(hm, q_mid, zero) for hm in head_mask]
        + [k_mid, (q * jnp.exp2(bch)).astype(BF16), (k * jnp.exp2(b_end - bch)).astype(BF16),
           v.astype(BF16)], axis=0)
    return rows, jnp.exp2(b_end), ok


def _gla_tile_fast(pb_scr, dec_scr, dirs, bd, head_mask, n_chunks):
    n_q = GLA_HEADS * GLA_CHUNK
    work = []
    for d, (_, _, _, st_ref, o_ref, caus_ref, rev) in enumerate(dirs):
        for c in (range(n_chunks - 1, -1, -1) if rev else range(n_chunks)):
            pb = pb_scr.at[d, c]
            vb = pb[n_q + 3 * GLA_CHUNK:n_q + 4 * GLA_CHUNK]
            upd = _dot_tn(vb, pb[n_q + 2 * GLA_CHUNK:n_q + 3 * GLA_CHUNK])
            a = _dot_nt(pb[0:n_q], pb[n_q:n_q + GLA_CHUNK])
            work.append(dict(d=d, c=c, pb=pb, vb=vb, upd=upd, a=a, o_ref=o_ref, caus=caus_ref))
    for d, (_, _, _, st_ref, _, _, _) in enumerate(dirs):
        st = st_ref[...]
        for w in (w for w in work if w["d"] == d):
            w["o_inter"] = _dot_nt(w["pb"][n_q + GLA_CHUNK:n_q + 2 * GLA_CHUNK], st.astype(BF16))
            st = st * dec_scr[d, w["c"], 0:1, :] + w["upd"] * bd
        st_ref[...] = st
    for w in work:
        w["r"] = _dot((w["a"] * w["caus"][...]).astype(BF16), w["vb"])
    for w in work:
        o = w["o_inter"]
        for hh, hm in enumerate(head_mask):
            o = o + jnp.where(hm, w["r"][hh * GLA_CHUNK:(hh + 1) * GLA_CHUNK, :], 0.0)
        w["o_ref"][w["c"] * GLA_CHUNK:(w["c"] + 1) * GLA_CHUNK, :] = o


def _gla_chunk(q, k, v, pf_ref, pb_ref, st_ref, o_ref, r0, ones, bd, head_mask, rev):
    n_sub = GLA_CHUNK // GLA_SUB
    order = list(range(n_sub))[::-1] if rev else list(range(n_sub))

    def rows(a, blk):
        return a[blk * GLA_SUB:(blk + 1) * GLA_SUB, :]

    half = GLA_SUB // 2
    beta = pf_ref[0:GLA_CHUNK]
    q_loc = pf_ref[2 * GLA_CHUNK:3 * GLA_CHUNK]
    k_loc = pf_ref[3 * GLA_CHUNK:4 * GLA_CHUNK]
    g_tot = [beta[_gla_last_row(b, rev):_gla_last_row(b, rev) + 1, :] for b in range(n_sub)]
    end_row = GLA_CHUNK + _gla_last_row(order[-1], rev)
    b_end = pf_ref[end_row:end_row + 1, :]
    half_row = lax.broadcasted_iota(jnp.int32, (half, 1), 0)

    st = st_ref[...]
    o_inter = _dot_nt(pb_ref[0:GLA_CHUNK], st.astype(BF16))
    upd = _dot_tn(v.astype(BF16), pb_ref[GLA_CHUNK:2 * GLA_CHUNK])
    scores, values = {}, {}
    for p in range(1, n_sub):
        ks, vs = [], []
        for pp in range(p):
            sb = order[pp]
            kk = rows(k_loc, sb)
            mids = [order[x] for x in range(pp + 1, p)]
            if mids:
                tot = g_tot[mids[0]]
                for mb in mids[1:]:
                    tot = tot + g_tot[mb]
                kk = kk * jnp.exp2(tot)
            ks.append(kk)
            vs.append(rows(v, sb))
        qt = rows(q_loc, order[p])
        q_heads = jnp.concatenate([jnp.where(hm, qt, 0.0) for hm in head_mask], axis=0)
        scores[p] = _dot_nt(q_heads.astype(BF16), jnp.concatenate(ks, axis=0).astype(BF16))
        values[p] = jnp.concatenate(vs, axis=0).astype(BF16)
    yield

    pairs = []
    for s in range(GLA_SUB):
        hs = s // half
        for hb in (range(0, hs + 1) if rev else range(hs, GLA_SUB // half)):
            pairs.append((s, hb))
    sums = {}
    for p, tb in enumerate(order):
        bt, qb, kb = rows(beta, tb), rows(q, tb), rows(k, tb)
        slabs = []
        for s, hb in pairs:
            d = bt[hb * half:(hb + 1) * half, :] - bt[s:s + 1, :]
            if hb == s // half:
                keep = (half_row <= s % half) if rev else (half_row >= s % half)
                d = jnp.where(keep, d, -jnp.inf)
            slabs.append(qb[hb * half:(hb + 1) * half, :] * jnp.exp2(d) * kb[s:s + 1, :])
        sums[p] = _dot(jnp.concatenate(slabs, axis=0).astype(BF16), ones)
        yield

    mixed = {p: _dot(scores[p].astype(BF16), values[p]) for p in range(1, n_sub)}
    yield

    for p, tb in enumerate(order):
        vb = rows(v, tb)
        acc = rows(o_inter, tb)
        if p > 0:
            for hh, hm in enumerate(head_mask):
                acc = acc + jnp.where(hm, mixed[p][hh * GLA_SUB:(hh + 1) * GLA_SUB, :], 0.0)
        halves = [acc[hb * half:(hb + 1) * half, :] for hb in range(GLA_SUB // half)]
        for idx, (s, hb) in enumerate(pairs):
            halves[hb] = halves[hb] + sums[p][idx * half:(idx + 1) * half, :] * vb[s:s + 1, :]
        o_ref[pl.ds(r0 + tb * GLA_SUB, GLA_SUB), :] = jnp.concatenate(halves, axis=0)
        yield

    st_ref[...] = st * jnp.exp2(b_end) + upd * bd


def _interleave(gens):
    active = list(gens)
    while active:
        for gen in list(active):
            try:
                next(gen)
            except StopIteration:
                active.remove(gen)


def _gla_kernel(cumf_ref, cumb_ref, ones_ref, bd_ref, causf_ref, causb_ref, glf_ref, glb_ref,
                gf_ref, gb_ref, s0f_ref, s0b_ref, of_ref, ob_ref, sff_ref, sfb_ref,
                stf_ref, stb_ref, pb_scr, dec_scr, ok_ref, *, n_chunks):
    i = pl.program_id(1)
    lane_head = lax.broadcasted_iota(jnp.int32, (1, GLA_W), 1) // HEAD_DIM
    head_mask = [lane_head == hh for hh in range(GLA_HEADS)]
    dirs = ((glf_ref, gf_ref, cumf_ref, stf_ref, of_ref, causf_ref, False),
            (glb_ref, gb_ref, cumb_ref, stb_ref, ob_ref, causb_ref, True))

    @pl.when(i == 0)
    def _():
        stf_ref[...] = s0f_ref[...]
        stb_ref[...] = s0b_ref[...]

    def load(gl_ref, g_ref, r0):
        rows = pl.ds(r0, GLA_CHUNK)
        return (gl_ref[rows, 0:GLA_W], gl_ref[rows, GLA_W:2 * GLA_W],
                gl_ref[rows, 2 * GLA_W:3 * GLA_W], g_ref[rows, :])

    for c in range(n_chunks):
        for d, (gl_ref, g_ref, cum_ref, _, _, _, rev) in enumerate(dirs):
            q, k, v, g = load(gl_ref, g_ref, c * GLA_CHUNK)
            rows, dec, ok = _gla_prepare_fast(q, k, v, g, cum_ref[GLA_CHUNK:2 * GLA_CHUNK, :],
                                              head_mask, rev)
            pb_scr[d, c] = rows
            dec_scr[d, c] = jnp.broadcast_to(dec, (8, GLA_W))
            ok_ref[d, c] = ok

    n_ok = ok_ref[0, 0]
    for d in range(2):
        for c in range(n_chunks):
            if (d, c) != (0, 0):
                n_ok = n_ok + ok_ref[d, c]
    fast = n_ok == 2 * n_chunks

    @pl.when(fast)
    def _():
        _gla_tile_fast(pb_scr, dec_scr, dirs, bd_ref[...], head_mask, n_chunks)

    @pl.when(jnp.logical_not(fast))
    def _():
        def chunk(ci, carry):
            cidx = (ci, n_chunks - 1 - ci)
            gens = []
            for d, (gl_ref, g_ref, cum_ref, st_ref, o_ref, _, rev) in enumerate(dirs):
                r0 = pl.multiple_of(cidx[d] * GLA_CHUNK, GLA_CHUNK)
                q, k, v, g = load(gl_ref, g_ref, r0)
                pf, pb = _gla_prepare(q, k, g, cum_ref[...], rev)
                gens.append(_gla_chunk(q, k, v, pf, pb, st_ref, o_ref, r0, ones_ref[...],
                                       bd_ref[...], head_mask, rev))
            _interleave(gens)
            return carry

        lax.fori_loop(0, n_chunks, chunk, 0)

    @pl.when(i == pl.num_programs(1) - 1)
    def _():
        sff_ref[...] = stf_ref[...]
        sfb_ref[...] = stb_ref[...]


def _gla(gl, gates, s0f, s0b, tile):
    B, S, _ = gl.shape
    nt = S // tile
    n_chunks = tile // GLA_CHUNK
    ones_bd = jnp.asarray(_head_blocks(), BF16)
    bd = jnp.asarray(_head_blocks(), F32)
    state_spec = pl.BlockSpec((None, GLA_W, GLA_W), lambda b, i: (b, 0, 0))
    return pl.pallas_call(
        functools.partial(_gla_kernel, n_chunks=n_chunks),
        grid=(B, nt),
        in_specs=[
            _const_spec((2 * GLA_CHUNK, GLA_CHUNK)),
            _const_spec((2 * GLA_CHUNK, GLA_CHUNK)),
            _const_spec((GLA_W, GLA_W)),
            _const_spec((GLA_W, GLA_W)),
            _const_spec((GLA_HEADS * GLA_CHUNK, GLA_CHUNK)),
            _const_spec((GLA_HEADS * GLA_CHUNK, GLA_CHUNK)),
            pl.BlockSpec((None, tile, 3 * GLA_W), lambda b, i: (b, i, 0)),
            pl.BlockSpec((None, tile, 3 * GLA_W), lambda b, i: (b, nt - 1 - i, 0)),
            pl.BlockSpec((None, tile, GLA_W), lambda b, i: (b, i, 0)),
            pl.BlockSpec((None, tile, GLA_W), lambda b, i: (b, nt - 1 - i, 1)),
            state_spec, state_spec,
        ],
        out_specs=[
            pl.BlockSpec((None, tile, GLA_W), lambda b, i: (b, i, 0)),
            pl.BlockSpec((None, tile, GLA_W), lambda b, i: (b, nt - 1 - i, 0)),
            state_spec, state_spec,
        ],
        out_shape=[jax.ShapeDtypeStruct((B, S, GLA_W), F32),
                   jax.ShapeDtypeStruct((B, S, GLA_W), F32),
                   jax.ShapeDtypeStruct((B, GLA_W, GLA_W), F32),
                   jax.ShapeDtypeStruct((B, GLA_W, GLA_W), F32)],
        scratch_shapes=[pltpu.VMEM((GLA_W, GLA_W), F32), pltpu.VMEM((GLA_W, GLA_W), F32),
                        pltpu.VMEM((2, n_chunks, (GLA_HEADS + 4) * GLA_CHUNK, GLA_W), BF16),
                        pltpu.VMEM((2, n_chunks, 8, GLA_W), F32),
                        pltpu.SMEM((2, n_chunks), jnp.int32)],
        compiler_params=_params(2),
        name="gla",
    )(jnp.asarray(_gla_consts(False), BF16), jnp.asarray(_gla_consts(True), BF16),
      ones_bd, bd, jnp.asarray(_gla_causal(False), F32), jnp.asarray(_gla_causal(True), F32),
      gl, gl, gates, gates, s0f, s0b)


def _post_kernel(x_ref, mod_ref, g2_ref, cu_ref, cup_ref, cun_ref, cb_ref, cw_ref, at_ref,
                 of_ref, ob_ref, gg_ref, ng_ref, ones_ref, wo_ref, wu_ref, wd_ref,
                 *rest, final):
    if final:
        fg_ref, o_ref, acc_ref = rest
    else:
        o_ref, acc_ref = rest
    i = pl.program_id(1)
    n = pl.num_programs(1)
    tm = x_ref.shape[0]

    u = cu_ref[...]
    row = lax.broadcasted_iota(jnp.int32, (tm, 1), 0)
    prev_row = jnp.where(i > 0, cup_ref[7:8, :], 0.0)
    next_row = jnp.where(i < n - 1, cun_ref[0:1, :], 0.0)
    u_prev = jnp.where(row == 0, prev_row, pltpu.roll(u, 1, axis=0))
    u_next = jnp.where(row == tm - 1, next_row, pltpu.roll(u, tm - 1, axis=0))
    conv = cb_ref[...] * (cw_ref[0:1, :] * u_prev + cw_ref[1:2, :] * u + cw_ref[2:3, :] * u_next)

    o = of_ref[...] + ob_ref[...]
    sq = o * o
    sq_hi = sq.astype(BF16)
    sq_lo = (sq - sq_hi.astype(F32)).astype(BF16)
    ms = (_dot(sq_hi, ones_ref[...]) + _dot(sq_lo, ones_ref[...])) * (1.0 / HEAD_DIM)
    gla = o * lax.rsqrt(ms + EPS) * ng_ref[...] * _silu(gg_ref[...])

    mix = (_dot(conv.astype(BF16), wo_ref[0:CONV_CH, :])
           + _dot(at_ref[...], wo_ref[CONV_CH:CONV_CH + ATT_W, :])
           + _dot(gla.astype(BF16), wo_ref[CONV_CH + ATT_W:, :]))
    x1 = x_ref[...] + mod_ref[2:3, :] * mix

    ms2 = jnp.mean(x1 * x1, axis=-1, keepdims=True)
    y2 = x1 * lax.rsqrt(ms2 + EPS) * g2_ref[...]
    h2 = (y2 * (1.0 + mod_ref[4:5, :]) + mod_ref[3:4, :]).astype(BF16)

    def up(jc):
        if isinstance(jc, int):
            ca, cb = jc * FF_CHUNK, D_FF + jc * FF_CHUNK
        else:
            ca = pl.multiple_of(jc * FF_CHUNK, FF_CHUNK)
            cb = pl.multiple_of(D_FF + jc * FF_CHUNK, FF_CHUNK)
        a = _dot(h2, wu_ref[:, pl.ds(ca, FF_CHUNK)])
        b = _dot(h2, wu_ref[:, pl.ds(cb, FF_CHUNK)])
        return a, b

    def act(ab):
        return (_silu(ab[0]) * ab[1]).astype(BF16)

    prev = act(up(0))
    ab = up(1)
    acc_ref[...] = _dot(prev, wd_ref[0])
    prev = act(ab)

    def ff(t, prev):
        for u in range(FF_UNROLL):
            jc = 2 + t * FF_UNROLL + u
            ab = up(jc)
            acc_ref[...] += _dot(prev, wd_ref[jc - 1])
            prev = act(ab)
        return prev

    prev = lax.fori_loop(0, (N_FF_CHUNKS - 2) // FF_UNROLL, ff, prev)
    x2 = x1 + mod_ref[5:6, :] * (acc_ref[...] + _dot(prev, wd_ref[N_FF_CHUNKS - 1]))
    if final:
        msf = jnp.mean(x2 * x2, axis=-1, keepdims=True)
        x2 = x2 * lax.rsqrt(msf + EPS) * fg_ref[...]
    o_ref[...] = x2


def _post(xs, mods_l, mod_row, g2, cu, cb, conv_w, attn, o_f, o_b, gl, norm_g, w_out, wu, wd,
          final_g, tm):
    B, S, D = xs.shape
    nt = S // tm
    hb = tm // 8
    row = lambda w: pl.BlockSpec((None, tm, w), lambda b, i: (b, i, 0))
    if mod_row is None:
        mod_spec = pl.BlockSpec((None, 8, D), lambda b, i: (b, 0, 0))
    else:
        mod_spec = pl.BlockSpec((None, 8, D), lambda b, i: (mod_row, 0, 0))
    final = final_g is not None
    in_specs = [
        row(D), mod_spec, _const_spec((1, D)),
        row(CONV_CH),
        pl.BlockSpec((None, 8, CONV_CH), lambda b, i: (b, jnp.maximum(i * hb - 1, 0), 0)),
        pl.BlockSpec((None, 8, CONV_CH), lambda b, i: (b, jnp.minimum((i + 1) * hb, S // 8 - 1), 0)),
        row(CONV_CH), _const_spec((8, CONV_CH)), row(ATT_W), row(GLA_W), row(GLA_W),
        pl.BlockSpec((None, tm, GLA_W), lambda b, i: (b, i, 3)),
        _const_spec((1, GLA_W)), _const_spec((GLA_W, GLA_W)), _const_spec((D, D)),
        _const_spec((D, 2 * D_FF)), _const_spec((N_FF_CHUNKS, FF_CHUNK, D)),
    ]
    args = [xs, mods_l, g2, cu, cu, cu, cb, conv_w, attn, o_f, o_b, gl, norm_g,
            jnp.asarray(_head_blocks(), BF16), w_out, wu, wd]
    if final:
        in_specs.append(_const_spec((1, D)))
        args.append(final_g)
    return pl.pallas_call(
        functools.partial(_post_kernel, final=final),
        grid=(B, nt),
        in_specs=in_specs,
        out_specs=row(D),
        out_shape=jax.ShapeDtypeStruct((B, S, D), F32),
        scratch_shapes=[pltpu.VMEM((tm, D), F32)],
        compiler_params=_params(2),
        name="post_final" if final else "post",
    )(*args)


def _rope_tables(seq):
    t = jnp.arange(seq)
    pos = jnp.stack([(t // GRID_W).astype(F32), (t % GRID_W).astype(F32)], axis=1)
    n_freq = HEAD_DIM // 4
    inv_freq = ROPE_BASE ** (-jnp.arange(n_freq, dtype=F32) / n_freq)
    ang = pos[:, :, None] * inv_freq
    cos, sin = jnp.cos(ang), jnp.sin(ang)
    zero = jnp.zeros_like(sin)
    cos_t = jnp.stack([cos, cos], axis=2).reshape(seq, HEAD_DIM)
    sa_t = jnp.stack([-sin, zero], axis=2).reshape(seq, HEAD_DIM)
    sb_t = jnp.stack([zero, sin], axis=2).reshape(seq, HEAD_DIM)
    rep = LANES // HEAD_DIM
    return tuple(jnp.tile(a, (1, rep)) for a in (cos_t, sa_t, sb_t))


def _tile_rows(S, want):
    t = min(want, S)
    while S % t:
        t //= 2
    return t


def kernel(x, c, ctx, c_ctx, w_mod, b_mod, norm1_g, norm2_g, w_in, conv_w, attn_sink,
           gla_gate_w, gla_gate_b, gla_norm_g, w_out, w_up, w_down, final_norm_g):
    B, S, D = x.shape
    L = w_mod.shape[0]
    Lc = ctx.shape[1]
    assert D == D_MODEL and S % WINDOW == 0 and Lc % GLA_CHUNK == 0 and B + 1 <= 8

    cvec = jnp.zeros((8, D), F32).at[:B].set(c).at[B].set(c_ctx)
    mods = _modulation(cvec, w_mod, b_mod).reshape(L, 8, N_MOD, D)
    mods = jnp.pad(mods, ((0, 0), (0, 0), (0, 8 - N_MOD), (0, 0)))

    order = np.array(Q_HEAD_ORDER)
    wq = w_in[:, :, 768:1280].reshape(L, D, ATT_HEADS, HEAD_DIM)[:, :, order].reshape(L, D, ATT_W)
    w_in_p = jnp.concatenate(
        [w_in[:, :, :768], wq, w_in[:, :, 1280:1536], w_in[:, :, 2560:2592],
         jnp.zeros((L, D, LANES - 2 * GLA_RANK), F32), w_in[:, :, 1536:2560]], axis=2).astype(BF16)
    wo_att = w_out[:, 256:768].reshape(L, ATT_HEADS, HEAD_DIM, D)[:, order].reshape(L, ATT_W, D)
    w_out_p = jnp.concatenate([w_out[:, :256], wo_att, w_out[:, 768:]], axis=1).astype(BF16)
    wg = jnp.zeros((L, LANES, 2 * GLA_W), F32)
    wg = wg.at[:, :GLA_RANK, :GLA_W].set(gla_gate_w[:, 0])
    wg = wg.at[:, GLA_RANK:2 * GLA_RANK, GLA_W:].set(gla_gate_w[:, 1]).astype(BF16)
    bg = gla_gate_b.reshape(L, 1, 2 * GLA_W)
    cw = jnp.pad(conv_w, ((0, 0), (0, 8 - CONV_K), (0, 0)))
    ng = jnp.tile(gla_norm_g, (1, GLA_HEADS)).reshape(L, 1, GLA_W)
    wu = w_up.astype(BF16)
    wd = w_down.reshape(L, N_FF_CHUNKS, FF_CHUNK, D).astype(BF16)
    g1 = norm1_g.reshape(L, 1, D)
    g2 = norm2_g.reshape(L, 1, D)

    rope_tabs = _rope_tables(S)
    tm = _tile_rows(S, 512)
    proj_tm = _tile_rows(S, 1024)
    gla_tile = _tile_rows(S, 512)
    zero_state = jnp.zeros((B, GLA_W, GLA_W), F32)

    for l in range(L):
        last = l == L - 1
        cu, cb, q, k, v, gl, gates = _proj(x, mods[l], None, g1[l], w_in_p[l], wg[l], bg[l],
                                           rope_tabs, proj_tm)
        ccu, ccb, cq, ck, cv, cgl, cgates = _proj(ctx, mods[l], B, g1[l], w_in_p[l], wg[l], bg[l],
                                                  None, Lc)
        attn = _attention(q, k, v, ck, cv, attn_sink[l], local=True)
        oc_f, oc_b, sc_f, sc_b = _gla(cgl, cgates, zero_state, zero_state, Lc)
        o_f, o_b, _, _ = _gla(gl, gates, sc_f, sc_b, gla_tile)
        fin = final_norm_g.reshape(1, D) if last else None
        x = _post(x, mods[l], None, g2[l], cu, cb, cw[l], attn, o_f, o_b, gl, ng[l],
                  w_out_p[l], wu[l], wd[l], fin, tm)
        if not last:
            attn_c = _attention(cq, None, None, ck, cv, attn_sink[l], local=False)
            ctx = _post(ctx, mods[l], B, g2[l], ccu, ccb, cw[l], attn_c, oc_f, oc_b, cgl, ng[l],
                        w_out_p[l], wu[l], wd[l], None, Lc)
    return x
```

```python
import functools

import numpy as np
import jax
import jax.numpy as jnp
from jax import lax
from jax.experimental import pallas as pl
from jax.experimental.pallas import tpu as pltpu

F32 = jnp.float32
BF16 = jnp.bfloat16

D_MODEL = 1024
HEAD_DIM = 64
CONV_CH = 256
CONV_K = 3
ATT_HEADS = 8
ATT_KV_HEADS = 2
ATT_W = ATT_HEADS * HEAD_DIM
KV_W = ATT_KV_HEADS * HEAD_DIM
WINDOW = 128
GLA_HEADS = 4
GLA_W = GLA_HEADS * HEAD_DIM
GLA_RANK = 16
GLA_NORMALIZER = 16.0
D_FF = 2816
N_MOD = 6
GRID_W = 64
ROPE_BASE = 10000.0
EPS = 1e-6
LOG2_E = 1.4426950408889634

LANES = 128
C_CONV = 0
C_QK = 768
C_V = 1408
C_LR = 1536
C_GLA = 1664
N_IN_PAD = C_GLA + 4 * GLA_W
Q_HEAD_ORDER = (0, 4, 1, 5, 2, 6, 3, 7)

ATT_ROW_GROUP = 16
GLA_CHUNK = 64
GLA_SUB = 16
GLA_FAST_RANGE = 60.0
FF_CHUNK = 256
N_FF_CHUNKS = D_FF // FF_CHUNK
POST_ROW_GROUPS = 2
FF_UNROLL = 3
assert (N_FF_CHUNKS - 2) % FF_UNROLL == 0

VMEM_LIMIT = 56 * 1024 * 1024


def _const_spec(shape):
    nd = len(shape)
    return pl.BlockSpec(shape, lambda *_: (0,) * nd, pipeline_mode=pl.Buffered(1))


def _layer_spec(shape, layer):
    nd = len(shape)
    return pl.BlockSpec((None,) + tuple(shape), lambda *_: (layer,) + (0,) * nd,
                        pipeline_mode=pl.Buffered(1))


def _params(n_grid):
    return pltpu.CompilerParams(
        dimension_semantics=("arbitrary",) * n_grid, vmem_limit_bytes=VMEM_LIMIT)


def _silu(a):
    return a * (1.0 / (1.0 + jnp.exp(-a)))


def _dot(a, b):
    return jnp.dot(a, b, preferred_element_type=F32)


def _dot_nt(a, b):
    return lax.dot_general(a, b, (((1,), (1,)), ((), ())), preferred_element_type=F32)


def _dot_tn(a, b):
    return lax.dot_general(a, b, (((0,), (0,)), ((), ())), preferred_element_type=F32)


def _mod_kernel(c_ref, w_ref, b_ref, o_ref):
    a = _silu(c_ref[...]).astype(BF16)
    o_ref[...] = _dot(a, w_ref[...].astype(BF16)) + b_ref[...]


def _modulation(cvec, w_mod, b_mod):
    L, D, W = w_mod.shape
    tn = 2048
    return pl.pallas_call(
        _mod_kernel,
        grid=(L, W // tn),
        in_specs=[
            pl.BlockSpec((8, D), lambda l, j: (0, 0)),
            pl.BlockSpec((None, D, tn), lambda l, j: (l, 0, j)),
            pl.BlockSpec((None, 1, tn), lambda l, j: (l, 0, j)),
        ],
        out_specs=pl.BlockSpec((None, 8, tn), lambda l, j: (l, 0, j)),
        out_shape=jax.ShapeDtypeStruct((L, 8, W), F32),
        compiler_params=_params(2),
        name="modulation",
    )(cvec, w_mod, b_mod.reshape(L, 1, W))


def _proj_kernel(x_ref, mod_ref, g1_ref, w_ref, wg_ref, bg_ref, *rest, rope):
    if rope:
        cos_ref, sa_ref, sb_ref = rest[:3]
        rest = rest[3:]
    cu_ref, cb_ref, q_ref, k_ref, v_ref, gl_ref, gate_ref = rest

    x = x_ref[...]
    ms = jnp.mean(x * x, axis=-1, keepdims=True)
    y = x * lax.rsqrt(ms + EPS) * g1_ref[...]
    h = (y * (1.0 + mod_ref[1:2, :]) + mod_ref[0:1, :]).astype(BF16)

    def proj(lo, hi):
        return _dot(h, w_ref[:, lo:hi])

    pc = proj(C_CONV, C_QK)
    cu_ref[...] = pc[:, 2 * CONV_CH:3 * CONV_CH] * pc[:, 0:CONV_CH]
    cb_ref[...] = pc[:, CONV_CH:2 * CONV_CH]

    pqk = proj(C_QK, C_GLA)
    scale = HEAD_DIM ** -0.5
    for m in range((ATT_W + KV_W) // LANES):
        col = pqk[:, m * LANES:(m + 1) * LANES]
        if rope:
            col = (col * cos_ref[...]
                   + pltpu.roll(col, LANES - HEAD_DIM // 4, axis=1) * sa_ref[...]
                   + pltpu.roll(col, HEAD_DIM // 4, axis=1) * sb_ref[...])
        if m < ATT_W // LANES:
            q_ref[:, m * LANES:(m + 1) * LANES] = (col * scale).astype(BF16)
        else:
            k_ref[...] = col.astype(BF16)

    v_ref[...] = pqk[:, C_V - C_QK:C_LR - C_QK].astype(BF16)

    pg = proj(C_GLA, N_IN_PAD)
    gl_ref[:, 0:GLA_W] = pg[:, 0:GLA_W] * (HEAD_DIM ** -0.5)
    gl_ref[:, GLA_W:4 * GLA_W] = pg[:, GLA_W:4 * GLA_W]

    lr = pqk[:, C_LR - C_QK:C_GLA - C_QK].astype(BF16)
    z = _dot(lr, wg_ref[...]) + bg_ref[...]
    log_sig = jnp.minimum(z, 0.0) - jnp.log1p(jnp.exp(-jnp.abs(z)))
    gate_ref[...] = log_sig * (1.0 / GLA_NORMALIZER)


def _proj(xs, mods_l, mod_row, g1, w_in, layer, wg, bg, rope_tabs, tm):
    B, S, D = xs.shape
    rope = rope_tabs is not None
    nt = S // tm
    row = lambda w: pl.BlockSpec((None, tm, w), lambda b, i: (b, i, 0))
    if mod_row is None:
        mod_spec = pl.BlockSpec((None, 8, D), lambda b, i: (b, 0, 0))
    else:
        mod_spec = pl.BlockSpec((None, 8, D), lambda b, i: (mod_row, 0, 0))
    in_specs = [row(D), mod_spec, _const_spec((1, D)), _layer_spec((D, N_IN_PAD), layer),
                _const_spec((LANES, 2 * GLA_W)), _const_spec((1, 2 * GLA_W))]
    args = [xs, mods_l, g1, w_in, wg, bg]
    if rope:
        in_specs += [pl.BlockSpec((tm, LANES), lambda b, i: (i, 0))] * 3
        args += list(rope_tabs)
    widths = (CONV_CH, CONV_CH, ATT_W, KV_W, KV_W, 4 * GLA_W, 2 * GLA_W)
    dtypes = (F32, F32, BF16, BF16, BF16, F32, F32)
    return pl.pallas_call(
        functools.partial(_proj_kernel, rope=rope),
        grid=(B, nt),
        in_specs=in_specs,
        out_specs=[row(w) for w in widths],
        out_shape=[jax.ShapeDtypeStruct((B, S, w), dt) for w, dt in zip(widths, dtypes)],
        compiler_params=_params(2),
        name="proj_rope" if rope else "proj_ctx",
    )(*args)


def _attn_kernel(sink_ref, q_ref, kc_ref, vc_ref, *rest, local, n_sub):
    tb = q_ref.shape[0] // n_sub
    n_pairs = ATT_W // LANES
    n_ctx = kc_ref.shape[0]
    rg = ATT_ROW_GROUP
    if local:
        kp_ref, km_ref, kn_ref, vp_ref, vm_ref, vn_ref, o_ref, s_scr, p_scr, e_scr, bias_scr = rest
        i = pl.program_id(1)
        n = pl.num_programs(1)
        k_loc = jnp.concatenate([kp_ref[...], km_ref[...], kn_ref[...]], axis=0)
        v_loc = jnp.concatenate([vp_ref[...], vm_ref[...], vn_ref[...]], axis=0)
        r = lax.broadcasted_iota(jnp.int32, (tb, tb), 0)
        j = lax.broadcasted_iota(jnp.int32, (tb, tb), 1)
        bias_prev = jnp.where(j >= r, 0.0, -jnp.inf)
        bias_next = jnp.where(j <= r, 0.0, -jnp.inf)
        bias_scr[0] = bias_prev + jnp.where(i > 0, 0.0, -jnp.inf)
        bias_scr[1] = bias_prev
        bias_scr[2] = bias_next
        bias_scr[3] = bias_next + jnp.where(i < n - 1, 0.0, -jnp.inf)
        nk = n_ctx + 3 * tb
    else:
        o_ref, s_scr, p_scr, e_scr = rest
        nk = n_ctx
    lane = lax.broadcasted_iota(jnp.int32, (1, LANES), 1)
    lo = lane < HEAD_DIM
    hi = jnp.logical_not(lo)
    lo_k = lax.broadcasted_iota(jnp.int32, (nk, LANES), 1) < HEAD_DIM
    group_ones = [jnp.where(lo_k, 1.0, 0.0).astype(BF16), jnp.where(lo_k, 0.0, 1.0).astype(BF16)]

    for jb in range(n_sub):
        q = q_ref[jb * tb:(jb + 1) * tb, :]
        lhs = jnp.concatenate([q[:, m * LANES:(m + 1) * LANES] for m in range(n_pairs)], axis=0)
        if local:
            kk = jnp.concatenate([kc_ref[...], k_loc[jb * tb:(jb + 3) * tb]], axis=0)
            vv = jnp.concatenate([vc_ref[...], v_loc[jb * tb:(jb + 3) * tb]], axis=0)
        else:
            kk, vv = kc_ref[...], vc_ref[...]
        k2 = jnp.concatenate([kk * group_ones[0], kk * group_ones[1]], axis=0)
        v2 = jnp.concatenate(
            [jnp.concatenate([vv * group_ones[0], group_ones[0]], axis=1),
             jnp.concatenate([vv * group_ones[1], group_ones[1]], axis=1)], axis=0)
        s_scr[jb] = _dot_nt(lhs, k2)

        for m in range(n_pairs):
            sinks = [sink_ref[m + (ATT_HEADS // ATT_KV_HEADS) * g] for g in range(ATT_KV_HEADS)]
            for gi in range(tb // rg):
                rq = gi * rg
                r0 = m * tb + rq
                s = s_scr[jb, r0:r0 + rg, :]
                probs, sink_terms = [], []
                for g in range(ATT_KV_HEADS):
                    sg = s[:, g * nk:(g + 1) * nk]
                    if local:
                        bp = bias_scr[0 if jb == 0 else 1, rq:rq + rg, :]
                        bn = bias_scr[3 if jb == n_sub - 1 else 2, rq:rq + rg, :]
                        sg = jnp.concatenate(
                            [sg[:, :n_ctx], sg[:, n_ctx:n_ctx + tb] + bp,
                             sg[:, n_ctx + tb:n_ctx + 2 * tb], sg[:, n_ctx + 2 * tb:] + bn], axis=1)
                    mx = jnp.maximum(jnp.max(sg, axis=-1, keepdims=True), sinks[g])
                    probs.append(jnp.exp(sg - mx).astype(BF16))
                    sink_terms.append(jnp.exp(sinks[g] - mx))
                p_scr[jb, r0:r0 + rg, :] = jnp.concatenate(probs, axis=1)
                e_scr[jb, r0:r0 + rg, :] = jnp.where(lo, sink_terms[0], sink_terms[1])

        o_aug = _dot(p_scr[jb], v2)
        o = o_aug[:, :LANES] * (1.0 / (o_aug[:, LANES:] + e_scr[jb]))
        for m in range(n_pairs):
            o_ref[jb * tb:(jb + 1) * tb, m * LANES:(m + 1) * LANES] = (
                o[m * tb:(m + 1) * tb].astype(BF16))


def _attention(q, k, v, k_ctx, v_ctx, sink, local, n_sub=4):
    B, S, _ = q.shape
    Lc = k_ctx.shape[1]
    if not local:
        n_sub = 1
    tq = WINDOW * n_sub if local else S
    nq = S // tq
    in_specs = [
        pl.BlockSpec(memory_space=pltpu.SMEM),
        pl.BlockSpec((None, tq, ATT_W), lambda b, i: (b, i, 0)),
        pl.BlockSpec((None, Lc, KV_W), lambda b, i: (b, 0, 0)),
        pl.BlockSpec((None, Lc, KV_W), lambda b, i: (b, 0, 0)),
    ]
    args = [sink, q, k_ctx, v_ctx]
    if local:
        nw = S // WINDOW
        nb = [pl.BlockSpec((None, WINDOW, KV_W), lambda b, i: (b, jnp.maximum(i * n_sub - 1, 0), 0)),
              pl.BlockSpec((None, tq, KV_W), lambda b, i: (b, i, 0)),
              pl.BlockSpec((None, WINDOW, KV_W),
                           lambda b, i: (b, jnp.minimum((i + 1) * n_sub, nw - 1), 0))]
        in_specs += nb + nb
        args += [k, k, k, v, v, v]
    tb = tq // n_sub
    nk = Lc + (3 * tb if local else 0)
    n_rows = (ATT_W // LANES) * tb
    scratch = [pltpu.VMEM((n_sub, n_rows, ATT_KV_HEADS * nk), F32),
               pltpu.VMEM((n_sub, n_rows, ATT_KV_HEADS * nk), BF16),
               pltpu.VMEM((n_sub, n_rows, LANES), F32)]
    if local:
        scratch.append(pltpu.VMEM((4, tb, tb), F32))
    return pl.pallas_call(
        functools.partial(_attn_kernel, local=local, n_sub=n_sub),
        grid=(B, nq),
        in_specs=in_specs,
        out_specs=pl.BlockSpec((None, tq, ATT_W), lambda b, i: (b, i, 0)),
        out_shape=jax.ShapeDtypeStruct((B, S, ATT_W), BF16),
        scratch_shapes=scratch,
        compiler_params=_params(2),
        name="attn_window" if local else "attn_ctx",
    )(*args)


def _gla_consts(rev):
    t = np.arange(GLA_CHUNK)[:, None]
    s = np.arange(GLA_CHUNK)[None, :]
    same = (t // GLA_SUB) == (s // GLA_SUB)
    before = (s >= t) if rev else (s <= t)
    return np.concatenate([same & before, before], axis=0).astype(np.float32)


def _head_blocks():
    h = np.arange(GLA_W) // HEAD_DIM
    return (h[:, None] == h[None, :]).astype(np.float32)


def _gla_last_row(blk, rev):
    return blk * GLA_SUB if rev else blk * GLA_SUB + GLA_SUB - 1


def _gla_prepare(q, k, g, cum, rev):
    n_sub = GLA_CHUNK // GLA_SUB
    g2 = g * LOG2_E
    g_hi = g2.astype(BF16)
    rem = g2 - g_hi.astype(F32)
    g_mid = rem.astype(BF16)
    g_lo = (rem - g_mid.astype(F32)).astype(BF16)
    cs = _dot(cum, g_hi) + _dot(cum, g_mid) + _dot(cum, g_lo)
    beta, bch = cs[0:GLA_CHUNK], cs[GLA_CHUNK:2 * GLA_CHUNK]
    gam = jnp.concatenate(
        [beta[_gla_last_row(b, rev):_gla_last_row(b, rev) + 1, :] - beta[b * GLA_SUB:(b + 1) * GLA_SUB, :]
         for b in range(n_sub)], axis=0)
    end_row = _gla_last_row(0 if rev else n_sub - 1, rev)
    dch = bch[end_row:end_row + 1, :] - bch
    f32_rows = jnp.concatenate([beta, bch, q * jnp.exp2(beta), k * jnp.exp2(gam)], axis=0)
    b16_rows = jnp.concatenate([(q * jnp.exp2(bch)).astype(BF16), (k * jnp.exp2(dch)).astype(BF16)],
                               axis=0)
    return f32_rows, b16_rows


def _gla_causal(rev):
    t = (np.arange(GLA_HEADS * GLA_CHUNK) % GLA_CHUNK)[:, None]
    s = np.arange(GLA_CHUNK)[None, :]
    return ((s >= t) if rev else (s <= t)).astype(np.float32)


def _gla_prepare_fast(q, k, v, g, cum_chunk, head_mask, rev):
    g2 = g * LOG2_E
    g_hi = g2.astype(BF16)
    g_lo = (g2 - g_hi.astype(F32)).astype(BF16)
    bch = _dot(cum_chunk, g_hi) + _dot(cum_chunk, g_lo)
    end_row = 0 if rev else GLA_CHUNK - 1
    mid_row = GLA_CHUNK // 2 if rev else GLA_CHUNK // 2 - 1
    b_end = bch[end_row:end_row + 1, :]
    b_mid = bch[mid_row:mid_row + 1, :]
    ok = (jnp.max(-b_end) <= GLA_FAST_RANGE).astype(jnp.int32)
    q_mid = (q * jnp.exp2(bch - b_mid)).astype(BF16)
    k_mid = (k * jnp.exp2(b_mid - bch)).astype(BF16)
    zero = jnp.zeros_like(q_mid)
    rows = jnp.concatenate(
        [jnp.where(hm, q_mid, zero) for hm in head_mask]
        + [k_mid, (q * jnp.exp2(bch)).astype(BF16), (k * jnp.exp2(b_end - bch)).astype(BF16),
           v.astype(BF16)], axis=0)
    return rows, jnp.exp2(b_end), ok


def _gla_tile_fast(pb_scr, dec_scr, dirs, bd, head_mask, n_chunks):
    n_q = GLA_HEADS * GLA_CHUNK
    work = []
    for d, (_, _, _, st_ref, o_ref, caus_ref, rev) in enumerate(dirs):
        for c in (range(n_chunks - 1, -1, -1) if rev else range(n_chunks)):
            pb = pb_scr.at[d, c]
            vb = pb[n_q + 3 * GLA_CHUNK:n_q + 4 * GLA_CHUNK]
            upd = _dot_tn(vb, pb[n_q + 2 * GLA_CHUNK:n_q + 3 * GLA_CHUNK])
            a = _dot_nt(pb[0:n_q], pb[n_q:n_q + GLA_CHUNK])
            work.append(dict(d=d, c=c, pb=pb, vb=vb, upd=upd, a=a, o_ref=o_ref, caus=caus_ref))
    for d, (_, _, _, st_ref, _, _, _) in enumerate(dirs):
        st = st_ref[...]
        for w in (w for w in work if w["d"] == d):
            w["o_inter"] = _dot_nt(w["pb"][n_q + GLA_CHUNK:n_q + 2 * GLA_CHUNK], st.astype(BF16))
            st = st * dec_scr[d, w["c"], 0:1, :] + w["upd"] * bd
        st_ref[...] = st
    for w in work:
        w["r"] = _dot((w["a"] * w["caus"][...]).astype(BF16), w["vb"])
    for w in work:
        o = w["o_inter"]
        for hh, hm in enumerate(head_mask):
            o = o + jnp.where(hm, w["r"][hh * GLA_CHUNK:(hh + 1) * GLA_CHUNK, :], 0.0)
        w["o_ref"][w["c"] * GLA_CHUNK:(w["c"] + 1) * GLA_CHUNK, :] = o


def _gla_chunk(q, k, v, pf_ref, pb_ref, st_ref, o_ref, r0, ones, bd, head_mask, rev):
    n_sub = GLA_CHUNK // GLA_SUB
    order = list(range(n_sub))[::-1] if rev else list(range(n_sub))

    def rows(a, blk):
        return a[blk * GLA_SUB:(blk + 1) * GLA_SUB, :]

    half = GLA_SUB // 2
    beta = pf_ref[0:GLA_CHUNK]
    q_loc = pf_ref[2 * GLA_CHUNK:3 * GLA_CHUNK]
    k_loc = pf_ref[3 * GLA_CHUNK:4 * GLA_CHUNK]
    g_tot = [beta[_gla_last_row(b, rev):_gla_last_row(b, rev) + 1, :] for b in range(n_sub)]
    end_row = GLA_CHUNK + _gla_last_row(order[-1], rev)
    b_end = pf_ref[end_row:end_row + 1, :]
    half_row = lax.broadcasted_iota(jnp.int32, (half, 1), 0)

    st = st_ref[...]
    o_inter = _dot_nt(pb_ref[0:GLA_CHUNK], st.astype(BF16))
    upd = _dot_tn(v.astype(BF16), pb_ref[GLA_CHUNK:2 * GLA_CHUNK])
    scores, values = {}, {}
    for p in range(1, n_sub):
        ks, vs = [], []
        for pp in range(p):
            sb = order[pp]
            kk = rows(k_loc, sb)
            mids = [order[x] for x in range(pp + 1, p)]
            if mids:
                tot = g_tot[mids[0]]
                for mb in mids[1:]:
                    tot = tot + g_tot[mb]
                kk = kk * jnp.exp2(tot)
            ks.append(kk)
            vs.append(rows(v, sb))
        qt = rows(q_loc, order[p])
        q_heads = jnp.concatenate([jnp.where(hm, qt, 0.0) for hm in head_mask], axis=0)
        scores[p] = _dot_nt(q_heads.astype(BF16), jnp.concatenate(ks, axis=0).astype(BF16))
        values[p] = jnp.concatenate(vs, axis=0).astype(BF16)
    yield

    pairs = []
    for s in range(GLA_SUB):
        hs = s // half
        for hb in (range(0, hs + 1) if rev else range(hs, GLA_SUB // half)):
            pairs.append((s, hb))
    sums = {}
    for p, tb in enumerate(order):
        bt, qb, kb = rows(beta, tb), rows(q, tb), rows(k, tb)
        slabs = []
        for s, hb in pairs:
            d = bt[hb * half:(hb + 1) * half, :] - bt[s:s + 1, :]
            if hb == s // half:
                keep = (half_row <= s % half) if rev else (half_row >= s % half)
                d = jnp.where(keep, d, -jnp.inf)
            slabs.append(qb[hb * half:(hb + 1) * half, :] * jnp.exp2(d) * kb[s:s + 1, :])
        sums[p] = _dot(jnp.concatenate(slabs, axis=0).astype(BF16), ones)
        yield

    mixed = {p: _dot(scores[p].astype(BF16), values[p]) for p in range(1, n_sub)}
    yield

    for p, tb in enumerate(order):
        vb = rows(v, tb)
        acc = rows(o_inter, tb)
        if p > 0:
            for hh, hm in enumerate(head_mask):
                acc = acc + jnp.where(hm, mixed[p][hh * GLA_SUB:(hh + 1) * GLA_SUB, :], 0.0)
        halves = [acc[hb * half:(hb + 1) * half, :] for hb in range(GLA_SUB // half)]
        for idx, (s, hb) in enumerate(pairs):
            halves[hb] = halves[hb] + sums[p][idx * half:(idx + 1) * half, :] * vb[s:s + 1, :]
        o_ref[pl.ds(r0 + tb * GLA_SUB, GLA_SUB), :] = jnp.concatenate(halves, axis=0)
        yield

    st_ref[...] = st * jnp.exp2(b_end) + upd * bd


def _interleave(gens):
    active = list(gens)
    while active:
        for gen in list(active):
            try:
                next(gen)
            except StopIteration:
                active.remove(gen)


def _gla_kernel(cumf_ref, cumb_ref, ones_ref, bd_ref, causf_ref, causb_ref, glf_ref, glb_ref,
                gf_ref, gb_ref, s0f_ref, s0b_ref, of_ref, ob_ref, sff_ref, sfb_ref,
                stf_ref, stb_ref, pb_scr, dec_scr, ok_ref, *, n_chunks):
    i = pl.program_id(1)
    lane_head = lax.broadcasted_iota(jnp.int32, (1, GLA_W), 1) // HEAD_DIM
    head_mask = [lane_head == hh for hh in range(GLA_HEADS)]
    dirs = ((glf_ref, gf_ref, cumf_ref, stf_ref, of_ref, causf_ref, False),
            (glb_ref, gb_ref, cumb_ref, stb_ref, ob_ref, causb_ref, True))

    @pl.when(i == 0)
    def _():
        stf_ref[...] = s0f_ref[...]
        stb_ref[...] = s0b_ref[...]

    def load(gl_ref, g_ref, r0):
        rows = pl.ds(r0, GLA_CHUNK)
        return (gl_ref[rows, 0:GLA_W], gl_ref[rows, GLA_W:2 * GLA_W],
                gl_ref[rows, 2 * GLA_W:3 * GLA_W], g_ref[rows, :])

    for c in range(n_chunks):
        for d, (gl_ref, g_ref, cum_ref, _, _, _, rev) in enumerate(dirs):
            q, k, v, g = load(gl_ref, g_ref, c * GLA_CHUNK)
            rows, dec, ok = _gla_prepare_fast(q, k, v, g, cum_ref[GLA_CHUNK:2 * GLA_CHUNK, :],
                                              head_mask, rev)
            pb_scr[d, c] = rows
            dec_scr[d, c] = jnp.broadcast_to(dec, (8, GLA_W))
            ok_ref[d, c] = ok

    n_ok = ok_ref[0, 0]
    for d in range(2):
        for c in range(n_chunks):
            if (d, c) != (0, 0):
                n_ok = n_ok + ok_ref[d, c]
    fast = n_ok == 2 * n_chunks

    @pl.when(fast)
    def _():
        _gla_tile_fast(pb_scr, dec_scr, dirs, bd_ref[...], head_mask, n_chunks)

    @pl.when(jnp.logical_not(fast))
    def _():
        def chunk(ci, carry):
            cidx = (ci, n_chunks - 1 - ci)
            gens = []
            for d, (gl_ref, g_ref, cum_ref, st_ref, o_ref, _, rev) in enumerate(dirs):
                r0 = pl.multiple_of(cidx[d] * GLA_CHUNK, GLA_CHUNK)
                q, k, v, g = load(gl_ref, g_ref, r0)
                pf, pb = _gla_prepare(q, k, g, cum_ref[...], rev)
                gens.append(_gla_chunk(q, k, v, pf, pb, st_ref, o_ref, r0, ones_ref[...],
                                       bd_ref[...], head_mask, rev))
            _interleave(gens)
            return carry

        lax.fori_loop(0, n_chunks, chunk, 0)

    @pl.when(i == pl.num_programs(1) - 1)
    def _():
        sff_ref[...] = stf_ref[...]
        sfb_ref[...] = stb_ref[...]


def _gla(gl, gates, s0f, s0b, tile):
    B, S, _ = gl.shape
    nt = S // tile
    n_chunks = tile // GLA_CHUNK
    ones_bd = jnp.asarray(_head_blocks(), BF16)
    bd = jnp.asarray(_head_blocks(), F32)
    state_spec = pl.BlockSpec((None, GLA_W, GLA_W), lambda b, i: (b, 0, 0))
    return pl.pallas_call(
        functools.partial(_gla_kernel, n_chunks=n_chunks),
        grid=(B, nt),
        in_specs=[
            _const_spec((2 * GLA_CHUNK, GLA_CHUNK)),
            _const_spec((2 * GLA_CHUNK, GLA_CHUNK)),
            _const_spec((GLA_W, GLA_W)),
            _const_spec((GLA_W, GLA_W)),
            _const_spec((GLA_HEADS * GLA_CHUNK, GLA_CHUNK)),
            _const_spec((GLA_HEADS * GLA_CHUNK, GLA_CHUNK)),
            pl.BlockSpec((None, tile, 3 * GLA_W), lambda b, i: (b, i, 0)),
            pl.BlockSpec((None, tile, 3 * GLA_W), lambda b, i: (b, nt - 1 - i, 0)),
            pl.BlockSpec((None, tile, GLA_W), lambda b, i: (b, i, 0)),
            pl.BlockSpec((None, tile, GLA_W), lambda b, i: (b, nt - 1 - i, 1)),
            state_spec, state_spec,
        ],
        out_specs=[
            pl.BlockSpec((None, tile, GLA_W), lambda b, i: (b, i, 0)),
            pl.BlockSpec((None, tile, GLA_W), lambda b, i: (b, nt - 1 - i, 0)),
            state_spec, state_spec,
        ],
        out_shape=[jax.ShapeDtypeStruct((B, S, GLA_W), F32),
                   jax.ShapeDtypeStruct((B, S, GLA_W), F32),
                   jax.ShapeDtypeStruct((B, GLA_W, GLA_W), F32),
                   jax.ShapeDtypeStruct((B, GLA_W, GLA_W), F32)],
        scratch_shapes=[pltpu.VMEM((GLA_W, GLA_W), F32), pltpu.VMEM((GLA_W, GLA_W), F32),
                        pltpu.VMEM((2, n_chunks, (GLA_HEADS + 4) * GLA_CHUNK, GLA_W), BF16),
                        pltpu.VMEM((2, n_chunks, 8, GLA_W), F32),
                        pltpu.SMEM((2, n_chunks), jnp.int32)],
        compiler_params=_params(2),
        name="gla",
    )(jnp.asarray(_gla_consts(False), BF16), jnp.asarray(_gla_consts(True), BF16),
      ones_bd, bd, jnp.asarray(_gla_causal(False), F32), jnp.asarray(_gla_causal(True), F32),
      gl, gl, gates, gates, s0f, s0b)


def _post_kernel(x_ref, mod_ref, g2_ref, cu_ref, cup_ref, cun_ref, cb_ref, cw_ref, at_ref,
                 of_ref, ob_ref, gg_ref, ng_ref, ones_ref, wo_ref, wu_ref, wd_ref,
                 *rest, final):
    if final:
        fg_ref, o_ref, acc_ref = rest
    else:
        o_ref, acc_ref = rest
    i = pl.program_id(1)
    n = pl.num_programs(1)
    tm = x_ref.shape[0]

    u = cu_ref[...]
    row = lax.broadcasted_iota(jnp.int32, (tm, 1), 0)
    prev_row = jnp.where(i > 0, cup_ref[7:8, :], 0.0)
    next_row = jnp.where(i < n - 1, cun_ref[0:1, :], 0.0)
    u_prev = jnp.where(row == 0, prev_row, pltpu.roll(u, 1, axis=0))
    u_next = jnp.where(row == tm - 1, next_row, pltpu.roll(u, tm - 1, axis=0))
    conv = cb_ref[...] * (cw_ref[0:1, :] * u_prev + cw_ref[1:2, :] * u + cw_ref[2:3, :] * u_next)

    x1_parts, h2_parts = [], []
    n_groups = POST_ROW_GROUPS if tm % (POST_ROW_GROUPS * 16) == 0 else 1
    rows_per = tm // n_groups
    for gr in range(n_groups):
        rs = slice(gr * rows_per, (gr + 1) * rows_per)
        o = of_ref[rs, :] + ob_ref[rs, :]
        sq = o * o
        sq_hi = sq.astype(BF16)
        sq_lo = (sq - sq_hi.astype(F32)).astype(BF16)
        ms = (_dot(sq_hi, ones_ref[...]) + _dot(sq_lo, ones_ref[...])) * (1.0 / HEAD_DIM)
        gla = o * lax.rsqrt(ms + EPS) * ng_ref[...] * _silu(gg_ref[rs, :])

        mix = (_dot(conv[rs, :].astype(BF16), wo_ref[0:CONV_CH, :])
               + _dot(at_ref[rs, :], wo_ref[CONV_CH:CONV_CH + ATT_W, :])
               + _dot(gla.astype(BF16), wo_ref[CONV_CH + ATT_W:, :]))
        x1_g = x_ref[rs, :] + mod_ref[2:3, :] * mix

        ms2 = jnp.mean(x1_g * x1_g, axis=-1, keepdims=True)
        y2 = x1_g * lax.rsqrt(ms2 + EPS) * g2_ref[...]
        x1_parts.append(x1_g)
        h2_parts.append((y2 * (1.0 + mod_ref[4:5, :]) + mod_ref[3:4, :]).astype(BF16))
    x1 = jnp.concatenate(x1_parts, axis=0)
    h2 = jnp.concatenate(h2_parts, axis=0)

    def up(jc):
        if isinstance(jc, int):
            ca, cb = jc * FF_CHUNK, D_FF + jc * FF_CHUNK
        else:
            ca = pl.multiple_of(jc * FF_CHUNK, FF_CHUNK)
            cb = pl.multiple_of(D_FF + jc * FF_CHUNK, FF_CHUNK)
        a = _dot(h2, wu_ref[:, pl.ds(ca, FF_CHUNK)])
        b = _dot(h2, wu_ref[:, pl.ds(cb, FF_CHUNK)])
        return a, b

    def act(ab):
        return (_silu(ab[0]) * ab[1]).astype(BF16)

    prev = act(up(0))
    ab = up(1)
    acc_ref[...] = _dot(prev, wd_ref[0])
    prev = act(ab)

    def ff(t, prev):
        for u in range(FF_UNROLL):
            jc = 2 + t * FF_UNROLL + u
            ab = up(jc)
            acc_ref[...] += _dot(prev, wd_ref[jc - 1])
            prev = act(ab)
        return prev

    prev = lax.fori_loop(0, (N_FF_CHUNKS - 2) // FF_UNROLL, ff, prev)
    x2 = x1 + mod_ref[5:6, :] * (acc_ref[...] + _dot(prev, wd_ref[N_FF_CHUNKS - 1]))
    if final:
        msf = jnp.mean(x2 * x2, axis=-1, keepdims=True)
        x2 = x2 * lax.rsqrt(msf + EPS) * fg_ref[...]
    o_ref[...] = x2


def _post(xs, mods_l, mod_row, g2, cu, cb, conv_w, attn, o_f, o_b, gl, norm_g, w_out, wu, wd,
          layer, final_g, tm):
    B, S, D = xs.shape
    nt = S // tm
    hb = tm // 8
    row = lambda w: pl.BlockSpec((None, tm, w), lambda b, i: (b, i, 0))
    if mod_row is None:
        mod_spec = pl.BlockSpec((None, 8, D), lambda b, i: (b, 0, 0))
    else:
        mod_spec = pl.BlockSpec((None, 8, D), lambda b, i: (mod_row, 0, 0))
    final = final_g is not None
    in_specs = [
        row(D), mod_spec, _const_spec((1, D)),
        row(CONV_CH),
        pl.BlockSpec((None, 8, CONV_CH), lambda b, i: (b, jnp.maximum(i * hb - 1, 0), 0)),
        pl.BlockSpec((None, 8, CONV_CH), lambda b, i: (b, jnp.minimum((i + 1) * hb, S // 8 - 1), 0)),
        row(CONV_CH), _const_spec((8, CONV_CH)), row(ATT_W), row(GLA_W), row(GLA_W),
        pl.BlockSpec((None, tm, GLA_W), lambda b, i: (b, i, 3)),
        _const_spec((1, GLA_W)), _const_spec((GLA_W, GLA_W)), _layer_spec((D, D), layer),
        _layer_spec((D, 2 * D_FF), layer), _layer_spec((N_FF_CHUNKS, FF_CHUNK, D), layer),
    ]
    args = [xs, mods_l, g2, cu, cu, cu, cb, conv_w, attn, o_f, o_b, gl, norm_g,
            jnp.asarray(_head_blocks(), BF16), w_out, wu, wd]
    if final:
        in_specs.append(_const_spec((1, D)))
        args.append(final_g)
    return pl.pallas_call(
        functools.partial(_post_kernel, final=final),
        grid=(B, nt),
        in_specs=in_specs,
        out_specs=row(D),
        out_shape=jax.ShapeDtypeStruct((B, S, D), F32),
        scratch_shapes=[pltpu.VMEM((tm, D), F32)],
        compiler_params=_params(2),
        name="post_final" if final else "post",
    )(*args)


def _rope_tables(seq):
    t = jnp.arange(seq)
    pos = jnp.stack([(t // GRID_W).astype(F32), (t % GRID_W).astype(F32)], axis=1)
    n_freq = HEAD_DIM // 4
    inv_freq = ROPE_BASE ** (-jnp.arange(n_freq, dtype=F32) / n_freq)
    ang = pos[:, :, None] * inv_freq
    cos, sin = jnp.cos(ang), jnp.sin(ang)
    zero = jnp.zeros_like(sin)
    cos_t = jnp.stack([cos, cos], axis=2).reshape(seq, HEAD_DIM)
    sa_t = jnp.stack([-sin, zero], axis=2).reshape(seq, HEAD_DIM)
    sb_t = jnp.stack([zero, sin], axis=2).reshape(seq, HEAD_DIM)
    rep = LANES // HEAD_DIM
    return tuple(jnp.tile(a, (1, rep)) for a in (cos_t, sa_t, sb_t))


def _tile_rows(S, want):
    t = min(want, S)
    while S % t:
        t //= 2
    return t


def kernel(x, c, ctx, c_ctx, w_mod, b_mod, norm1_g, norm2_g, w_in, conv_w, attn_sink,
           gla_gate_w, gla_gate_b, gla_norm_g, w_out, w_up, w_down, final_norm_g):
    B, S, D = x.shape
    L = w_mod.shape[0]
    Lc = ctx.shape[1]
    assert D == D_MODEL and S % WINDOW == 0 and Lc % GLA_CHUNK == 0 and B + 1 <= 8

    cvec = jnp.zeros((8, D), F32).at[:B].set(c).at[B].set(c_ctx)
    mods = _modulation(cvec, w_mod, b_mod).reshape(L, 8, N_MOD, D)
    mods = jnp.pad(mods, ((0, 0), (0, 0), (0, 8 - N_MOD), (0, 0)))

    order = np.array(Q_HEAD_ORDER)
    wq = w_in[:, :, 768:1280].reshape(L, D, ATT_HEADS, HEAD_DIM)[:, :, order].reshape(L, D, ATT_W)
    w_in_p = jnp.concatenate(
        [w_in[:, :, :768], wq, w_in[:, :, 1280:1536], w_in[:, :, 2560:2592],
         jnp.zeros((L, D, LANES - 2 * GLA_RANK), F32), w_in[:, :, 1536:2560]], axis=2).astype(BF16)
    wo_att = w_out[:, 256:768].reshape(L, ATT_HEADS, HEAD_DIM, D)[:, order].reshape(L, ATT_W, D)
    w_out_p = jnp.concatenate([w_out[:, :256], wo_att, w_out[:, 768:]], axis=1).astype(BF16)
    wg = jnp.zeros((L, LANES, 2 * GLA_W), F32)
    wg = wg.at[:, :GLA_RANK, :GLA_W].set(gla_gate_w[:, 0])
    wg = wg.at[:, GLA_RANK:2 * GLA_RANK, GLA_W:].set(gla_gate_w[:, 1]).astype(BF16)
    bg = gla_gate_b.reshape(L, 1, 2 * GLA_W)
    cw = jnp.pad(conv_w, ((0, 0), (0, 8 - CONV_K), (0, 0)))
    ng = jnp.tile(gla_norm_g, (1, GLA_HEADS)).reshape(L, 1, GLA_W)
    wu = w_up.astype(BF16)
    wd = w_down.reshape(L, N_FF_CHUNKS, FF_CHUNK, D).astype(BF16)
    g1 = norm1_g.reshape(L, 1, D)
    g2 = norm2_g.reshape(L, 1, D)

    rope_tabs = _rope_tables(S)
    tm = _tile_rows(S, 512)
    proj_tm = _tile_rows(S, 1024)
    gla_tile = _tile_rows(S, 512)
    zero_state = jnp.zeros((B, GLA_W, GLA_W), F32)

    for l in range(L):
        last = l == L - 1
        cu, cb, q, k, v, gl, gates = _proj(x, mods[l], None, g1[l], w_in_p, l, wg[l], bg[l],
                                           rope_tabs, proj_tm)
        ccu, ccb, cq, ck, cv, cgl, cgates = _proj(ctx, mods[l], B, g1[l], w_in_p, l, wg[l], bg[l],
                                                  None, Lc)
        attn = _attention(q, k, v, ck, cv, attn_sink[l], local=True)
        oc_f, oc_b, sc_f, sc_b = _gla(cgl, cgates, zero_state, zero_state, Lc)
        o_f, o_b, _, _ = _gla(gl, gates, sc_f, sc_b, gla_tile)
        fin = final_norm_g.reshape(1, D) if last else None
        x = _post(x, mods[l], None, g2[l], cu, cb, cw[l], attn, o_f, o_b, gl, ng[l],
                  w_out_p, wu, wd, l, fin, tm)
        if not last:
            attn_c = _attention(cq, None, None, ck, cv, attn_sink[l], local=False)
            ctx = _post(ctx, mods[l], B, g2[l], ccu, ccb, cw[l], attn_c, oc_f, oc_b, cgl, ng[l],
                        w_out_p, wu, wd, l, None, Lc)
    return x
```

```python
import functools

import numpy as np
import jax
import jax.numpy as jnp
from jax import lax
from jax.experimental import pallas as pl
from jax.experimental.pallas import tpu as pltpu

F32 = jnp.float32
BF16 = jnp.bfloat16

D_MODEL = 1024
HEAD_DIM = 64
CONV_CH = 256
CONV_K = 3
ATT_HEADS = 8
ATT_KV_HEADS = 2
ATT_W = ATT_HEADS * HEAD_DIM
KV_W = ATT_KV_HEADS * HEAD_DIM
WINDOW = 128
GLA_HEADS = 4
GLA_W = GLA_HEADS * HEAD_DIM
GLA_RANK = 16
GLA_NORMALIZER = 16.0
D_FF = 2816
N_MOD = 6
GRID_W = 64
ROPE_BASE = 10000.0
EPS = 1e-6
LOG2_E = 1.4426950408889634

LANES = 128
C_CONV = 0
C_QK = 768
C_V = 1408
C_LR = 1536
C_GLA = 1664
N_IN_PAD = C_GLA + 4 * GLA_W
Q_HEAD_ORDER = (0, 4, 1, 5, 2, 6, 3, 7)

ATT_ROW_GROUP = 16
GLA_CHUNK = 64
GLA_SUB = 16
GLA_FAST_RANGE = 60.0
FF_CHUNK = 256
N_FF_CHUNKS = D_FF // FF_CHUNK
POST_ROW_GROUPS = 2
FF_UNROLL = 3
assert (N_FF_CHUNKS - 2) % FF_UNROLL == 0

VMEM_LIMIT = 56 * 1024 * 1024


def _const_spec(shape):
    nd = len(shape)
    return pl.BlockSpec(shape, lambda *_: (0,) * nd, pipeline_mode=pl.Buffered(1))


def _layer_spec(shape, layer):
    nd = len(shape)
    return pl.BlockSpec((None,) + tuple(shape), lambda *_: (layer,) + (0,) * nd,
                        pipeline_mode=pl.Buffered(1))


def _params(n_grid):
    return pltpu.CompilerParams(
        dimension_semantics=("arbitrary",) * n_grid, vmem_limit_bytes=VMEM_LIMIT)


def _silu(a):
    return a * (1.0 / (1.0 + jnp.exp(-a)))


def _dot(a, b):
    return jnp.dot(a, b, preferred_element_type=F32)


def _dot_nt(a, b):
    return lax.dot_general(a, b, (((1,), (1,)), ((), ())), preferred_element_type=F32)


def _dot_tn(a, b):
    return lax.dot_general(a, b, (((0,), (0,)), ((), ())), preferred_element_type=F32)


def _mod_kernel(c_ref, w_ref, b_ref, o_ref):
    a = _silu(c_ref[...]).astype(BF16)
    o_ref[...] = _dot(a, w_ref[...].astype(BF16)) + b_ref[...]


def _modulation(cvec, w_mod, b_mod):
    L, D, W = w_mod.shape
    tn = 2048
    return pl.pallas_call(
        _mod_kernel,
        grid=(L, W // tn),
        in_specs=[
            pl.BlockSpec((8, D), lambda l, j: (0, 0)),
            pl.BlockSpec((None, D, tn), lambda l, j: (l, 0, j)),
            pl.BlockSpec((None, 1, tn), lambda l, j: (l, 0, j)),
        ],
        out_specs=pl.BlockSpec((None, 8, tn), lambda l, j: (l, 0, j)),
        out_shape=jax.ShapeDtypeStruct((L, 8, W), F32),
        compiler_params=_params(2),
        name="modulation",
    )(cvec, w_mod, b_mod.reshape(L, 1, W))


def _proj_kernel(x_ref, mod_ref, g1_ref, w_ref, wg_ref, bg_ref, *rest, rope):
    if rope:
        cos_ref, sa_ref, sb_ref = rest[:3]
        rest = rest[3:]
    cu_ref, cb_ref, q_ref, k_ref, v_ref, gl_ref, gate_ref = rest

    x = x_ref[...]
    ms = jnp.mean(x * x, axis=-1, keepdims=True)
    y = x * lax.rsqrt(ms + EPS) * g1_ref[...]
    h = (y * (1.0 + mod_ref[1:2, :]) + mod_ref[0:1, :]).astype(BF16)

    def proj(lo, hi):
        return _dot(h, w_ref[:, lo:hi])

    pc = proj(C_CONV, C_QK)
    cu_ref[...] = pc[:, 2 * CONV_CH:3 * CONV_CH] * pc[:, 0:CONV_CH]
    cb_ref[...] = pc[:, CONV_CH:2 * CONV_CH]

    pqk = proj(C_QK, C_GLA)
    scale = HEAD_DIM ** -0.5
    for m in range((ATT_W + KV_W) // LANES):
        col = pqk[:, m * LANES:(m + 1) * LANES]
        if rope:
            col = (col * cos_ref[...]
                   + pltpu.roll(col, LANES - HEAD_DIM // 4, axis=1) * sa_ref[...]
                   + pltpu.roll(col, HEAD_DIM // 4, axis=1) * sb_ref[...])
        if m < ATT_W // LANES:
            q_ref[:, m * LANES:(m + 1) * LANES] = (col * scale).astype(BF16)
        else:
            k_ref[...] = col.astype(BF16)

    v_ref[...] = pqk[:, C_V - C_QK:C_LR - C_QK].astype(BF16)

    pg = proj(C_GLA, N_IN_PAD)
    gl_ref[:, 0:GLA_W] = pg[:, 0:GLA_W] * (HEAD_DIM ** -0.5)
    gl_ref[:, GLA_W:4 * GLA_W] = pg[:, GLA_W:4 * GLA_W]

    lr = pqk[:, C_LR - C_QK:C_GLA - C_QK].astype(BF16)
    z = _dot(lr, wg_ref[...]) + bg_ref[...]
    log_sig = jnp.minimum(z, 0.0) - jnp.log1p(jnp.exp(-jnp.abs(z)))
    gate_ref[...] = log_sig * (1.0 / GLA_NORMALIZER)


def _proj(xs, mods_l, mod_row, g1, w_in, layer, wg, bg, rope_tabs, tm):
    B, S, D = xs.shape
    rope = rope_tabs is not None
    nt = S // tm
    row = lambda w: pl.BlockSpec((None, tm, w), lambda b, i: (b, i, 0))
    if mod_row is None:
        mod_spec = pl.BlockSpec((None, 8, D), lambda b, i: (b, 0, 0))
    else:
        mod_spec = pl.BlockSpec((None, 8, D), lambda b, i: (mod_row, 0, 0))
    in_specs = [row(D), mod_spec, _const_spec((1, D)), _layer_spec((D, N_IN_PAD), layer),
                _const_spec((LANES, 2 * GLA_W)), _const_spec((1, 2 * GLA_W))]
    args = [xs, mods_l, g1, w_in, wg, bg]
    if rope:
        in_specs += [pl.BlockSpec((tm, LANES), lambda b, i: (i, 0))] * 3
        args += list(rope_tabs)
    widths = (CONV_CH, CONV_CH, ATT_W, KV_W, KV_W, 4 * GLA_W, 2 * GLA_W)
    dtypes = (F32, F32, BF16, BF16, BF16, F32, F32)
    return pl.pallas_call(
        functools.partial(_proj_kernel, rope=rope),
        grid=(B, nt),
        in_specs=in_specs,
        out_specs=[row(w) for w in widths],
        out_shape=[jax.ShapeDtypeStruct((B, S, w), dt) for w, dt in zip(widths, dtypes)],
        compiler_params=_params(2),
        name="proj_rope" if rope else "proj_ctx",
    )(*args)


def _attn_kernel(sink_ref, q_ref, kc_ref, vc_ref, *rest, local, n_sub):
    tb = q_ref.shape[0] // n_sub
    n_pairs = ATT_W // LANES
    n_ctx = kc_ref.shape[0]
    rg = ATT_ROW_GROUP
    if local:
        kp_ref, km_ref, kn_ref, vp_ref, vm_ref, vn_ref, o_ref, s_scr, p_scr, e_scr, bias_scr = rest
        i = pl.program_id(1)
        n = pl.num_programs(1)
        k_loc = jnp.concatenate([kp_ref[...], km_ref[...], kn_ref[...]], axis=0)
        v_loc = jnp.concatenate([vp_ref[...], vm_ref[...], vn_ref[...]], axis=0)
        r = lax.broadcasted_iota(jnp.int32, (tb, tb), 0)
        j = lax.broadcasted_iota(jnp.int32, (tb, tb), 1)
        bias_prev = jnp.where(j >= r, 0.0, -jnp.inf)
        bias_next = jnp.where(j <= r, 0.0, -jnp.inf)
        bias_scr[0] = bias_prev + jnp.where(i > 0, 0.0, -jnp.inf)
        bias_scr[1] = bias_prev
        bias_scr[2] = bias_next
        bias_scr[3] = bias_next + jnp.where(i < n - 1, 0.0, -jnp.inf)
        nk = n_ctx + 3 * tb
    else:
        o_ref, s_scr, p_scr, e_scr = rest
        nk = n_ctx
    lane = lax.broadcasted_iota(jnp.int32, (1, LANES), 1)
    lo = lane < HEAD_DIM
    hi = jnp.logical_not(lo)
    lo_k = lax.broadcasted_iota(jnp.int32, (nk, LANES), 1) < HEAD_DIM
    group_ones = [jnp.where(lo_k, 1.0, 0.0).astype(BF16), jnp.where(lo_k, 0.0, 1.0).astype(BF16)]

    def scores(jb):
        q = q_ref[jb * tb:(jb + 1) * tb, :]
        lhs = jnp.concatenate([q[:, m * LANES:(m + 1) * LANES] for m in range(n_pairs)], axis=0)
        kk = (jnp.concatenate([kc_ref[...], k_loc[jb * tb:(jb + 3) * tb]], axis=0) if local
              else kc_ref[...])
        k2 = jnp.concatenate([kk * group_ones[0], kk * group_ones[1]], axis=0)
        s_scr[jb] = _dot_nt(lhs, k2)

    def softmax(jb):
        for m in range(n_pairs):
            sinks = [sink_ref[m + (ATT_HEADS // ATT_KV_HEADS) * g] for g in range(ATT_KV_HEADS)]
            for gi in range(tb // rg):
                rq = gi * rg
                r0 = m * tb + rq
                s = s_scr[jb, r0:r0 + rg, :]
                probs, sink_terms = [], []
                for g in range(ATT_KV_HEADS):
                    sg = s[:, g * nk:(g + 1) * nk]
                    if local:
                        bp = bias_scr[0 if jb == 0 else 1, rq:rq + rg, :]
                        bn = bias_scr[3 if jb == n_sub - 1 else 2, rq:rq + rg, :]
                        sg = jnp.concatenate(
                            [sg[:, :n_ctx], sg[:, n_ctx:n_ctx + tb] + bp,
                             sg[:, n_ctx + tb:n_ctx + 2 * tb], sg[:, n_ctx + 2 * tb:] + bn], axis=1)
                    mx = jnp.maximum(jnp.max(sg, axis=-1, keepdims=True), sinks[g])
                    probs.append(jnp.exp(sg - mx).astype(BF16))
                    sink_terms.append(jnp.exp(sinks[g] - mx))
                p_scr[jb, r0:r0 + rg, :] = jnp.concatenate(probs, axis=1)
                e_scr[jb, r0:r0 + rg, :] = jnp.where(lo, sink_terms[0], sink_terms[1])

    def values(jb):
        vv = (jnp.concatenate([vc_ref[...], v_loc[jb * tb:(jb + 3) * tb]], axis=0) if local
              else vc_ref[...])
        v2 = jnp.concatenate(
            [jnp.concatenate([vv * group_ones[0], group_ones[0]], axis=1),
             jnp.concatenate([vv * group_ones[1], group_ones[1]], axis=1)], axis=0)
        o_aug = _dot(p_scr[jb], v2)
        o = o_aug[:, :LANES] * (1.0 / (o_aug[:, LANES:] + e_scr[jb]))
        for m in range(n_pairs):
            o_ref[jb * tb:(jb + 1) * tb, m * LANES:(m + 1) * LANES] = (
                o[m * tb:(m + 1) * tb].astype(BF16))

    scores(0)
    for jb in range(n_sub):
        if jb + 1 < n_sub:
            scores(jb + 1)
        softmax(jb)
        values(jb)


def _attention(q, k, v, k_ctx, v_ctx, sink, local, n_sub=4):
    B, S, _ = q.shape
    Lc = k_ctx.shape[1]
    if not local:
        n_sub = 1
    tq = WINDOW * n_sub if local else S
    nq = S // tq
    in_specs = [
        pl.BlockSpec(memory_space=pltpu.SMEM),
        pl.BlockSpec((None, tq, ATT_W), lambda b, i: (b, i, 0)),
        pl.BlockSpec((None, Lc, KV_W), lambda b, i: (b, 0, 0)),
        pl.BlockSpec((None, Lc, KV_W), lambda b, i: (b, 0, 0)),
    ]
    args = [sink, q, k_ctx, v_ctx]
    if local:
        nw = S // WINDOW
        nb = [pl.BlockSpec((None, WINDOW, KV_W), lambda b, i: (b, jnp.maximum(i * n_sub - 1, 0), 0)),
              pl.BlockSpec((None, tq, KV_W), lambda b, i: (b, i, 0)),
              pl.BlockSpec((None, WINDOW, KV_W),
                           lambda b, i: (b, jnp.minimum((i + 1) * n_sub, nw - 1), 0))]
        in_specs += nb + nb
        args += [k, k, k, v, v, v]
    tb = tq // n_sub
    nk = Lc + (3 * tb if local else 0)
    n_rows = (ATT_W // LANES) * tb
    scratch = [pltpu.VMEM((n_sub, n_rows, ATT_KV_HEADS * nk), F32),
               pltpu.VMEM((n_sub, n_rows, ATT_KV_HEADS * nk), BF16),
               pltpu.VMEM((n_sub, n_rows, LANES), F32)]
    if local:
        scratch.append(pltpu.VMEM((4, tb, tb), F32))
    return pl.pallas_call(
        functools.partial(_attn_kernel, local=local, n_sub=n_sub),
        grid=(B, nq),
        in_specs=in_specs,
        out_specs=pl.BlockSpec((None, tq, ATT_W), lambda b, i: (b, i, 0)),
        out_shape=jax.ShapeDtypeStruct((B, S, ATT_W), BF16),
        scratch_shapes=scratch,
        compiler_params=_params(2),
        name="attn_window" if local else "attn_ctx",
    )(*args)


def _gla_consts(rev):
    t = np.arange(GLA_CHUNK)[:, None]
    s = np.arange(GLA_CHUNK)[None, :]
    same = (t // GLA_SUB) == (s // GLA_SUB)
    before = (s >= t) if rev else (s <= t)
    return np.concatenate([same & before, before], axis=0).astype(np.float32)


def _head_blocks():
    h = np.arange(GLA_W) // HEAD_DIM
    return (h[:, None] == h[None, :]).astype(np.float32)


def _gla_last_row(blk, rev):
    return blk * GLA_SUB if rev else blk * GLA_SUB + GLA_SUB - 1


def _gla_prepare(q, k, g, cum, rev):
    n_sub = GLA_CHUNK // GLA_SUB
    g2 = g * LOG2_E
    g_hi = g2.astype(BF16)
    rem = g2 - g_hi.astype(F32)
    g_mid = rem.astype(BF16)
    g_lo = (rem - g_mid.astype(F32)).astype(BF16)
    cs = _dot(cum, g_hi) + _dot(cum, g_mid) + _dot(cum, g_lo)
    beta, bch = cs[0:GLA_CHUNK], cs[GLA_CHUNK:2 * GLA_CHUNK]
    gam = jnp.concatenate(
        [beta[_gla_last_row(b, rev):_gla_last_row(b, rev) + 1, :] - beta[b * GLA_SUB:(b + 1) * GLA_SUB, :]
         for b in range(n_sub)], axis=0)
    end_row = _gla_last_row(0 if rev else n_sub - 1, rev)
    dch = bch[end_row:end_row + 1, :] - bch
    f32_rows = jnp.concatenate([beta, bch, q * jnp.exp2(beta), k * jnp.exp2(gam)], axis=0)
    b16_rows = jnp.concatenate([(q * jnp.exp2(bch)).astype(BF16), (k * jnp.exp2(dch)).astype(BF16)],
                               axis=0)
    return f32_rows, b16_rows


def _gla_causal(rev):
    t = (np.arange(GLA_HEADS * GLA_CHUNK) % GLA_CHUNK)[:, None]
    s = np.arange(GLA_CHUNK)[None, :]
    return ((s >= t) if rev else (s <= t)).astype(np.float32)


def _gla_prepare_fast(q, k, v, g, cum_chunk, head_mask, rev):
    g2 = g * LOG2_E
    g_hi = g2.astype(BF16)
    g_lo = (g2 - g_hi.astype(F32)).astype(BF16)
    bch = _dot(cum_chunk, g_hi) + _dot(cum_chunk, g_lo)
    end_row = 0 if rev else GLA_CHUNK - 1
    mid_row = GLA_CHUNK // 2 if rev else GLA_CHUNK // 2 - 1
    b_end = bch[end_row:end_row + 1, :]
    b_mid = bch[mid_row:mid_row + 1, :]
    ok = (jnp.max(-b_end) <= GLA_FAST_RANGE).astype(jnp.int32)
    q_mid = (q * jnp.exp2(bch - b_mid)).astype(BF16)
    k_mid = (k * jnp.exp2(b_mid - bch)).astype(BF16)
    zero = jnp.zeros_like(q_mid)
    rows = jnp.concatenate(
        [jnp.where(hm, q_mid, zero) for hm in head_mask]
        + [k_mid, (q * jnp.exp2(bch)).astype(BF16), (k * jnp.exp2(b_end - bch)).astype(BF16),
           v.astype(BF16)], axis=0)
    return rows, jnp.exp2(b_end), ok


def _gla_tile_fast(pb_scr, dec_scr, dirs, bd, head_mask, n_chunks):
    n_q = GLA_HEADS * GLA_CHUNK
    work = []
    for d, (_, _, _, st_ref, o_ref, caus_ref, rev) in enumerate(dirs):
        for c in (range(n_chunks - 1, -1, -1) if rev else range(n_chunks)):
            pb = pb_scr.at[d, c]
            vb = pb[n_q + 3 * GLA_CHUNK:n_q + 4 * GLA_CHUNK]
            upd = _dot_tn(vb, pb[n_q + 2 * GLA_CHUNK:n_q + 3 * GLA_CHUNK])
            a = _dot_nt(pb[0:n_q], pb[n_q:n_q + GLA_CHUNK])
            work.append(dict(d=d, c=c, pb=pb, vb=vb, upd=upd, a=a, o_ref=o_ref, caus=caus_ref))
    for d, (_, _, _, st_ref, _, _, _) in enumerate(dirs):
        st = st_ref[...]
        for w in (w for w in work if w["d"] == d):
            w["o_inter"] = _dot_nt(w["pb"][n_q + GLA_CHUNK:n_q + 2 * GLA_CHUNK], st.astype(BF16))
            st = st * dec_scr[d, w["c"], 0:1, :] + w["upd"] * bd
        st_ref[...] = st
    for w in work:
        w["r"] = _dot((w["a"] * w["caus"][...]).astype(BF16), w["vb"])
    for w in work:
        o = w["o_inter"]
        for hh, hm in enumerate(head_mask):
            o = o + jnp.where(hm, w["r"][hh * GLA_CHUNK:(hh + 1) * GLA_CHUNK, :], 0.0)
        w["o_ref"][w["c"] * GLA_CHUNK:(w["c"] + 1) * GLA_CHUNK, :] = o


def _gla_chunk(q, k, v, pf_ref, pb_ref, st_ref, o_ref, r0, ones, bd, head_mask, rev):
    n_sub = GLA_CHUNK // GLA_SUB
    order = list(range(n_sub))[::-1] if rev else list(range(n_sub))

    def rows(a, blk):
        return a[blk * GLA_SUB:(blk + 1) * GLA_SUB, :]

    half = GLA_SUB // 2
    beta = pf_ref[0:GLA_CHUNK]
    q_loc = pf_ref[2 * GLA_CHUNK:3 * GLA_CHUNK]
    k_loc = pf_ref[3 * GLA_CHUNK:4 * GLA_CHUNK]
    g_tot = [beta[_gla_last_row(b, rev):_gla_last_row(b, rev) + 1, :] for b in range(n_sub)]
    end_row = GLA_CHUNK + _gla_last_row(order[-1], rev)
    b_end = pf_ref[end_row:end_row + 1, :]
    half_row = lax.broadcasted_iota(jnp.int32, (half, 1), 0)

    st = st_ref[...]
    o_inter = _dot_nt(pb_ref[0:GLA_CHUNK], st.astype(BF16))
    upd = _dot_tn(v.astype(BF16), pb_ref[GLA_CHUNK:2 * GLA_CHUNK])
    scores, values = {}, {}
    for p in range(1, n_sub):
        ks, vs = [], []
        for pp in range(p):
            sb = order[pp]
            kk = rows(k_loc, sb)
            mids = [order[x] for x in range(pp + 1, p)]
            if mids:
                tot = g_tot[mids[0]]
                for mb in mids[1:]:
                    tot = tot + g_tot[mb]
                kk = kk * jnp.exp2(tot)
            ks.append(kk)
            vs.append(rows(v, sb))
        qt = rows(q_loc, order[p])
        q_heads = jnp.concatenate([jnp.where(hm, qt, 0.0) for hm in head_mask], axis=0)
        scores[p] = _dot_nt(q_heads.astype(BF16), jnp.concatenate(ks, axis=0).astype(BF16))
        values[p] = jnp.concatenate(vs, axis=0).astype(BF16)
    yield

    pairs = []
    for s in range(GLA_SUB):
        hs = s // half
        for hb in (range(0, hs + 1) if rev else range(hs, GLA_SUB // half)):
            pairs.append((s, hb))
    sums = {}
    for p, tb in enumerate(order):
        bt, qb, kb = rows(beta, tb), rows(q, tb), rows(k, tb)
        slabs = []
        for s, hb in pairs:
            d = bt[hb * half:(hb + 1) * half, :] - bt[s:s + 1, :]
            if hb == s // half:
                keep = (half_row <= s % half) if rev else (half_row >= s % half)
                d = jnp.where(keep, d, -jnp.inf)
            slabs.append(qb[hb * half:(hb + 1) * half, :] * jnp.exp2(d) * kb[s:s + 1, :])
        sums[p] = _dot(jnp.concatenate(slabs, axis=0).astype(BF16), ones)
        yield

    mixed = {p: _dot(scores[p].astype(BF16), values[p]) for p in range(1, n_sub)}
    yield

    for p, tb in enumerate(order):
        vb = rows(v, tb)
        acc = rows(o_inter, tb)
        if p > 0:
            for hh, hm in enumerate(head_mask):
                acc = acc + jnp.where(hm, mixed[p][hh * GLA_SUB:(hh + 1) * GLA_SUB, :], 0.0)
        halves = [acc[hb * half:(hb + 1) * half, :] for hb in range(GLA_SUB // half)]
        for idx, (s, hb) in enumerate(pairs):
            halves[hb] = halves[hb] + sums[p][idx * half:(idx + 1) * half, :] * vb[s:s + 1, :]
        o_ref[pl.ds(r0 + tb * GLA_SUB, GLA_SUB), :] = jnp.concatenate(halves, axis=0)
        yield

    st_ref[...] = st * jnp.exp2(b_end) + upd * bd


def _interleave(gens):
    active = list(gens)
    while active:
        for gen in list(active):
            try:
                next(gen)
            except StopIteration:
                active.remove(gen)


def _gla_kernel(cumf_ref, cumb_ref, ones_ref, bd_ref, causf_ref, causb_ref, glf_ref, glb_ref,
                gf_ref, gb_ref, s0f_ref, s0b_ref, of_ref, ob_ref, sff_ref, sfb_ref,
                stf_ref, stb_ref, pb_scr, dec_scr, ok_ref, *, n_chunks):
    i = pl.program_id(1)
    lane_head = lax.broadcasted_iota(jnp.int32, (1, GLA_W), 1) // HEAD_DIM
    head_mask = [lane_head == hh for hh in range(GLA_HEADS)]
    dirs = ((glf_ref, gf_ref, cumf_ref, stf_ref, of_ref, causf_ref, False),
            (glb_ref, gb_ref, cumb_ref, stb_ref, ob_ref, causb_ref, True))

    @pl.when(i == 0)
    def _():
        stf_ref[...] = s0f_ref[...]
        stb_ref[...] = s0b_ref[...]

    def load(gl_ref, g_ref, r0):
        rows = pl.ds(r0, GLA_CHUNK)
        return (gl_ref[rows, 0:GLA_W], gl_ref[rows, GLA_W:2 * GLA_W],
                gl_ref[rows, 2 * GLA_W:3 * GLA_W], g_ref[rows, :])

    for c in range(n_chunks):
        for d, (gl_ref, g_ref, cum_ref, _, _, _, rev) in enumerate(dirs):
            q, k, v, g = load(gl_ref, g_ref, c * GLA_CHUNK)
            rows, dec, ok = _gla_prepare_fast(q, k, v, g, cum_ref[GLA_CHUNK:2 * GLA_CHUNK, :],
                                              head_mask, rev)
            pb_scr[d, c] = rows
            dec_scr[d, c] = jnp.broadcast_to(dec, (8, GLA_W))
            ok_ref[d, c] = ok

    n_ok = ok_ref[0, 0]
    for d in range(2):
        for c in range(n_chunks):
            if (d, c) != (0, 0):
                n_ok = n_ok + ok_ref[d, c]
    fast = n_ok == 2 * n_chunks

    @pl.when(fast)
    def _():
        _gla_tile_fast(pb_scr, dec_scr, dirs, bd_ref[...], head_mask, n_chunks)

    @pl.when(jnp.logical_not(fast))
    def _():
        def chunk(ci, carry):
            cidx = (ci, n_chunks - 1 - ci)
            gens = []
            for d, (gl_ref, g_ref, cum_ref, st_ref, o_ref, _, rev) in enumerate(dirs):
                r0 = pl.multiple_of(cidx[d] * GLA_CHUNK, GLA_CHUNK)
                q, k, v, g = load(gl_ref, g_ref, r0)
                pf, pb = _gla_prepare(q, k, g, cum_ref[...], rev)
                gens.append(_gla_chunk(q, k, v, pf, pb, st_ref, o_ref, r0, ones_ref[...],
                                       bd_ref[...], head_mask, rev))
            _interleave(gens)
            return carry

        lax.fori_loop(0, n_chunks, chunk, 0)

    @pl.when(i == pl.num_programs(1) - 1)
    def _():
        sff_ref[...] = stf_ref[...]
        sfb_ref[...] = stb_ref[...]


def _gla(gl, gates, s0f, s0b, tile):
    B, S, _ = gl.shape
    nt = S // tile
    n_chunks = tile // GLA_CHUNK
    ones_bd = jnp.asarray(_head_blocks(), BF16)
    bd = jnp.asarray(_head_blocks(), F32)
    state_spec = pl.BlockSpec((None, GLA_W, GLA_W), lambda b, i: (b, 0, 0))
    return pl.pallas_call(
        functools.partial(_gla_kernel, n_chunks=n_chunks),
        grid=(B, nt),
        in_specs=[
            _const_spec((2 * GLA_CHUNK, GLA_CHUNK)),
            _const_spec((2 * GLA_CHUNK, GLA_CHUNK)),
            _const_spec((GLA_W, GLA_W)),
            _const_spec((GLA_W, GLA_W)),
            _const_spec((GLA_HEADS * GLA_CHUNK, GLA_CHUNK)),
            _const_spec((GLA_HEADS * GLA_CHUNK, GLA_CHUNK)),
            pl.BlockSpec((None, tile, 3 * GLA_W), lambda b, i: (b, i, 0)),
            pl.BlockSpec((None, tile, 3 * GLA_W), lambda b, i: (b, nt - 1 - i, 0)),
            pl.BlockSpec((None, tile, GLA_W), lambda b, i: (b, i, 0)),
            pl.BlockSpec((None, tile, GLA_W), lambda b, i: (b, nt - 1 - i, 1)),
            state_spec, state_spec,
        ],
        out_specs=[
            pl.BlockSpec((None, tile, GLA_W), lambda b, i: (b, i, 0)),
            pl.BlockSpec((None, tile, GLA_W), lambda b, i: (b, nt - 1 - i, 0)),
            state_spec, state_spec,
        ],
        out_shape=[jax.ShapeDtypeStruct((B, S, GLA_W), F32),
                   jax.ShapeDtypeStruct((B, S, GLA_W), F32),
                   jax.ShapeDtypeStruct((B, GLA_W, GLA_W), F32),
                   jax.ShapeDtypeStruct((B, GLA_W, GLA_W), F32)],
        scratch_shapes=[pltpu.VMEM((GLA_W, GLA_W), F32), pltpu.VMEM((GLA_W, GLA_W), F32),
                        pltpu.VMEM((2, n_chunks, (GLA_HEADS + 4) * GLA_CHUNK, GLA_W), BF16),
                        pltpu.VMEM((2, n_chunks, 8, GLA_W), F32),
                        pltpu.SMEM((2, n_chunks), jnp.int32)],
        compiler_params=_params(2),
        name="gla",
    )(jnp.asarray(_gla_consts(False), BF16), jnp.asarray(_gla_consts(True), BF16),
      ones_bd, bd, jnp.asarray(_gla_causal(False), F32), jnp.asarray(_gla_causal(True), F32),
      gl, gl, gates, gates, s0f, s0b)


def _post_kernel(x_ref, mod_ref, g2_ref, cu_ref, cup_ref, cun_ref, cb_ref, cw_ref, at_ref,
                 of_ref, ob_ref, gg_ref, ng_ref, ones_ref, wo_ref, wu_ref, wd_ref,
                 *rest, final):
    if final:
        fg_ref, o_ref, acc_ref = rest
    else:
        o_ref, acc_ref = rest
    i = pl.program_id(1)
    n = pl.num_programs(1)
    tm = x_ref.shape[0]

    u = cu_ref[...]
    row = lax.broadcasted_iota(jnp.int32, (tm, 1), 0)
    prev_row = jnp.where(i > 0, cup_ref[7:8, :], 0.0)
    next_row = jnp.where(i < n - 1, cun_ref[0:1, :], 0.0)
    u_prev = jnp.where(row == 0, prev_row, pltpu.roll(u, 1, axis=0))
    u_next = jnp.where(row == tm - 1, next_row, pltpu.roll(u, tm - 1, axis=0))
    conv = cb_ref[...] * (cw_ref[0:1, :] * u_prev + cw_ref[1:2, :] * u + cw_ref[2:3, :] * u_next)

    x1_parts, h2_parts = [], []
    n_groups = POST_ROW_GROUPS if tm % (POST_ROW_GROUPS * 16) == 0 else 1
    rows_per = tm // n_groups
    for gr in range(n_groups):
        rs = slice(gr * rows_per, (gr + 1) * rows_per)
        o = of_ref[rs, :] + ob_ref[rs, :]
        sq = o * o
        sq_hi = sq.astype(BF16)
        sq_lo = (sq - sq_hi.astype(F32)).astype(BF16)
        ms = (_dot(sq_hi, ones_ref[...]) + _dot(sq_lo, ones_ref[...])) * (1.0 / HEAD_DIM)
        gla = o * lax.rsqrt(ms + EPS) * ng_ref[...] * _silu(gg_ref[rs, :])

        mix = (_dot(conv[rs, :].astype(BF16), wo_ref[0:CONV_CH, :])
               + _dot(at_ref[rs, :], wo_ref[CONV_CH:CONV_CH + ATT_W, :])
               + _dot(gla.astype(BF16), wo_ref[CONV_CH + ATT_W:, :]))
        x1_g = x_ref[rs, :] + mod_ref[2:3, :] * mix

        ms2 = jnp.mean(x1_g * x1_g, axis=-1, keepdims=True)
        y2 = x1_g * lax.rsqrt(ms2 + EPS) * g2_ref[...]
        x1_parts.append(x1_g)
        h2_parts.append((y2 * (1.0 + mod_ref[4:5, :]) + mod_ref[3:4, :]).astype(BF16))
    x1 = jnp.concatenate(x1_parts, axis=0)
    h2 = jnp.concatenate(h2_parts, axis=0)

    def up(jc):
        if isinstance(jc, int):
            ca, cb = jc * FF_CHUNK, D_FF + jc * FF_CHUNK
        else:
            ca = pl.multiple_of(jc * FF_CHUNK, FF_CHUNK)
            cb = pl.multiple_of(D_FF + jc * FF_CHUNK, FF_CHUNK)
        a = _dot(h2, wu_ref[:, pl.ds(ca, FF_CHUNK)])
        b = _dot(h2, wu_ref[:, pl.ds(cb, FF_CHUNK)])
        return a, b

    def act(ab):
        return (_silu(ab[0]) * ab[1]).astype(BF16)

    prev = act(up(0))
    ab = up(1)
    acc_ref[...] = _dot(prev, wd_ref[0])
    prev = act(ab)

    def ff(t, prev):
        for u in range(FF_UNROLL):
            jc = 2 + t * FF_UNROLL + u
            ab = up(jc)
            acc_ref[...] += _dot(prev, wd_ref[jc - 1])
            prev = act(ab)
        return prev

    prev = lax.fori_loop(0, (N_FF_CHUNKS - 2) // FF_UNROLL, ff, prev)
    x2 = x1 + mod_ref[5:6, :] * (acc_ref[...] + _dot(prev, wd_ref[N_FF_CHUNKS - 1]))
    if final:
        msf = jnp.mean(x2 * x2, axis=-1, keepdims=True)
        x2 = x2 * lax.rsqrt(msf + EPS) * fg_ref[...]
    o_ref[...] = x2


def _post(xs, mods_l, mod_row, g2, cu, cb, conv_w, attn, o_f, o_b, gl, norm_g, w_out, wu, wd,
          layer, final_g, tm):
    B, S, D = xs.shape
    nt = S // tm
    hb = tm // 8
    row = lambda w: pl.BlockSpec((None, tm, w), lambda b, i: (b, i, 0))
    if mod_row is None:
        mod_spec = pl.BlockSpec((None, 8, D), lambda b, i: (b, 0, 0))
    else:
        mod_spec = pl.BlockSpec((None, 8, D), lambda b, i: (mod_row, 0, 0))
    final = final_g is not None
    in_specs = [
        row(D), mod_spec, _const_spec((1, D)),
        row(CONV_CH),
        pl.BlockSpec((None, 8, CONV_CH), lambda b, i: (b, jnp.maximum(i * hb - 1, 0), 0)),
        pl.BlockSpec((None, 8, CONV_CH), lambda b, i: (b, jnp.minimum((i + 1) * hb, S // 8 - 1), 0)),
        row(CONV_CH), _const_spec((8, CONV_CH)), row(ATT_W), row(GLA_W), row(GLA_W),
        pl.BlockSpec((None, tm, GLA_W), lambda b, i: (b, i, 3)),
        _const_spec((1, GLA_W)), _const_spec((GLA_W, GLA_W)), _layer_spec((D, D), layer),
        _layer_spec((D, 2 * D_FF), layer), _layer_spec((N_FF_CHUNKS, FF_CHUNK, D), layer),
    ]
    args = [xs, mods_l, g2, cu, cu, cu, cb, conv_w, attn, o_f, o_b, gl, norm_g,
            jnp.asarray(_head_blocks(), BF16), w_out, wu, wd]
    if final:
        in_specs.append(_const_spec((1, D)))
        args.append(final_g)
    return pl.pallas_call(
        functools.partial(_post_kernel, final=final),
        grid=(B, nt),
        in_specs=in_specs,
        out_specs=row(D),
        out_shape=jax.ShapeDtypeStruct((B, S, D), F32),
        scratch_shapes=[pltpu.VMEM((tm, D), F32)],
        compiler_params=_params(2),
        name="post_final" if final else "post",
    )(*args)


def _rope_tables(seq):
    t = jnp.arange(seq)
    pos = jnp.stack([(t // GRID_W).astype(F32), (t % GRID_W).astype(F32)], axis=1)
    n_freq = HEAD_DIM // 4
    inv_freq = ROPE_BASE ** (-jnp.arange(n_freq, dtype=F32) / n_freq)
    ang = pos[:, :, None] * inv_freq
    cos, sin = jnp.cos(ang), jnp.sin(ang)
    zero = jnp.zeros_like(sin)
    cos_t = jnp.stack([cos, cos], axis=2).reshape(seq, HEAD_DIM)
    sa_t = jnp.stack([-sin, zero], axis=2).reshape(seq, HEAD_DIM)
    sb_t = jnp.stack([zero, sin], axis=2).reshape(seq, HEAD_DIM)
    rep = LANES // HEAD_DIM
    return tuple(jnp.tile(a, (1, rep)) for a in (cos_t, sa_t, sb_t))


def _tile_rows(S, want):
    t = min(want, S)
    while S % t:
        t //= 2
    return t


def kernel(x, c, ctx, c_ctx, w_mod, b_mod, norm1_g, norm2_g, w_in, conv_w, attn_sink,
           gla_gate_w, gla_gate_b, gla_norm_g, w_out, w_up, w_down, final_norm_g):
    B, S, D = x.shape
    L = w_mod.shape[0]
    Lc = ctx.shape[1]
    assert D == D_MODEL and S % WINDOW == 0 and Lc % GLA_CHUNK == 0 and B + 1 <= 8

    cvec = jnp.zeros((8, D), F32).at[:B].set(c).at[B].set(c_ctx)
    mods = _modulation(cvec, w_mod, b_mod).reshape(L, 8, N_MOD, D)
    mods = jnp.pad(mods, ((0, 0), (0, 0), (0, 8 - N_MOD), (0, 0)))

    order = np.array(Q_HEAD_ORDER)
    wq = w_in[:, :, 768:1280].reshape(L, D, ATT_HEADS, HEAD_DIM)[:, :, order].reshape(L, D, ATT_W)
    w_in_p = jnp.concatenate(
        [w_in[:, :, :768], wq, w_in[:, :, 1280:1536], w_in[:, :, 2560:2592],
         jnp.zeros((L, D, LANES - 2 * GLA_RANK), F32), w_in[:, :, 1536:2560]], axis=2).astype(BF16)
    wo_att = w_out[:, 256:768].reshape(L, ATT_HEADS, HEAD_DIM, D)[:, order].reshape(L, ATT_W, D)
    w_out_p = jnp.concatenate([w_out[:, :256], wo_att, w_out[:, 768:]], axis=1).astype(BF16)
    wg = jnp.zeros((L, LANES, 2 * GLA_W), F32)
    wg = wg.at[:, :GLA_RANK, :GLA_W].set(gla_gate_w[:, 0])
    wg = wg.at[:, GLA_RANK:2 * GLA_RANK, GLA_W:].set(gla_gate_w[:, 1]).astype(BF16)
    bg = gla_gate_b.reshape(L, 1, 2 * GLA_W)
    cw = jnp.pad(conv_w, ((0, 0), (0, 8 - CONV_K), (0, 0)))
    ng = jnp.tile(gla_norm_g, (1, GLA_HEADS)).reshape(L, 1, GLA_W)
    wu = w_up.astype(BF16)
    wd = w_down.reshape(L, N_FF_CHUNKS, FF_CHUNK, D).astype(BF16)
    g1 = norm1_g.reshape(L, 1, D)
    g2 = norm2_g.reshape(L, 1, D)

    rope_tabs = _rope_tables(S)
    tm = _tile_rows(S, 512)
    proj_tm = _tile_rows(S, 1024)
    gla_tile = _tile_rows(S, 1024)
    zero_state = jnp.zeros((B, GLA_W, GLA_W), F32)

    for l in range(L):
        last = l == L - 1
        cu, cb, q, k, v, gl, gates = _proj(x, mods[l], None, g1[l], w_in_p, l, wg[l], bg[l],
                                           rope_tabs, proj_tm)
        ccu, ccb, cq, ck, cv, cgl, cgates = _proj(ctx, mods[l], B, g1[l], w_in_p, l, wg[l], bg[l],
                                                  None, Lc)
        attn = _attention(q, k, v, ck, cv, attn_sink[l], local=True)
        oc_f, oc_b, sc_f, sc_b = _gla(cgl, cgates, zero_state, zero_state, Lc)
        o_f, o_b, _, _ = _gla(gl, gates, sc_f, sc_b, gla_tile)
        fin = final_norm_g.reshape(1, D) if last else None
        x = _post(x, mods[l], None, g2[l], cu, cb, cw[l], attn, o_f, o_b, gl, ng[l],
                  w_out_p, wu, wd, l, fin, tm)
        if not last:
            attn_c = _attention(cq, None, None, ck, cv, attn_sink[l], local=False)
            ctx = _post(ctx, mods[l], B, g2[l], ccu, ccb, cw[l], attn_c, oc_f, oc_b, cgl, ng[l],
                        w_out_p, wu, wd, l, None, Lc)
    return x
```

```python
import functools

import numpy as np
import jax
import jax.numpy as jnp
from jax import lax
from jax.experimental import pallas as pl
from jax.experimental.pallas import tpu as pltpu

F32 = jnp.float32
BF16 = jnp.bfloat16

D_MODEL = 1024
HEAD_DIM = 64
CONV_CH = 256
CONV_K = 3
ATT_HEADS = 8
ATT_KV_HEADS = 2
ATT_W = ATT_HEADS * HEAD_DIM
KV_W = ATT_KV_HEADS * HEAD_DIM
WINDOW = 128
GLA_HEADS = 4
GLA_W = GLA_HEADS * HEAD_DIM
GLA_RANK = 16
GLA_NORMALIZER = 16.0
D_FF = 2816
N_MOD = 6
GRID_W = 64
ROPE_BASE = 10000.0
EPS = 1e-6
LOG2_E = 1.4426950408889634

LANES = 128
C_CONV = 0
C_QK = 768
C_V = 1408
C_LR = 1536
C_GLA = 1664
N_IN_PAD = C_GLA + 4 * GLA_W
Q_HEAD_ORDER = (0, 4, 1, 5, 2, 6, 3, 7)

ATT_ROW_GROUP = 16
GLA_CHUNK = 64
GLA_SUB = 16
GLA_FAST_RANGE = 60.0
FF_CHUNK = 256
N_FF_CHUNKS = D_FF // FF_CHUNK
PROJ_GROUP_ROWS = 512
POST_ROW_GROUPS = 2
FF_UNROLL = 3
assert (N_FF_CHUNKS - 2) % FF_UNROLL == 0

VMEM_LIMIT = 56 * 1024 * 1024


def _const_spec(shape):
    nd = len(shape)
    return pl.BlockSpec(shape, lambda *_: (0,) * nd, pipeline_mode=pl.Buffered(1))


def _layer_spec(shape, layer):
    nd = len(shape)
    return pl.BlockSpec((None,) + tuple(shape), lambda *_: (layer,) + (0,) * nd,
                        pipeline_mode=pl.Buffered(1))


def _params(n_grid):
    return pltpu.CompilerParams(
        dimension_semantics=("arbitrary",) * n_grid, vmem_limit_bytes=VMEM_LIMIT)


def _silu(a):
    return a * (1.0 / (1.0 + jnp.exp(-a)))


def _dot(a, b):
    return jnp.dot(a, b, preferred_element_type=F32)


def _dot_nt(a, b):
    return lax.dot_general(a, b, (((1,), (1,)), ((), ())), preferred_element_type=F32)


def _dot_tn(a, b):
    return lax.dot_general(a, b, (((0,), (0,)), ((), ())), preferred_element_type=F32)


def _mod_kernel(c_ref, w_ref, b_ref, o_ref):
    a = _silu(c_ref[...]).astype(BF16)
    o_ref[...] = _dot(a, w_ref[...].astype(BF16)) + b_ref[...]


def _modulation(cvec, w_mod, b_mod):
    L, D, W = w_mod.shape
    tn = 2048
    return pl.pallas_call(
        _mod_kernel,
        grid=(L, W // tn),
        in_specs=[
            pl.BlockSpec((8, D), lambda l, j: (0, 0)),
            pl.BlockSpec((None, D, tn), lambda l, j: (l, 0, j)),
            pl.BlockSpec((None, 1, tn), lambda l, j: (l, 0, j)),
        ],
        out_specs=pl.BlockSpec((None, 8, tn), lambda l, j: (l, 0, j)),
        out_shape=jax.ShapeDtypeStruct((L, 8, W), F32),
        compiler_params=_params(2),
        name="modulation",
    )(cvec, w_mod, b_mod.reshape(L, 1, W))


def _proj_kernel(x_ref, mod_ref, g1_ref, w_ref, wg_ref, bg_ref, *rest, rope):
    if rope:
        cos_ref, sa_ref, sb_ref = rest[:3]
        rest = rest[3:]
    cu_ref, cb_ref, q_ref, k_ref, v_ref, gl_ref, gate_ref = rest
    tm = x_ref.shape[0]
    rows_per = PROJ_GROUP_ROWS if tm % PROJ_GROUP_ROWS == 0 else tm
    scale = HEAD_DIM ** -0.5

    for gr in range(tm // rows_per):
        rs = slice(gr * rows_per, (gr + 1) * rows_per)
        x = x_ref[rs, :]
        ms = jnp.mean(x * x, axis=-1, keepdims=True)
        y = x * lax.rsqrt(ms + EPS) * g1_ref[...]
        h = (y * (1.0 + mod_ref[1:2, :]) + mod_ref[0:1, :]).astype(BF16)

        def proj(lo, hi, h=h):
            return _dot(h, w_ref[:, lo:hi])

        pc = proj(C_CONV, C_QK)
        cu_ref[rs, :] = pc[:, 2 * CONV_CH:3 * CONV_CH] * pc[:, 0:CONV_CH]
        cb_ref[rs, :] = pc[:, CONV_CH:2 * CONV_CH]

        pqk = proj(C_QK, C_GLA)
        for m in range((ATT_W + KV_W) // LANES):
            col = pqk[:, m * LANES:(m + 1) * LANES]
            if rope:
                col = (col * cos_ref[rs, :]
                       + pltpu.roll(col, LANES - HEAD_DIM // 4, axis=1) * sa_ref[rs, :]
                       + pltpu.roll(col, HEAD_DIM // 4, axis=1) * sb_ref[rs, :])
            if m < ATT_W // LANES:
                q_ref[rs, m * LANES:(m + 1) * LANES] = (col * scale).astype(BF16)
            else:
                k_ref[rs, :] = col.astype(BF16)

        v_ref[rs, :] = pqk[:, C_V - C_QK:C_LR - C_QK].astype(BF16)

        pg = proj(C_GLA, N_IN_PAD)
        gl_ref[rs, 0:GLA_W] = pg[:, 0:GLA_W] * scale
        gl_ref[rs, GLA_W:4 * GLA_W] = pg[:, GLA_W:4 * GLA_W]

        lr = pqk[:, C_LR - C_QK:C_GLA - C_QK].astype(BF16)
        z = _dot(lr, wg_ref[...]) + bg_ref[...]
        log_sig = jnp.minimum(z, 0.0) - jnp.log1p(jnp.exp(-jnp.abs(z)))
        gate_ref[rs, :] = log_sig * (1.0 / GLA_NORMALIZER)


def _proj(xs, mods_l, mod_row, g1, w_in, layer, wg, bg, rope_tabs, tm):
    B, S, D = xs.shape
    rope = rope_tabs is not None
    nt = S // tm
    row = lambda w: pl.BlockSpec((None, tm, w), lambda b, i: (b, i, 0))
    if mod_row is None:
        mod_spec = pl.BlockSpec((None, 8, D), lambda b, i: (b, 0, 0))
    else:
        mod_spec = pl.BlockSpec((None, 8, D), lambda b, i: (mod_row, 0, 0))
    in_specs = [row(D), mod_spec, _const_spec((1, D)), _layer_spec((D, N_IN_PAD), layer),
                _const_spec((LANES, 2 * GLA_W)), _const_spec((1, 2 * GLA_W))]
    args = [xs, mods_l, g1, w_in, wg, bg]
    if rope:
        in_specs += [pl.BlockSpec((tm, LANES), lambda b, i: (i, 0))] * 3
        args += list(rope_tabs)
    widths = (CONV_CH, CONV_CH, ATT_W, KV_W, KV_W, 4 * GLA_W, 2 * GLA_W)
    dtypes = (F32, F32, BF16, BF16, BF16, F32, F32)
    return pl.pallas_call(
        functools.partial(_proj_kernel, rope=rope),
        grid=(B, nt),
        in_specs=in_specs,
        out_specs=[row(w) for w in widths],
        out_shape=[jax.ShapeDtypeStruct((B, S, w), dt) for w, dt in zip(widths, dtypes)],
        compiler_params=_params(2),
        name="proj_rope" if rope else "proj_ctx",
    )(*args)


def _attn_kernel(sink_ref, q_ref, kc_ref, vc_ref, *rest, local, n_sub):
    tb = q_ref.shape[0] // n_sub
    n_pairs = ATT_W // LANES
    n_ctx = kc_ref.shape[0]
    rg = ATT_ROW_GROUP
    if local:
        kp_ref, km_ref, kn_ref, vp_ref, vm_ref, vn_ref, o_ref, s_scr, p_scr, e_scr, bias_scr = rest
        i = pl.program_id(1)
        n = pl.num_programs(1)
        k_loc = jnp.concatenate([kp_ref[...], km_ref[...], kn_ref[...]], axis=0)
        v_loc = jnp.concatenate([vp_ref[...], vm_ref[...], vn_ref[...]], axis=0)
        r = lax.broadcasted_iota(jnp.int32, (tb, tb), 0)
        j = lax.broadcasted_iota(jnp.int32, (tb, tb), 1)
        bias_prev = jnp.where(j >= r, 0.0, -jnp.inf)
        bias_next = jnp.where(j <= r, 0.0, -jnp.inf)
        bias_scr[0] = bias_prev + jnp.where(i > 0, 0.0, -jnp.inf)
        bias_scr[1] = bias_prev
        bias_scr[2] = bias_next
        bias_scr[3] = bias_next + jnp.where(i < n - 1, 0.0, -jnp.inf)
        nk = n_ctx + 3 * tb
    else:
        o_ref, s_scr, p_scr, e_scr = rest
        nk = n_ctx
    lane = lax.broadcasted_iota(jnp.int32, (1, LANES), 1)
    lo = lane < HEAD_DIM
    hi = jnp.logical_not(lo)
    lo_k = lax.broadcasted_iota(jnp.int32, (nk, LANES), 1) < HEAD_DIM
    group_ones = [jnp.where(lo_k, 1.0, 0.0).astype(BF16), jnp.where(lo_k, 0.0, 1.0).astype(BF16)]

    def scores(jb):
        q = q_ref[jb * tb:(jb + 1) * tb, :]
        lhs = jnp.concatenate([q[:, m * LANES:(m + 1) * LANES] for m in range(n_pairs)], axis=0)
        kk = (jnp.concatenate([kc_ref[...], k_loc[jb * tb:(jb + 3) * tb]], axis=0) if local
              else kc_ref[...])
        k2 = jnp.concatenate([kk * group_ones[0], kk * group_ones[1]], axis=0)
        s_scr[jb] = _dot_nt(lhs, k2)

    def softmax(jb):
        for m in range(n_pairs):
            sinks = [sink_ref[m + (ATT_HEADS // ATT_KV_HEADS) * g] for g in range(ATT_KV_HEADS)]
            for gi in range(tb // rg):
                rq = gi * rg
                r0 = m * tb + rq
                s = s_scr[jb, r0:r0 + rg, :]
                probs, sink_terms = [], []
                for g in range(ATT_KV_HEADS):
                    sg = s[:, g * nk:(g + 1) * nk]
                    if local:
                        bp = bias_scr[0 if jb == 0 else 1, rq:rq + rg, :]
                        bn = bias_scr[3 if jb == n_sub - 1 else 2, rq:rq + rg, :]
                        sg = jnp.concatenate(
                            [sg[:, :n_ctx], sg[:, n_ctx:n_ctx + tb] + bp,
                             sg[:, n_ctx + tb:n_ctx + 2 * tb], sg[:, n_ctx + 2 * tb:] + bn], axis=1)
                    mx = jnp.maximum(jnp.max(sg, axis=-1, keepdims=True), sinks[g])
                    probs.append(jnp.exp(sg - mx).astype(BF16))
                    sink_terms.append(jnp.exp(sinks[g] - mx))
                p_scr[jb, r0:r0 + rg, :] = jnp.concatenate(probs, axis=1)
                e_scr[jb, r0:r0 + rg, :] = jnp.where(lo, sink_terms[0], sink_terms[1])

    def values(jb):
        vv = (jnp.concatenate([vc_ref[...], v_loc[jb * tb:(jb + 3) * tb]], axis=0) if local
              else vc_ref[...])
        v2 = jnp.concatenate(
            [jnp.concatenate([vv * group_ones[0], group_ones[0]], axis=1),
             jnp.concatenate([vv * group_ones[1], group_ones[1]], axis=1)], axis=0)
        o_aug = _dot(p_scr[jb], v2)
        o = o_aug[:, :LANES] * (1.0 / (o_aug[:, LANES:] + e_scr[jb]))
        for m in range(n_pairs):
            o_ref[jb * tb:(jb + 1) * tb, m * LANES:(m + 1) * LANES] = (
                o[m * tb:(m + 1) * tb].astype(BF16))

    scores(0)
    for jb in range(n_sub):
        if jb + 1 < n_sub:
            scores(jb + 1)
        softmax(jb)
        values(jb)


def _attention(q, k, v, k_ctx, v_ctx, sink, local, n_sub=4):
    B, S, _ = q.shape
    Lc = k_ctx.shape[1]
    if not local:
        n_sub = 1
    tq = WINDOW * n_sub if local else S
    nq = S // tq
    in_specs = [
        pl.BlockSpec(memory_space=pltpu.SMEM),
        pl.BlockSpec((None, tq, ATT_W), lambda b, i: (b, i, 0)),
        pl.BlockSpec((None, Lc, KV_W), lambda b, i: (b, 0, 0)),
        pl.BlockSpec((None, Lc, KV_W), lambda b, i: (b, 0, 0)),
    ]
    args = [sink, q, k_ctx, v_ctx]
    if local:
        nw = S // WINDOW
        nb = [pl.BlockSpec((None, WINDOW, KV_W), lambda b, i: (b, jnp.maximum(i * n_sub - 1, 0), 0)),
              pl.BlockSpec((None, tq, KV_W), lambda b, i: (b, i, 0)),
              pl.BlockSpec((None, WINDOW, KV_W),
                           lambda b, i: (b, jnp.minimum((i + 1) * n_sub, nw - 1), 0))]
        in_specs += nb + nb
        args += [k, k, k, v, v, v]
    tb = tq // n_sub
    nk = Lc + (3 * tb if local else 0)
    n_rows = (ATT_W // LANES) * tb
    scratch = [pltpu.VMEM((n_sub, n_rows, ATT_KV_HEADS * nk), F32),
               pltpu.VMEM((n_sub, n_rows, ATT_KV_HEADS * nk), BF16),
               pltpu.VMEM((n_sub, n_rows, LANES), F32)]
    if local:
        scratch.append(pltpu.VMEM((4, tb, tb), F32))
    return pl.pallas_call(
        functools.partial(_attn_kernel, local=local, n_sub=n_sub),
        grid=(B, nq),
        in_specs=in_specs,
        out_specs=pl.BlockSpec((None, tq, ATT_W), lambda b, i: (b, i, 0)),
        out_shape=jax.ShapeDtypeStruct((B, S, ATT_W), BF16),
        scratch_shapes=scratch,
        compiler_params=_params(2),
        name="attn_window" if local else "attn_ctx",
    )(*args)


def _gla_consts(rev):
    t = np.arange(GLA_CHUNK)[:, None]
    s = np.arange(GLA_CHUNK)[None, :]
    same = (t // GLA_SUB) == (s // GLA_SUB)
    before = (s >= t) if rev else (s <= t)
    return np.concatenate([same & before, before], axis=0).astype(np.float32)


def _head_blocks():
    h = np.arange(GLA_W) // HEAD_DIM
    return (h[:, None] == h[None, :]).astype(np.float32)


def _gla_last_row(blk, rev):
    return blk * GLA_SUB if rev else blk * GLA_SUB + GLA_SUB - 1


def _gla_prepare(q, k, g, cum, rev):
    n_sub = GLA_CHUNK // GLA_SUB
    g2 = g * LOG2_E
    g_hi = g2.astype(BF16)
    rem = g2 - g_hi.astype(F32)
    g_mid = rem.astype(BF16)
    g_lo = (rem - g_mid.astype(F32)).astype(BF16)
    cs = _dot(cum, g_hi) + _dot(cum, g_mid) + _dot(cum, g_lo)
    beta, bch = cs[0:GLA_CHUNK], cs[GLA_CHUNK:2 * GLA_CHUNK]
    gam = jnp.concatenate(
        [beta[_gla_last_row(b, rev):_gla_last_row(b, rev) + 1, :] - beta[b * GLA_SUB:(b + 1) * GLA_SUB, :]
         for b in range(n_sub)], axis=0)
    end_row = _gla_last_row(0 if rev else n_sub - 1, rev)
    dch = bch[end_row:end_row + 1, :] - bch
    f32_rows = jnp.concatenate([beta, bch, q * jnp.exp2(beta), k * jnp.exp2(gam)], axis=0)
    b16_rows = jnp.concatenate([(q * jnp.exp2(bch)).astype(BF16), (k * jnp.exp2(dch)).astype(BF16)],
                               axis=0)
    return f32_rows, b16_rows


def _gla_causal(rev):
    t = (np.arange(GLA_HEADS * GLA_CHUNK) % GLA_CHUNK)[:, None]
    s = np.arange(GLA_CHUNK)[None, :]
    return ((s >= t) if rev else (s <= t)).astype(np.float32)


def _gla_prepare_fast(q, k, v, g, cum_chunk, head_mask, rev):
    g2 = g * LOG2_E
    g_hi = g2.astype(BF16)
    g_lo = (g2 - g_hi.astype(F32)).astype(BF16)
    bch = _dot(cum_chunk, g_hi) + _dot(cum_chunk, g_lo)
    end_row = 0 if rev else GLA_CHUNK - 1
    mid_row = GLA_CHUNK // 2 if rev else GLA_CHUNK // 2 - 1
    b_end = bch[end_row:end_row + 1, :]
    b_mid = bch[mid_row:mid_row + 1, :]
    ok = (jnp.max(-b_end) <= GLA_FAST_RANGE).astype(jnp.int32)
    q_mid = (q * jnp.exp2(bch - b_mid)).astype(BF16)
    k_mid = (k * jnp.exp2(b_mid - bch)).astype(BF16)
    zero = jnp.zeros_like(q_mid)
    rows = jnp.concatenate(
        [jnp.where(hm, q_mid, zero) for hm in head_mask]
        + [k_mid, (q * jnp.exp2(bch)).astype(BF16), (k * jnp.exp2(b_end - bch)).astype(BF16),
           v.astype(BF16)], axis=0)
    return rows, jnp.exp2(b_end), ok


def _gla_tile_fast(pb_scr, dec_scr, dirs, bd, head_mask, n_chunks):
    n_q = GLA_HEADS * GLA_CHUNK
    work = []
    for d, (_, _, _, st_ref, o_ref, caus_ref, rev) in enumerate(dirs):
        for c in (range(n_chunks - 1, -1, -1) if rev else range(n_chunks)):
            pb = pb_scr.at[d, c]
            vb = pb[n_q + 3 * GLA_CHUNK:n_q + 4 * GLA_CHUNK]
            upd = _dot_tn(vb, pb[n_q + 2 * GLA_CHUNK:n_q + 3 * GLA_CHUNK])
            a = _dot_nt(pb[0:n_q], pb[n_q:n_q + GLA_CHUNK])
            work.append(dict(d=d, c=c, pb=pb, vb=vb, upd=upd, a=a, o_ref=o_ref, caus=caus_ref))
    for d, (_, _, _, st_ref, _, _, _) in enumerate(dirs):
        st = st_ref[...]
        for w in (w for w in work if w["d"] == d):
            w["o_inter"] = _dot_nt(w["pb"][n_q + GLA_CHUNK:n_q + 2 * GLA_CHUNK], st.astype(BF16))
            st = st * dec_scr[d, w["c"], 0:1, :] + w["upd"] * bd
        st_ref[...] = st
    for w in work:
        w["r"] = _dot((w["a"] * w["caus"][...]).astype(BF16), w["vb"])
    for w in work:
        o = w["o_inter"]
        for hh, hm in enumerate(head_mask):
            o = o + jnp.where(hm, w["r"][hh * GLA_CHUNK:(hh + 1) * GLA_CHUNK, :], 0.0)
        w["o_ref"][w["c"] * GLA_CHUNK:(w["c"] + 1) * GLA_CHUNK, :] = o


def _gla_chunk(q, k, v, pf_ref, pb_ref, st_ref, o_ref, r0, ones, bd, head_mask, rev):
    n_sub = GLA_CHUNK // GLA_SUB
    order = list(range(n_sub))[::-1] if rev else list(range(n_sub))

    def rows(a, blk):
        return a[blk * GLA_SUB:(blk + 1) * GLA_SUB, :]

    half = GLA_SUB // 2
    beta = pf_ref[0:GLA_CHUNK]
    q_loc = pf_ref[2 * GLA_CHUNK:3 * GLA_CHUNK]
    k_loc = pf_ref[3 * GLA_CHUNK:4 * GLA_CHUNK]
    g_tot = [beta[_gla_last_row(b, rev):_gla_last_row(b, rev) + 1, :] for b in range(n_sub)]
    end_row = GLA_CHUNK + _gla_last_row(order[-1], rev)
    b_end = pf_ref[end_row:end_row + 1, :]
    half_row = lax.broadcasted_iota(jnp.int32, (half, 1), 0)

    st = st_ref[...]
    o_inter = _dot_nt(pb_ref[0:GLA_CHUNK], st.astype(BF16))
    upd = _dot_tn(v.astype(BF16), pb_ref[GLA_CHUNK:2 * GLA_CHUNK])
    scores, values = {}, {}
    for p in range(1, n_sub):
        ks, vs = [], []
        for pp in range(p):
            sb = order[pp]
            kk = rows(k_loc, sb)
            mids = [order[x] for x in range(pp + 1, p)]
            if mids:
                tot = g_tot[mids[0]]
                for mb in mids[1:]:
                    tot = tot + g_tot[mb]
                kk = kk * jnp.exp2(tot)
            ks.append(kk)
            vs.append(rows(v, sb))
        qt = rows(q_loc, order[p])
        q_heads = jnp.concatenate([jnp.where(hm, qt, 0.0) for hm in head_mask], axis=0)
        scores[p] = _dot_nt(q_heads.astype(BF16), jnp.concatenate(ks, axis=0).astype(BF16))
        values[p] = jnp.concatenate(vs, axis=0).astype(BF16)
    yield

    pairs = []
    for s in range(GLA_SUB):
        hs = s // half
        for hb in (range(0, hs + 1) if rev else range(hs, GLA_SUB // half)):
            pairs.append((s, hb))
    sums = {}
    for p, tb in enumerate(order):
        bt, qb, kb = rows(beta, tb), rows(q, tb), rows(k, tb)
        slabs = []
        for s, hb in pairs:
            d = bt[hb * half:(hb + 1) * half, :] - bt[s:s + 1, :]
            if hb == s // half:
                keep = (half_row <= s % half) if rev else (half_row >= s % half)
                d = jnp.where(keep, d, -jnp.inf)
            slabs.append(qb[hb * half:(hb + 1) * half, :] * jnp.exp2(d) * kb[s:s + 1, :])
        sums[p] = _dot(jnp.concatenate(slabs, axis=0).astype(BF16), ones)
        yield

    mixed = {p: _dot(scores[p].astype(BF16), values[p]) for p in range(1, n_sub)}
    yield

    for p, tb in enumerate(order):
        vb = rows(v, tb)
        acc = rows(o_inter, tb)
        if p > 0:
            for hh, hm in enumerate(head_mask):
                acc = acc + jnp.where(hm, mixed[p][hh * GLA_SUB:(hh + 1) * GLA_SUB, :], 0.0)
        halves = [acc[hb * half:(hb + 1) * half, :] for hb in range(GLA_SUB // half)]
        for idx, (s, hb) in enumerate(pairs):
            halves[hb] = halves[hb] + sums[p][idx * half:(idx + 1) * half, :] * vb[s:s + 1, :]
        o_ref[pl.ds(r0 + tb * GLA_SUB, GLA_SUB), :] = jnp.concatenate(halves, axis=0)
        yield

    st_ref[...] = st * jnp.exp2(b_end) + upd * bd


def _interleave(gens):
    active = list(gens)
    while active:
        for gen in list(active):
            try:
                next(gen)
            except StopIteration:
                active.remove(gen)


def _gla_kernel(cumf_ref, cumb_ref, ones_ref, bd_ref, causf_ref, causb_ref, glf_ref, glb_ref,
                gf_ref, gb_ref, s0f_ref, s0b_ref, of_ref, ob_ref, sff_ref, sfb_ref,
                stf_ref, stb_ref, pb_scr, dec_scr, ok_ref, *, n_chunks):
    i = pl.program_id(1)
    lane_head = lax.broadcasted_iota(jnp.int32, (1, GLA_W), 1) // HEAD_DIM
    head_mask = [lane_head == hh for hh in range(GLA_HEADS)]
    dirs = ((glf_ref, gf_ref, cumf_ref, stf_ref, of_ref, causf_ref, False),
            (glb_ref, gb_ref, cumb_ref, stb_ref, ob_ref, causb_ref, True))

    @pl.when(i == 0)
    def _():
        stf_ref[...] = s0f_ref[...]
        stb_ref[...] = s0b_ref[...]

    def load(gl_ref, g_ref, r0):
        rows = pl.ds(r0, GLA_CHUNK)
        return (gl_ref[rows, 0:GLA_W], gl_ref[rows, GLA_W:2 * GLA_W],
                gl_ref[rows, 2 * GLA_W:3 * GLA_W], g_ref[rows, :])

    for c in range(n_chunks):
        for d, (gl_ref, g_ref, cum_ref, _, _, _, rev) in enumerate(dirs):
            q, k, v, g = load(gl_ref, g_ref, c * GLA_CHUNK)
            rows, dec, ok = _gla_prepare_fast(q, k, v, g, cum_ref[GLA_CHUNK:2 * GLA_CHUNK, :],
                                              head_mask, rev)
            pb_scr[d, c] = rows
            dec_scr[d, c] = jnp.broadcast_to(dec, (8, GLA_W))
            ok_ref[d, c] = ok

    n_ok = ok_ref[0, 0]
    for d in range(2):
        for c in range(n_chunks):
            if (d, c) != (0, 0):
                n_ok = n_ok + ok_ref[d, c]
    fast = n_ok == 2 * n_chunks

    @pl.when(fast)
    def _():
        _gla_tile_fast(pb_scr, dec_scr, dirs, bd_ref[...], head_mask, n_chunks)

    @pl.when(jnp.logical_not(fast))
    def _():
        def chunk(ci, carry):
            cidx = (ci, n_chunks - 1 - ci)
            gens = []
            for d, (gl_ref, g_ref, cum_ref, st_ref, o_ref, _, rev) in enumerate(dirs):
                r0 = pl.multiple_of(cidx[d] * GLA_CHUNK, GLA_CHUNK)
                q, k, v, g = load(gl_ref, g_ref, r0)
                pf, pb = _gla_prepare(q, k, g, cum_ref[...], rev)
                gens.append(_gla_chunk(q, k, v, pf, pb, st_ref, o_ref, r0, ones_ref[...],
                                       bd_ref[...], head_mask, rev))
            _interleave(gens)
            return carry

        lax.fori_loop(0, n_chunks, chunk, 0)

    @pl.when(i == pl.num_programs(1) - 1)
    def _():
        sff_ref[...] = stf_ref[...]
        sfb_ref[...] = stb_ref[...]


def _gla(gl, gates, s0f, s0b, tile):
    B, S, _ = gl.shape
    nt = S // tile
    n_chunks = tile // GLA_CHUNK
    ones_bd = jnp.asarray(_head_blocks(), BF16)
    bd = jnp.asarray(_head_blocks(), F32)
    state_spec = pl.BlockSpec((None, GLA_W, GLA_W), lambda b, i: (b, 0, 0))
    return pl.pallas_call(
        functools.partial(_gla_kernel, n_chunks=n_chunks),
        grid=(B, nt),
        in_specs=[
            _const_spec((2 * GLA_CHUNK, GLA_CHUNK)),
            _const_spec((2 * GLA_CHUNK, GLA_CHUNK)),
            _const_spec((GLA_W, GLA_W)),
            _const_spec((GLA_W, GLA_W)),
            _const_spec((GLA_HEADS * GLA_CHUNK, GLA_CHUNK)),
            _const_spec((GLA_HEADS * GLA_CHUNK, GLA_CHUNK)),
            pl.BlockSpec((None, tile, 3 * GLA_W), lambda b, i: (b, i, 0)),
            pl.BlockSpec((None, tile, 3 * GLA_W), lambda b, i: (b, nt - 1 - i, 0)),
            pl.BlockSpec((None, tile, GLA_W), lambda b, i: (b, i, 0)),
            pl.BlockSpec((None, tile, GLA_W), lambda b, i: (b, nt - 1 - i, 1)),
            state_spec, state_spec,
        ],
        out_specs=[
            pl.BlockSpec((None, tile, GLA_W), lambda b, i: (b, i, 0)),
            pl.BlockSpec((None, tile, GLA_W), lambda b, i: (b, nt - 1 - i, 0)),
            state_spec, state_spec,
        ],
        out_shape=[jax.ShapeDtypeStruct((B, S, GLA_W), F32),
                   jax.ShapeDtypeStruct((B, S, GLA_W), F32),
                   jax.ShapeDtypeStruct((B, GLA_W, GLA_W), F32),
                   jax.ShapeDtypeStruct((B, GLA_W, GLA_W), F32)],
        scratch_shapes=[pltpu.VMEM((GLA_W, GLA_W), F32), pltpu.VMEM((GLA_W, GLA_W), F32),
                        pltpu.VMEM((2, n_chunks, (GLA_HEADS + 4) * GLA_CHUNK, GLA_W), BF16),
                        pltpu.VMEM((2, n_chunks, 8, GLA_W), F32),
                        pltpu.SMEM((2, n_chunks), jnp.int32)],
        compiler_params=_params(2),
        name="gla",
    )(jnp.asarray(_gla_consts(False), BF16), jnp.asarray(_gla_consts(True), BF16),
      ones_bd, bd, jnp.asarray(_gla_causal(False), F32), jnp.asarray(_gla_causal(True), F32),
      gl, gl, gates, gates, s0f, s0b)


def _post_kernel(x_ref, mod_ref, g2_ref, cu_ref, cup_ref, cun_ref, cb_ref, cw_ref, at_ref,
                 of_ref, ob_ref, gg_ref, ng_ref, ones_ref, wo_ref, wu_ref, wd_ref,
                 *rest, final):
    if final:
        fg_ref, o_ref, acc_ref = rest
    else:
        o_ref, acc_ref = rest
    i = pl.program_id(1)
    n = pl.num_programs(1)
    tm = x_ref.shape[0]

    u = cu_ref[...]
    row = lax.broadcasted_iota(jnp.int32, (tm, 1), 0)
    prev_row = jnp.where(i > 0, cup_ref[7:8, :], 0.0)
    next_row = jnp.where(i < n - 1, cun_ref[0:1, :], 0.0)
    u_prev = jnp.where(row == 0, prev_row, pltpu.roll(u, 1, axis=0))
    u_next = jnp.where(row == tm - 1, next_row, pltpu.roll(u, tm - 1, axis=0))
    conv = cb_ref[...] * (cw_ref[0:1, :] * u_prev + cw_ref[1:2, :] * u + cw_ref[2:3, :] * u_next)

    x1_parts, h2_parts = [], []
    n_groups = POST_ROW_GROUPS if tm % (POST_ROW_GROUPS * 16) == 0 else 1
    rows_per = tm // n_groups
    for gr in range(n_groups):
        rs = slice(gr * rows_per, (gr + 1) * rows_per)
        o = of_ref[rs, :] + ob_ref[rs, :]
        sq = o * o
        sq_hi = sq.astype(BF16)
        sq_lo = (sq - sq_hi.astype(F32)).astype(BF16)
        ms = (_dot(sq_hi, ones_ref[...]) + _dot(sq_lo, ones_ref[...])) * (1.0 / HEAD_DIM)
        gla = o * lax.rsqrt(ms + EPS) * ng_ref[...] * _silu(gg_ref[rs, :])

        mix = (_dot(conv[rs, :].astype(BF16), wo_ref[0:CONV_CH, :])
               + _dot(at_ref[rs, :], wo_ref[CONV_CH:CONV_CH + ATT_W, :])
               + _dot(gla.astype(BF16), wo_ref[CONV_CH + ATT_W:, :]))
        x1_g = x_ref[rs, :] + mod_ref[2:3, :] * mix

        ms2 = jnp.mean(x1_g * x1_g, axis=-1, keepdims=True)
        y2 = x1_g * lax.rsqrt(ms2 + EPS) * g2_ref[...]
        x1_parts.append(x1_g)
        h2_parts.append((y2 * (1.0 + mod_ref[4:5, :]) + mod_ref[3:4, :]).astype(BF16))
    x1 = jnp.concatenate(x1_parts, axis=0)
    h2 = jnp.concatenate(h2_parts, axis=0)

    def up(jc):
        if isinstance(jc, int):
            ca, cb = jc * FF_CHUNK, D_FF + jc * FF_CHUNK
        else:
            ca = pl.multiple_of(jc * FF_CHUNK, FF_CHUNK)
            cb = pl.multiple_of(D_FF + jc * FF_CHUNK, FF_CHUNK)
        a = _dot(h2, wu_ref[:, pl.ds(ca, FF_CHUNK)])
        b = _dot(h2, wu_ref[:, pl.ds(cb, FF_CHUNK)])
        return a, b

    def act(ab):
        return (_silu(ab[0]) * ab[1]).astype(BF16)

    prev = act(up(0))
    ab = up(1)
    acc_ref[...] = _dot(prev, wd_ref[0])
    prev = act(ab)

    def ff(t, prev):
        for u in range(FF_UNROLL):
            jc = 2 + t * FF_UNROLL + u
            ab = up(jc)
            acc_ref[...] += _dot(prev, wd_ref[jc - 1])
            prev = act(ab)
        return prev

    prev = lax.fori_loop(0, (N_FF_CHUNKS - 2) // FF_UNROLL, ff, prev)
    x2 = x1 + mod_ref[5:6, :] * (acc_ref[...] + _dot(prev, wd_ref[N_FF_CHUNKS - 1]))
    if final:
        msf = jnp.mean(x2 * x2, axis=-1, keepdims=True)
        x2 = x2 * lax.rsqrt(msf + EPS) * fg_ref[...]
    o_ref[...] = x2


def _post(xs, mods_l, mod_row, g2, cu, cb, conv_w, attn, o_f, o_b, gl, norm_g, w_out, wu, wd,
          layer, final_g, tm):
    B, S, D = xs.shape
    nt = S // tm
    hb = tm // 8
    row = lambda w: pl.BlockSpec((None, tm, w), lambda b, i: (b, i, 0))
    if mod_row is None:
        mod_spec = pl.BlockSpec((None, 8, D), lambda b, i: (b, 0, 0))
    else:
        mod_spec = pl.BlockSpec((None, 8, D), lambda b, i: (mod_row, 0, 0))
    final = final_g is not None
    in_specs = [
        row(D), mod_spec, _const_spec((1, D)),
        row(CONV_CH),
        pl.BlockSpec((None, 8, CONV_CH), lambda b, i: (b, jnp.maximum(i * hb - 1, 0), 0)),
        pl.BlockSpec((None, 8, CONV_CH), lambda b, i: (b, jnp.minimum((i + 1) * hb, S // 8 - 1), 0)),
        row(CONV_CH), _const_spec((8, CONV_CH)), row(ATT_W), row(GLA_W), row(GLA_W),
        pl.BlockSpec((None, tm, GLA_W), lambda b, i: (b, i, 3)),
        _const_spec((1, GLA_W)), _const_spec((GLA_W, GLA_W)), _layer_spec((D, D), layer),
        _layer_spec((D, 2 * D_FF), layer), _layer_spec((N_FF_CHUNKS, FF_CHUNK, D), layer),
    ]
    args = [xs, mods_l, g2, cu, cu, cu, cb, conv_w, attn, o_f, o_b, gl, norm_g,
            jnp.asarray(_head_blocks(), BF16), w_out, wu, wd]
    if final:
        in_specs.append(_const_spec((1, D)))
        args.append(final_g)
    return pl.pallas_call(
        functools.partial(_post_kernel, final=final),
        grid=(B, nt),
        in_specs=in_specs,
        out_specs=row(D),
        out_shape=jax.ShapeDtypeStruct((B, S, D), F32),
        scratch_shapes=[pltpu.VMEM((tm, D), F32)],
        compiler_params=_params(2),
        name="post_final" if final else "post",
    )(*args)


def _rope_tables(seq):
    t = jnp.arange(seq)
    pos = jnp.stack([(t // GRID_W).astype(F32), (t % GRID_W).astype(F32)], axis=1)
    n_freq = HEAD_DIM // 4
    inv_freq = ROPE_BASE ** (-jnp.arange(n_freq, dtype=F32) / n_freq)
    ang = pos[:, :, None] * inv_freq
    cos, sin = jnp.cos(ang), jnp.sin(ang)
    zero = jnp.zeros_like(sin)
    cos_t = jnp.stack([cos, cos], axis=2).reshape(seq, HEAD_DIM)
    sa_t = jnp.stack([-sin, zero], axis=2).reshape(seq, HEAD_DIM)
    sb_t = jnp.stack([zero, sin], axis=2).reshape(seq, HEAD_DIM)
    rep = LANES // HEAD_DIM
    return tuple(jnp.tile(a, (1, rep)) for a in (cos_t, sa_t, sb_t))


def _tile_rows(S, want):
    t = min(want, S)
    while S % t:
        t //= 2
    return t


def kernel(x, c, ctx, c_ctx, w_mod, b_mod, norm1_g, norm2_g, w_in, conv_w, attn_sink,
           gla_gate_w, gla_gate_b, gla_norm_g, w_out, w_up, w_down, final_norm_g):
    B, S, D = x.shape
    L = w_mod.shape[0]
    Lc = ctx.shape[1]
    assert D == D_MODEL and S % WINDOW == 0 and Lc % GLA_CHUNK == 0 and B + 1 <= 8

    cvec = jnp.zeros((8, D), F32).at[:B].set(c).at[B].set(c_ctx)
    mods = _modulation(cvec, w_mod, b_mod).reshape(L, 8, N_MOD, D)
    mods = jnp.pad(mods, ((0, 0), (0, 0), (0, 8 - N_MOD), (0, 0)))

    order = np.array(Q_HEAD_ORDER)
    wq = w_in[:, :, 768:1280].reshape(L, D, ATT_HEADS, HEAD_DIM)[:, :, order].reshape(L, D, ATT_W)
    w_in_p = jnp.concatenate(
        [w_in[:, :, :768], wq, w_in[:, :, 1280:1536], w_in[:, :, 2560:2592],
         jnp.zeros((L, D, LANES - 2 * GLA_RANK), F32), w_in[:, :, 1536:2560]], axis=2).astype(BF16)
    wo_att = w_out[:, 256:768].reshape(L, ATT_HEADS, HEAD_DIM, D)[:, order].reshape(L, ATT_W, D)
    w_out_p = jnp.concatenate([w_out[:, :256], wo_att, w_out[:, 768:]], axis=1).astype(BF16)
    wg = jnp.zeros((L, LANES, 2 * GLA_W), F32)
    wg = wg.at[:, :GLA_RANK, :GLA_W].set(gla_gate_w[:, 0])
    wg = wg.at[:, GLA_RANK:2 * GLA_RANK, GLA_W:].set(gla_gate_w[:, 1]).astype(BF16)
    bg = gla_gate_b.reshape(L, 1, 2 * GLA_W)
    cw = jnp.pad(conv_w, ((0, 0), (0, 8 - CONV_K), (0, 0)))
    ng = jnp.tile(gla_norm_g, (1, GLA_HEADS)).reshape(L, 1, GLA_W)
    wu = w_up.astype(BF16)
    wd = w_down.reshape(L, N_FF_CHUNKS, FF_CHUNK, D).astype(BF16)
    g1 = norm1_g.reshape(L, 1, D)
    g2 = norm2_g.reshape(L, 1, D)

    rope_tabs = _rope_tables(S)
    tm = _tile_rows(S, 512)
    proj_tm = _tile_rows(S, 1024)
    gla_tile = _tile_rows(S, 1024)
    zero_state = jnp.zeros((B, GLA_W, GLA_W), F32)

    for l in range(L):
        last = l == L - 1
        cu, cb, q, k, v, gl, gates = _proj(x, mods[l], None, g1[l], w_in_p, l, wg[l], bg[l],
                                           rope_tabs, proj_tm)
        ccu, ccb, cq, ck, cv, cgl, cgates = _proj(ctx, mods[l], B, g1[l], w_in_p, l, wg[l], bg[l],
                                                  None, Lc)
        attn = _attention(q, k, v, ck, cv, attn_sink[l], local=True)
        oc_f, oc_b, sc_f, sc_b = _gla(cgl, cgates, zero_state, zero_state, Lc)
        o_f, o_b, _, _ = _gla(gl, gates, sc_f, sc_b, gla_tile)
        fin = final_norm_g.reshape(1, D) if last else None
        x = _post(x, mods[l], None, g2[l], cu, cb, cw[l], attn, o_f, o_b, gl, ng[l],
                  w_out_p, wu, wd, l, fin, tm)
        if not last:
            attn_c = _attention(cq, None, None, ck, cv, attn_sink[l], local=False)
            ctx = _post(ctx, mods[l], B, g2[l], ccu, ccb, cw[l], attn_c, oc_f, oc_b, cgl, ng[l],
                        w_out_p, wu, wd, l, None, Lc)
    return x
```

```python
import functools

import numpy as np
import jax
import jax.numpy as jnp
from jax import lax
from jax.experimental import pallas as pl
from jax.experimental.pallas import tpu as pltpu

F32 = jnp.float32
BF16 = jnp.bfloat16

D_MODEL = 1024
HEAD_DIM = 64
CONV_CH = 256
CONV_K = 3
ATT_HEADS = 8
ATT_KV_HEADS = 2
ATT_W = ATT_HEADS * HEAD_DIM
KV_W = ATT_KV_HEADS * HEAD_DIM
WINDOW = 128
GLA_HEADS = 4
GLA_W = GLA_HEADS * HEAD_DIM
GLA_RANK = 16
GLA_NORMALIZER = 16.0
D_FF = 2816
N_MOD = 6
GRID_W = 64
ROPE_BASE = 10000.0
EPS = 1e-6
LOG2_E = 1.4426950408889634

LANES = 128
C_CONV = 0
C_QK = 768
C_V = 1408
C_LR = 1536
C_GLA = 1664
N_IN_PAD = C_GLA + 4 * GLA_W
Q_HEAD_ORDER = (0, 4, 1, 5, 2, 6, 3, 7)

ATT_ROW_GROUP = 16
GLA_CHUNK = 64
GLA_SUB = 16
GLA_FAST_RANGE = 60.0
FF_CHUNK = 256
N_FF_CHUNKS = D_FF // FF_CHUNK
PROJ_GROUP_ROWS = 512
POST_ROW_GROUPS = 2
FF_UNROLL = 9
assert (N_FF_CHUNKS - 2) % FF_UNROLL == 0

VMEM_LIMIT = 56 * 1024 * 1024


def _const_spec(shape):
    nd = len(shape)
    return pl.BlockSpec(shape, lambda *_: (0,) * nd, pipeline_mode=pl.Buffered(1))


def _layer_spec(shape, layer):
    nd = len(shape)
    return pl.BlockSpec((None,) + tuple(shape), lambda *_: (layer,) + (0,) * nd,
                        pipeline_mode=pl.Buffered(1))


def _params(n_grid):
    return pltpu.CompilerParams(
        dimension_semantics=("arbitrary",) * n_grid, vmem_limit_bytes=VMEM_LIMIT)


def _silu(a):
    return a * (1.0 / (1.0 + jnp.exp(-a)))


def _dot(a, b):
    return jnp.dot(a, b, preferred_element_type=F32)


def _dot_nt(a, b):
    return lax.dot_general(a, b, (((1,), (1,)), ((), ())), preferred_element_type=F32)


def _dot_tn(a, b):
    return lax.dot_general(a, b, (((0,), (0,)), ((), ())), preferred_element_type=F32)


def _mod_kernel(c_ref, w_ref, b_ref, o_ref):
    a = _silu(c_ref[...]).astype(BF16)
    o_ref[...] = _dot(a, w_ref[...].astype(BF16)) + b_ref[...]


def _modulation(cvec, w_mod, b_mod):
    L, D, W = w_mod.shape
    tn = 2048
    return pl.pallas_call(
        _mod_kernel,
        grid=(L, W // tn),
        in_specs=[
            pl.BlockSpec((8, D), lambda l, j: (0, 0)),
            pl.BlockSpec((None, D, tn), lambda l, j: (l, 0, j)),
            pl.BlockSpec((None, 1, tn), lambda l, j: (l, 0, j)),
        ],
        out_specs=pl.BlockSpec((None, 8, tn), lambda l, j: (l, 0, j)),
        out_shape=jax.ShapeDtypeStruct((L, 8, W), F32),
        compiler_params=_params(2),
        name="modulation",
    )(cvec, w_mod, b_mod.reshape(L, 1, W))


def _proj_kernel(x_ref, mod_ref, g1_ref, w_ref, wg_ref, bg_ref, *rest, rope):
    if rope:
        cos_ref, sa_ref, sb_ref = rest[:3]
        rest = rest[3:]
    cu_ref, cb_ref, q_ref, k_ref, v_ref, gl_ref, gate_ref = rest
    tm = x_ref.shape[0]
    rows_per = PROJ_GROUP_ROWS if tm % PROJ_GROUP_ROWS == 0 else tm
    scale = HEAD_DIM ** -0.5

    for gr in range(tm // rows_per):
        rs = slice(gr * rows_per, (gr + 1) * rows_per)
        x = x_ref[rs, :]
        ms = jnp.mean(x * x, axis=-1, keepdims=True)
        y = x * lax.rsqrt(ms + EPS) * g1_ref[...]
        h = (y * (1.0 + mod_ref[1:2, :]) + mod_ref[0:1, :]).astype(BF16)

        def proj(lo, hi, h=h):
            return _dot(h, w_ref[:, lo:hi])

        pc = proj(C_CONV, C_QK)
        cu_ref[rs, :] = pc[:, 2 * CONV_CH:3 * CONV_CH] * pc[:, 0:CONV_CH]
        cb_ref[rs, :] = pc[:, CONV_CH:2 * CONV_CH]

        pqk = proj(C_QK, C_GLA)
        for m in range((ATT_W + KV_W) // LANES):
            col = pqk[:, m * LANES:(m + 1) * LANES]
            if rope:
                col = (col * cos_ref[rs, :]
                       + pltpu.roll(col, LANES - HEAD_DIM // 4, axis=1) * sa_ref[rs, :]
                       + pltpu.roll(col, HEAD_DIM // 4, axis=1) * sb_ref[rs, :])
            if m < ATT_W // LANES:
                q_ref[rs, m * LANES:(m + 1) * LANES] = (col * scale).astype(BF16)
            else:
                k_ref[rs, :] = col.astype(BF16)

        v_ref[rs, :] = pqk[:, C_V - C_QK:C_LR - C_QK].astype(BF16)

        pg = proj(C_GLA, N_IN_PAD)
        gl_ref[rs, 0:GLA_W] = pg[:, 0:GLA_W] * scale
        gl_ref[rs, GLA_W:4 * GLA_W] = pg[:, GLA_W:4 * GLA_W]

        lr = pqk[:, C_LR - C_QK:C_GLA - C_QK].astype(BF16)
        z = _dot(lr, wg_ref[...]) + bg_ref[...]
        log_sig = jnp.minimum(z, 0.0) - jnp.log1p(jnp.exp(-jnp.abs(z)))
        gate_ref[rs, :] = log_sig * (1.0 / GLA_NORMALIZER)


def _proj(xs, mods_l, mod_row, g1, w_in, layer, wg, bg, rope_tabs, tm):
    B, S, D = xs.shape
    rope = rope_tabs is not None
    nt = S // tm
    row = lambda w: pl.BlockSpec((None, tm, w), lambda b, i: (b, i, 0))
    if mod_row is None:
        mod_spec = pl.BlockSpec((None, 8, D), lambda b, i: (b, 0, 0))
    else:
        mod_spec = pl.BlockSpec((None, 8, D), lambda b, i: (mod_row, 0, 0))
    in_specs = [row(D), mod_spec, _const_spec((1, D)), _layer_spec((D, N_IN_PAD), layer),
                _const_spec((LANES, 2 * GLA_W)), _const_spec((1, 2 * GLA_W))]
    args = [xs, mods_l, g1, w_in, wg, bg]
    if rope:
        in_specs += [pl.BlockSpec((tm, LANES), lambda b, i: (i, 0))] * 3
        args += list(rope_tabs)
    widths = (CONV_CH, CONV_CH, ATT_W, KV_W, KV_W, 4 * GLA_W, 2 * GLA_W)
    dtypes = (F32, F32, BF16, BF16, BF16, F32, F32)
    return pl.pallas_call(
        functools.partial(_proj_kernel, rope=rope),
        grid=(B, nt),
        in_specs=in_specs,
        out_specs=[row(w) for w in widths],
        out_shape=[jax.ShapeDtypeStruct((B, S, w), dt) for w, dt in zip(widths, dtypes)],
        compiler_params=_params(2),
        name="proj_rope" if rope else "proj_ctx",
    )(*args)


def _attn_kernel(sink_ref, q_ref, kc_ref, vc_ref, *rest, local, n_sub):
    tb = q_ref.shape[0] // n_sub
    n_pairs = ATT_W // LANES
    n_ctx = kc_ref.shape[0]
    rg = ATT_ROW_GROUP
    if local:
        kp_ref, km_ref, kn_ref, vp_ref, vm_ref, vn_ref, o_ref, s_scr, p_scr, e_scr, bias_scr = rest
        i = pl.program_id(1)
        n = pl.num_programs(1)
        k_loc = jnp.concatenate([kp_ref[...], km_ref[...], kn_ref[...]], axis=0)
        v_loc = jnp.concatenate([vp_ref[...], vm_ref[...], vn_ref[...]], axis=0)
        r = lax.broadcasted_iota(jnp.int32, (tb, tb), 0)
        j = lax.broadcasted_iota(jnp.int32, (tb, tb), 1)
        bias_prev = jnp.where(j >= r, 0.0, -jnp.inf)
        bias_next = jnp.where(j <= r, 0.0, -jnp.inf)
        bias_scr[0] = bias_prev + jnp.where(i > 0, 0.0, -jnp.inf)
        bias_scr[1] = bias_prev
        bias_scr[2] = bias_next
        bias_scr[3] = bias_next + jnp.where(i < n - 1, 0.0, -jnp.inf)
        nk = n_ctx + 3 * tb
    else:
        o_ref, s_scr, p_scr, e_scr = rest
        nk = n_ctx
    lane = lax.broadcasted_iota(jnp.int32, (1, LANES), 1)
    lo = lane < HEAD_DIM
    hi = jnp.logical_not(lo)
    lo_k = lax.broadcasted_iota(jnp.int32, (nk, LANES), 1) < HEAD_DIM
    group_ones = [jnp.where(lo_k, 1.0, 0.0).astype(BF16), jnp.where(lo_k, 0.0, 1.0).astype(BF16)]

    def scores(jb):
        q = q_ref[jb * tb:(jb + 1) * tb, :]
        lhs = jnp.concatenate([q[:, m * LANES:(m + 1) * LANES] for m in range(n_pairs)], axis=0)
        kk = (jnp.concatenate([kc_ref[...], k_loc[jb * tb:(jb + 3) * tb]], axis=0) if local
              else kc_ref[...])
        k2 = jnp.concatenate([kk * group_ones[0], kk * group_ones[1]], axis=0)
        s_scr[jb] = _dot_nt(lhs, k2)

    def softmax(jb):
        for m in range(n_pairs):
            sinks = [sink_ref[m + (ATT_HEADS // ATT_KV_HEADS) * g] for g in range(ATT_KV_HEADS)]
            for gi in range(tb // rg):
                rq = gi * rg
                r0 = m * tb + rq
                s = s_scr[jb, r0:r0 + rg, :]
                probs, sink_terms = [], []
                for g in range(ATT_KV_HEADS):
                    sg = s[:, g * nk:(g + 1) * nk]
                    if local:
                        bp = bias_scr[0 if jb == 0 else 1, rq:rq + rg, :]
                        bn = bias_scr[3 if jb == n_sub - 1 else 2, rq:rq + rg, :]
                        sg = jnp.concatenate(
                            [sg[:, :n_ctx], sg[:, n_ctx:n_ctx + tb] + bp,
                             sg[:, n_ctx + tb:n_ctx + 2 * tb], sg[:, n_ctx + 2 * tb:] + bn], axis=1)
                    mx = jnp.maximum(jnp.max(sg, axis=-1, keepdims=True), sinks[g])
                    probs.append(jnp.exp(sg - mx).astype(BF16))
                    sink_terms.append(jnp.exp(sinks[g] - mx))
                p_scr[jb, r0:r0 + rg, :] = jnp.concatenate(probs, axis=1)
                e_scr[jb, r0:r0 + rg, :] = jnp.where(lo, sink_terms[0], sink_terms[1])

    def values(jb):
        vv = (jnp.concatenate([vc_ref[...], v_loc[jb * tb:(jb + 3) * tb]], axis=0) if local
              else vc_ref[...])
        v2 = jnp.concatenate(
            [jnp.concatenate([vv * group_ones[0], group_ones[0]], axis=1),
             jnp.concatenate([vv * group_ones[1], group_ones[1]], axis=1)], axis=0)
        o_aug = _dot(p_scr[jb], v2)
        o = o_aug[:, :LANES] * (1.0 / (o_aug[:, LANES:] + e_scr[jb]))
        for m in range(n_pairs):
            o_ref[jb * tb:(jb + 1) * tb, m * LANES:(m + 1) * LANES] = (
                o[m * tb:(m + 1) * tb].astype(BF16))

    scores(0)
    for jb in range(n_sub):
        if jb + 1 < n_sub:
            scores(jb + 1)
        softmax(jb)
        values(jb)


def _attention(q, k, v, k_ctx, v_ctx, sink, local, n_sub=4):
    B, S, _ = q.shape
    Lc = k_ctx.shape[1]
    if not local:
        n_sub = 1
    tq = WINDOW * n_sub if local else S
    nq = S // tq
    in_specs = [
        pl.BlockSpec(memory_space=pltpu.SMEM),
        pl.BlockSpec((None, tq, ATT_W), lambda b, i: (b, i, 0)),
        pl.BlockSpec((None, Lc, KV_W), lambda b, i: (b, 0, 0)),
        pl.BlockSpec((None, Lc, KV_W), lambda b, i: (b, 0, 0)),
    ]
    args = [sink, q, k_ctx, v_ctx]
    if local:
        nw = S // WINDOW
        nb = [pl.BlockSpec((None, WINDOW, KV_W), lambda b, i: (b, jnp.maximum(i * n_sub - 1, 0), 0)),
              pl.BlockSpec((None, tq, KV_W), lambda b, i: (b, i, 0)),
              pl.BlockSpec((None, WINDOW, KV_W),
                           lambda b, i: (b, jnp.minimum((i + 1) * n_sub, nw - 1), 0))]
        in_specs += nb + nb
        args += [k, k, k, v, v, v]
    tb = tq // n_sub
    nk = Lc + (3 * tb if local else 0)
    n_rows = (ATT_W // LANES) * tb
    scratch = [pltpu.VMEM((n_sub, n_rows, ATT_KV_HEADS * nk), F32),
               pltpu.VMEM((n_sub, n_rows, ATT_KV_HEADS * nk), BF16),
               pltpu.VMEM((n_sub, n_rows, LANES), F32)]
    if local:
        scratch.append(pltpu.VMEM((4, tb, tb), F32))
    return pl.pallas_call(
        functools.partial(_attn_kernel, local=local, n_sub=n_sub),
        grid=(B, nq),
        in_specs=in_specs,
        out_specs=pl.BlockSpec((None, tq, ATT_W), lambda b, i: (b, i, 0)),
        out_shape=jax.ShapeDtypeStruct((B, S, ATT_W), BF16),
        scratch_shapes=scratch,
        compiler_params=_params(2),
        name="attn_window" if local else "attn_ctx",
    )(*args)


def _gla_consts(rev):
    t = np.arange(GLA_CHUNK)[:, None]
    s = np.arange(GLA_CHUNK)[None, :]
    same = (t // GLA_SUB) == (s // GLA_SUB)
    before = (s >= t) if rev else (s <= t)
    return np.concatenate([same & before, before], axis=0).astype(np.float32)


def _head_blocks():
    h = np.arange(GLA_W) // HEAD_DIM
    return (h[:, None] == h[None, :]).astype(np.float32)


def _gla_last_row(blk, rev):
    return blk * GLA_SUB if rev else blk * GLA_SUB + GLA_SUB - 1


def _gla_prepare(q, k, g, cum, rev):
    n_sub = GLA_CHUNK // GLA_SUB
    g2 = g * LOG2_E
    g_hi = g2.astype(BF16)
    rem = g2 - g_hi.astype(F32)
    g_mid = rem.astype(BF16)
    g_lo = (rem - g_mid.astype(F32)).astype(BF16)
    cs = _dot(cum, g_hi) + _dot(cum, g_mid) + _dot(cum, g_lo)
    beta, bch = cs[0:GLA_CHUNK], cs[GLA_CHUNK:2 * GLA_CHUNK]
    gam = jnp.concatenate(
        [beta[_gla_last_row(b, rev):_gla_last_row(b, rev) + 1, :] - beta[b * GLA_SUB:(b + 1) * GLA_SUB, :]
         for b in range(n_sub)], axis=0)
    end_row = _gla_last_row(0 if rev else n_sub - 1, rev)
    dch = bch[end_row:end_row + 1, :] - bch
    f32_rows = jnp.concatenate([beta, bch, q * jnp.exp2(beta), k * jnp.exp2(gam)], axis=0)
    b16_rows = jnp.concatenate([(q * jnp.exp2(bch)).astype(BF16), (k * jnp.exp2(dch)).astype(BF16)],
                               axis=0)
    return f32_rows, b16_rows


def _gla_causal(rev):
    t = (np.arange(GLA_HEADS * GLA_CHUNK) % GLA_CHUNK)[:, None]
    s = np.arange(GLA_CHUNK)[None, :]
    return ((s >= t) if rev else (s <= t)).astype(np.float32)


def _gla_prepare_fast(q, k, v, g, cum_chunk, head_mask, rev):
    g2 = g * LOG2_E
    g_hi = g2.astype(BF16)
    g_lo = (g2 - g_hi.astype(F32)).astype(BF16)
    bch = _dot(cum_chunk, g_hi) + _dot(cum_chunk, g_lo)
    end_row = 0 if rev else GLA_CHUNK - 1
    mid_row = GLA_CHUNK // 2 if rev else GLA_CHUNK // 2 - 1
    b_end = bch[end_row:end_row + 1, :]
    b_mid = bch[mid_row:mid_row + 1, :]
    ok = (jnp.max(-b_end) <= GLA_FAST_RANGE).astype(jnp.int32)
    q_mid = (q * jnp.exp2(bch - b_mid)).astype(BF16)
    k_mid = (k * jnp.exp2(b_mid - bch)).astype(BF16)
    zero = jnp.zeros_like(q_mid)
    rows = jnp.concatenate(
        [jnp.where(hm, q_mid, zero) for hm in head_mask]
        + [k_mid, (q * jnp.exp2(bch)).astype(BF16), (k * jnp.exp2(b_end - bch)).astype(BF16),
           v.astype(BF16)], axis=0)
    return rows, jnp.exp2(b_end), ok


def _gla_tile_fast(pb_scr, dec_scr, dirs, bd, head_mask, n_chunks):
    n_q = GLA_HEADS * GLA_CHUNK
    work = []
    for d, (_, _, _, st_ref, o_ref, caus_ref, rev) in enumerate(dirs):
        for c in (range(n_chunks - 1, -1, -1) if rev else range(n_chunks)):
            pb = pb_scr.at[d, c]
            vb = pb[n_q + 3 * GLA_CHUNK:n_q + 4 * GLA_CHUNK]
            upd = _dot_tn(vb, pb[n_q + 2 * GLA_CHUNK:n_q + 3 * GLA_CHUNK])
            a = _dot_nt(pb[0:n_q], pb[n_q:n_q + GLA_CHUNK])
            work.append(dict(d=d, c=c, pb=pb, vb=vb, upd=upd, a=a, o_ref=o_ref, caus=caus_ref))
    for d, (_, _, _, st_ref, _, _, _) in enumerate(dirs):
        st = st_ref[...]
        for w in (w for w in work if w["d"] == d):
            w["o_inter"] = _dot_nt(w["pb"][n_q + GLA_CHUNK:n_q + 2 * GLA_CHUNK], st.astype(BF16))
            st = st * dec_scr[d, w["c"], 0:1, :] + w["upd"] * bd
        st_ref[...] = st
    for w in work:
        w["r"] = _dot((w["a"] * w["caus"][...]).astype(BF16), w["vb"])
    for w in work:
        o = w["o_inter"]
        for hh, hm in enumerate(head_mask):
            o = o + jnp.where(hm, w["r"][hh * GLA_CHUNK:(hh + 1) * GLA_CHUNK, :], 0.0)
        w["o_ref"][w["c"] * GLA_CHUNK:(w["c"] + 1) * GLA_CHUNK, :] = o


def _gla_chunk(q, k, v, pf_ref, pb_ref, st_ref, o_ref, r0, ones, bd, head_mask, rev):
    n_sub = GLA_CHUNK // GLA_SUB
    order = list(range(n_sub))[::-1] if rev else list(range(n_sub))

    def rows(a, blk):
        return a[blk * GLA_SUB:(blk + 1) * GLA_SUB, :]

    half = GLA_SUB // 2
    beta = pf_ref[0:GLA_CHUNK]
    q_loc = pf_ref[2 * GLA_CHUNK:3 * GLA_CHUNK]
    k_loc = pf_ref[3 * GLA_CHUNK:4 * GLA_CHUNK]
    g_tot = [beta[_gla_last_row(b, rev):_gla_last_row(b, rev) + 1, :] for b in range(n_sub)]
    end_row = GLA_CHUNK + _gla_last_row(order[-1], rev)
    b_end = pf_ref[end_row:end_row + 1, :]
    half_row = lax.broadcasted_iota(jnp.int32, (half, 1), 0)

    st = st_ref[...]
    o_inter = _dot_nt(pb_ref[0:GLA_CHUNK], st.astype(BF16))
    upd = _dot_tn(v.astype(BF16), pb_ref[GLA_CHUNK:2 * GLA_CHUNK])
    scores, values = {}, {}
    for p in range(1, n_sub):
        ks, vs = [], []
        for pp in range(p):
            sb = order[pp]
            kk = rows(k_loc, sb)
            mids = [order[x] for x in range(pp + 1, p)]
            if mids:
                tot = g_tot[mids[0]]
                for mb in mids[1:]:
                    tot = tot + g_tot[mb]
                kk = kk * jnp.exp2(tot)
            ks.append(kk)
            vs.append(rows(v, sb))
        qt = rows(q_loc, order[p])
        q_heads = jnp.concatenate([jnp.where(hm, qt, 0.0) for hm in head_mask], axis=0)
        scores[p] = _dot_nt(q_heads.astype(BF16), jnp.concatenate(ks, axis=0).astype(BF16))
        values[p] = jnp.concatenate(vs, axis=0).astype(BF16)
    yield

    pairs = []
    for s in range(GLA_SUB):
        hs = s // half
        for hb in (range(0, hs + 1) if rev else range(hs, GLA_SUB // half)):
            pairs.append((s, hb))
    sums = {}
    for p, tb in enumerate(order):
        bt, qb, kb = rows(beta, tb), rows(q, tb), rows(k, tb)
        slabs = []
        for s, hb in pairs:
            d = bt[hb * half:(hb + 1) * half, :] - bt[s:s + 1, :]
            if hb == s // half:
                keep = (half_row <= s % half) if rev else (half_row >= s % half)
                d = jnp.where(keep, d, -jnp.inf)
            slabs.append(qb[hb * half:(hb + 1) * half, :] * jnp.exp2(d) * kb[s:s + 1, :])
        sums[p] = _dot(jnp.concatenate(slabs, axis=0).astype(BF16), ones)
        yield

    mixed = {p: _dot(scores[p].astype(BF16), values[p]) for p in range(1, n_sub)}
    yield

    for p, tb in enumerate(order):
        vb = rows(v, tb)
        acc = rows(o_inter, tb)
        if p > 0:
            for hh, hm in enumerate(head_mask):
                acc = acc + jnp.where(hm, mixed[p][hh * GLA_SUB:(hh + 1) * GLA_SUB, :], 0.0)
        halves = [acc[hb * half:(hb + 1) * half, :] for hb in range(GLA_SUB // half)]
        for idx, (s, hb) in enumerate(pairs):
            halves[hb] = halves[hb] + sums[p][idx * half:(idx + 1) * half, :] * vb[s:s + 1, :]
        o_ref[pl.ds(r0 + tb * GLA_SUB, GLA_SUB), :] = jnp.concatenate(halves, axis=0)
        yield

    st_ref[...] = st * jnp.exp2(b_end) + upd * bd


def _interleave(gens):
    active = list(gens)
    while active:
        for gen in list(active):
            try:
                next(gen)
            except StopIteration:
                active.remove(gen)


def _gla_kernel(cumf_ref, cumb_ref, ones_ref, bd_ref, causf_ref, causb_ref, glf_ref, glb_ref,
                gf_ref, gb_ref, s0f_ref, s0b_ref, of_ref, ob_ref, sff_ref, sfb_ref,
                stf_ref, stb_ref, pb_scr, dec_scr, ok_ref, *, n_chunks):
    i = pl.program_id(1)
    lane_head = lax.broadcasted_iota(jnp.int32, (1, GLA_W), 1) // HEAD_DIM
    head_mask = [lane_head == hh for hh in range(GLA_HEADS)]
    dirs = ((glf_ref, gf_ref, cumf_ref, stf_ref, of_ref, causf_ref, False),
            (glb_ref, gb_ref, cumb_ref, stb_ref, ob_ref, causb_ref, True))

    @pl.when(i == 0)
    def _():
        stf_ref[...] = s0f_ref[...]
        stb_ref[...] = s0b_ref[...]

    def load(gl_ref, g_ref, r0):
        rows = pl.ds(r0, GLA_CHUNK)
        return (gl_ref[rows, 0:GLA_W], gl_ref[rows, GLA_W:2 * GLA_W],
                gl_ref[rows, 2 * GLA_W:3 * GLA_W], g_ref[rows, :])

    for c in range(n_chunks):
        for d, (gl_ref, g_ref, cum_ref, _, _, _, rev) in enumerate(dirs):
            q, k, v, g = load(gl_ref, g_ref, c * GLA_CHUNK)
            rows, dec, ok = _gla_prepare_fast(q, k, v, g, cum_ref[GLA_CHUNK:2 * GLA_CHUNK, :],
                                              head_mask, rev)
            pb_scr[d, c] = rows
            dec_scr[d, c] = jnp.broadcast_to(dec, (8, GLA_W))
            ok_ref[d, c] = ok

    n_ok = ok_ref[0, 0]
    for d in range(2):
        for c in range(n_chunks):
            if (d, c) != (0, 0):
                n_ok = n_ok + ok_ref[d, c]
    fast = n_ok == 2 * n_chunks

    @pl.when(fast)
    def _():
        _gla_tile_fast(pb_scr, dec_scr, dirs, bd_ref[...], head_mask, n_chunks)

    @pl.when(jnp.logical_not(fast))
    def _():
        def chunk(ci, carry):
            cidx = (ci, n_chunks - 1 - ci)
            gens = []
            for d, (gl_ref, g_ref, cum_ref, st_ref, o_ref, _, rev) in enumerate(dirs):
                r0 = pl.multiple_of(cidx[d] * GLA_CHUNK, GLA_CHUNK)
                q, k, v, g = load(gl_ref, g_ref, r0)
                pf, pb = _gla_prepare(q, k, g, cum_ref[...], rev)
                gens.append(_gla_chunk(q, k, v, pf, pb, st_ref, o_ref, r0, ones_ref[...],
                                       bd_ref[...], head_mask, rev))
            _interleave(gens)
            return carry

        lax.fori_loop(0, n_chunks, chunk, 0)

    @pl.when(i == pl.num_programs(1) - 1)
    def _():
        sff_ref[...] = stf_ref[...]
        sfb_ref[...] = stb_ref[...]


def _gla(gl, gates, s0f, s0b, tile):
    B, S, _ = gl.shape
    nt = S // tile
    n_chunks = tile // GLA_CHUNK
    ones_bd = jnp.asarray(_head_blocks(), BF16)
    bd = jnp.asarray(_head_blocks(), F32)
    state_spec = pl.BlockSpec((None, GLA_W, GLA_W), lambda b, i: (b, 0, 0))
    return pl.pallas_call(
        functools.partial(_gla_kernel, n_chunks=n_chunks),
        grid=(B, nt),
        in_specs=[
            _const_spec((2 * GLA_CHUNK, GLA_CHUNK)),
            _const_spec((2 * GLA_CHUNK, GLA_CHUNK)),
            _const_spec((GLA_W, GLA_W)),
            _const_spec((GLA_W, GLA_W)),
            _const_spec((GLA_HEADS * GLA_CHUNK, GLA_CHUNK)),
            _const_spec((GLA_HEADS * GLA_CHUNK, GLA_CHUNK)),
            pl.BlockSpec((None, tile, 3 * GLA_W), lambda b, i: (b, i, 0)),
            pl.BlockSpec((None, tile, 3 * GLA_W), lambda b, i: (b, nt - 1 - i, 0)),
            pl.BlockSpec((None, tile, GLA_W), lambda b, i: (b, i, 0)),
            pl.BlockSpec((None, tile, GLA_W), lambda b, i: (b, nt - 1 - i, 1)),
            state_spec, state_spec,
        ],
        out_specs=[
            pl.BlockSpec((None, tile, GLA_W), lambda b, i: (b, i, 0)),
            pl.BlockSpec((None, tile, GLA_W), lambda b, i: (b, nt - 1 - i, 0)),
            state_spec, state_spec,
        ],
        out_shape=[jax.ShapeDtypeStruct((B, S, GLA_W), F32),
                   jax.ShapeDtypeStruct((B, S, GLA_W), F32),
                   jax.ShapeDtypeStruct((B, GLA_W, GLA_W), F32),
                   jax.ShapeDtypeStruct((B, GLA_W, GLA_W), F32)],
        scratch_shapes=[pltpu.VMEM((GLA_W, GLA_W), F32), pltpu.VMEM((GLA_W, GLA_W), F32),
                        pltpu.VMEM((2, n_chunks, (GLA_HEADS + 4) * GLA_CHUNK, GLA_W), BF16),
                        pltpu.VMEM((2, n_chunks, 8, GLA_W), F32),
                        pltpu.SMEM((2, n_chunks), jnp.int32)],
        compiler_params=_params(2),
        name="gla",
    )(jnp.asarray(_gla_consts(False), BF16), jnp.asarray(_gla_consts(True), BF16),
      ones_bd, bd, jnp.asarray(_gla_causal(False), F32), jnp.asarray(_gla_causal(True), F32),
      gl, gl, gates, gates, s0f, s0b)


def _post_kernel(x_ref, mod_ref, g2_ref, cu_ref, cup_ref, cun_ref, cb_ref, cw_ref, at_ref,
                 of_ref, ob_ref, gg_ref, ng_ref, ones_ref, wo_ref, wu_ref, wd_ref,
                 *rest, final):
    if final:
        fg_ref, o_ref, acc_ref = rest
    else:
        o_ref, acc_ref = rest
    i = pl.program_id(1)
    n = pl.num_programs(1)
    tm = x_ref.shape[0]

    u = cu_ref[...]
    row = lax.broadcasted_iota(jnp.int32, (tm, 1), 0)
    prev_row = jnp.where(i > 0, cup_ref[7:8, :], 0.0)
    next_row = jnp.where(i < n - 1, cun_ref[0:1, :], 0.0)
    u_prev = jnp.where(row == 0, prev_row, pltpu.roll(u, 1, axis=0))
    u_next = jnp.where(row == tm - 1, next_row, pltpu.roll(u, tm - 1, axis=0))
    conv = cb_ref[...] * (cw_ref[0:1, :] * u_prev + cw_ref[1:2, :] * u + cw_ref[2:3, :] * u_next)

    x1_parts, h2_parts = [], []
    n_groups = POST_ROW_GROUPS if tm % (POST_ROW_GROUPS * 16) == 0 else 1
    rows_per = tm // n_groups
    for gr in range(n_groups):
        rs = slice(gr * rows_per, (gr + 1) * rows_per)
        o = of_ref[rs, :] + ob_ref[rs, :]
        sq = o * o
        sq_hi = sq.astype(BF16)
        sq_lo = (sq - sq_hi.astype(F32)).astype(BF16)
        ms = (_dot(sq_hi, ones_ref[...]) + _dot(sq_lo, ones_ref[...])) * (1.0 / HEAD_DIM)
        gla = o * lax.rsqrt(ms + EPS) * ng_ref[...] * _silu(gg_ref[rs, :])

        mix = (_dot(conv[rs, :].astype(BF16), wo_ref[0:CONV_CH, :])
               + _dot(at_ref[rs, :], wo_ref[CONV_CH:CONV_CH + ATT_W, :])
               + _dot(gla.astype(BF16), wo_ref[CONV_CH + ATT_W:, :]))
        x1_g = x_ref[rs, :] + mod_ref[2:3, :] * mix

        ms2 = jnp.mean(x1_g * x1_g, axis=-1, keepdims=True)
        y2 = x1_g * lax.rsqrt(ms2 + EPS) * g2_ref[...]
        x1_parts.append(x1_g)
        h2_parts.append((y2 * (1.0 + mod_ref[4:5, :]) + mod_ref[3:4, :]).astype(BF16))
    x1 = jnp.concatenate(x1_parts, axis=0)
    h2 = jnp.concatenate(h2_parts, axis=0)

    def up(jc):
        if isinstance(jc, int):
            ca, cb = jc * FF_CHUNK, D_FF + jc * FF_CHUNK
        else:
            ca = pl.multiple_of(jc * FF_CHUNK, FF_CHUNK)
            cb = pl.multiple_of(D_FF + jc * FF_CHUNK, FF_CHUNK)
        a = _dot(h2, wu_ref[:, pl.ds(ca, FF_CHUNK)])
        b = _dot(h2, wu_ref[:, pl.ds(cb, FF_CHUNK)])
        return a, b

    def act(ab):
        return (_silu(ab[0]) * ab[1]).astype(BF16)

    prev = act(up(0))
    ab = up(1)
    acc_ref[...] = _dot(prev, wd_ref[0])
    prev = act(ab)

    def ff(t, prev):
        for u in range(FF_UNROLL):
            jc = 2 + t * FF_UNROLL + u
            ab = up(jc)
            acc_ref[...] += _dot(prev, wd_ref[jc - 1])
            prev = act(ab)
        return prev

    n_trips = (N_FF_CHUNKS - 2) // FF_UNROLL
    prev = ff(0, prev) if n_trips == 1 else lax.fori_loop(0, n_trips, ff, prev)
    x2 = x1 + mod_ref[5:6, :] * (acc_ref[...] + _dot(prev, wd_ref[N_FF_CHUNKS - 1]))
    if final:
        msf = jnp.mean(x2 * x2, axis=-1, keepdims=True)
        x2 = x2 * lax.rsqrt(msf + EPS) * fg_ref[...]
    o_ref[...] = x2


def _post(xs, mods_l, mod_row, g2, cu, cb, conv_w, attn, o_f, o_b, gl, norm_g, w_out, wu, wd,
          layer, final_g, tm):
    B, S, D = xs.shape
    nt = S // tm
    hb = tm // 8
    row = lambda w: pl.BlockSpec((None, tm, w), lambda b, i: (b, i, 0))
    if mod_row is None:
        mod_spec = pl.BlockSpec((None, 8, D), lambda b, i: (b, 0, 0))
    else:
        mod_spec = pl.BlockSpec((None, 8, D), lambda b, i: (mod_row, 0, 0))
    final = final_g is not None
    in_specs = [
        row(D), mod_spec, _const_spec((1, D)),
        row(CONV_CH),
        pl.BlockSpec((None, 8, CONV_CH), lambda b, i: (b, jnp.maximum(i * hb - 1, 0), 0)),
        pl.BlockSpec((None, 8, CONV_CH), lambda b, i: (b, jnp.minimum((i + 1) * hb, S // 8 - 1), 0)),
        row(CONV_CH), _const_spec((8, CONV_CH)), row(ATT_W), row(GLA_W), row(GLA_W),
        pl.BlockSpec((None, tm, GLA_W), lambda b, i: (b, i, 3)),
        _const_spec((1, GLA_W)), _const_spec((GLA_W, GLA_W)), _layer_spec((D, D), layer),
        _layer_spec((D, 2 * D_FF), layer), _layer_spec((N_FF_CHUNKS, FF_CHUNK, D), layer),
    ]
    args = [xs, mods_l, g2, cu, cu, cu, cb, conv_w, attn, o_f, o_b, gl, norm_g,
            jnp.asarray(_head_blocks(), BF16), w_out, wu, wd]
    if final:
        in_specs.append(_const_spec((1, D)))
        args.append(final_g)
    return pl.pallas_call(
        functools.partial(_post_kernel, final=final),
        grid=(B, nt),
        in_specs=in_specs,
        out_specs=row(D),
        out_shape=jax.ShapeDtypeStruct((B, S, D), F32),
        scratch_shapes=[pltpu.VMEM((tm, D), F32)],
        compiler_params=_params(2),
        name="post_final" if final else "post",
    )(*args)


def _rope_tables(seq):
    t = jnp.arange(seq)
    pos = jnp.stack([(t // GRID_W).astype(F32), (t % GRID_W).astype(F32)], axis=1)
    n_freq = HEAD_DIM // 4
    inv_freq = ROPE_BASE ** (-jnp.arange(n_freq, dtype=F32) / n_freq)
    ang = pos[:, :, None] * inv_freq
    cos, sin = jnp.cos(ang), jnp.sin(ang)
    zero = jnp.zeros_like(sin)
    cos_t = jnp.stack([cos, cos], axis=2).reshape(seq, HEAD_DIM)
    sa_t = jnp.stack([-sin, zero], axis=2).reshape(seq, HEAD_DIM)
    sb_t = jnp.stack([zero, sin], axis=2).reshape(seq, HEAD_DIM)
    rep = LANES // HEAD_DIM
    return tuple(jnp.tile(a, (1, rep)) for a in (cos_t, sa_t, sb_t))


def _tile_rows(S, want):
    t = min(want, S)
    while S % t:
        t //= 2
    return t


def kernel(x, c, ctx, c_ctx, w_mod, b_mod, norm1_g, norm2_g, w_in, conv_w, attn_sink,
           gla_gate_w, gla_gate_b, gla_norm_g, w_out, w_up, w_down, final_norm_g):
    B, S, D = x.shape
    L = w_mod.shape[0]
    Lc = ctx.shape[1]
    assert D == D_MODEL and S % WINDOW == 0 and Lc % GLA_CHUNK == 0 and B + 1 <= 8

    cvec = jnp.zeros((8, D), F32).at[:B].set(c).at[B].set(c_ctx)
    mods = _modulation(cvec, w_mod, b_mod).reshape(L, 8, N_MOD, D)
    mods = jnp.pad(mods, ((0, 0), (0, 0), (0, 8 - N_MOD), (0, 0)))

    order = np.array(Q_HEAD_ORDER)
    wq = w_in[:, :, 768:1280].reshape(L, D, ATT_HEADS, HEAD_DIM)[:, :, order].reshape(L, D, ATT_W)
    w_in_p = jnp.concatenate(
        [w_in[:, :, :768], wq, w_in[:, :, 1280:1536], w_in[:, :, 2560:2592],
         jnp.zeros((L, D, LANES - 2 * GLA_RANK), F32), w_in[:, :, 1536:2560]], axis=2).astype(BF16)
    wo_att = w_out[:, 256:768].reshape(L, ATT_HEADS, HEAD_DIM, D)[:, order].reshape(L, ATT_W, D)
    w_out_p = jnp.concatenate([w_out[:, :256], wo_att, w_out[:, 768:]], axis=1).astype(BF16)
    wg = jnp.zeros((L, LANES, 2 * GLA_W), F32)
    wg = wg.at[:, :GLA_RANK, :GLA_W].set(gla_gate_w[:, 0])
    wg = wg.at[:, GLA_RANK:2 * GLA_RANK, GLA_W:].set(gla_gate_w[:, 1]).astype(BF16)
    bg = gla_gate_b.reshape(L, 1, 2 * GLA_W)
    cw = jnp.pad(conv_w, ((0, 0), (0, 8 - CONV_K), (0, 0)))
    ng = jnp.tile(gla_norm_g, (1, GLA_HEADS)).reshape(L, 1, GLA_W)
    wu = w_up.astype(BF16)
    wd = w_down.reshape(L, N_FF_CHUNKS, FF_CHUNK, D).astype(BF16)
    g1 = norm1_g.reshape(L, 1, D)
    g2 = norm2_g.reshape(L, 1, D)

    rope_tabs = _rope_tables(S)
    tm = _tile_rows(S, 512)
    proj_tm = _tile_rows(S, 1024)
    gla_tile = _tile_rows(S, 1024)
    zero_state = jnp.zeros((B, GLA_W, GLA_W), F32)

    for l in range(L):
        last = l == L - 1
        cu, cb, q, k, v, gl, gates = _proj(x, mods[l], None, g1[l], w_in_p, l, wg[l], bg[l],
                                           rope_tabs, proj_tm)
        ccu, ccb, cq, ck, cv, cgl, cgates = _proj(ctx, mods[l], B, g1[l], w_in_p, l, wg[l], bg[l],
                                                  None, Lc)
        attn = _attention(q, k, v, ck, cv, attn_sink[l], local=True)
        oc_f, oc_b, sc_f, sc_b = _gla(cgl, cgates, zero_state, zero_state, Lc)
        o_f, o_b, _, _ = _gla(gl, gates, sc_f, sc_b, gla_tile)
        fin = final_norm_g.reshape(1, D) if last else None
        x = _post(x, mods[l], None, g2[l], cu, cb, cw[l], attn, o_f, o_b, gl, ng[l],
                  w_out_p, wu, wd, l, fin, tm)
        if not last:
            attn_c = _attention(cq, None, None, ck, cv, attn_sink[l], local=False)
            ctx = _post(ctx, mods[l], B, g2[l], ccu, ccb, cw[l], attn_c, oc_f, oc_b, cgl, ng[l],
                        w_out_p, wu, wd, l, None, Lc)
    return x
```

```python
import functools

import numpy as np
import jax
import jax.numpy as jnp
from jax import lax
from jax.experimental import pallas as pl
from jax.experimental.pallas import tpu as pltpu

F32 = jnp.float32
BF16 = jnp.bfloat16

D_MODEL = 1024
HEAD_DIM = 64
CONV_CH = 256
CONV_K = 3
ATT_HEADS = 8
ATT_KV_HEADS = 2
ATT_W = ATT_HEADS * HEAD_DIM
KV_W = ATT_KV_HEADS * HEAD_DIM
WINDOW = 128
GLA_HEADS = 4
GLA_W = GLA_HEADS * HEAD_DIM
GLA_RANK = 16
GLA_NORMALIZER = 16.0
D_FF = 2816
N_MOD = 6
GRID_W = 64
ROPE_BASE = 10000.0
EPS = 1e-6
LOG2_E = 1.4426950408889634

LANES = 128
C_CONV = 0
C_QK = 768
C_V = 1408
C_LR = 1536
C_GLA = 1664
N_IN_PAD = C_GLA + 4 * GLA_W
Q_HEAD_ORDER = (0, 4, 1, 5, 2, 6, 3, 7)

ATT_ROW_GROUP = 16
GLA_CHUNK = 64
GLA_SUB = 16
GLA_FAST_CHUNK = 256
GLA_FAST_RANGE = 80.0
FF_CHUNK = 256
N_FF_CHUNKS = D_FF // FF_CHUNK
PROJ_GROUP_ROWS = 512
POST_ROW_GROUPS = 2
FF_UNROLL = 9
assert (N_FF_CHUNKS - 2) % FF_UNROLL == 0

VMEM_LIMIT = 56 * 1024 * 1024


def _const_spec(shape):
    nd = len(shape)
    return pl.BlockSpec(shape, lambda *_: (0,) * nd, pipeline_mode=pl.Buffered(1))


def _layer_spec(shape, layer):
    nd = len(shape)
    return pl.BlockSpec((None,) + tuple(shape), lambda *_: (layer,) + (0,) * nd,
                        pipeline_mode=pl.Buffered(1))


def _params(n_grid):
    return pltpu.CompilerParams(
        dimension_semantics=("arbitrary",) * n_grid, vmem_limit_bytes=VMEM_LIMIT)


def _silu(a):
    return a * (1.0 / (1.0 + jnp.exp(-a)))


def _dot(a, b):
    return jnp.dot(a, b, preferred_element_type=F32)


def _dot_nt(a, b):
    return lax.dot_general(a, b, (((1,), (1,)), ((), ())), preferred_element_type=F32)


def _dot_tn(a, b):
    return lax.dot_general(a, b, (((0,), (0,)), ((), ())), preferred_element_type=F32)


def _mod_kernel(c_ref, w_ref, b_ref, o_ref):
    a = _silu(c_ref[...]).astype(BF16)
    o_ref[...] = _dot(a, w_ref[...].astype(BF16)) + b_ref[...]


def _modulation(cvec, w_mod, b_mod):
    L, D, W = w_mod.shape
    tn = 2048
    return pl.pallas_call(
        _mod_kernel,
        grid=(L, W // tn),
        in_specs=[
            pl.BlockSpec((8, D), lambda l, j: (0, 0)),
            pl.BlockSpec((None, D, tn), lambda l, j: (l, 0, j)),
            pl.BlockSpec((None, 1, tn), lambda l, j: (l, 0, j)),
        ],
        out_specs=pl.BlockSpec((None, 8, tn), lambda l, j: (l, 0, j)),
        out_shape=jax.ShapeDtypeStruct((L, 8, W), F32),
        compiler_params=_params(2),
        name="modulation",
    )(cvec, w_mod, b_mod.reshape(L, 1, W))


def _proj_kernel(x_ref, mod_ref, g1_ref, w_ref, wg_ref, bg_ref, *rest, rope):
    if rope:
        cos_ref, sa_ref, sb_ref = rest[:3]
        rest = rest[3:]
    cu_ref, cb_ref, q_ref, k_ref, v_ref, gl_ref, gate_ref = rest
    tm = x_ref.shape[0]
    rows_per = PROJ_GROUP_ROWS if tm % PROJ_GROUP_ROWS == 0 else tm
    scale = HEAD_DIM ** -0.5

    for gr in range(tm // rows_per):
        rs = slice(gr * rows_per, (gr + 1) * rows_per)
        x = x_ref[rs, :]
        ms = jnp.mean(x * x, axis=-1, keepdims=True)
        y = x * lax.rsqrt(ms + EPS) * g1_ref[...]
        h = (y * (1.0 + mod_ref[1:2, :]) + mod_ref[0:1, :]).astype(BF16)

        def proj(lo, hi, h=h):
            return _dot(h, w_ref[:, lo:hi])

        pc = proj(C_CONV, C_QK)
        cu_ref[rs, :] = pc[:, 2 * CONV_CH:3 * CONV_CH] * pc[:, 0:CONV_CH]
        cb_ref[rs, :] = pc[:, CONV_CH:2 * CONV_CH]

        pqk = proj(C_QK, C_GLA)
        for m in range((ATT_W + KV_W) // LANES):
            col = pqk[:, m * LANES:(m + 1) * LANES]
            if rope:
                col = (col * cos_ref[rs, :]
                       + pltpu.roll(col, LANES - HEAD_DIM // 4, axis=1) * sa_ref[rs, :]
                       + pltpu.roll(col, HEAD_DIM // 4, axis=1) * sb_ref[rs, :])
            if m < ATT_W // LANES:
                q_ref[rs, m * LANES:(m + 1) * LANES] = (col * scale).astype(BF16)
            else:
                k_ref[rs, :] = col.astype(BF16)

        v_ref[rs, :] = pqk[:, C_V - C_QK:C_LR - C_QK].astype(BF16)

        pg = proj(C_GLA, N_IN_PAD)
        gl_ref[rs, 0:GLA_W] = pg[:, 0:GLA_W] * scale
        gl_ref[rs, GLA_W:4 * GLA_W] = pg[:, GLA_W:4 * GLA_W]

        lr = pqk[:, C_LR - C_QK:C_GLA - C_QK].astype(BF16)
        z = _dot(lr, wg_ref[...]) + bg_ref[...]
        log_sig = jnp.minimum(z, 0.0) - jnp.log1p(jnp.exp(-jnp.abs(z)))
        gate_ref[rs, :] = log_sig * (1.0 / GLA_NORMALIZER)


def _proj(xs, mods_l, mod_row, g1, w_in, layer, wg, bg, rope_tabs, tm):
    B, S, D = xs.shape
    rope = rope_tabs is not None
    nt = S // tm
    row = lambda w: pl.BlockSpec((None, tm, w), lambda b, i: (b, i, 0))
    if mod_row is None:
        mod_spec = pl.BlockSpec((None, 8, D), lambda b, i: (b, 0, 0))
    else:
        mod_spec = pl.BlockSpec((None, 8, D), lambda b, i: (mod_row, 0, 0))
    in_specs = [row(D), mod_spec, _const_spec((1, D)), _layer_spec((D, N_IN_PAD), layer),
                _const_spec((LANES, 2 * GLA_W)), _const_spec((1, 2 * GLA_W))]
    args = [xs, mods_l, g1, w_in, wg, bg]
    if rope:
        in_specs += [pl.BlockSpec((tm, LANES), lambda b, i: (i, 0))] * 3
        args += list(rope_tabs)
    widths = (CONV_CH, CONV_CH, ATT_W, KV_W, KV_W, 4 * GLA_W, 2 * GLA_W)
    dtypes = (F32, F32, BF16, BF16, BF16, F32, F32)
    return pl.pallas_call(
        functools.partial(_proj_kernel, rope=rope),
        grid=(B, nt),
        in_specs=in_specs,
        out_specs=[row(w) for w in widths],
        out_shape=[jax.ShapeDtypeStruct((B, S, w), dt) for w, dt in zip(widths, dtypes)],
        compiler_params=_params(2),
        name="proj_rope" if rope else "proj_ctx",
    )(*args)


def _attn_kernel(sink_ref, q_ref, kc_ref, vc_ref, *rest, local, n_sub):
    tb = q_ref.shape[0] // n_sub
    n_pairs = ATT_W // LANES
    n_ctx = kc_ref.shape[0]
    rg = ATT_ROW_GROUP
    if local:
        kp_ref, km_ref, kn_ref, vp_ref, vm_ref, vn_ref, o_ref, s_scr, p_scr, e_scr, bias_scr = rest
        i = pl.program_id(1)
        n = pl.num_programs(1)
        k_loc = jnp.concatenate([kp_ref[...], km_ref[...], kn_ref[...]], axis=0)
        v_loc = jnp.concatenate([vp_ref[...], vm_ref[...], vn_ref[...]], axis=0)
        r = lax.broadcasted_iota(jnp.int32, (tb, tb), 0)
        j = lax.broadcasted_iota(jnp.int32, (tb, tb), 1)
        bias_prev = jnp.where(j >= r, 0.0, -jnp.inf)
        bias_next = jnp.where(j <= r, 0.0, -jnp.inf)
        bias_scr[0] = bias_prev + jnp.where(i > 0, 0.0, -jnp.inf)
        bias_scr[1] = bias_prev
        bias_scr[2] = bias_next
        bias_scr[3] = bias_next + jnp.where(i < n - 1, 0.0, -jnp.inf)
        nk = n_ctx + 3 * tb
    else:
        o_ref, s_scr, p_scr, e_scr = rest
        nk = n_ctx
    lane = lax.broadcasted_iota(jnp.int32, (1, LANES), 1)
    lo = lane < HEAD_DIM
    hi = jnp.logical_not(lo)
    lo_k = lax.broadcasted_iota(jnp.int32, (nk, LANES), 1) < HEAD_DIM
    group_ones = [jnp.where(lo_k, 1.0, 0.0).astype(BF16), jnp.where(lo_k, 0.0, 1.0).astype(BF16)]

    def scores(jb):
        q = q_ref[jb * tb:(jb + 1) * tb, :]
        lhs = jnp.concatenate([q[:, m * LANES:(m + 1) * LANES] for m in range(n_pairs)], axis=0)
        kk = (jnp.concatenate([kc_ref[...], k_loc[jb * tb:(jb + 3) * tb]], axis=0) if local
              else kc_ref[...])
        k2 = jnp.concatenate([kk * group_ones[0], kk * group_ones[1]], axis=0)
        s_scr[jb] = _dot_nt(lhs, k2)

    def softmax(jb):
        for m in range(n_pairs):
            sinks = [sink_ref[m + (ATT_HEADS // ATT_KV_HEADS) * g] for g in range(ATT_KV_HEADS)]
            for gi in range(tb // rg):
                rq = gi * rg
                r0 = m * tb + rq
                s = s_scr[jb, r0:r0 + rg, :]
                probs, sink_terms = [], []
                for g in range(ATT_KV_HEADS):
                    sg = s[:, g * nk:(g + 1) * nk]
                    if local:
                        bp = bias_scr[0 if jb == 0 else 1, rq:rq + rg, :]
                        bn = bias_scr[3 if jb == n_sub - 1 else 2, rq:rq + rg, :]
                        sg = jnp.concatenate(
                            [sg[:, :n_ctx], sg[:, n_ctx:n_ctx + tb] + bp,
                             sg[:, n_ctx + tb:n_ctx + 2 * tb], sg[:, n_ctx + 2 * tb:] + bn], axis=1)
                    mx = jnp.maximum(jnp.max(sg, axis=-1, keepdims=True), sinks[g])
                    probs.append(jnp.exp(sg - mx).astype(BF16))
                    sink_terms.append(jnp.exp(sinks[g] - mx))
                p_scr[jb, r0:r0 + rg, :] = jnp.concatenate(probs, axis=1)
                e_scr[jb, r0:r0 + rg, :] = jnp.where(lo, sink_terms[0], sink_terms[1])

    def values(jb):
        vv = (jnp.concatenate([vc_ref[...], v_loc[jb * tb:(jb + 3) * tb]], axis=0) if local
              else vc_ref[...])
        v2 = jnp.concatenate(
            [jnp.concatenate([vv * group_ones[0], group_ones[0]], axis=1),
             jnp.concatenate([vv * group_ones[1], group_ones[1]], axis=1)], axis=0)
        o_aug = _dot(p_scr[jb], v2)
        o = o_aug[:, :LANES] * (1.0 / (o_aug[:, LANES:] + e_scr[jb]))
        for m in range(n_pairs):
            o_ref[jb * tb:(jb + 1) * tb, m * LANES:(m + 1) * LANES] = (
                o[m * tb:(m + 1) * tb].astype(BF16))

    scores(0)
    for jb in range(n_sub):
        if jb + 1 < n_sub:
            scores(jb + 1)
        softmax(jb)
        values(jb)


def _attention(q, k, v, k_ctx, v_ctx, sink, local, n_sub=4):
    B, S, _ = q.shape
    Lc = k_ctx.shape[1]
    if not local:
        n_sub = 1
    tq = WINDOW * n_sub if local else S
    nq = S // tq
    in_specs = [
        pl.BlockSpec(memory_space=pltpu.SMEM),
        pl.BlockSpec((None, tq, ATT_W), lambda b, i: (b, i, 0)),
        pl.BlockSpec((None, Lc, KV_W), lambda b, i: (b, 0, 0)),
        pl.BlockSpec((None, Lc, KV_W), lambda b, i: (b, 0, 0)),
    ]
    args = [sink, q, k_ctx, v_ctx]
    if local:
        nw = S // WINDOW
        nb = [pl.BlockSpec((None, WINDOW, KV_W), lambda b, i: (b, jnp.maximum(i * n_sub - 1, 0), 0)),
              pl.BlockSpec((None, tq, KV_W), lambda b, i: (b, i, 0)),
              pl.BlockSpec((None, WINDOW, KV_W),
                           lambda b, i: (b, jnp.minimum((i + 1) * n_sub, nw - 1), 0))]
        in_specs += nb + nb
        args += [k, k, k, v, v, v]
    tb = tq // n_sub
    nk = Lc + (3 * tb if local else 0)
    n_rows = (ATT_W // LANES) * tb
    scratch = [pltpu.VMEM((n_sub, n_rows, ATT_KV_HEADS * nk), F32),
               pltpu.VMEM((n_sub, n_rows, ATT_KV_HEADS * nk), BF16),
               pltpu.VMEM((n_sub, n_rows, LANES), F32)]
    if local:
        scratch.append(pltpu.VMEM((4, tb, tb), F32))
    return pl.pallas_call(
        functools.partial(_attn_kernel, local=local, n_sub=n_sub),
        grid=(B, nq),
        in_specs=in_specs,
        out_specs=pl.BlockSpec((None, tq, ATT_W), lambda b, i: (b, i, 0)),
        out_shape=jax.ShapeDtypeStruct((B, S, ATT_W), BF16),
        scratch_shapes=scratch,
        compiler_params=_params(2),
        name="attn_window" if local else "attn_ctx",
    )(*args)


def _gla_consts(rev):
    t = np.arange(GLA_CHUNK)[:, None]
    s = np.arange(GLA_CHUNK)[None, :]
    same = (t // GLA_SUB) == (s // GLA_SUB)
    before = (s >= t) if rev else (s <= t)
    return np.concatenate([same & before, before], axis=0).astype(np.float32)


def _head_blocks():
    h = np.arange(GLA_W) // HEAD_DIM
    return (h[:, None] == h[None, :]).astype(np.float32)


def _gla_last_row(blk, rev):
    return blk * GLA_SUB if rev else blk * GLA_SUB + GLA_SUB - 1


def _gla_prepare(q, k, g, cum, rev):
    n_sub = GLA_CHUNK // GLA_SUB
    g2 = g * LOG2_E
    g_hi = g2.astype(BF16)
    rem = g2 - g_hi.astype(F32)
    g_mid = rem.astype(BF16)
    g_lo = (rem - g_mid.astype(F32)).astype(BF16)
    cs = _dot(cum, g_hi) + _dot(cum, g_mid) + _dot(cum, g_lo)
    beta, bch = cs[0:GLA_CHUNK], cs[GLA_CHUNK:2 * GLA_CHUNK]
    gam = jnp.concatenate(
        [beta[_gla_last_row(b, rev):_gla_last_row(b, rev) + 1, :] - beta[b * GLA_SUB:(b + 1) * GLA_SUB, :]
         for b in range(n_sub)], axis=0)
    end_row = _gla_last_row(0 if rev else n_sub - 1, rev)
    dch = bch[end_row:end_row + 1, :] - bch
    f32_rows = jnp.concatenate([beta, bch, q * jnp.exp2(beta), k * jnp.exp2(gam)], axis=0)
    b16_rows = jnp.concatenate([(q * jnp.exp2(bch)).astype(BF16), (k * jnp.exp2(dch)).astype(BF16)],
                               axis=0)
    return f32_rows, b16_rows


def _gla_causal(rev):
    t = (np.arange(GLA_HEADS * GLA_FAST_CHUNK) % GLA_FAST_CHUNK)[:, None]
    s = np.arange(GLA_FAST_CHUNK)[None, :]
    return ((s >= t) if rev else (s <= t)).astype(np.float32)


def _gla_before(rev):
    t = np.arange(GLA_FAST_CHUNK)[:, None]
    s = np.arange(GLA_FAST_CHUNK)[None, :]
    return ((s >= t) if rev else (s <= t)).astype(np.float32)


def _gla_prepare_fast(q, k, v, g, cum_chunk, head_mask, rev):
    g2 = g * LOG2_E
    g_hi = g2.astype(BF16)
    g_lo = (g2 - g_hi.astype(F32)).astype(BF16)
    bch = _dot(cum_chunk, g_hi) + _dot(cum_chunk, g_lo)
    end_row = 0 if rev else GLA_FAST_CHUNK - 1
    mid_row = GLA_FAST_CHUNK // 2 if rev else GLA_FAST_CHUNK // 2 - 1
    b_end = bch[end_row:end_row + 1, :]
    b_mid = bch[mid_row:mid_row + 1, :]
    ok = (jnp.max(-b_end) <= GLA_FAST_RANGE).astype(jnp.int32)
    q_mid = (q * jnp.exp2(bch - b_mid)).astype(BF16)
    k_mid = (k * jnp.exp2(b_mid - bch)).astype(BF16)
    zero = jnp.zeros_like(q_mid)
    rows = jnp.concatenate(
        [jnp.where(hm, q_mid, zero) for hm in head_mask]
        + [k_mid, (q * jnp.exp2(bch)).astype(BF16), (k * jnp.exp2(b_end - bch)).astype(BF16),
           v.astype(BF16)], axis=0)
    return rows, jnp.exp2(b_end), ok


def _gla_tile_fast(pb_scr, dec_scr, dirs, bd, head_mask, n_chunks):
    cf = GLA_FAST_CHUNK
    n_q = GLA_HEADS * cf
    work = []
    for d, (_, _, _, st_ref, o_ref, caus_ref, rev) in enumerate(dirs):
        for c in (range(n_chunks - 1, -1, -1) if rev else range(n_chunks)):
            pb = pb_scr.at[d, c]
            vb = pb[n_q + 3 * cf:n_q + 4 * cf]
            upd = _dot_tn(vb, pb[n_q + 2 * cf:n_q + 3 * cf])
            a = _dot_nt(pb[0:n_q], pb[n_q:n_q + cf])
            work.append(dict(d=d, c=c, pb=pb, vb=vb, upd=upd, a=a, o_ref=o_ref, caus=caus_ref))
    for d, (_, _, _, st_ref, _, _, _) in enumerate(dirs):
        st = st_ref[...]
        for w in (w for w in work if w["d"] == d):
            w["o_inter"] = _dot_nt(w["pb"][n_q + cf:n_q + 2 * cf], st.astype(BF16))
            st = st * dec_scr[d, w["c"], 0:1, :] + w["upd"] * bd
        st_ref[...] = st
    for w in work:
        w["r"] = _dot((w["a"] * w["caus"][...]).astype(BF16), w["vb"])
    for w in work:
        o = w["o_inter"]
        for hh, hm in enumerate(head_mask):
            o = o + jnp.where(hm, w["r"][hh * cf:(hh + 1) * cf, :], 0.0)
        w["o_ref"][w["c"] * cf:(w["c"] + 1) * cf, :] = o


def _gla_chunk(q, k, v, pf_ref, pb_ref, st_ref, o_ref, r0, ones, bd, head_mask, rev):
    n_sub = GLA_CHUNK // GLA_SUB
    order = list(range(n_sub))[::-1] if rev else list(range(n_sub))

    def rows(a, blk):
        return a[blk * GLA_SUB:(blk + 1) * GLA_SUB, :]

    half = GLA_SUB // 2
    beta = pf_ref[0:GLA_CHUNK]
    q_loc = pf_ref[2 * GLA_CHUNK:3 * GLA_CHUNK]
    k_loc = pf_ref[3 * GLA_CHUNK:4 * GLA_CHUNK]
    g_tot = [beta[_gla_last_row(b, rev):_gla_last_row(b, rev) + 1, :] for b in range(n_sub)]
    end_row = GLA_CHUNK + _gla_last_row(order[-1], rev)
    b_end = pf_ref[end_row:end_row + 1, :]
    half_row = lax.broadcasted_iota(jnp.int32, (half, 1), 0)

    st = st_ref[...]
    o_inter = _dot_nt(pb_ref[0:GLA_CHUNK], st.astype(BF16))
    upd = _dot_tn(v.astype(BF16), pb_ref[GLA_CHUNK:2 * GLA_CHUNK])
    scores, values = {}, {}
    for p in range(1, n_sub):
        ks, vs = [], []
        for pp in range(p):
            sb = order[pp]
            kk = rows(k_loc, sb)
            mids = [order[x] for x in range(pp + 1, p)]
            if mids:
                tot = g_tot[mids[0]]
                for mb in mids[1:]:
                    tot = tot + g_tot[mb]
                kk = kk * jnp.exp2(tot)
            ks.append(kk)
            vs.append(rows(v, sb))
        qt = rows(q_loc, order[p])
        q_heads = jnp.concatenate([jnp.where(hm, qt, 0.0) for hm in head_mask], axis=0)
        scores[p] = _dot_nt(q_heads.astype(BF16), jnp.concatenate(ks, axis=0).astype(BF16))
        values[p] = jnp.concatenate(vs, axis=0).astype(BF16)
    yield

    pairs = []
    for s in range(GLA_SUB):
        hs = s // half
        for hb in (range(0, hs + 1) if rev else range(hs, GLA_SUB // half)):
            pairs.append((s, hb))
    sums = {}
    for p, tb in enumerate(order):
        bt, qb, kb = rows(beta, tb), rows(q, tb), rows(k, tb)
        slabs = []
        for s, hb in pairs:
            d = bt[hb * half:(hb + 1) * half, :] - bt[s:s + 1, :]
            if hb == s // half:
                keep = (half_row <= s % half) if rev else (half_row >= s % half)
                d = jnp.where(keep, d, -jnp.inf)
            slabs.append(qb[hb * half:(hb + 1) * half, :] * jnp.exp2(d) * kb[s:s + 1, :])
        sums[p] = _dot(jnp.concatenate(slabs, axis=0).astype(BF16), ones)
        yield

    mixed = {p: _dot(scores[p].astype(BF16), values[p]) for p in range(1, n_sub)}
    yield

    for p, tb in enumerate(order):
        vb = rows(v, tb)
        acc = rows(o_inter, tb)
        if p > 0:
            for hh, hm in enumerate(head_mask):
                acc = acc + jnp.where(hm, mixed[p][hh * GLA_SUB:(hh + 1) * GLA_SUB, :], 0.0)
        halves = [acc[hb * half:(hb + 1) * half, :] for hb in range(GLA_SUB // half)]
        for idx, (s, hb) in enumerate(pairs):
            halves[hb] = halves[hb] + sums[p][idx * half:(idx + 1) * half, :] * vb[s:s + 1, :]
        o_ref[pl.ds(r0 + tb * GLA_SUB, GLA_SUB), :] = jnp.concatenate(halves, axis=0)
        yield

    st_ref[...] = st * jnp.exp2(b_end) + upd * bd


def _interleave(gens):
    active = list(gens)
    while active:
        for gen in list(active):
            try:
                next(gen)
            except StopIteration:
                active.remove(gen)


def _gla_kernel(cumf_ref, cumb_ref, ones_ref, bd_ref, causf_ref, causb_ref, beff_ref, befb_ref,
                glf_ref, glb_ref, gf_ref, gb_ref, s0f_ref, s0b_ref, of_ref, ob_ref, sff_ref,
                sfb_ref, stf_ref, stb_ref, pb_scr, dec_scr, ok_ref, *, n_chunks, n_fast):
    i = pl.program_id(1)
    lane_head = lax.broadcasted_iota(jnp.int32, (1, GLA_W), 1) // HEAD_DIM
    head_mask = [lane_head == hh for hh in range(GLA_HEADS)]
    dirs = ((glf_ref, gf_ref, cumf_ref, stf_ref, of_ref, causf_ref, False),
            (glb_ref, gb_ref, cumb_ref, stb_ref, ob_ref, causb_ref, True))

    @pl.when(i == 0)
    def _():
        stf_ref[...] = s0f_ref[...]
        stb_ref[...] = s0b_ref[...]

    def load(gl_ref, g_ref, r0, size=GLA_CHUNK):
        rows = pl.ds(r0, size)
        return (gl_ref[rows, 0:GLA_W], gl_ref[rows, GLA_W:2 * GLA_W],
                gl_ref[rows, 2 * GLA_W:3 * GLA_W], g_ref[rows, :])

    for c in range(n_fast):
        for d, (bef_ref, (gl_ref, g_ref, _, _, _, _, rev)) in enumerate(
                zip((beff_ref, befb_ref), dirs)):
            q, k, v, g = load(gl_ref, g_ref, c * GLA_FAST_CHUNK, GLA_FAST_CHUNK)
            rows, dec, ok = _gla_prepare_fast(q, k, v, g, bef_ref[...], head_mask, rev)
            pb_scr[d, c] = rows
            dec_scr[d, c] = jnp.broadcast_to(dec, (8, GLA_W))
            ok_ref[d, c] = ok

    n_ok = ok_ref[0, 0]
    for d in range(2):
        for c in range(n_fast):
            if (d, c) != (0, 0):
                n_ok = n_ok + ok_ref[d, c]
    fast = n_ok == 2 * n_fast

    @pl.when(fast)
    def _():
        _gla_tile_fast(pb_scr, dec_scr, dirs, bd_ref[...], head_mask, n_fast)

    @pl.when(jnp.logical_not(fast))
    def _():
        def chunk(ci, carry):
            cidx = (ci, n_chunks - 1 - ci)
            gens = []
            for d, (gl_ref, g_ref, cum_ref, st_ref, o_ref, _, rev) in enumerate(dirs):
                r0 = pl.multiple_of(cidx[d] * GLA_CHUNK, GLA_CHUNK)
                q, k, v, g = load(gl_ref, g_ref, r0)
                pf, pb = _gla_prepare(q, k, g, cum_ref[...], rev)
                gens.append(_gla_chunk(q, k, v, pf, pb, st_ref, o_ref, r0, ones_ref[...],
                                       bd_ref[...], head_mask, rev))
            _interleave(gens)
            return carry

        lax.fori_loop(0, n_chunks, chunk, 0)

    @pl.when(i == pl.num_programs(1) - 1)
    def _():
        sff_ref[...] = stf_ref[...]
        sfb_ref[...] = stb_ref[...]


def _gla(gl, gates, s0f, s0b, tile):
    B, S, _ = gl.shape
    nt = S // tile
    assert tile % GLA_FAST_CHUNK == 0
    n_chunks = tile // GLA_CHUNK
    n_fast = tile // GLA_FAST_CHUNK
    ones_bd = jnp.asarray(_head_blocks(), BF16)
    bd = jnp.asarray(_head_blocks(), F32)
    state_spec = pl.BlockSpec((None, GLA_W, GLA_W), lambda b, i: (b, 0, 0))
    return pl.pallas_call(
        functools.partial(_gla_kernel, n_chunks=n_chunks, n_fast=n_fast),
        grid=(B, nt),
        in_specs=[
            _const_spec((2 * GLA_CHUNK, GLA_CHUNK)),
            _const_spec((2 * GLA_CHUNK, GLA_CHUNK)),
            _const_spec((GLA_W, GLA_W)),
            _const_spec((GLA_W, GLA_W)),
            _const_spec((GLA_HEADS * GLA_FAST_CHUNK, GLA_FAST_CHUNK)),
            _const_spec((GLA_HEADS * GLA_FAST_CHUNK, GLA_FAST_CHUNK)),
            _const_spec((GLA_FAST_CHUNK, GLA_FAST_CHUNK)),
            _const_spec((GLA_FAST_CHUNK, GLA_FAST_CHUNK)),
            pl.BlockSpec((None, tile, 3 * GLA_W), lambda b, i: (b, i, 0)),
            pl.BlockSpec((None, tile, 3 * GLA_W), lambda b, i: (b, nt - 1 - i, 0)),
            pl.BlockSpec((None, tile, GLA_W), lambda b, i: (b, i, 0)),
            pl.BlockSpec((None, tile, GLA_W), lambda b, i: (b, nt - 1 - i, 1)),
            state_spec, state_spec,
        ],
        out_specs=[
            pl.BlockSpec((None, tile, GLA_W), lambda b, i: (b, i, 0)),
            pl.BlockSpec((None, tile, GLA_W), lambda b, i: (b, nt - 1 - i, 0)),
            state_spec, state_spec,
        ],
        out_shape=[jax.ShapeDtypeStruct((B, S, GLA_W), F32),
                   jax.ShapeDtypeStruct((B, S, GLA_W), F32),
                   jax.ShapeDtypeStruct((B, GLA_W, GLA_W), F32),
                   jax.ShapeDtypeStruct((B, GLA_W, GLA_W), F32)],
        scratch_shapes=[pltpu.VMEM((GLA_W, GLA_W), F32), pltpu.VMEM((GLA_W, GLA_W), F32),
                        pltpu.VMEM((2, n_fast, (GLA_HEADS + 4) * GLA_FAST_CHUNK, GLA_W), BF16),
                        pltpu.VMEM((2, n_fast, 8, GLA_W), F32),
                        pltpu.SMEM((2, n_fast), jnp.int32)],
        compiler_params=_params(2),
        name="gla",
    )(jnp.asarray(_gla_consts(False), BF16), jnp.asarray(_gla_consts(True), BF16),
      ones_bd, bd, jnp.asarray(_gla_causal(False), F32), jnp.asarray(_gla_causal(True), F32),
      jnp.asarray(_gla_before(False), BF16), jnp.asarray(_gla_before(True), BF16),
      gl, gl, gates, gates, s0f, s0b)


def _post_kernel(x_ref, mod_ref, g2_ref, cu_ref, cup_ref, cun_ref, cb_ref, cw_ref, at_ref,
                 of_ref, ob_ref, gg_ref, ng_ref, ones_ref, wo_ref, wu_ref, wd_ref,
                 *rest, final):
    if final:
        fg_ref, o_ref, acc_ref = rest
    else:
        o_ref, acc_ref = rest
    i = pl.program_id(1)
    n = pl.num_programs(1)
    tm = x_ref.shape[0]

    u = cu_ref[...]
    row = lax.broadcasted_iota(jnp.int32, (tm, 1), 0)
    prev_row = jnp.where(i > 0, cup_ref[7:8, :], 0.0)
    next_row = jnp.where(i < n - 1, cun_ref[0:1, :], 0.0)
    u_prev = jnp.where(row == 0, prev_row, pltpu.roll(u, 1, axis=0))
    u_next = jnp.where(row == tm - 1, next_row, pltpu.roll(u, tm - 1, axis=0))
    conv = cb_ref[...] * (cw_ref[0:1, :] * u_prev + cw_ref[1:2, :] * u + cw_ref[2:3, :] * u_next)

    x1_parts, h2_parts = [], []
    n_groups = POST_ROW_GROUPS if tm % (POST_ROW_GROUPS * 16) == 0 else 1
    rows_per = tm // n_groups
    for gr in range(n_groups):
        rs = slice(gr * rows_per, (gr + 1) * rows_per)
        o = of_ref[rs, :] + ob_ref[rs, :]
        sq = o * o
        sq_hi = sq.astype(BF16)
        sq_lo = (sq - sq_hi.astype(F32)).astype(BF16)
        ms = (_dot(sq_hi, ones_ref[...]) + _dot(sq_lo, ones_ref[...])) * (1.0 / HEAD_DIM)
        gla = o * lax.rsqrt(ms + EPS) * ng_ref[...] * _silu(gg_ref[rs, :])

        mix = (_dot(conv[rs, :].astype(BF16), wo_ref[0:CONV_CH, :])
               + _dot(at_ref[rs, :], wo_ref[CONV_CH:CONV_CH + ATT_W, :])
               + _dot(gla.astype(BF16), wo_ref[CONV_CH + ATT_W:, :]))
        x1_g = x_ref[rs, :] + mod_ref[2:3, :] * mix

        ms2 = jnp.mean(x1_g * x1_g, axis=-1, keepdims=True)
        y2 = x1_g * lax.rsqrt(ms2 + EPS) * g2_ref[...]
        x1_parts.append(x1_g)
        h2_parts.append((y2 * (1.0 + mod_ref[4:5, :]) + mod_ref[3:4, :]).astype(BF16))
    x1 = jnp.concatenate(x1_parts, axis=0)
    h2 = jnp.concatenate(h2_parts, axis=0)

    def up(jc):
        if isinstance(jc, int):
            ca, cb = jc * FF_CHUNK, D_FF + jc * FF_CHUNK
        else:
            ca = pl.multiple_of(jc * FF_CHUNK, FF_CHUNK)
            cb = pl.multiple_of(D_FF + jc * FF_CHUNK, FF_CHUNK)
        a = _dot(h2, wu_ref[:, pl.ds(ca, FF_CHUNK)])
        b = _dot(h2, wu_ref[:, pl.ds(cb, FF_CHUNK)])
        return a, b

    def act(ab):
        return (_silu(ab[0]) * ab[1]).astype(BF16)

    prev = act(up(0))
    ab = up(1)
    acc_ref[...] = _dot(prev, wd_ref[0])
    prev = act(ab)

    def ff(t, prev):
        for u in range(FF_UNROLL):
            jc = 2 + t * FF_UNROLL + u
            ab = up(jc)
            acc_ref[...] += _dot(prev, wd_ref[jc - 1])
            prev = act(ab)
        return prev

    n_trips = (N_FF_CHUNKS - 2) // FF_UNROLL
    prev = ff(0, prev) if n_trips == 1 else lax.fori_loop(0, n_trips, ff, prev)
    x2 = x1 + mod_ref[5:6, :] * (acc_ref[...] + _dot(prev, wd_ref[N_FF_CHUNKS - 1]))
    if final:
        msf = jnp.mean(x2 * x2, axis=-1, keepdims=True)
        x2 = x2 * lax.rsqrt(msf + EPS) * fg_ref[...]
    o_ref[...] = x2


def _post(xs, mods_l, mod_row, g2, cu, cb, conv_w, attn, o_f, o_b, gl, norm_g, w_out, wu, wd,
          layer, final_g, tm):
    B, S, D = xs.shape
    nt = S // tm
    hb = tm // 8
    row = lambda w: pl.BlockSpec((None, tm, w), lambda b, i: (b, i, 0))
    if mod_row is None:
        mod_spec = pl.BlockSpec((None, 8, D), lambda b, i: (b, 0, 0))
    else:
        mod_spec = pl.BlockSpec((None, 8, D), lambda b, i: (mod_row, 0, 0))
    final = final_g is not None
    in_specs = [
        row(D), mod_spec, _const_spec((1, D)),
        row(CONV_CH),
        pl.BlockSpec((None, 8, CONV_CH), lambda b, i: (b, jnp.maximum(i * hb - 1, 0), 0)),
        pl.BlockSpec((None, 8, CONV_CH), lambda b, i: (b, jnp.minimum((i + 1) * hb, S // 8 - 1), 0)),
        row(CONV_CH), _const_spec((8, CONV_CH)), row(ATT_W), row(GLA_W), row(GLA_W),
        pl.BlockSpec((None, tm, GLA_W), lambda b, i: (b, i, 3)),
        _const_spec((1, GLA_W)), _const_spec((GLA_W, GLA_W)), _layer_spec((D, D), layer),
        _layer_spec((D, 2 * D_FF), layer), _layer_spec((N_FF_CHUNKS, FF_CHUNK, D), layer),
    ]
    args = [xs, mods_l, g2, cu, cu, cu, cb, conv_w, attn, o_f, o_b, gl, norm_g,
            jnp.asarray(_head_blocks(), BF16), w_out, wu, wd]
    if final:
        in_specs.append(_const_spec((1, D)))
        args.append(final_g)
    return pl.pallas_call(
        functools.partial(_post_kernel, final=final),
        grid=(B, nt),
        in_specs=in_specs,
        out_specs=row(D),
        out_shape=jax.ShapeDtypeStruct((B, S, D), F32),
        scratch_shapes=[pltpu.VMEM((tm, D), F32)],
        compiler_params=_params(2),
        name="post_final" if final else "post",
    )(*args)


def _rope_tables(seq):
    t = jnp.arange(seq)
    pos = jnp.stack([(t // GRID_W).astype(F32), (t % GRID_W).astype(F32)], axis=1)
    n_freq = HEAD_DIM // 4
    inv_freq = ROPE_BASE ** (-jnp.arange(n_freq, dtype=F32) / n_freq)
    ang = pos[:, :, None] * inv_freq
    cos, sin = jnp.cos(ang), jnp.sin(ang)
    zero = jnp.zeros_like(sin)
    cos_t = jnp.stack([cos, cos], axis=2).reshape(seq, HEAD_DIM)
    sa_t = jnp.stack([-sin, zero], axis=2).reshape(seq, HEAD_DIM)
    sb_t = jnp.stack([zero, sin], axis=2).reshape(seq, HEAD_DIM)
    rep = LANES // HEAD_DIM
    return tuple(jnp.tile(a, (1, rep)) for a in (cos_t, sa_t, sb_t))


def _tile_rows(S, want):
    t = min(want, S)
    while S % t:
        t //= 2
    return t


def kernel(x, c, ctx, c_ctx, w_mod, b_mod, norm1_g, norm2_g, w_in, conv_w, attn_sink,
           gla_gate_w, gla_gate_b, gla_norm_g, w_out, w_up, w_down, final_norm_g):
    B, S, D = x.shape
    L = w_mod.shape[0]
    Lc = ctx.shape[1]
    assert D == D_MODEL and S % WINDOW == 0 and Lc % GLA_CHUNK == 0 and B + 1 <= 8

    cvec = jnp.zeros((8, D), F32).at[:B].set(c).at[B].set(c_ctx)
    mods = _modulation(cvec, w_mod, b_mod).reshape(L, 8, N_MOD, D)
    mods = jnp.pad(mods, ((0, 0), (0, 0), (0, 8 - N_MOD), (0, 0)))

    order = np.array(Q_HEAD_ORDER)
    wq = w_in[:, :, 768:1280].reshape(L, D, ATT_HEADS, HEAD_DIM)[:, :, order].reshape(L, D, ATT_W)
    w_in_p = jnp.concatenate(
        [w_in[:, :, :768], wq, w_in[:, :, 1280:1536], w_in[:, :, 2560:2592],
         jnp.zeros((L, D, LANES - 2 * GLA_RANK), F32), w_in[:, :, 1536:2560]], axis=2).astype(BF16)
    wo_att = w_out[:, 256:768].reshape(L, ATT_HEADS, HEAD_DIM, D)[:, order].reshape(L, ATT_W, D)
    w_out_p = jnp.concatenate([w_out[:, :256], wo_att, w_out[:, 768:]], axis=1).astype(BF16)
    wg = jnp.zeros((L, LANES, 2 * GLA_W), F32)
    wg = wg.at[:, :GLA_RANK, :GLA_W].set(gla_gate_w[:, 0])
    wg = wg.at[:, GLA_RANK:2 * GLA_RANK, GLA_W:].set(gla_gate_w[:, 1]).astype(BF16)
    bg = gla_gate_b.reshape(L, 1, 2 * GLA_W)
    cw = jnp.pad(conv_w, ((0, 0), (0, 8 - CONV_K), (0, 0)))
    ng = jnp.tile(gla_norm_g, (1, GLA_HEADS)).reshape(L, 1, GLA_W)
    wu = w_up.astype(BF16)
    wd = w_down.reshape(L, N_FF_CHUNKS, FF_CHUNK, D).astype(BF16)
    g1 = norm1_g.reshape(L, 1, D)
    g2 = norm2_g.reshape(L, 1, D)

    rope_tabs = _rope_tables(S)
    tm = _tile_rows(S, 512)
    proj_tm = _tile_rows(S, 1024)
    gla_tile = _tile_rows(S, 1024)
    zero_state = jnp.zeros((B, GLA_W, GLA_W), F32)

    for l in range(L):
        last = l == L - 1
        cu, cb, q, k, v, gl, gates = _proj(x, mods[l], None, g1[l], w_in_p, l, wg[l], bg[l],
                                           rope_tabs, proj_tm)
        ccu, ccb, cq, ck, cv, cgl, cgates = _proj(ctx, mods[l], B, g1[l], w_in_p, l, wg[l], bg[l],
                                                  None, Lc)
        attn = _attention(q, k, v, ck, cv, attn_sink[l], local=True)
        oc_f, oc_b, sc_f, sc_b = _gla(cgl, cgates, zero_state, zero_state, Lc)
        o_f, o_b, _, _ = _gla(gl, gates, sc_f, sc_b, gla_tile)
        fin = final_norm_g.reshape(1, D) if last else None
        x = _post(x, mods[l], None, g2[l], cu, cb, cw[l], attn, o_f, o_b, gl, ng[l],
                  w_out_p, wu, wd, l, fin, tm)
        if not last:
            attn_c = _attention(cq, None, None, ck, cv, attn_sink[l], local=False)
            ctx = _post(ctx, mods[l], B, g2[l], ccu, ccb, cw[l], attn_c, oc_f, oc_b, cgl, ng[l],
                        w_out_p, wu, wd, l, None, Lc)
    return x
```

```python
import functools

import numpy as np
import jax
import jax.numpy as jnp
from jax import lax
from jax.experimental import pallas as pl
from jax.experimental.pallas import tpu as pltpu

F32 = jnp.float32
BF16 = jnp.bfloat16

D_MODEL = 1024
HEAD_DIM = 64
CONV_CH = 256
CONV_K = 3
ATT_HEADS = 8
ATT_KV_HEADS = 2
ATT_W = ATT_HEADS * HEAD_DIM
KV_W = ATT_KV_HEADS * HEAD_DIM
WINDOW = 128
GLA_HEADS = 4
GLA_W = GLA_HEADS * HEAD_DIM
GLA_RANK = 16
GLA_NORMALIZER = 16.0
D_FF = 2816
N_MOD = 6
GRID_W = 64
ROPE_BASE = 10000.0
EPS = 1e-6
LOG2_E = 1.4426950408889634

LANES = 128
C_CONV = 0
C_QK = 768
C_V = 1408
C_LR = 1536
C_GLA = 1664
N_IN_PAD = C_GLA + 4 * GLA_W
Q_HEAD_ORDER = (0, 4, 1, 5, 2, 6, 3, 7)

ATT_ROW_GROUP = 16
GLA_CHUNK = 64
GLA_SUB = 16
GLA_FAST_CHUNK = 256
GLA_FAST_RANGE = 80.0
FF_CHUNK = 256
N_FF_CHUNKS = D_FF // FF_CHUNK
PROJ_GROUP_ROWS = 512
POST_ROW_GROUPS = 2
FF_UNROLL = 9
assert (N_FF_CHUNKS - 2) % FF_UNROLL == 0

VMEM_LIMIT = 56 * 1024 * 1024


def _const_spec(shape):
    nd = len(shape)
    return pl.BlockSpec(shape, lambda *_: (0,) * nd, pipeline_mode=pl.Buffered(1))


def _layer_spec(shape, layer):
    nd = len(shape)
    return pl.BlockSpec((None,) + tuple(shape), lambda *_: (layer,) + (0,) * nd,
                        pipeline_mode=pl.Buffered(1))


def _params(n_grid):
    return pltpu.CompilerParams(
        dimension_semantics=("arbitrary",) * n_grid, vmem_limit_bytes=VMEM_LIMIT)


def _silu(a):
    return a * (1.0 / (1.0 + jnp.exp(-a)))


def _dot(a, b):
    return jnp.dot(a, b, preferred_element_type=F32)


def _dot_nt(a, b):
    return lax.dot_general(a, b, (((1,), (1,)), ((), ())), preferred_element_type=F32)


def _dot_tn(a, b):
    return lax.dot_general(a, b, (((0,), (0,)), ((), ())), preferred_element_type=F32)


def _mod_kernel(c_ref, w_ref, b_ref, o_ref):
    a = _silu(c_ref[...]).astype(BF16)
    o_ref[...] = _dot(a, w_ref[...].astype(BF16)) + b_ref[...]


def _modulation(cvec, w_mod, b_mod):
    L, D, W = w_mod.shape
    tn = 2048
    return pl.pallas_call(
        _mod_kernel,
        grid=(L, W // tn),
        in_specs=[
            pl.BlockSpec((8, D), lambda l, j: (0, 0)),
            pl.BlockSpec((None, D, tn), lambda l, j: (l, 0, j)),
            pl.BlockSpec((None, 1, tn), lambda l, j: (l, 0, j)),
        ],
        out_specs=pl.BlockSpec((None, 8, tn), lambda l, j: (l, 0, j)),
        out_shape=jax.ShapeDtypeStruct((L, 8, W), F32),
        compiler_params=_params(2),
        name="modulation",
    )(cvec, w_mod, b_mod.reshape(L, 1, W))


def _proj_kernel(x_ref, mod_ref, g1_ref, w_ref, wg_ref, bg_ref, *rest, rope):
    if rope:
        cos_ref, sa_ref, sb_ref = rest[:3]
        rest = rest[3:]
    cu_ref, cb_ref, q_ref, k_ref, v_ref, gl_ref, gate_ref = rest
    tm = x_ref.shape[0]
    rows_per = PROJ_GROUP_ROWS if tm % PROJ_GROUP_ROWS == 0 else tm
    scale = HEAD_DIM ** -0.5

    for gr in range(tm // rows_per):
        rs = slice(gr * rows_per, (gr + 1) * rows_per)
        x = x_ref[rs, :]
        ms = jnp.mean(x * x, axis=-1, keepdims=True)
        y = x * lax.rsqrt(ms + EPS) * g1_ref[...]
        h = (y * (1.0 + mod_ref[1:2, :]) + mod_ref[0:1, :]).astype(BF16)

        def proj(lo, hi, h=h):
            return _dot(h, w_ref[:, lo:hi])

        pc = proj(C_CONV, C_QK)
        cu_ref[rs, :] = pc[:, 2 * CONV_CH:3 * CONV_CH] * pc[:, 0:CONV_CH]
        cb_ref[rs, :] = pc[:, CONV_CH:2 * CONV_CH]

        pqk = proj(C_QK, C_GLA)
        for m in range((ATT_W + KV_W) // LANES):
            col = pqk[:, m * LANES:(m + 1) * LANES]
            if rope:
                col = (col * cos_ref[rs, :]
                       + pltpu.roll(col, LANES - HEAD_DIM // 4, axis=1) * sa_ref[rs, :]
                       + pltpu.roll(col, HEAD_DIM // 4, axis=1) * sb_ref[rs, :])
            if m < ATT_W // LANES:
                q_ref[rs, m * LANES:(m + 1) * LANES] = (col * scale).astype(BF16)
            else:
                k_ref[rs, :] = col.astype(BF16)

        v_ref[rs, :] = pqk[:, C_V - C_QK:C_LR - C_QK].astype(BF16)

        pg = proj(C_GLA, N_IN_PAD)
        gl_ref[rs, 0:GLA_W] = pg[:, 0:GLA_W] * scale
        gl_ref[rs, GLA_W:4 * GLA_W] = pg[:, GLA_W:4 * GLA_W]

        lr = pqk[:, C_LR - C_QK:C_GLA - C_QK].astype(BF16)
        z = _dot(lr, wg_ref[...]) + bg_ref[...]
        log_sig = jnp.minimum(z, 0.0) - jnp.log1p(jnp.exp(-jnp.abs(z)))
        gate_ref[rs, :] = log_sig * (1.0 / GLA_NORMALIZER)


def _proj(xs, mods_l, mod_row, g1, w_in, layer, wg, bg, rope_tabs, tm):
    B, S, D = xs.shape
    rope = rope_tabs is not None
    nt = S // tm
    row = lambda w: pl.BlockSpec((None, tm, w), lambda b, i: (b, i, 0))
    if mod_row is None:
        mod_spec = pl.BlockSpec((None, 8, D), lambda b, i: (b, 0, 0))
    else:
        mod_spec = pl.BlockSpec((None, 8, D), lambda b, i: (mod_row, 0, 0))
    in_specs = [row(D), mod_spec, _const_spec((1, D)), _layer_spec((D, N_IN_PAD), layer),
                _const_spec((LANES, 2 * GLA_W)), _const_spec((1, 2 * GLA_W))]
    args = [xs, mods_l, g1, w_in, wg, bg]
    if rope:
        in_specs += [pl.BlockSpec((tm, LANES), lambda b, i: (i, 0))] * 3
        args += list(rope_tabs)
    widths = (CONV_CH, CONV_CH, ATT_W, KV_W, KV_W, 4 * GLA_W, 2 * GLA_W)
    dtypes = (F32, F32, BF16, BF16, BF16, F32, F32)
    return pl.pallas_call(
        functools.partial(_proj_kernel, rope=rope),
        grid=(B, nt),
        in_specs=in_specs,
        out_specs=[row(w) for w in widths],
        out_shape=[jax.ShapeDtypeStruct((B, S, w), dt) for w, dt in zip(widths, dtypes)],
        compiler_params=_params(2),
        name="proj_rope" if rope else "proj_ctx",
    )(*args)


def _attn_kernel(sink_ref, q_ref, kc_ref, vc_ref, *rest, local, n_sub):
    tb = q_ref.shape[0] // n_sub
    n_pairs = ATT_W // LANES
    n_ctx = kc_ref.shape[0]
    rg = ATT_ROW_GROUP
    if local:
        kp_ref, km_ref, kn_ref, vp_ref, vm_ref, vn_ref, o_ref, s_scr, p_scr, e_scr, bias_scr = rest
        i = pl.program_id(1)
        n = pl.num_programs(1)
        k_loc = jnp.concatenate([kp_ref[...], km_ref[...], kn_ref[...]], axis=0)
        v_loc = jnp.concatenate([vp_ref[...], vm_ref[...], vn_ref[...]], axis=0)
        r = lax.broadcasted_iota(jnp.int32, (tb, tb), 0)
        j = lax.broadcasted_iota(jnp.int32, (tb, tb), 1)
        bias_prev = jnp.where(j >= r, 0.0, -jnp.inf)
        bias_next = jnp.where(j <= r, 0.0, -jnp.inf)
        bias_scr[0] = bias_prev + jnp.where(i > 0, 0.0, -jnp.inf)
        bias_scr[1] = bias_prev
        bias_scr[2] = bias_next
        bias_scr[3] = bias_next + jnp.where(i < n - 1, 0.0, -jnp.inf)
        nk = n_ctx + 3 * tb
    else:
        o_ref, s_scr, p_scr, e_scr = rest
        nk = n_ctx
    lane = lax.broadcasted_iota(jnp.int32, (1, LANES), 1)
    lo = lane < HEAD_DIM
    hi = jnp.logical_not(lo)
    lo_k = lax.broadcasted_iota(jnp.int32, (nk, LANES), 1) < HEAD_DIM
    group_ones = [jnp.where(lo_k, 1.0, 0.0).astype(BF16), jnp.where(lo_k, 0.0, 1.0).astype(BF16)]

    def scores(jb):
        q = q_ref[jb * tb:(jb + 1) * tb, :]
        lhs = jnp.concatenate([q[:, m * LANES:(m + 1) * LANES] for m in range(n_pairs)], axis=0)
        kk = (jnp.concatenate([kc_ref[...], k_loc[jb * tb:(jb + 3) * tb]], axis=0) if local
              else kc_ref[...])
        k2 = jnp.concatenate([kk * group_ones[0], kk * group_ones[1]], axis=0)
        s_scr[jb] = _dot_nt(lhs, k2)

    def softmax(jb):
        for m in range(n_pairs):
            sinks = [sink_ref[m + (ATT_HEADS // ATT_KV_HEADS) * g] for g in range(ATT_KV_HEADS)]
            for gi in range(tb // rg):
                rq = gi * rg
                r0 = m * tb + rq
                s = s_scr[jb, r0:r0 + rg, :]
                probs, sink_terms = [], []
                for g in range(ATT_KV_HEADS):
                    sg = s[:, g * nk:(g + 1) * nk]
                    if local:
                        bp = bias_scr[0 if jb == 0 else 1, rq:rq + rg, :]
                        bn = bias_scr[3 if jb == n_sub - 1 else 2, rq:rq + rg, :]
                        sg = jnp.concatenate(
                            [sg[:, :n_ctx], sg[:, n_ctx:n_ctx + tb] + bp,
                             sg[:, n_ctx + tb:n_ctx + 2 * tb], sg[:, n_ctx + 2 * tb:] + bn], axis=1)
                    mx = jnp.maximum(jnp.max(sg, axis=-1, keepdims=True), sinks[g])
                    probs.append(jnp.exp(sg - mx).astype(BF16))
                    sink_terms.append(jnp.exp(sinks[g] - mx))
                p_scr[jb, r0:r0 + rg, :] = jnp.concatenate(probs, axis=1)
                e_scr[jb, r0:r0 + rg, :] = jnp.where(lo, sink_terms[0], sink_terms[1])

    def values(jb):
        vv = (jnp.concatenate([vc_ref[...], v_loc[jb * tb:(jb + 3) * tb]], axis=0) if local
              else vc_ref[...])
        v2 = jnp.concatenate(
            [jnp.concatenate([vv * group_ones[0], group_ones[0]], axis=1),
             jnp.concatenate([vv * group_ones[1], group_ones[1]], axis=1)], axis=0)
        o_aug = _dot(p_scr[jb], v2)
        o = o_aug[:, :LANES] * (1.0 / (o_aug[:, LANES:] + e_scr[jb]))
        for m in range(n_pairs):
            o_ref[jb * tb:(jb + 1) * tb, m * LANES:(m + 1) * LANES] = (
                o[m * tb:(m + 1) * tb].astype(BF16))

    scores(0)
    for jb in range(n_sub):
        if jb + 1 < n_sub:
            scores(jb + 1)
        softmax(jb)
        values(jb)


def _attention(q, k, v, k_ctx, v_ctx, sink, local, n_sub=8):
    B, S, _ = q.shape
    Lc = k_ctx.shape[1]
    if not local:
        n_sub = 1
    tq = WINDOW * n_sub if local else S
    nq = S // tq
    in_specs = [
        pl.BlockSpec(memory_space=pltpu.SMEM),
        pl.BlockSpec((None, tq, ATT_W), lambda b, i: (b, i, 0)),
        pl.BlockSpec((None, Lc, KV_W), lambda b, i: (b, 0, 0)),
        pl.BlockSpec((None, Lc, KV_W), lambda b, i: (b, 0, 0)),
    ]
    args = [sink, q, k_ctx, v_ctx]
    if local:
        nw = S // WINDOW
        nb = [pl.BlockSpec((None, WINDOW, KV_W), lambda b, i: (b, jnp.maximum(i * n_sub - 1, 0), 0)),
              pl.BlockSpec((None, tq, KV_W), lambda b, i: (b, i, 0)),
              pl.BlockSpec((None, WINDOW, KV_W),
                           lambda b, i: (b, jnp.minimum((i + 1) * n_sub, nw - 1), 0))]
        in_specs += nb + nb
        args += [k, k, k, v, v, v]
    tb = tq // n_sub
    nk = Lc + (3 * tb if local else 0)
    n_rows = (ATT_W // LANES) * tb
    scratch = [pltpu.VMEM((n_sub, n_rows, ATT_KV_HEADS * nk), F32),
               pltpu.VMEM((n_sub, n_rows, ATT_KV_HEADS * nk), BF16),
               pltpu.VMEM((n_sub, n_rows, LANES), F32)]
    if local:
        scratch.append(pltpu.VMEM((4, tb, tb), F32))
    return pl.pallas_call(
        functools.partial(_attn_kernel, local=local, n_sub=n_sub),
        grid=(B, nq),
        in_specs=in_specs,
        out_specs=pl.BlockSpec((None, tq, ATT_W), lambda b, i: (b, i, 0)),
        out_shape=jax.ShapeDtypeStruct((B, S, ATT_W), BF16),
        scratch_shapes=scratch,
        compiler_params=_params(2),
        name="attn_window" if local else "attn_ctx",
    )(*args)


def _gla_consts(rev):
    t = np.arange(GLA_CHUNK)[:, None]
    s = np.arange(GLA_CHUNK)[None, :]
    same = (t // GLA_SUB) == (s // GLA_SUB)
    before = (s >= t) if rev else (s <= t)
    return np.concatenate([same & before, before], axis=0).astype(np.float32)


def _head_blocks():
    h = np.arange(GLA_W) // HEAD_DIM
    return (h[:, None] == h[None, :]).astype(np.float32)


def _gla_last_row(blk, rev):
    return blk * GLA_SUB if rev else blk * GLA_SUB + GLA_SUB - 1


def _gla_prepare(q, k, g, cum, rev):
    n_sub = GLA_CHUNK // GLA_SUB
    g2 = g * LOG2_E
    g_hi = g2.astype(BF16)
    rem = g2 - g_hi.astype(F32)
    g_mid = rem.astype(BF16)
    g_lo = (rem - g_mid.astype(F32)).astype(BF16)
    cs = _dot(cum, g_hi) + _dot(cum, g_mid) + _dot(cum, g_lo)
    beta, bch = cs[0:GLA_CHUNK], cs[GLA_CHUNK:2 * GLA_CHUNK]
    gam = jnp.concatenate(
        [beta[_gla_last_row(b, rev):_gla_last_row(b, rev) + 1, :] - beta[b * GLA_SUB:(b + 1) * GLA_SUB, :]
         for b in range(n_sub)], axis=0)
    end_row = _gla_last_row(0 if rev else n_sub - 1, rev)
    dch = bch[end_row:end_row + 1, :] - bch
    f32_rows = jnp.concatenate([beta, bch, q * jnp.exp2(beta), k * jnp.exp2(gam)], axis=0)
    b16_rows = jnp.concatenate([(q * jnp.exp2(bch)).astype(BF16), (k * jnp.exp2(dch)).astype(BF16)],
                               axis=0)
    return f32_rows, b16_rows


def _gla_causal(rev):
    t = (np.arange(GLA_HEADS * GLA_FAST_CHUNK) % GLA_FAST_CHUNK)[:, None]
    s = np.arange(GLA_FAST_CHUNK)[None, :]
    return ((s >= t) if rev else (s <= t)).astype(np.float32)


def _gla_before(rev):
    t = np.arange(GLA_FAST_CHUNK)[:, None]
    s = np.arange(GLA_FAST_CHUNK)[None, :]
    return ((s >= t) if rev else (s <= t)).astype(np.float32)


def _gla_prepare_fast(q, k, v, g, cum_chunk, head_mask, rev):
    g2 = g * LOG2_E
    g_hi = g2.astype(BF16)
    g_lo = (g2 - g_hi.astype(F32)).astype(BF16)
    bch = _dot(cum_chunk, g_hi) + _dot(cum_chunk, g_lo)
    end_row = 0 if rev else GLA_FAST_CHUNK - 1
    mid_row = GLA_FAST_CHUNK // 2 if rev else GLA_FAST_CHUNK // 2 - 1
    b_end = bch[end_row:end_row + 1, :]
    b_mid = bch[mid_row:mid_row + 1, :]
    ok = (jnp.max(-b_end) <= GLA_FAST_RANGE).astype(jnp.int32)
    q_mid = (q * jnp.exp2(bch - b_mid)).astype(BF16)
    k_mid = (k * jnp.exp2(b_mid - bch)).astype(BF16)
    zero = jnp.zeros_like(q_mid)
    rows = jnp.concatenate(
        [jnp.where(hm, q_mid, zero) for hm in head_mask]
        + [k_mid, (q * jnp.exp2(bch)).astype(BF16), (k * jnp.exp2(b_end - bch)).astype(BF16),
           v.astype(BF16)], axis=0)
    return rows, jnp.exp2(b_end), ok


def _gla_tile_fast(pb_scr, dec_scr, dirs, bd, head_mask, n_chunks):
    cf = GLA_FAST_CHUNK
    n_q = GLA_HEADS * cf
    work = []
    for d, (_, _, _, st_ref, o_ref, caus_ref, rev) in enumerate(dirs):
        for c in (range(n_chunks - 1, -1, -1) if rev else range(n_chunks)):
            pb = pb_scr.at[d, c]
            vb = pb[n_q + 3 * cf:n_q + 4 * cf]
            upd = _dot_tn(vb, pb[n_q + 2 * cf:n_q + 3 * cf])
            a = _dot_nt(pb[0:n_q], pb[n_q:n_q + cf])
            work.append(dict(d=d, c=c, pb=pb, vb=vb, upd=upd, a=a, o_ref=o_ref, caus=caus_ref))
    for d, (_, _, _, st_ref, _, _, _) in enumerate(dirs):
        st = st_ref[...]
        for w in (w for w in work if w["d"] == d):
            w["o_inter"] = _dot_nt(w["pb"][n_q + cf:n_q + 2 * cf], st.astype(BF16))
            st = st * dec_scr[d, w["c"], 0:1, :] + w["upd"] * bd
        st_ref[...] = st
    for w in work:
        w["r"] = _dot((w["a"] * w["caus"][...]).astype(BF16), w["vb"])
    for w in work:
        o = w["o_inter"]
        for hh, hm in enumerate(head_mask):
            o = o + jnp.where(hm, w["r"][hh * cf:(hh + 1) * cf, :], 0.0)
        w["o_ref"][w["c"] * cf:(w["c"] + 1) * cf, :] = o


def _gla_chunk(q, k, v, pf_ref, pb_ref, st_ref, o_ref, r0, ones, bd, head_mask, rev):
    n_sub = GLA_CHUNK // GLA_SUB
    order = list(range(n_sub))[::-1] if rev else list(range(n_sub))

    def rows(a, blk):
        return a[blk * GLA_SUB:(blk + 1) * GLA_SUB, :]

    half = GLA_SUB // 2
    beta = pf_ref[0:GLA_CHUNK]
    q_loc = pf_ref[2 * GLA_CHUNK:3 * GLA_CHUNK]
    k_loc = pf_ref[3 * GLA_CHUNK:4 * GLA_CHUNK]
    g_tot = [beta[_gla_last_row(b, rev):_gla_last_row(b, rev) + 1, :] for b in range(n_sub)]
    end_row = GLA_CHUNK + _gla_last_row(order[-1], rev)
    b_end = pf_ref[end_row:end_row + 1, :]
    half_row = lax.broadcasted_iota(jnp.int32, (half, 1), 0)

    st = st_ref[...]
    o_inter = _dot_nt(pb_ref[0:GLA_CHUNK], st.astype(BF16))
    upd = _dot_tn(v.astype(BF16), pb_ref[GLA_CHUNK:2 * GLA_CHUNK])
    scores, values = {}, {}
    for p in range(1, n_sub):
        ks, vs = [], []
        for pp in range(p):
            sb = order[pp]
            kk = rows(k_loc, sb)
            mids = [order[x] for x in range(pp + 1, p)]
            if mids:
                tot = g_tot[mids[0]]
                for mb in mids[1:]:
                    tot = tot + g_tot[mb]
                kk = kk * jnp.exp2(tot)
            ks.append(kk)
            vs.append(rows(v, sb))
        qt = rows(q_loc, order[p])
        q_heads = jnp.concatenate([jnp.where(hm, qt, 0.0) for hm in head_mask], axis=0)
        scores[p] = _dot_nt(q_heads.astype(BF16), jnp.concatenate(ks, axis=0).astype(BF16))
        values[p] = jnp.concatenate(vs, axis=0).astype(BF16)
    yield

    pairs = []
    for s in range(GLA_SUB):
        hs = s // half
        for hb in (range(0, hs + 1) if rev else range(hs, GLA_SUB // half)):
            pairs.append((s, hb))
    sums = {}
    for p, tb in enumerate(order):
        bt, qb, kb = rows(beta, tb), rows(q, tb), rows(k, tb)
        slabs = []
        for s, hb in pairs:
            d = bt[hb * half:(hb + 1) * half, :] - bt[s:s + 1, :]
            if hb == s // half:
                keep = (half_row <= s % half) if rev else (half_row >= s % half)
                d = jnp.where(keep, d, -jnp.inf)
            slabs.append(qb[hb * half:(hb + 1) * half, :] * jnp.exp2(d) * kb[s:s + 1, :])
        sums[p] = _dot(jnp.concatenate(slabs, axis=0).astype(BF16), ones)
        yield

    mixed = {p: _dot(scores[p].astype(BF16), values[p]) for p in range(1, n_sub)}
    yield

    for p, tb in enumerate(order):
        vb = rows(v, tb)
        acc = rows(o_inter, tb)
        if p > 0:
            for hh, hm in enumerate(head_mask):
                acc = acc + jnp.where(hm, mixed[p][hh * GLA_SUB:(hh + 1) * GLA_SUB, :], 0.0)
        halves = [acc[hb * half:(hb + 1) * half, :] for hb in range(GLA_SUB // half)]
        for idx, (s, hb) in enumerate(pairs):
            halves[hb] = halves[hb] + sums[p][idx * half:(idx + 1) * half, :] * vb[s:s + 1, :]
        o_ref[pl.ds(r0 + tb * GLA_SUB, GLA_SUB), :] = jnp.concatenate(halves, axis=0)
        yield

    st_ref[...] = st * jnp.exp2(b_end) + upd * bd


def _interleave(gens):
    active = list(gens)
    while active:
        for gen in list(active):
            try:
                next(gen)
            except StopIteration:
                active.remove(gen)


def _gla_kernel(cumf_ref, cumb_ref, ones_ref, bd_ref, causf_ref, causb_ref, beff_ref, befb_ref,
                glf_ref, glb_ref, gf_ref, gb_ref, s0f_ref, s0b_ref, of_ref, ob_ref, sff_ref,
                sfb_ref, stf_ref, stb_ref, pb_scr, dec_scr, ok_ref, *, n_chunks, n_fast):
    i = pl.program_id(1)
    lane_head = lax.broadcasted_iota(jnp.int32, (1, GLA_W), 1) // HEAD_DIM
    head_mask = [lane_head == hh for hh in range(GLA_HEADS)]
    dirs = ((glf_ref, gf_ref, cumf_ref, stf_ref, of_ref, causf_ref, False),
            (glb_ref, gb_ref, cumb_ref, stb_ref, ob_ref, causb_ref, True))

    @pl.when(i == 0)
    def _():
        stf_ref[...] = s0f_ref[...]
        stb_ref[...] = s0b_ref[...]

    def load(gl_ref, g_ref, r0, size=GLA_CHUNK):
        rows = pl.ds(r0, size)
        return (gl_ref[rows, 0:GLA_W], gl_ref[rows, GLA_W:2 * GLA_W],
                gl_ref[rows, 2 * GLA_W:3 * GLA_W], g_ref[rows, :])

    for c in range(n_fast):
        for d, (bef_ref, (gl_ref, g_ref, _, _, _, _, rev)) in enumerate(
                zip((beff_ref, befb_ref), dirs)):
            q, k, v, g = load(gl_ref, g_ref, c * GLA_FAST_CHUNK, GLA_FAST_CHUNK)
            rows, dec, ok = _gla_prepare_fast(q, k, v, g, bef_ref[...], head_mask, rev)
            pb_scr[d, c] = rows
            dec_scr[d, c] = jnp.broadcast_to(dec, (8, GLA_W))
            ok_ref[d, c] = ok

    n_ok = ok_ref[0, 0]
    for d in range(2):
        for c in range(n_fast):
            if (d, c) != (0, 0):
                n_ok = n_ok + ok_ref[d, c]
    fast = n_ok == 2 * n_fast

    @pl.when(fast)
    def _():
        _gla_tile_fast(pb_scr, dec_scr, dirs, bd_ref[...], head_mask, n_fast)

    @pl.when(jnp.logical_not(fast))
    def _():
        def chunk(ci, carry):
            cidx = (ci, n_chunks - 1 - ci)
            gens = []
            for d, (gl_ref, g_ref, cum_ref, st_ref, o_ref, _, rev) in enumerate(dirs):
                r0 = pl.multiple_of(cidx[d] * GLA_CHUNK, GLA_CHUNK)
                q, k, v, g = load(gl_ref, g_ref, r0)
                pf, pb = _gla_prepare(q, k, g, cum_ref[...], rev)
                gens.append(_gla_chunk(q, k, v, pf, pb, st_ref, o_ref, r0, ones_ref[...],
                                       bd_ref[...], head_mask, rev))
            _interleave(gens)
            return carry

        lax.fori_loop(0, n_chunks, chunk, 0)

    @pl.when(i == pl.num_programs(1) - 1)
    def _():
        sff_ref[...] = stf_ref[...]
        sfb_ref[...] = stb_ref[...]


def _gla(gl, gates, s0f, s0b, tile):
    B, S, _ = gl.shape
    nt = S // tile
    assert tile % GLA_FAST_CHUNK == 0
    n_chunks = tile // GLA_CHUNK
    n_fast = tile // GLA_FAST_CHUNK
    ones_bd = jnp.asarray(_head_blocks(), BF16)
    bd = jnp.asarray(_head_blocks(), F32)
    state_spec = pl.BlockSpec((None, GLA_W, GLA_W), lambda b, i: (b, 0, 0))
    return pl.pallas_call(
        functools.partial(_gla_kernel, n_chunks=n_chunks, n_fast=n_fast),
        grid=(B, nt),
        in_specs=[
            _const_spec((2 * GLA_CHUNK, GLA_CHUNK)),
            _const_spec((2 * GLA_CHUNK, GLA_CHUNK)),
            _const_spec((GLA_W, GLA_W)),
            _const_spec((GLA_W, GLA_W)),
            _const_spec((GLA_HEADS * GLA_FAST_CHUNK, GLA_FAST_CHUNK)),
            _const_spec((GLA_HEADS * GLA_FAST_CHUNK, GLA_FAST_CHUNK)),
            _const_spec((GLA_FAST_CHUNK, GLA_FAST_CHUNK)),
            _const_spec((GLA_FAST_CHUNK, GLA_FAST_CHUNK)),
            pl.BlockSpec((None, tile, 3 * GLA_W), lambda b, i: (b, i, 0)),
            pl.BlockSpec((None, tile, 3 * GLA_W), lambda b, i: (b, nt - 1 - i, 0)),
            pl.BlockSpec((None, tile, GLA_W), lambda b, i: (b, i, 0)),
            pl.BlockSpec((None, tile, GLA_W), lambda b, i: (b, nt - 1 - i, 1)),
            state_spec, state_spec,
        ],
        out_specs=[
            pl.BlockSpec((None, tile, GLA_W), lambda b, i: (b, i, 0)),
            pl.BlockSpec((None, tile, GLA_W), lambda b, i: (b, nt - 1 - i, 0)),
            state_spec, state_spec,
        ],
        out_shape=[jax.ShapeDtypeStruct((B, S, GLA_W), F32),
                   jax.ShapeDtypeStruct((B, S, GLA_W), F32),
                   jax.ShapeDtypeStruct((B, GLA_W, GLA_W), F32),
                   jax.ShapeDtypeStruct((B, GLA_W, GLA_W), F32)],
        scratch_shapes=[pltpu.VMEM((GLA_W, GLA_W), F32), pltpu.VMEM((GLA_W, GLA_W), F32),
                        pltpu.VMEM((2, n_fast, (GLA_HEADS + 4) * GLA_FAST_CHUNK, GLA_W), BF16),
                        pltpu.VMEM((2, n_fast, 8, GLA_W), F32),
                        pltpu.SMEM((2, n_fast), jnp.int32)],
        compiler_params=_params(2),
        name="gla",
    )(jnp.asarray(_gla_consts(False), BF16), jnp.asarray(_gla_consts(True), BF16),
      ones_bd, bd, jnp.asarray(_gla_causal(False), F32), jnp.asarray(_gla_causal(True), F32),
      jnp.asarray(_gla_before(False), BF16), jnp.asarray(_gla_before(True), BF16),
      gl, gl, gates, gates, s0f, s0b)


def _post_kernel(x_ref, mod_ref, g2_ref, cu_ref, cup_ref, cun_ref, cb_ref, cw_ref, at_ref,
                 of_ref, ob_ref, gg_ref, ng_ref, ones_ref, wo_ref, wu_ref, wd_ref,
                 *rest, final):
    if final:
        fg_ref, o_ref, acc_ref = rest
    else:
        o_ref, acc_ref = rest
    i = pl.program_id(1)
    n = pl.num_programs(1)
    tm = x_ref.shape[0]

    u = cu_ref[...]
    row = lax.broadcasted_iota(jnp.int32, (tm, 1), 0)
    prev_row = jnp.where(i > 0, cup_ref[7:8, :], 0.0)
    next_row = jnp.where(i < n - 1, cun_ref[0:1, :], 0.0)
    u_prev = jnp.where(row == 0, prev_row, pltpu.roll(u, 1, axis=0))
    u_next = jnp.where(row == tm - 1, next_row, pltpu.roll(u, tm - 1, axis=0))
    conv = cb_ref[...] * (cw_ref[0:1, :] * u_prev + cw_ref[1:2, :] * u + cw_ref[2:3, :] * u_next)

    x1_parts, h2_parts = [], []
    n_groups = POST_ROW_GROUPS if tm % (POST_ROW_GROUPS * 16) == 0 else 1
    rows_per = tm // n_groups
    for gr in range(n_groups):
        rs = slice(gr * rows_per, (gr + 1) * rows_per)
        o = of_ref[rs, :] + ob_ref[rs, :]
        sq = o * o
        sq_hi = sq.astype(BF16)
        sq_lo = (sq - sq_hi.astype(F32)).astype(BF16)
        ms = (_dot(sq_hi, ones_ref[...]) + _dot(sq_lo, ones_ref[...])) * (1.0 / HEAD_DIM)
        gla = o * lax.rsqrt(ms + EPS) * ng_ref[...] * _silu(gg_ref[rs, :])

        mix = (_dot(conv[rs, :].astype(BF16), wo_ref[0:CONV_CH, :])
               + _dot(at_ref[rs, :], wo_ref[CONV_CH:CONV_CH + ATT_W, :])
               + _dot(gla.astype(BF16), wo_ref[CONV_CH + ATT_W:, :]))
        x1_g = x_ref[rs, :] + mod_ref[2:3, :] * mix

        ms2 = jnp.mean(x1_g * x1_g, axis=-1, keepdims=True)
        y2 = x1_g * lax.rsqrt(ms2 + EPS) * g2_ref[...]
        x1_parts.append(x1_g)
        h2_parts.append((y2 * (1.0 + mod_ref[4:5, :]) + mod_ref[3:4, :]).astype(BF16))
    x1 = jnp.concatenate(x1_parts, axis=0)
    h2 = jnp.concatenate(h2_parts, axis=0)

    def up(jc):
        if isinstance(jc, int):
            ca, cb = jc * FF_CHUNK, D_FF + jc * FF_CHUNK
        else:
            ca = pl.multiple_of(jc * FF_CHUNK, FF_CHUNK)
            cb = pl.multiple_of(D_FF + jc * FF_CHUNK, FF_CHUNK)
        a = _dot(h2, wu_ref[:, pl.ds(ca, FF_CHUNK)])
        b = _dot(h2, wu_ref[:, pl.ds(cb, FF_CHUNK)])
        return a, b

    def act(ab):
        return (_silu(ab[0]) * ab[1]).astype(BF16)

    prev = act(up(0))
    ab = up(1)
    acc_ref[...] = _dot(prev, wd_ref[0])
    prev = act(ab)

    def ff(t, prev):
        for u in range(FF_UNROLL):
            jc = 2 + t * FF_UNROLL + u
            ab = up(jc)
            acc_ref[...] += _dot(prev, wd_ref[jc - 1])
            prev = act(ab)
        return prev

    n_trips = (N_FF_CHUNKS - 2) // FF_UNROLL
    prev = ff(0, prev) if n_trips == 1 else lax.fori_loop(0, n_trips, ff, prev)
    x2 = x1 + mod_ref[5:6, :] * (acc_ref[...] + _dot(prev, wd_ref[N_FF_CHUNKS - 1]))
    if final:
        msf = jnp.mean(x2 * x2, axis=-1, keepdims=True)
        x2 = x2 * lax.rsqrt(msf + EPS) * fg_ref[...]
    o_ref[...] = x2


def _post(xs, mods_l, mod_row, g2, cu, cb, conv_w, attn, o_f, o_b, gl, norm_g, w_out, wu, wd,
          layer, final_g, tm):
    B, S, D = xs.shape
    nt = S // tm
    hb = tm // 8
    row = lambda w: pl.BlockSpec((None, tm, w), lambda b, i: (b, i, 0))
    if mod_row is None:
        mod_spec = pl.BlockSpec((None, 8, D), lambda b, i: (b, 0, 0))
    else:
        mod_spec = pl.BlockSpec((None, 8, D), lambda b, i: (mod_row, 0, 0))
    final = final_g is not None
    in_specs = [
        row(D), mod_spec, _const_spec((1, D)),
        row(CONV_CH),
        pl.BlockSpec((None, 8, CONV_CH), lambda b, i: (b, jnp.maximum(i * hb - 1, 0), 0)),
        pl.BlockSpec((None, 8, CONV_CH), lambda b, i: (b, jnp.minimum((i + 1) * hb, S // 8 - 1), 0)),
        row(CONV_CH), _const_spec((8, CONV_CH)), row(ATT_W), row(GLA_W), row(GLA_W),
        pl.BlockSpec((None, tm, GLA_W), lambda b, i: (b, i, 3)),
        _const_spec((1, GLA_W)), _const_spec((GLA_W, GLA_W)), _layer_spec((D, D), layer),
        _layer_spec((D, 2 * D_FF), layer), _layer_spec((N_FF_CHUNKS, FF_CHUNK, D), layer),
    ]
    args = [xs, mods_l, g2, cu, cu, cu, cb, conv_w, attn, o_f, o_b, gl, norm_g,
            jnp.asarray(_head_blocks(), BF16), w_out, wu, wd]
    if final:
        in_specs.append(_const_spec((1, D)))
        args.append(final_g)
    return pl.pallas_call(
        functools.partial(_post_kernel, final=final),
        grid=(B, nt),
        in_specs=in_specs,
        out_specs=row(D),
        out_shape=jax.ShapeDtypeStruct((B, S, D), F32),
        scratch_shapes=[pltpu.VMEM((tm, D), F32)],
        compiler_params=_params(2),
        name="post_final" if final else "post",
    )(*args)


def _rope_tables(seq):
    t = jnp.arange(seq)
    pos = jnp.stack([(t // GRID_W).astype(F32), (t % GRID_W).astype(F32)], axis=1)
    n_freq = HEAD_DIM // 4
    inv_freq = ROPE_BASE ** (-jnp.arange(n_freq, dtype=F32) / n_freq)
    ang = pos[:, :, None] * inv_freq
    cos, sin = jnp.cos(ang), jnp.sin(ang)
    zero = jnp.zeros_like(sin)
    cos_t = jnp.stack([cos, cos], axis=2).reshape(seq, HEAD_DIM)
    sa_t = jnp.stack([-sin, zero], axis=2).reshape(seq, HEAD_DIM)
    sb_t = jnp.stack([zero, sin], axis=2).reshape(seq, HEAD_DIM)
    rep = LANES // HEAD_DIM
    return tuple(jnp.tile(a, (1, rep)) for a in (cos_t, sa_t, sb_t))


def _tile_rows(S, want):
    t = min(want, S)
    while S % t:
        t //= 2
    return t


def kernel(x, c, ctx, c_ctx, w_mod, b_mod, norm1_g, norm2_g, w_in, conv_w, attn_sink,
           gla_gate_w, gla_gate_b, gla_norm_g, w_out, w_up, w_down, final_norm_g):
    B, S, D = x.shape
    L = w_mod.shape[0]
    Lc = ctx.shape[1]
    assert D == D_MODEL and S % WINDOW == 0 and Lc % GLA_CHUNK == 0 and B + 1 <= 8

    cvec = jnp.zeros((8, D), F32).at[:B].set(c).at[B].set(c_ctx)
    mods = _modulation(cvec, w_mod, b_mod).reshape(L, 8, N_MOD, D)
    mods = jnp.pad(mods, ((0, 0), (0, 0), (0, 8 - N_MOD), (0, 0)))

    order = np.array(Q_HEAD_ORDER)
    wq = w_in[:, :, 768:1280].reshape(L, D, ATT_HEADS, HEAD_DIM)[:, :, order].reshape(L, D, ATT_W)
    w_in_p = jnp.concatenate(
        [w_in[:, :, :768], wq, w_in[:, :, 1280:1536], w_in[:, :, 2560:2592],
         jnp.zeros((L, D, LANES - 2 * GLA_RANK), F32), w_in[:, :, 1536:2560]], axis=2).astype(BF16)
    wo_att = w_out[:, 256:768].reshape(L, ATT_HEADS, HEAD_DIM, D)[:, order].reshape(L, ATT_W, D)
    w_out_p = jnp.concatenate([w_out[:, :256], wo_att, w_out[:, 768:]], axis=1).astype(BF16)
    wg = jnp.zeros((L, LANES, 2 * GLA_W), F32)
    wg = wg.at[:, :GLA_RANK, :GLA_W].set(gla_gate_w[:, 0])
    wg = wg.at[:, GLA_RANK:2 * GLA_RANK, GLA_W:].set(gla_gate_w[:, 1]).astype(BF16)
    bg = gla_gate_b.reshape(L, 1, 2 * GLA_W)
    cw = jnp.pad(conv_w, ((0, 0), (0, 8 - CONV_K), (0, 0)))
    ng = jnp.tile(gla_norm_g, (1, GLA_HEADS)).reshape(L, 1, GLA_W)
    wu = w_up.astype(BF16)
    wd = w_down.reshape(L, N_FF_CHUNKS, FF_CHUNK, D).astype(BF16)
    g1 = norm1_g.reshape(L, 1, D)
    g2 = norm2_g.reshape(L, 1, D)

    rope_tabs = _rope_tables(S)
    tm = _tile_rows(S, 512)
    proj_tm = _tile_rows(S, 1024)
    gla_tile = _tile_rows(S, 1024)
    zero_state = jnp.zeros((B, GLA_W, GLA_W), F32)

    for l in range(L):
        last = l == L - 1
        cu, cb, q, k, v, gl, gates = _proj(x, mods[l], None, g1[l], w_in_p, l, wg[l], bg[l],
                                           rope_tabs, proj_tm)
        ccu, ccb, cq, ck, cv, cgl, cgates = _proj(ctx, mods[l], B, g1[l], w_in_p, l, wg[l], bg[l],
                                                  None, Lc)
        attn = _attention(q, k, v, ck, cv, attn_sink[l], local=True)
        oc_f, oc_b, sc_f, sc_b = _gla(cgl, cgates, zero_state, zero_state, Lc)
        o_f, o_b, _, _ = _gla(gl, gates, sc_f, sc_b, gla_tile)
        fin = final_norm_g.reshape(1, D) if last else None
        x = _post(x, mods[l], None, g2[l], cu, cb, cw[l], attn, o_f, o_b, gl, ng[l],
                  w_out_p, wu, wd, l, fin, tm)
        if not last:
            attn_c = _attention(cq, None, None, ck, cv, attn_sink[l], local=False)
            ctx = _post(ctx, mods[l], B, g2[l], ccu, ccb, cw[l], attn_c, oc_f, oc_b, cgl, ng[l],
                        w_out_p, wu, wd, l, None, Lc)
    return x
```

```python
import functools

import numpy as np
import jax
import jax.numpy as jnp
from jax import lax
from jax.experimental import pallas as pl
from jax.experimental.pallas import tpu as pltpu

F32 = jnp.float32
BF16 = jnp.bfloat16

D_MODEL = 1024
HEAD_DIM = 64
CONV_CH = 256
CONV_K = 3
ATT_HEADS = 8
ATT_KV_HEADS = 2
ATT_W = ATT_HEADS * HEAD_DIM
KV_W = ATT_KV_HEADS * HEAD_DIM
WINDOW = 128
GLA_HEADS = 4
GLA_W = GLA_HEADS * HEAD_DIM
GLA_RANK = 16
GLA_NORMALIZER = 16.0
D_FF = 2816
N_MOD = 6
GRID_W = 64
ROPE_BASE = 10000.0
EPS = 1e-6
LOG2_E = 1.4426950408889634

LANES = 128
C_CONV = 0
C_QK = 768
C_V = 1408
C_LR = 1536
C_GLA = 1664
N_IN_PAD = C_GLA + 4 * GLA_W
Q_HEAD_ORDER = (0, 4, 1, 5, 2, 6, 3, 7)

ATT_ROW_GROUP = 16
GLA_CHUNK = 64
GLA_SUB = 16
GLA_FAST_CHUNK = 256
GLA_FAST_RANGE = 80.0
FF_CHUNK = 256
N_FF_CHUNKS = D_FF // FF_CHUNK
PROJ_GROUP_ROWS = 512
POST_ROW_GROUPS = 2
FF_UNROLL = 9
assert (N_FF_CHUNKS - 2) % FF_UNROLL == 0

VMEM_LIMIT = 56 * 1024 * 1024


def _const_spec(shape):
    nd = len(shape)
    return pl.BlockSpec(shape, lambda *_: (0,) * nd, pipeline_mode=pl.Buffered(1))


def _layer_spec(shape, layer):
    nd = len(shape)
    return pl.BlockSpec((None,) + tuple(shape), lambda *_: (layer,) + (0,) * nd,
                        pipeline_mode=pl.Buffered(1))


def _params(n_grid):
    return pltpu.CompilerParams(
        dimension_semantics=("arbitrary",) * n_grid, vmem_limit_bytes=VMEM_LIMIT)


def _silu(a):
    return a * (1.0 / (1.0 + jnp.exp(-a)))


def _dot(a, b):
    return jnp.dot(a, b, preferred_element_type=F32)


def _dot_nt(a, b):
    return lax.dot_general(a, b, (((1,), (1,)), ((), ())), preferred_element_type=F32)


def _dot_tn(a, b):
    return lax.dot_general(a, b, (((0,), (0,)), ((), ())), preferred_element_type=F32)


def _mod_kernel(c_ref, w_ref, b_ref, o_ref):
    a = _silu(c_ref[...]).astype(BF16)
    o_ref[...] = _dot(a, w_ref[...].astype(BF16)) + b_ref[...]


def _modulation(cvec, w_mod, b_mod):
    L, D, W = w_mod.shape
    tn = 2048
    return pl.pallas_call(
        _mod_kernel,
        grid=(L, W // tn),
        in_specs=[
            pl.BlockSpec((8, D), lambda l, j: (0, 0)),
            pl.BlockSpec((None, D, tn), lambda l, j: (l, 0, j)),
            pl.BlockSpec((None, 1, tn), lambda l, j: (l, 0, j)),
        ],
        out_specs=pl.BlockSpec((None, 8, tn), lambda l, j: (l, 0, j)),
        out_shape=jax.ShapeDtypeStruct((L, 8, W), F32),
        compiler_params=_params(2),
        name="modulation",
    )(cvec, w_mod, b_mod.reshape(L, 1, W))


def _proj_kernel(x_ref, mod_ref, g1_ref, w_ref, wg_ref, bg_ref, *rest, rope):
    if rope:
        cos_ref, sa_ref, sb_ref = rest[:3]
        rest = rest[3:]
    cu_ref, cb_ref, q_ref, k_ref, v_ref, gl_ref, gate_ref = rest
    tm = x_ref.shape[0]
    rows_per = PROJ_GROUP_ROWS if tm % PROJ_GROUP_ROWS == 0 else tm
    scale = HEAD_DIM ** -0.5

    for gr in range(tm // rows_per):
        rs = slice(gr * rows_per, (gr + 1) * rows_per)
        x = x_ref[rs, :]
        ms = jnp.mean(x * x, axis=-1, keepdims=True)
        y = x * lax.rsqrt(ms + EPS) * g1_ref[...]
        h = (y * (1.0 + mod_ref[1:2, :]) + mod_ref[0:1, :]).astype(BF16)

        def proj(lo, hi, h=h):
            return _dot(h, w_ref[:, lo:hi])

        pc = proj(C_CONV, C_QK)
        cu_ref[rs, :] = pc[:, 2 * CONV_CH:3 * CONV_CH] * pc[:, 0:CONV_CH]
        cb_ref[rs, :] = pc[:, CONV_CH:2 * CONV_CH]

        pqk = proj(C_QK, C_GLA)
        for m in range((ATT_W + KV_W) // LANES):
            col = pqk[:, m * LANES:(m + 1) * LANES]
            if rope:
                col = (col * cos_ref[rs, :]
                       + pltpu.roll(col, LANES - HEAD_DIM // 4, axis=1) * sa_ref[rs, :]
                       + pltpu.roll(col, HEAD_DIM // 4, axis=1) * sb_ref[rs, :])
            if m < ATT_W // LANES:
                q_ref[rs, m * LANES:(m + 1) * LANES] = (col * scale).astype(BF16)
            else:
                k_ref[rs, :] = col.astype(BF16)

        v_ref[rs, :] = pqk[:, C_V - C_QK:C_LR - C_QK].astype(BF16)

        pg = proj(C_GLA, N_IN_PAD)
        gl_ref[rs, 0:GLA_W] = pg[:, 0:GLA_W] * scale
        gl_ref[rs, GLA_W:4 * GLA_W] = pg[:, GLA_W:4 * GLA_W]

        lr = pqk[:, C_LR - C_QK:C_GLA - C_QK].astype(BF16)
        z = _dot(lr, wg_ref[...]) + bg_ref[...]
        log_sig = jnp.minimum(z, 0.0) - jnp.log1p(jnp.exp(-jnp.abs(z)))
        gate_ref[rs, :] = log_sig * (1.0 / GLA_NORMALIZER)


def _proj(xs, mods_l, mod_row, g1, w_in, layer, wg, bg, rope_tabs, tm):
    B, S, D = xs.shape
    rope = rope_tabs is not None
    nt = S // tm
    row = lambda w: pl.BlockSpec((None, tm, w), lambda b, i: (b, i, 0))
    if mod_row is None:
        mod_spec = pl.BlockSpec((None, 8, D), lambda b, i: (b, 0, 0))
    else:
        mod_spec = pl.BlockSpec((None, 8, D), lambda b, i: (mod_row, 0, 0))
    in_specs = [row(D), mod_spec, _const_spec((1, D)), _layer_spec((D, N_IN_PAD), layer),
                _const_spec((LANES, 2 * GLA_W)), _const_spec((1, 2 * GLA_W))]
    args = [xs, mods_l, g1, w_in, wg, bg]
    if rope:
        in_specs += [pl.BlockSpec((tm, LANES), lambda b, i: (i, 0))] * 3
        args += list(rope_tabs)
    widths = (CONV_CH, CONV_CH, ATT_W, KV_W, KV_W, 4 * GLA_W, 2 * GLA_W)
    dtypes = (F32, F32, BF16, BF16, BF16, F32, F32)
    return pl.pallas_call(
        functools.partial(_proj_kernel, rope=rope),
        grid=(B, nt),
        in_specs=in_specs,
        out_specs=[row(w) for w in widths],
        out_shape=[jax.ShapeDtypeStruct((B, S, w), dt) for w, dt in zip(widths, dtypes)],
        compiler_params=_params(2),
        name="proj_rope" if rope else "proj_ctx",
    )(*args)


def _attn_kernel(sink_ref, q_ref, kc_ref, vc_ref, *rest, local, n_sub):
    tb = q_ref.shape[0] // n_sub
    n_pairs = ATT_W // LANES
    n_ctx = kc_ref.shape[0]
    rg = ATT_ROW_GROUP
    if local:
        kp_ref, km_ref, kn_ref, vp_ref, vm_ref, vn_ref, o_ref, s_scr, p_scr, e_scr, bias_scr = rest
        i = pl.program_id(1)
        n = pl.num_programs(1)
        k_loc = jnp.concatenate([kp_ref[...], km_ref[...], kn_ref[...]], axis=0)
        v_loc = jnp.concatenate([vp_ref[...], vm_ref[...], vn_ref[...]], axis=0)
        r = lax.broadcasted_iota(jnp.int32, (tb, tb), 0)
        j = lax.broadcasted_iota(jnp.int32, (tb, tb), 1)
        bias_prev = jnp.where(j >= r, 0.0, -jnp.inf)
        bias_next = jnp.where(j <= r, 0.0, -jnp.inf)
        bias_scr[0] = bias_prev + jnp.where(i > 0, 0.0, -jnp.inf)
        bias_scr[1] = bias_prev
        bias_scr[2] = bias_next
        bias_scr[3] = bias_next + jnp.where(i < n - 1, 0.0, -jnp.inf)
        nk = n_ctx + 3 * tb
    else:
        o_ref, s_scr, p_scr, e_scr = rest
        nk = n_ctx
    lane = lax.broadcasted_iota(jnp.int32, (1, LANES), 1)
    lo = lane < HEAD_DIM
    lo_k = lax.broadcasted_iota(jnp.int32, (nk, LANES), 1) < HEAD_DIM
    group_ones = [jnp.where(lo_k, 1.0, 0.0).astype(BF16), jnp.where(lo_k, 0.0, 1.0).astype(BF16)]

    def scores(jb):
        q = q_ref[jb * tb:(jb + 1) * tb, :]
        lhs = jnp.concatenate([q[:, m * LANES:(m + 1) * LANES] for m in range(n_pairs)], axis=0)
        kk = (jnp.concatenate([kc_ref[...], k_loc[jb * tb:(jb + 3) * tb]], axis=0) if local
              else kc_ref[...])
        k2 = jnp.concatenate([kk * group_ones[0], kk * group_ones[1]], axis=0)
        s_scr[jb] = _dot_nt(lhs, k2)

    def softmax(jb):
        for m in range(n_pairs):
            sinks = [sink_ref[m + (ATT_HEADS // ATT_KV_HEADS) * g] for g in range(ATT_KV_HEADS)]
            for gi in range(tb // rg):
                rq = gi * rg
                r0 = m * tb + rq
                s = s_scr[jb, r0:r0 + rg, :]
                probs, sink_terms = [], []
                for g in range(ATT_KV_HEADS):
                    sg = s[:, g * nk:(g + 1) * nk]
                    if local:
                        bp = bias_scr[0 if jb == 0 else 1, rq:rq + rg, :]
                        bn = bias_scr[3 if jb == n_sub - 1 else 2, rq:rq + rg, :]
                        sg = jnp.concatenate(
                            [sg[:, :n_ctx], sg[:, n_ctx:n_ctx + tb] + bp,
                             sg[:, n_ctx + tb:n_ctx + 2 * tb], sg[:, n_ctx + 2 * tb:] + bn], axis=1)
                    mx = jnp.maximum(jnp.max(sg, axis=-1, keepdims=True), sinks[g])
                    probs.append(jnp.exp(sg - mx).astype(BF16))
                    sink_terms.append(jnp.exp(sinks[g] - mx))
                p_scr[jb, r0:r0 + rg, :] = jnp.concatenate(probs, axis=1)
                e_scr[jb, r0:r0 + rg, :] = jnp.where(lo, sink_terms[0], sink_terms[1])

    def values(jb):
        vv = (jnp.concatenate([vc_ref[...], v_loc[jb * tb:(jb + 3) * tb]], axis=0) if local
              else vc_ref[...])
        v2 = jnp.concatenate(
            [jnp.concatenate([vv * group_ones[0], group_ones[0]], axis=1),
             jnp.concatenate([vv * group_ones[1], group_ones[1]], axis=1)], axis=0)
        o_aug = _dot(p_scr[jb], v2)
        o = o_aug[:, :LANES] * (1.0 / (o_aug[:, LANES:] + e_scr[jb]))
        for m in range(n_pairs):
            o_ref[jb * tb:(jb + 1) * tb, m * LANES:(m + 1) * LANES] = (
                o[m * tb:(m + 1) * tb].astype(BF16))

    scores(0)
    for jb in range(n_sub):
        if jb + 1 < n_sub:
            scores(jb + 1)
        softmax(jb)
        values(jb)


def _attention(q, k, v, k_ctx, v_ctx, sink, local, n_sub=8):
    B, S, _ = q.shape
    Lc = k_ctx.shape[1]
    if not local:
        n_sub = 1
    while S % (WINDOW * n_sub):
        n_sub //= 2
    tq = WINDOW * n_sub if local else S
    nq = S // tq
    in_specs = [
        pl.BlockSpec(memory_space=pltpu.SMEM),
        pl.BlockSpec((None, tq, ATT_W), lambda b, i: (b, i, 0)),
        pl.BlockSpec((None, Lc, KV_W), lambda b, i: (b, 0, 0)),
        pl.BlockSpec((None, Lc, KV_W), lambda b, i: (b, 0, 0)),
    ]
    args = [sink, q, k_ctx, v_ctx]
    if local:
        nw = S // WINDOW
        nb = [pl.BlockSpec((None, WINDOW, KV_W), lambda b, i: (b, jnp.maximum(i * n_sub - 1, 0), 0)),
              pl.BlockSpec((None, tq, KV_W), lambda b, i: (b, i, 0)),
              pl.BlockSpec((None, WINDOW, KV_W),
                           lambda b, i: (b, jnp.minimum((i + 1) * n_sub, nw - 1), 0))]
        in_specs += nb + nb
        args += [k, k, k, v, v, v]
    tb = tq // n_sub
    nk = Lc + (3 * tb if local else 0)
    n_rows = (ATT_W // LANES) * tb
    scratch = [pltpu.VMEM((n_sub, n_rows, ATT_KV_HEADS * nk), F32),
               pltpu.VMEM((n_sub, n_rows, ATT_KV_HEADS * nk), BF16),
               pltpu.VMEM((n_sub, n_rows, LANES), F32)]
    if local:
        scratch.append(pltpu.VMEM((4, tb, tb), F32))
    return pl.pallas_call(
        functools.partial(_attn_kernel, local=local, n_sub=n_sub),
        grid=(B, nq),
        in_specs=in_specs,
        out_specs=pl.BlockSpec((None, tq, ATT_W), lambda b, i: (b, i, 0)),
        out_shape=jax.ShapeDtypeStruct((B, S, ATT_W), BF16),
        scratch_shapes=scratch,
        compiler_params=_params(2),
        name="attn_window" if local else "attn_ctx",
    )(*args)


def _gla_consts(rev):
    t = np.arange(GLA_CHUNK)[:, None]
    s = np.arange(GLA_CHUNK)[None, :]
    same = (t // GLA_SUB) == (s // GLA_SUB)
    before = (s >= t) if rev else (s <= t)
    return np.concatenate([same & before, before], axis=0).astype(np.float32)


def _head_blocks():
    h = np.arange(GLA_W) // HEAD_DIM
    return (h[:, None] == h[None, :]).astype(np.float32)


def _gla_last_row(blk, rev):
    return blk * GLA_SUB if rev else blk * GLA_SUB + GLA_SUB - 1


def _gla_prepare(q, k, g, cum, rev):
    n_sub = GLA_CHUNK // GLA_SUB
    g2 = g * LOG2_E
    g_hi = g2.astype(BF16)
    rem = g2 - g_hi.astype(F32)
    g_mid = rem.astype(BF16)
    g_lo = (rem - g_mid.astype(F32)).astype(BF16)
    cs = _dot(cum, g_hi) + _dot(cum, g_mid) + _dot(cum, g_lo)
    beta, bch = cs[0:GLA_CHUNK], cs[GLA_CHUNK:2 * GLA_CHUNK]
    gam = jnp.concatenate(
        [beta[_gla_last_row(b, rev):_gla_last_row(b, rev) + 1, :] - beta[b * GLA_SUB:(b + 1) * GLA_SUB, :]
         for b in range(n_sub)], axis=0)
    end_row = _gla_last_row(0 if rev else n_sub - 1, rev)
    dch = bch[end_row:end_row + 1, :] - bch
    f32_rows = jnp.concatenate([beta, bch, q * jnp.exp2(beta), k * jnp.exp2(gam)], axis=0)
    b16_rows = jnp.concatenate([(q * jnp.exp2(bch)).astype(BF16), (k * jnp.exp2(dch)).astype(BF16)],
                               axis=0)
    return f32_rows, b16_rows


def _gla_causal(rev):
    t = (np.arange(GLA_HEADS * GLA_FAST_CHUNK) % GLA_FAST_CHUNK)[:, None]
    s = np.arange(GLA_FAST_CHUNK)[None, :]
    return ((s >= t) if rev else (s <= t)).astype(np.float32)


def _gla_before(rev):
    t = np.arange(GLA_FAST_CHUNK)[:, None]
    s = np.arange(GLA_FAST_CHUNK)[None, :]
    return ((s >= t) if rev else (s <= t)).astype(np.float32)


def _gla_prepare_fast(q, k, v, g, cum_chunk, head_mask, rev):
    g2 = g * LOG2_E
    g_hi = g2.astype(BF16)
    g_lo = (g2 - g_hi.astype(F32)).astype(BF16)
    bch = _dot(cum_chunk, g_hi) + _dot(cum_chunk, g_lo)
    end_row = 0 if rev else GLA_FAST_CHUNK - 1
    mid_row = GLA_FAST_CHUNK // 2 if rev else GLA_FAST_CHUNK // 2 - 1
    b_end = bch[end_row:end_row + 1, :]
    b_mid = bch[mid_row:mid_row + 1, :]
    ok = (jnp.max(-b_end) <= GLA_FAST_RANGE).astype(jnp.int32)
    q_mid_f = q * jnp.exp2(bch - b_mid)
    k_mid_f = k * jnp.exp2(b_mid - bch)
    q_mid = q_mid_f.astype(BF16)
    zero = jnp.zeros_like(q_mid)
    rows = jnp.concatenate(
        [jnp.where(hm, q_mid, zero) for hm in head_mask]
        + [k_mid_f.astype(BF16), (q_mid_f * jnp.exp2(b_mid)).astype(BF16),
           (k_mid_f * jnp.exp2(b_end - b_mid)).astype(BF16), v.astype(BF16)], axis=0)
    return rows, jnp.exp2(b_end), ok


def _gla_tile_fast(pb_scr, dec_scr, dirs, bd, head_mask, n_chunks):
    cf = GLA_FAST_CHUNK
    n_q = GLA_HEADS * cf
    work = []
    for d, (_, _, _, st_ref, o_ref, caus_ref, rev) in enumerate(dirs):
        for c in (range(n_chunks - 1, -1, -1) if rev else range(n_chunks)):
            pb = pb_scr.at[d, c]
            vb = pb[n_q + 3 * cf:n_q + 4 * cf]
            upd = _dot_tn(vb, pb[n_q + 2 * cf:n_q + 3 * cf])
            a = _dot_nt(pb[0:n_q], pb[n_q:n_q + cf])
            work.append(dict(d=d, c=c, pb=pb, vb=vb, upd=upd, a=a, o_ref=o_ref, caus=caus_ref))
    for d, (_, _, _, st_ref, _, _, _) in enumerate(dirs):
        st = st_ref[...]
        for w in (w for w in work if w["d"] == d):
            w["o_inter"] = _dot_nt(w["pb"][n_q + cf:n_q + 2 * cf], st.astype(BF16))
            st = st * dec_scr[d, w["c"], 0:1, :] + w["upd"] * bd
        st_ref[...] = st
    for w in work:
        w["r"] = _dot((w["a"] * w["caus"][...]).astype(BF16), w["vb"])
    for w in work:
        o = w["o_inter"]
        for hh, hm in enumerate(head_mask):
            o = o + jnp.where(hm, w["r"][hh * cf:(hh + 1) * cf, :], 0.0)
        w["o_ref"][w["c"] * cf:(w["c"] + 1) * cf, :] = o


def _gla_chunk(q, k, v, pf_ref, pb_ref, st_ref, o_ref, r0, ones, bd, head_mask, rev):
    n_sub = GLA_CHUNK // GLA_SUB
    order = list(range(n_sub))[::-1] if rev else list(range(n_sub))

    def rows(a, blk):
        return a[blk * GLA_SUB:(blk + 1) * GLA_SUB, :]

    half = GLA_SUB // 2
    beta = pf_ref[0:GLA_CHUNK]
    q_loc = pf_ref[2 * GLA_CHUNK:3 * GLA_CHUNK]
    k_loc = pf_ref[3 * GLA_CHUNK:4 * GLA_CHUNK]
    g_tot = [beta[_gla_last_row(b, rev):_gla_last_row(b, rev) + 1, :] for b in range(n_sub)]
    end_row = GLA_CHUNK + _gla_last_row(order[-1], rev)
    b_end = pf_ref[end_row:end_row + 1, :]
    half_row = lax.broadcasted_iota(jnp.int32, (half, 1), 0)

    st = st_ref[...]
    o_inter = _dot_nt(pb_ref[0:GLA_CHUNK], st.astype(BF16))
    upd = _dot_tn(v.astype(BF16), pb_ref[GLA_CHUNK:2 * GLA_CHUNK])
    scores, values = {}, {}
    for p in range(1, n_sub):
        ks, vs = [], []
        for pp in range(p):
            sb = order[pp]
            kk = rows(k_loc, sb)
            mids = [order[x] for x in range(pp + 1, p)]
            if mids:
                tot = g_tot[mids[0]]
                for mb in mids[1:]:
                    tot = tot + g_tot[mb]
                kk = kk * jnp.exp2(tot)
            ks.append(kk)
            vs.append(rows(v, sb))
        qt = rows(q_loc, order[p])
        q_heads = jnp.concatenate([jnp.where(hm, qt, 0.0) for hm in head_mask], axis=0)
        scores[p] = _dot_nt(q_heads.astype(BF16), jnp.concatenate(ks, axis=0).astype(BF16))
        values[p] = jnp.concatenate(vs, axis=0).astype(BF16)
    yield

    pairs = []
    for s in range(GLA_SUB):
        hs = s // half
        for hb in (range(0, hs + 1) if rev else range(hs, GLA_SUB // half)):
            pairs.append((s, hb))
    sums = {}
    for p, tb in enumerate(order):
        bt, qb, kb = rows(beta, tb), rows(q, tb), rows(k, tb)
        slabs = []
        for s, hb in pairs:
            d = bt[hb * half:(hb + 1) * half, :] - bt[s:s + 1, :]
            if hb == s // half:
                keep = (half_row <= s % half) if rev else (half_row >= s % half)
                d = jnp.where(keep, d, -jnp.inf)
            slabs.append(qb[hb * half:(hb + 1) * half, :] * jnp.exp2(d) * kb[s:s + 1, :])
        sums[p] = _dot(jnp.concatenate(slabs, axis=0).astype(BF16), ones)
        yield

    mixed = {p: _dot(scores[p].astype(BF16), values[p]) for p in range(1, n_sub)}
    yield

    for p, tb in enumerate(order):
        vb = rows(v, tb)
        acc = rows(o_inter, tb)
        if p > 0:
            for hh, hm in enumerate(head_mask):
                acc = acc + jnp.where(hm, mixed[p][hh * GLA_SUB:(hh + 1) * GLA_SUB, :], 0.0)
        halves = [acc[hb * half:(hb + 1) * half, :] for hb in range(GLA_SUB // half)]
        for idx, (s, hb) in enumerate(pairs):
            halves[hb] = halves[hb] + sums[p][idx * half:(idx + 1) * half, :] * vb[s:s + 1, :]
        o_ref[pl.ds(r0 + tb * GLA_SUB, GLA_SUB), :] = jnp.concatenate(halves, axis=0)
        yield

    st_ref[...] = st * jnp.exp2(b_end) + upd * bd


def _interleave(gens):
    active = list(gens)
    while active:
        for gen in list(active):
            try:
                next(gen)
            except StopIteration:
                active.remove(gen)


def _gla_kernel(cumf_ref, cumb_ref, ones_ref, bd_ref, causf_ref, causb_ref, beff_ref, befb_ref,
                glf_ref, glb_ref, gf_ref, gb_ref, s0f_ref, s0b_ref, of_ref, ob_ref, sff_ref,
                sfb_ref, stf_ref, stb_ref, pb_scr, dec_scr, ok_ref, *, n_chunks, n_fast):
    i = pl.program_id(1)
    lane_head = lax.broadcasted_iota(jnp.int32, (1, GLA_W), 1) // HEAD_DIM
    head_mask = [lane_head == hh for hh in range(GLA_HEADS)]
    dirs = ((glf_ref, gf_ref, cumf_ref, stf_ref, of_ref, causf_ref, False),
            (glb_ref, gb_ref, cumb_ref, stb_ref, ob_ref, causb_ref, True))

    @pl.when(i == 0)
    def _():
        stf_ref[...] = s0f_ref[...]
        stb_ref[...] = s0b_ref[...]

    def load(gl_ref, g_ref, r0, size=GLA_CHUNK):
        rows = pl.ds(r0, size)
        return (gl_ref[rows, 0:GLA_W], gl_ref[rows, GLA_W:2 * GLA_W],
                gl_ref[rows, 2 * GLA_W:3 * GLA_W], g_ref[rows, :])

    for c in range(n_fast):
        for d, (bef_ref, (gl_ref, g_ref, _, _, _, _, rev)) in enumerate(
                zip((beff_ref, befb_ref), dirs)):
            q, k, v, g = load(gl_ref, g_ref, c * GLA_FAST_CHUNK, GLA_FAST_CHUNK)
            rows, dec, ok = _gla_prepare_fast(q, k, v, g, bef_ref[...], head_mask, rev)
            pb_scr[d, c] = rows
            dec_scr[d, c] = jnp.broadcast_to(dec, (8, GLA_W))
            ok_ref[d, c] = ok

    n_ok = ok_ref[0, 0]
    for d in range(2):
        for c in range(n_fast):
            if (d, c) != (0, 0):
                n_ok = n_ok + ok_ref[d, c]
    fast = n_ok == 2 * n_fast

    @pl.when(fast)
    def _():
        _gla_tile_fast(pb_scr, dec_scr, dirs, bd_ref[...], head_mask, n_fast)

    @pl.when(jnp.logical_not(fast))
    def _():
        def chunk(ci, carry):
            cidx = (ci, n_chunks - 1 - ci)
            gens = []
            for d, (gl_ref, g_ref, cum_ref, st_ref, o_ref, _, rev) in enumerate(dirs):
                r0 = pl.multiple_of(cidx[d] * GLA_CHUNK, GLA_CHUNK)
                q, k, v, g = load(gl_ref, g_ref, r0)
                pf, pb = _gla_prepare(q, k, g, cum_ref[...], rev)
                gens.append(_gla_chunk(q, k, v, pf, pb, st_ref, o_ref, r0, ones_ref[...],
                                       bd_ref[...], head_mask, rev))
            _interleave(gens)
            return carry

        lax.fori_loop(0, n_chunks, chunk, 0)

    @pl.when(i == pl.num_programs(1) - 1)
    def _():
        sff_ref[...] = stf_ref[...]
        sfb_ref[...] = stb_ref[...]


def _gla(gl, gates, s0f, s0b, tile):
    B, S, _ = gl.shape
    nt = S // tile
    assert tile % GLA_FAST_CHUNK == 0
    n_chunks = tile // GLA_CHUNK
    n_fast = tile // GLA_FAST_CHUNK
    ones_bd = jnp.asarray(_head_blocks(), BF16)
    bd = jnp.asarray(_head_blocks(), F32)
    state_spec = pl.BlockSpec((None, GLA_W, GLA_W), lambda b, i: (b, 0, 0))
    return pl.pallas_call(
        functools.partial(_gla_kernel, n_chunks=n_chunks, n_fast=n_fast),
        grid=(B, nt),
        in_specs=[
            _const_spec((2 * GLA_CHUNK, GLA_CHUNK)),
            _const_spec((2 * GLA_CHUNK, GLA_CHUNK)),
            _const_spec((GLA_W, GLA_W)),
            _const_spec((GLA_W, GLA_W)),
            _const_spec((GLA_HEADS * GLA_FAST_CHUNK, GLA_FAST_CHUNK)),
            _const_spec((GLA_HEADS * GLA_FAST_CHUNK, GLA_FAST_CHUNK)),
            _const_spec((GLA_FAST_CHUNK, GLA_FAST_CHUNK)),
            _const_spec((GLA_FAST_CHUNK, GLA_FAST_CHUNK)),
            pl.BlockSpec((None, tile, 3 * GLA_W), lambda b, i: (b, i, 0)),
            pl.BlockSpec((None, tile, 3 * GLA_W), lambda b, i: (b, nt - 1 - i, 0)),
            pl.BlockSpec((None, tile, GLA_W), lambda b, i: (b, i, 0)),
            pl.BlockSpec((None, tile, GLA_W), lambda b, i: (b, nt - 1 - i, 1)),
            state_spec, state_spec,
        ],
        out_specs=[
            pl.BlockSpec((None, tile, GLA_W), lambda b, i: (b, i, 0)),
            pl.BlockSpec((None, tile, GLA_W), lambda b, i: (b, nt - 1 - i, 0)),
            state_spec, state_spec,
        ],
        out_shape=[jax.ShapeDtypeStruct((B, S, GLA_W), F32),
                   jax.ShapeDtypeStruct((B, S, GLA_W), F32),
                   jax.ShapeDtypeStruct((B, GLA_W, GLA_W), F32),
                   jax.ShapeDtypeStruct((B, GLA_W, GLA_W), F32)],
        scratch_shapes=[pltpu.VMEM((GLA_W, GLA_W), F32), pltpu.VMEM((GLA_W, GLA_W), F32),
                        pltpu.VMEM((2, n_fast, (GLA_HEADS + 4) * GLA_FAST_CHUNK, GLA_W), BF16),
                        pltpu.VMEM((2, n_fast, 8, GLA_W), F32),
                        pltpu.SMEM((2, n_fast), jnp.int32)],
        compiler_params=_params(2),
        name="gla",
    )(jnp.asarray(_gla_consts(False), BF16), jnp.asarray(_gla_consts(True), BF16),
      ones_bd, bd, jnp.asarray(_gla_causal(False), F32), jnp.asarray(_gla_causal(True), F32),
      jnp.asarray(_gla_before(False), BF16), jnp.asarray(_gla_before(True), BF16),
      gl, gl, gates, gates, s0f, s0b)


def _post_kernel(x_ref, mod_ref, g2_ref, cu_ref, cup_ref, cun_ref, cb_ref, cw_ref, at_ref,
                 of_ref, ob_ref, gg_ref, ng_ref, ones_ref, wo_ref, wu_ref, wd_ref,
                 *rest, final):
    if final:
        fg_ref, o_ref, acc_ref = rest
    else:
        o_ref, acc_ref = rest
    i = pl.program_id(1)
    n = pl.num_programs(1)
    tm = x_ref.shape[0]

    u = cu_ref[...]
    row = lax.broadcasted_iota(jnp.int32, (tm, 1), 0)
    prev_row = jnp.where(i > 0, cup_ref[7:8, :], 0.0)
    next_row = jnp.where(i < n - 1, cun_ref[0:1, :], 0.0)
    u_prev = jnp.where(row == 0, prev_row, pltpu.roll(u, 1, axis=0))
    u_next = jnp.where(row == tm - 1, next_row, pltpu.roll(u, tm - 1, axis=0))
    conv = cb_ref[...] * (cw_ref[0:1, :] * u_prev + cw_ref[1:2, :] * u + cw_ref[2:3, :] * u_next)

    x1_parts, h2_parts = [], []
    n_groups = POST_ROW_GROUPS if tm % (POST_ROW_GROUPS * 16) == 0 else 1
    rows_per = tm // n_groups
    for gr in range(n_groups):
        rs = slice(gr * rows_per, (gr + 1) * rows_per)
        o = of_ref[rs, :] + ob_ref[rs, :]
        sq = o * o
        sq_hi = sq.astype(BF16)
        sq_lo = (sq - sq_hi.astype(F32)).astype(BF16)
        ms = (_dot(sq_hi, ones_ref[...]) + _dot(sq_lo, ones_ref[...])) * (1.0 / HEAD_DIM)
        gla = o * lax.rsqrt(ms + EPS) * ng_ref[...] * _silu(gg_ref[rs, :])

        mix = (_dot(conv[rs, :].astype(BF16), wo_ref[0:CONV_CH, :])
               + _dot(at_ref[rs, :], wo_ref[CONV_CH:CONV_CH + ATT_W, :])
               + _dot(gla.astype(BF16), wo_ref[CONV_CH + ATT_W:, :]))
        x1_g = x_ref[rs, :] + mod_ref[2:3, :] * mix

        ms2 = jnp.mean(x1_g * x1_g, axis=-1, keepdims=True)
        y2 = x1_g * lax.rsqrt(ms2 + EPS) * g2_ref[...]
        x1_parts.append(x1_g)
        h2_parts.append((y2 * (1.0 + mod_ref[4:5, :]) + mod_ref[3:4, :]).astype(BF16))
    x1 = jnp.concatenate(x1_parts, axis=0)
    h2 = jnp.concatenate(h2_parts, axis=0)

    def up(jc):
        if isinstance(jc, int):
            ca, cb = jc * FF_CHUNK, D_FF + jc * FF_CHUNK
        else:
            ca = pl.multiple_of(jc * FF_CHUNK, FF_CHUNK)
            cb = pl.multiple_of(D_FF + jc * FF_CHUNK, FF_CHUNK)
        a = _dot(h2, wu_ref[:, pl.ds(ca, FF_CHUNK)])
        b = _dot(h2, wu_ref[:, pl.ds(cb, FF_CHUNK)])
        return a, b

    def act(ab):
        return (_silu(ab[0]) * ab[1]).astype(BF16)

    prev = act(up(0))
    ab = up(1)
    acc_ref[...] = _dot(prev, wd_ref[0])
    prev = act(ab)

    def ff(t, prev):
        for u in range(FF_UNROLL):
            jc = 2 + t * FF_UNROLL + u
            ab = up(jc)
            acc_ref[...] += _dot(prev, wd_ref[jc - 1])
            prev = act(ab)
        return prev

    n_trips = (N_FF_CHUNKS - 2) // FF_UNROLL
    prev = ff(0, prev) if n_trips == 1 else lax.fori_loop(0, n_trips, ff, prev)
    x2 = x1 + mod_ref[5:6, :] * (acc_ref[...] + _dot(prev, wd_ref[N_FF_CHUNKS - 1]))
    if final:
        msf = jnp.mean(x2 * x2, axis=-1, keepdims=True)
        x2 = x2 * lax.rsqrt(msf + EPS) * fg_ref[...]
    o_ref[...] = x2


def _post(xs, mods_l, mod_row, g2, cu, cb, conv_w, attn, o_f, o_b, gl, norm_g, w_out, wu, wd,
          layer, final_g, tm):
    B, S, D = xs.shape
    nt = S // tm
    hb = tm // 8
    row = lambda w: pl.BlockSpec((None, tm, w), lambda b, i: (b, i, 0))
    if mod_row is None:
        mod_spec = pl.BlockSpec((None, 8, D), lambda b, i: (b, 0, 0))
    else:
        mod_spec = pl.BlockSpec((None, 8, D), lambda b, i: (mod_row, 0, 0))
    final = final_g is not None
    in_specs = [
        row(D), mod_spec, _const_spec((1, D)),
        row(CONV_CH),
        pl.BlockSpec((None, 8, CONV_CH), lambda b, i: (b, jnp.maximum(i * hb - 1, 0), 0)),
        pl.BlockSpec((None, 8, CONV_CH), lambda b, i: (b, jnp.minimum((i + 1) * hb, S // 8 - 1), 0)),
        row(CONV_CH), _const_spec((8, CONV_CH)), row(ATT_W), row(GLA_W), row(GLA_W),
        pl.BlockSpec((None, tm, GLA_W), lambda b, i: (b, i, 3)),
        _const_spec((1, GLA_W)), _const_spec((GLA_W, GLA_W)), _layer_spec((D, D), layer),
        _layer_spec((D, 2 * D_FF), layer), _layer_spec((N_FF_CHUNKS, FF_CHUNK, D), layer),
    ]
    args = [xs, mods_l, g2, cu, cu, cu, cb, conv_w, attn, o_f, o_b, gl, norm_g,
            jnp.asarray(_head_blocks(), BF16), w_out, wu, wd]
    if final:
        in_specs.append(_const_spec((1, D)))
        args.append(final_g)
    return pl.pallas_call(
        functools.partial(_post_kernel, final=final),
        grid=(B, nt),
        in_specs=in_specs,
        out_specs=row(D),
        out_shape=jax.ShapeDtypeStruct((B, S, D), F32),
        scratch_shapes=[pltpu.VMEM((tm, D), F32)],
        compiler_params=_params(2),
        name="post_final" if final else "post",
    )(*args)


def _rope_tables(seq):
    t = jnp.arange(seq)
    pos = jnp.stack([(t // GRID_W).astype(F32), (t % GRID_W).astype(F32)], axis=1)
    n_freq = HEAD_DIM // 4
    inv_freq = ROPE_BASE ** (-jnp.arange(n_freq, dtype=F32) / n_freq)
    ang = pos[:, :, None] * inv_freq
    cos, sin = jnp.cos(ang), jnp.sin(ang)
    zero = jnp.zeros_like(sin)
    cos_t = jnp.stack([cos, cos], axis=2).reshape(seq, HEAD_DIM)
    sa_t = jnp.stack([-sin, zero], axis=2).reshape(seq, HEAD_DIM)
    sb_t = jnp.stack([zero, sin], axis=2).reshape(seq, HEAD_DIM)
    rep = LANES // HEAD_DIM
    return tuple(jnp.tile(a, (1, rep)) for a in (cos_t, sa_t, sb_t))


def _tile_rows(S, want):
    t = min(want, S)
    while S % t:
        t //= 2
    return t


def kernel(x, c, ctx, c_ctx, w_mod, b_mod, norm1_g, norm2_g, w_in, conv_w, attn_sink,
           gla_gate_w, gla_gate_b, gla_norm_g, w_out, w_up, w_down, final_norm_g):
    B, S, D = x.shape
    L = w_mod.shape[0]
    Lc = ctx.shape[1]
    assert D == D_MODEL and S % WINDOW == 0 and Lc % GLA_CHUNK == 0 and B + 1 <= 8

    cvec = jnp.zeros((8, D), F32).at[:B].set(c).at[B].set(c_ctx)
    mods = _modulation(cvec, w_mod, b_mod).reshape(L, 8, N_MOD, D)
    mods = jnp.pad(mods, ((0, 0), (0, 0), (0, 8 - N_MOD), (0, 0)))

    order = np.array(Q_HEAD_ORDER)
    wq = w_in[:, :, 768:1280].reshape(L, D, ATT_HEADS, HEAD_DIM)[:, :, order].reshape(L, D, ATT_W)
    w_in_p = jnp.concatenate(
        [w_in[:, :, :768], wq, w_in[:, :, 1280:1536], w_in[:, :, 2560:2592],
         jnp.zeros((L, D, LANES - 2 * GLA_RANK), F32), w_in[:, :, 1536:2560]], axis=2).astype(BF16)
    wo_att = w_out[:, 256:768].reshape(L, ATT_HEADS, HEAD_DIM, D)[:, order].reshape(L, ATT_W, D)
    w_out_p = jnp.concatenate([w_out[:, :256], wo_att, w_out[:, 768:]], axis=1).astype(BF16)
    wg = jnp.zeros((L, LANES, 2 * GLA_W), F32)
    wg = wg.at[:, :GLA_RANK, :GLA_W].set(gla_gate_w[:, 0])
    wg = wg.at[:, GLA_RANK:2 * GLA_RANK, GLA_W:].set(gla_gate_w[:, 1]).astype(BF16)
    bg = gla_gate_b.reshape(L, 1, 2 * GLA_W)
    cw = jnp.pad(conv_w, ((0, 0), (0, 8 - CONV_K), (0, 0)))
    ng = jnp.tile(gla_norm_g, (1, GLA_HEADS)).reshape(L, 1, GLA_W)
    wu = w_up.astype(BF16)
    wd = w_down.reshape(L, N_FF_CHUNKS, FF_CHUNK, D).astype(BF16)
    g1 = norm1_g.reshape(L, 1, D)
    g2 = norm2_g.reshape(L, 1, D)

    rope_tabs = _rope_tables(S)
    tm = _tile_rows(S, 512)
    proj_tm = _tile_rows(S, 1024)
    gla_tile = _tile_rows(S, 1024)
    zero_state = jnp.zeros((B, GLA_W, GLA_W), F32)

    for l in range(L):
        last = l == L - 1
        cu, cb, q, k, v, gl, gates = _proj(x, mods[l], None, g1[l], w_in_p, l, wg[l], bg[l],
                                           rope_tabs, proj_tm)
        ccu, ccb, cq, ck, cv, cgl, cgates = _proj(ctx, mods[l], B, g1[l], w_in_p, l, wg[l], bg[l],
                                                  None, Lc)
        attn = _attention(q, k, v, ck, cv, attn_sink[l], local=True)
        oc_f, oc_b, sc_f, sc_b = _gla(cgl, cgates, zero_state, zero_state, Lc)
        o_f, o_b, _, _ = _gla(gl, gates, sc_f, sc_b, gla_tile)
        fin = final_norm_g.reshape(1, D) if last else None
        x = _post(x, mods[l], None, g2[l], cu, cb, cw[l], attn, o_f, o_b, gl, ng[l],
                  w_out_p, wu, wd, l, fin, tm)
        if not last:
            attn_c = _attention(cq, None, None, ck, cv, attn_sink[l], local=False)
            ctx = _post(ctx, mods[l], B, g2[l], ccu, ccb, cw[l], attn_c, oc_f, oc_b, cgl, ng[l],
                        w_out_p, wu, wd, l, None, Lc)
    return x
```
